```python
import math
import jax, jax.numpy as jnp
from jax import lax
import numpy as np

D_MODEL = 1024
BATCH = 8
SEQ = 2048
DEPTH = 1
DEC_BATCH = 32
DEC_SEQ = 1
PAST_LEN = 16384
PAGE_SIZE = 128

A_HEADS = 8
HEAD_DIM = 64
A_WIDTH = A_HEADS * HEAD_DIM
PATTERNS = ((128, 1), (512, 4), (2048, 16))
MAX_WINDOW = 2048
BLOCK = 128
NUM_BUCKETS = 32
MAX_DISTANCE = 2048
SCALE = HEAD_DIM ** -0.5
NEG = -1e30
R_HEADS = 8
R_HEAD = 64
R_WIDTH = R_HEADS * R_HEAD
DECAY_LORA = 64
AAA_LORA = 64
GATE_LORA = 128
GN_EPS = 64e-5
N_RWKV_COLS = 3 * R_WIDTH + DECAY_LORA + AAA_LORA + GATE_LORA
N_IN_COLS = 3 * A_WIDTH + N_RWKV_COLS + 2 * D_MODEL
N_EXPERTS = 32
TOP_K = 4
D_EXPERT = D_MODEL
SWIGLU_LIMIT = 7.0
SWIGLU_ALPHA = 1.702
MOE_BLOCK = 128
RMS_EPS = 1e-6

kernel_name = 'dilated_rwkv7_moe_hybrid_step'


def rmsnorm(x, g):
    xf = x.astype(jnp.float32)
    y = xf * lax.rsqrt(jnp.mean(xf * xf, axis=-1, keepdims=True) + RMS_EPS)
    return (y * g.astype(jnp.float32)).astype(x.dtype)


def t5_bucket(dist):
    max_exact = NUM_BUCKETS // 2
    d = jnp.maximum(dist, 1).astype(jnp.float32)
    large = max_exact + (jnp.log(d / max_exact) / math.log(MAX_DISTANCE / max_exact)
                         * (NUM_BUCKETS - max_exact)).astype(jnp.int32)
    large = jnp.minimum(large, NUM_BUCKETS - 1)
    return jnp.where(dist < max_exact, dist, large)


def dilated_pattern_prompt(q, k, v, rel_bias, window, dil):
    B, S, H, E = q.shape
    n = window // dil
    assert n <= BLOCK
    sd = -(-S // dil)
    nb = -(-sd // BLOCK)
    s_pad = nb * BLOCK * dil

    def to_blocks(t):
        t = jnp.pad(t, ((0, 0), (0, s_pad - S), (0, 0), (0, 0)))
        t = t.reshape(B, nb * BLOCK, dil, H, E).transpose(0, 2, 1, 3, 4)
        return t.reshape(B, dil, nb, BLOCK, H, E)

    def with_prev(t):
        prev = jnp.pad(t, ((0, 0), (0, 0), (1, 0), (0, 0), (0, 0), (0, 0)))[:, :, :-1]
        return jnp.concatenate([prev, t], axis=3)

    qb = to_blocks(q)
    kb = with_prev(to_blocks(k))
    vb = with_prev(to_blocks(v))
    i = jnp.arange(BLOCK)[:, None]
    j = jnp.arange(2 * BLOCK)[None, :]
    delta = i + BLOCK - j
    band = (delta >= 0) & (delta <= n)
    has_prev = (jnp.arange(nb) > 0)[:, None, None] | (j >= BLOCK)[None]
    mask = band[None] & has_prev
    bias = rel_bias.astype(jnp.float32)[t5_bucket(jnp.clip(delta, 0, n) * dil)].transpose(2, 0, 1)
    s = jnp.einsum('brcihe,brcjhe->brchij', qb, kb) * SCALE + bias[None, None, None]
    s = jnp.where(mask[None, None, :, None], s, NEG)
    m = jnp.max(s, axis=-1, keepdims=True)
    p = jnp.exp(s - m)
    l = jnp.sum(p, axis=-1, keepdims=True)
    o = jnp.einsum('brchij,brcjhe->brcihe', p / l, vb)
    lse = jnp.swapaxes((m + jnp.log(l))[..., 0], 3, 4)
    o = o.reshape(B, dil, nb * BLOCK, H, E).transpose(0, 2, 1, 3, 4).reshape(B, s_pad, H, E)[:, :S]
    lse = lse.reshape(B, dil, nb * BLOCK, H).transpose(0, 2, 1, 3).reshape(B, s_pad, H)[:, :S]
    return o, lse


def merge_patterns(outs, lses):
    wts = jax.nn.softmax(jnp.stack(lses, 0), axis=0)
    return jnp.einsum('gbth,gbthe->bthe', wts, jnp.stack(outs, 0))


def dilated_attn_prompt(q, k, v, rel_bias):
    q, k, v = (t.astype(jnp.float32) for t in (q, k, v))
    res = [dilated_pattern_prompt(q, k, v, rel_bias, w, d) for (w, d) in PATTERNS]
    return merge_patterns([r[0] for r in res], [r[1] for r in res])


def dilated_attn_sample(q, k_new, v_new, cache_k, cache_v, rel_bias):
    wb = cache_k.shape[1]
    L = q.shape[1]
    q = q.astype(jnp.float32)
    kc = jnp.concatenate([cache_k.astype(jnp.float32), k_new.astype(jnp.float32)], axis=1)
    vc = jnp.concatenate([cache_v.astype(jnp.float32), v_new.astype(jnp.float32)], axis=1)
    outs, lses = [], []
    for (window, dil) in PATTERNS:
        n = window // dil
        jj = jnp.arange(n + 1)
        idx = wb + jnp.arange(L)[:, None] - jj[None, :] * dil
        valid = idx >= 0
        idx = jnp.maximum(idx, 0)
        kg = kc[:, idx]
        vg = vc[:, idx]
        bias = rel_bias.astype(jnp.float32)[t5_bucket(jj * dil)]
        s = jnp.einsum('blhe,bljhe->bhlj', q, kg) * SCALE + bias.T[None, :, None, :]
        s = jnp.where(valid[None, None], s, NEG)
        lse = jax.nn.logsumexp(s, axis=-1)
        p = jnp.exp(s - lse[..., None])
        outs.append(jnp.einsum('bhlj,bljhe->blhe', p, vg))
        lses.append(lse.transpose(0, 2, 1))
    return merge_patterns(outs, lses)


def rwkv_scan(S0, r, decay, k, v, kk, a):
    def step(S, inp):
        r_t, w_t, k_t, v_t, kk_t, a_t = inp
        sa = jnp.einsum('bhvk,bhk->bhv', S, -kk_t)
        S = (S * w_t[:, :, None, :] + sa[..., None] * (kk_t * a_t)[:, :, None, :]
             + v_t[..., None] * k_t[:, :, None, :])
        return S, jnp.einsum('bhvk,bhk->bhv', S, r_t)
    xs = tuple(t.transpose(1, 0, 2, 3) for t in (r, decay, k, v, kk, a))
    S, ys = lax.scan(step, S0, xs)
    return S, ys.transpose(1, 0, 2, 3)


def moe(h, w_router, b_router, w_gate_up, b_gate_up, w_down, b_down):
    B, T, D = h.shape
    xt = h.reshape(-1, D)
    N = xt.shape[0]
    logits = (xt @ w_router + b_router).astype(jnp.float32)
    top_v, top_i = lax.top_k(logits, TOP_K)
    gate = jax.nn.softmax(top_v, axis=-1)
    nk = N * TOP_K
    e_flat = top_i.reshape(-1)
    tok_flat = jnp.repeat(jnp.arange(N, dtype=jnp.int32), TOP_K)
    g_flat = gate.reshape(-1)
    order = jnp.argsort(e_flat)
    e_s, tok_s, g_s = e_flat[order], tok_flat[order], g_flat[order]
    sizes = jnp.bincount(e_flat, length=N_EXPERTS)
    starts = jnp.cumsum(sizes) - sizes
    padded = (sizes + MOE_BLOCK - 1) // MOE_BLOCK * MOE_BLOCK
    pends = jnp.cumsum(padded)
    pstarts = pends - padded
    dest = pstarts[e_s] + jnp.arange(nk) - starts[e_s]
    n_slots = -(-(nk + N_EXPERTS * (MOE_BLOCK - 1)) // MOE_BLOCK) * MOE_BLOCK
    n_blocks = n_slots // MOE_BLOCK
    slot_tok = jnp.full((n_slots,), N, jnp.int32).at[dest].set(tok_s)
    slot_gate = jnp.zeros((n_slots,), jnp.float32).at[dest].set(g_s)
    block_e = jnp.minimum(jnp.searchsorted(pends, jnp.arange(n_blocks) * MOE_BLOCK, side='right'), N_EXPERTS - 1)
    xpad = jnp.concatenate([xt, jnp.zeros((1, D), xt.dtype)], axis=0)
    xs = xpad[slot_tok].reshape(n_blocks, MOE_BLOCK, D)

    def expert_block(args):
        xb, e = args
        gu = xb @ w_gate_up[e] + b_gate_up[e]
        gt, up = gu[..., :D_EXPERT], gu[..., D_EXPERT:]
        gt = jnp.minimum(gt, SWIGLU_LIMIT)
        up = jnp.clip(up, -SWIGLU_LIMIT, SWIGLU_LIMIT)
        act = (up + 1) * (gt * jax.nn.sigmoid(gt * SWIGLU_ALPHA))
        return act @ w_down[e] + b_down[e]

    ys = lax.map(expert_block, (xs, block_e)).reshape(n_slots, D)
    out = jnp.zeros((N + 1, D), jnp.float32).at[slot_tok].add(ys.astype(jnp.float32) * slot_gate[:, None])[:N]
    return out.reshape(B, T, D).astype(h.dtype)


def hybrid_layer(x, shift_prev, wkv0, attend, ln1, w_in, rwkv_mu, w0, w_decay_up, a0, w_a_up, w_g_up,
                 k_k, k_a, r_k, gn_w, gn_b, w_branch_a, w_branch_b, w_out, ln2,
                 w_router, b_router, w_gate_up, b_gate_up, w_down, b_down):
    B, T, _ = x.shape
    f32 = jnp.float32
    h = rmsnorm(x, ln1)
    proj = h @ w_in
    c1 = 3 * A_WIDTH
    c2 = c1 + N_RWKV_COLS
    qkv = proj[..., :c1].reshape(B, T, 3, A_HEADS, HEAD_DIM)
    q, k, v = qkv[:, :, 0], qkv[:, :, 1], qkv[:, :, 2]
    rc = proj[..., c1:c2]
    gates = proj[..., c2:]
    o_a = attend(q, k, v)
    prev = jnp.concatenate([shift_prev[:, None].astype(rc.dtype), rc[:, :-1]], axis=1)
    xr = (rc + (prev - rc) * rwkv_mu).astype(f32)
    r, kr, vr, xw, xa, xg = jnp.split(
        xr, [R_WIDTH, 2 * R_WIDTH, 3 * R_WIDTH, 3 * R_WIDTH + DECAY_LORA, 3 * R_WIDTH + DECAY_LORA + AAA_LORA], axis=-1)
    wlog = -jax.nn.softplus(-(w0 + jnp.tanh(xw) @ w_decay_up)) - 0.5
    decay = jnp.exp(-jnp.exp(wlog))
    a = jax.nn.sigmoid(a0 + xa @ w_a_up)
    g = jax.nn.sigmoid(xg) @ w_g_up
    kk = kr * k_k
    kr = kr * (1 + (a - 1) * k_a)
    heads = lambda t: t.reshape(B, T, R_HEADS, R_HEAD)
    kk = heads(kk)
    kk = kk / jnp.maximum(jnp.sqrt(jnp.sum(kk * kk, axis=-1, keepdims=True)), 1e-12)
    rh, kh, vh = heads(r), heads(kr), heads(vr)
    wkv_T, y = rwkv_scan(wkv0.astype(f32), rh, heads(decay), kh, vh, kk, heads(a))
    mu = jnp.mean(y, axis=-1, keepdims=True)
    var = jnp.mean(jnp.square(y - mu), axis=-1, keepdims=True)
    yn = ((y - mu) * lax.rsqrt(var + GN_EPS)).reshape(B, T, R_WIDTH) * gn_w + gn_b
    bonus = (jnp.sum(rh * kh * r_k, axis=-1, keepdims=True) * vh).reshape(B, T, R_WIDTH)
    o_b = (yn + bonus) * g
    gate_a, gate_b = jnp.split(gates, 2, axis=-1)
    mixed = (jax.nn.sigmoid(gate_a) * (o_a.reshape(B, T, A_WIDTH).astype(x.dtype) @ w_branch_a)
             + jax.nn.sigmoid(gate_b) * (o_b.astype(x.dtype) @ w_branch_b))
    x = x + mixed @ w_out
    x = x + moe(rmsnorm(x, ln2), w_router, b_router, w_gate_up, b_gate_up, w_down, b_down)
    return x, k, v, wkv_T, rc[:, -1]


def setup_inputs(seed: int = 0) -> dict:
    key = jax.random.key(seed)
    ks = iter(jax.random.split(key, 40))
    nrm = lambda shape, scale: jax.random.normal(next(ks), shape, jnp.float32) * scale
    wbuf = min(MAX_WINDOW, PAST_LEN)
    L = DEPTH
    return {
        'x_prompt': nrm((BATCH, SEQ, D_MODEL), 1.0),
        'x_sample': nrm((DEC_BATCH, DEC_SEQ, D_MODEL), 1.0),
        'cache_k': nrm((L, DEC_BATCH, wbuf, A_HEADS, HEAD_DIM), 1.0),
        'cache_v': nrm((L, DEC_BATCH, wbuf, A_HEADS, HEAD_DIM), 1.0),
        'state_wkv': nrm((L, DEC_BATCH, R_HEADS, R_HEAD, R_HEAD), 0.3),
        'state_shift': nrm((L, DEC_BATCH, N_RWKV_COLS), 1.0),
        'rel_bias': nrm((NUM_BUCKETS, A_HEADS), 0.3),
        'ln1': 1.0 + nrm((L, D_MODEL), 0.01),
        'w_in': nrm((L, D_MODEL, N_IN_COLS), D_MODEL ** -0.5),
        'rwkv_mu': jax.random.uniform(next(ks), (L, N_RWKV_COLS), jnp.float32),
        'w0': jax.random.uniform(next(ks), (L, R_WIDTH), jnp.float32, -4.0, 1.0),
        'w_decay_up': nrm((L, DECAY_LORA, R_WIDTH), 0.5 * DECAY_LORA ** -0.5),
        'a0': nrm((L, R_WIDTH), 0.1),
        'w_a_up': nrm((L, AAA_LORA, R_WIDTH), AAA_LORA ** -0.5),
        'w_g_up': nrm((L, GATE_LORA, R_WIDTH), GATE_LORA ** -0.5),
        'k_k': 0.85 + nrm((L, R_WIDTH), 0.05),
        'k_a': 1.0 + nrm((L, R_WIDTH), 0.05),
        'r_k': nrm((L, R_HEADS, R_HEAD), 0.1),
        'gn_w': 1.0 + nrm((L, R_WIDTH), 0.01),
        'gn_b': nrm((L, R_WIDTH), 0.01),
        'w_branch_a': nrm((L, A_WIDTH, D_MODEL), A_WIDTH ** -0.5),
        'w_branch_b': nrm((L, R_WIDTH, D_MODEL), R_WIDTH ** -0.5),
        'w_out': nrm((L, D_MODEL, D_MODEL), D_MODEL ** -0.5),
        'ln2': 1.0 + nrm((L, D_MODEL), 0.01),
        'w_router': nrm((L, D_MODEL, N_EXPERTS), D_MODEL ** -0.5),
        'b_router': nrm((L, N_EXPERTS), 0.01),
        'w_gate_up': nrm((L, N_EXPERTS, D_MODEL, 2 * D_EXPERT), D_MODEL ** -0.5),
        'b_gate_up': nrm((L, N_EXPERTS, 2 * D_EXPERT), 0.01),
        'w_down': nrm((L, N_EXPERTS, D_EXPERT, D_MODEL), D_EXPERT ** -0.5),
        'b_down': nrm((L, N_EXPERTS, D_MODEL), 0.01),
        'ln_f': 1.0 + nrm((D_MODEL,), 0.01),
    }


def reference(x_prompt, x_sample, cache_k, cache_v, state_wkv, state_shift, rel_bias, ln1, w_in, rwkv_mu,
              w0, w_decay_up, a0, w_a_up, w_g_up, k_k, k_a, r_k, gn_w, gn_b, w_branch_a, w_branch_b, w_out,
              ln2, w_router, b_router, w_gate_up, b_gate_up, w_down, b_down, ln_f):
    xp, xs = x_prompt, x_sample
    Bp = xp.shape[0]
    wp = min(MAX_WINDOW, xp.shape[1])
    pk, pv, pwkv, psh, sk, sv, swkv, ssh = [], [], [], [], [], [], [], []
    for l in range(DEPTH):
        lw = (ln1[l], w_in[l], rwkv_mu[l], w0[l], w_decay_up[l], a0[l], w_a_up[l], w_g_up[l], k_k[l], k_a[l],
              r_k[l], gn_w[l], gn_b[l], w_branch_a[l], w_branch_b[l], w_out[l], ln2[l], w_router[l], b_router[l],
              w_gate_up[l], b_gate_up[l], w_down[l], b_down[l])
        xp, kp_, vp_, wkvp_, shp_ = hybrid_layer(
            xp, jnp.zeros((Bp, N_RWKV_COLS), xp.dtype), jnp.zeros((Bp, R_HEADS, R_HEAD, R_HEAD), jnp.float32),
            lambda q, k, v: dilated_attn_prompt(q, k, v, rel_bias), *lw)
        xs, ks_, vs_, wkvs_, shs_ = hybrid_layer(
            xs, state_shift[l], state_wkv[l],
            lambda q, k, v, l=l: dilated_attn_sample(q, k, v, cache_k[l], cache_v[l], rel_bias), *lw)
        pk.append(kp_[:, -wp:])
        pv.append(vp_[:, -wp:])
        pwkv.append(wkvp_)
        psh.append(shp_)
        sk.append(ks_)
        sv.append(vs_)
        swkv.append(wkvs_)
        ssh.append(shs_)
    y_prompt = rmsnorm(xp, ln_f)
    y_sample = rmsnorm(xs, ln_f)
    return (y_prompt, y_sample, jnp.stack(pk, 0), jnp.stack(pv, 0), jnp.stack(pwkv, 0), jnp.stack(psh, 0),
            jnp.stack(sk, 0), jnp.stack(sv, 0), jnp.stack(swkv, 0), jnp.stack(ssh, 0))
```

```python
import functools
import math

import jax
import jax.numpy as jnp
from jax import lax
from jax.experimental import pallas as pl
from jax.experimental.pallas import tpu as pltpu

F32 = jnp.float32
BF16 = jnp.bfloat16

D_MODEL = 1024
A_HEADS = 8
HEAD_DIM = 64
A_WIDTH = A_HEADS * HEAD_DIM
PATTERNS = ((128, 1), (512, 4), (2048, 16))
BLOCK = 128
NUM_BUCKETS = 32
MAX_DISTANCE = 2048
SCALE = HEAD_DIM ** -0.5
NEG = -1e30
R_HEADS = 8
R_HEAD = 64
R_WIDTH = R_HEADS * R_HEAD
DECAY_LORA = 64
AAA_LORA = 64
GATE_LORA = 128
GN_EPS = 64e-5
N_RWKV_COLS = 3 * R_WIDTH + DECAY_LORA + AAA_LORA + GATE_LORA
N_IN_COLS = 3 * A_WIDTH + N_RWKV_COLS + 2 * D_MODEL
N_EXPERTS = 32
TOP_K = 4
SWIGLU_LIMIT = 7.0
SWIGLU_ALPHA = 1.702
RMS_EPS = 1e-6

LANES = 128
VMEM_LIMIT = 56 * 1024 * 1024
TOKEN_TILE = 256
MOE_ROWS = 256
SCAN_BATCH = 2
SCAN_CHUNK = 256


def _cparams(*sem):
    return pltpu.CompilerParams(dimension_semantics=sem, vmem_limit_bytes=VMEM_LIMIT)


def _rms(x, g):
    return x * lax.rsqrt(jnp.mean(x * x, axis=-1, keepdims=True) + RMS_EPS) * g


def _dot(a, b):
    return jnp.dot(a.astype(BF16), b.astype(BF16), preferred_element_type=F32)


def _bf(x):
    return x.astype(BF16).astype(F32)


def _split3(x):
    hi = x.astype(BF16)
    r1 = x - hi.astype(F32)
    mid = r1.astype(BF16)
    lo = (r1 - mid.astype(F32)).astype(BF16)
    return hi, mid, lo


def _segdot(x, seg):
    hi, mid, lo = _split3(x)
    d = lambda p: jnp.dot(p, seg, preferred_element_type=F32)
    return d(hi) + d(mid) + d(lo)


def _inproj_kernel(x_ref, g_ref, w_ref, q_ref, k_ref, v_ref, rc_ref, sg_ref):
    h = _rms(x_ref[...], g_ref[...]).astype(BF16)

    def mm(c0, c1):
        return _dot(h, w_ref[:, c0:c1])

    c1 = 3 * A_WIDTH
    c2 = c1 + N_RWKV_COLS
    q_ref[...] = mm(0, A_WIDTH)
    k_ref[...] = mm(A_WIDTH, 2 * A_WIDTH)
    v_ref[...] = mm(2 * A_WIDTH, c1)
    rc_ref[...] = mm(c1, c2)
    sg_ref[:, :D_MODEL] = jax.nn.sigmoid(mm(c2, c2 + D_MODEL))
    sg_ref[:, D_MODEL:] = jax.nn.sigmoid(mm(c2 + D_MODEL, N_IN_COLS))


def _inproj(x, ln1, w, *, tm):
    n = x.shape[0]
    row = lambda c: pl.BlockSpec((tm, c), lambda i: (i, 0))
    full = lambda a: pl.BlockSpec(a.shape, lambda i: (0,) * a.ndim)
    out_cols = (A_WIDTH, A_WIDTH, A_WIDTH, N_RWKV_COLS, 2 * D_MODEL)
    return pl.pallas_call(
        _inproj_kernel,
        grid=(n // tm,),
        in_specs=[row(D_MODEL), full(ln1), full(w)],
        out_specs=[row(c) for c in out_cols],
        out_shape=[jax.ShapeDtypeStruct((n, c), F32) for c in out_cols],
        compiler_params=_cparams("parallel"),
        name="inproj",
    )(x, ln1, w)


def _t5_bucket(dist):
    max_exact = NUM_BUCKETS // 2
    d = jnp.maximum(dist, 1).astype(F32)
    large = max_exact + (jnp.log(d / max_exact) / math.log(MAX_DISTANCE / max_exact)
                         * (NUM_BUCKETS - max_exact)).astype(jnp.int32)
    large = jnp.minimum(large, NUM_BUCKETS - 1)
    return jnp.where(dist < max_exact, dist, large)


def _prompt_bias(rel_bias):
    i = jnp.arange(BLOCK)[:, None]
    j = jnp.arange(2 * BLOCK)[None, :]
    delta = i + BLOCK - j
    out = []
    for window, dil in PATTERNS:
        n = window // dil
        band = (delta >= 0) & (delta <= n)
        b = rel_bias.astype(F32)[_t5_bucket(jnp.clip(delta, 0, n) * dil)].transpose(2, 0, 1)
        out.append(jnp.where(band[None], b, NEG))
    return jnp.stack(out, 0)


def _sample_bias(rel_bias):
    rb = rel_bias.astype(F32)
    out = []
    for window, dil in PATTERNS:
        n = window // dil
        jj = n - jnp.arange(n)
        out.append(rb[_t5_bucket(jj * dil)])
    b = jnp.stack(out, 0)
    b = jnp.pad(b, ((0, 0), (0, 0), (0, LANES - A_HEADS)))
    b0 = jnp.pad(rb[_t5_bucket(jnp.zeros((1,), jnp.int32))], ((0, 0), (0, LANES - A_HEADS)))
    return b, b0


def _attn_prompt_kernel(q_ref, k_ref, v_ref, bias_ref, o_ref, os_ref, ls_ref, *, seq):
    nt = (((1,), (1,)), ((), ()))
    npat = len(PATTERNS)

    for g, (window, dil) in enumerate(PATTERNS):
        nb = seq // (BLOCK * dil)
        sh = dil.bit_length() - 1

        def body(it, carry, g=g, dil=dil, nb=nb, sh=sh):
            r = it & (dil - 1)
            c = it >> sh
            start = c * (BLOCK * dil) + r
            rows = pl.ds(start, BLOCK, stride=dil) if dil > 1 else pl.ds(start, BLOCK)
            qb = q_ref[0, rows, :]
            kb = k_ref[0, rows, :]
            vb = v_ref[0, rows, :]
            if nb > 1:
                pstart = jnp.maximum(c - 1, 0) * (BLOCK * dil) + r
                prow = pl.ds(pstart, BLOCK, stride=dil) if dil > 1 else pl.ds(pstart, BLOCK)
                kp = k_ref[0, prow, :]
                vp = v_ref[0, prow, :]
            outs, lses = [], []
            for hh in range(2):
                sl = slice(hh * HEAD_DIM, (hh + 1) * HEAD_DIM)
                qh = qb[:, sl].astype(BF16)
                b_own = bias_ref[g, hh, :, BLOCK:]
                s = lax.dot_general(qh, kb[:, sl].astype(BF16), nt, preferred_element_type=F32)
                s = jnp.where(b_own > 0.5 * NEG, s * SCALE + b_own, NEG)
                m = jnp.max(s, axis=-1, keepdims=True)
                if nb > 1:
                    b_prev = bias_ref[g, hh, :, :BLOCK]
                    sp = lax.dot_general(qh, kp[:, sl].astype(BF16), nt, preferred_element_type=F32)
                    sp = jnp.where((b_prev > 0.5 * NEG) & (c > 0), sp * SCALE + b_prev, NEG)
                    m = jnp.maximum(m, jnp.max(sp, axis=-1, keepdims=True))
                p = jnp.exp(s - m)
                l = jnp.sum(p, axis=-1, keepdims=True)
                if nb > 1:
                    pp = jnp.exp(sp - m)
                    l = l + jnp.sum(pp, axis=-1, keepdims=True)
                inv = 1.0 / l
                o = jnp.dot((p * inv).astype(BF16), vb[:, sl].astype(BF16), preferred_element_type=F32)
                if nb > 1:
                    o = o + jnp.dot((pp * inv).astype(BF16), vp[:, sl].astype(BF16), preferred_element_type=F32)
                outs.append(o)
                lses.append(jnp.broadcast_to(m + jnp.log(l), (BLOCK, HEAD_DIM)))
            os_ref[g, rows, :] = jnp.concatenate(outs, axis=1)
            ls_ref[g, rows, :] = jnp.concatenate(lses, axis=1)
            return carry

        lax.fori_loop(0, dil * nb, body, 0)

    chunk = 256

    def merge(ch, carry):
        rows = pl.ds(pl.multiple_of(ch * chunk, chunk), chunk)
        lse = [ls_ref[g, rows, :] for g in range(npat)]
        top = functools.reduce(jnp.maximum, lse)
        e = [jnp.exp(x - top) for x in lse]
        inv = 1.0 / functools.reduce(lambda a, b_: a + b_, e)
        acc = jnp.zeros((chunk, 2 * HEAD_DIM), F32)
        for g in range(npat):
            acc = acc + (e[g] * inv) * os_ref[g, rows, :]
        o_ref[0, rows, :] = acc
        return carry

    lax.fori_loop(0, seq // chunk, merge, 0)


def _attn_prompt(q, k, v, bias):
    b, s, _ = q.shape
    w2 = 2 * HEAD_DIM
    qspec = pl.BlockSpec((1, s, w2), lambda i, p: (i, 0, p))
    return pl.pallas_call(
        functools.partial(_attn_prompt_kernel, seq=s),
        grid=(b, A_HEADS // 2),
        in_specs=[qspec, qspec, qspec,
                  pl.BlockSpec((len(PATTERNS), 2, BLOCK, 2 * BLOCK), lambda i, p: (0, p, 0, 0))],
        out_specs=qspec,
        out_shape=jax.ShapeDtypeStruct((b, s, A_WIDTH), F32),
        scratch_shapes=[pltpu.VMEM((len(PATTERNS), s, w2), F32)] * 2,
        compiler_params=_cparams("parallel", "parallel"),
        name="attn_prompt",
    )(q, k, v, bias)


def _attn_sample_kernel(q_ref, kn_ref, vn_ref, ck_ref, cv_ref, bias_ref, b0_ref, seg_ref, segt_ref, o_ref, *, wb):
    q = _bf(q_ref[0])
    seg = seg_ref[...]
    segt = segt_ref[...]
    npat = len(PATTERNS)
    v_new = _bf(vn_ref[0])
    s0 = _segdot(q * _bf(kn_ref[0]), seg) * SCALE + b0_ref[...]
    nch = A_WIDTH // LANES

    def cache_rows(ref, dil, n):
        return _bf(jnp.concatenate(
            [ref[0, pl.ds((wb - n * dil) * nch + c, n, stride=dil * nch), :] for c in range(nch)], axis=1))

    outs, lses = [], []
    for g, (window, dil) in enumerate(PATTERNS):
        n = window // dil
        s = _segdot(cache_rows(ck_ref, dil, n) * q, seg) * SCALE + bias_ref[g]
        m = jnp.maximum(jnp.max(s, axis=0, keepdims=True), s0)
        l = jnp.sum(jnp.exp(s - m), axis=0, keepdims=True) + jnp.exp(s0 - m)
        lse = m + jnp.log(l)
        p = _bf(jnp.exp(s - lse))
        p0 = _bf(jnp.exp(s0 - lse))
        o = jnp.sum(_segdot(p, segt) * cache_rows(cv_ref, dil, n), axis=0, keepdims=True) + _segdot(p0, segt) * v_new
        outs.append(o)
        lses.append(lse)
    top = functools.reduce(jnp.maximum, lses)
    e = [jnp.exp(x - top) for x in lses]
    inv = 1.0 / functools.reduce(lambda a, b_: a + b_, e)
    acc = jnp.zeros((1, A_WIDTH), F32)
    for g in range(npat):
        acc = acc + _segdot(_bf(e[g] * inv), segt) * _bf(outs[g])
    o_ref[0] = acc


def _attn_sample(q, k_new, v_new, cache_k, cache_v, bias, b0):
    db, wb, _ = cache_k.shape
    head = jnp.arange(A_WIDTH) // HEAD_DIM
    seg = (head[:, None] == jnp.arange(LANES)[None, :]).astype(BF16)
    segt = seg.T
    vec = pl.BlockSpec((1, 1, A_WIDTH), lambda i: (i, 0, 0))
    nch = A_WIDTH // LANES
    cache = pl.BlockSpec((1, wb * nch, LANES), lambda i: (i, 0, 0))
    full = lambda a: pl.BlockSpec(a.shape, lambda i: (0,) * a.ndim)
    lanes = lambda a: a.reshape(db, wb * nch, LANES)
    args = (q[:, None], k_new[:, None], v_new[:, None], lanes(cache_k), lanes(cache_v), bias, b0, seg, segt)
    return pl.pallas_call(
        functools.partial(_attn_sample_kernel, wb=wb),
        grid=(db,),
        in_specs=[vec, vec, vec, cache, cache] + [full(a) for a in args[5:]],
        out_specs=vec,
        out_shape=jax.ShapeDtypeStruct((db, 1, A_WIDTH), F32),
        compiler_params=_cparams("parallel"),
        name="attn_sample",
    )(*args)[:, 0]


def _rwkv_prep_math(rc, prev, mu, w0, wd, a0, wa, wg, k_k, k_a, r_k, seg):
    xr = rc + (prev - rc) * mu
    o1, o2, o3 = R_WIDTH, 2 * R_WIDTH, 3 * R_WIDTH
    r = xr[:, :o1]
    kr = xr[:, o1:o2]
    vr = xr[:, o2:o3]
    xw = xr[:, o3:o3 + DECAY_LORA]
    xa = xr[:, o3 + DECAY_LORA:o3 + DECAY_LORA + AAA_LORA]
    xg = xr[:, o3 + DECAY_LORA + AAA_LORA:]
    z = -(w0 + _dot(jnp.tanh(xw), wd))
    softplus = jnp.maximum(z, 0.0) + jnp.log(1.0 + jnp.exp(-jnp.abs(z)))
    decay = jnp.exp(-jnp.exp(-softplus - 0.5))
    a = jax.nn.sigmoid(a0 + _dot(xa, wa))
    g = _dot(jax.nn.sigmoid(xg), wg)
    kk = kr * k_k
    k2 = kr * (1.0 + (a - 1.0) * k_a)
    kk = kk / jnp.maximum(jnp.sqrt(_segdot(kk * kk, seg)), 1e-12)
    bonus = _segdot(r * k2 * r_k, seg) * vr
    return _bf(r), decay, k2, vr, _bf(kk), kk * a, bonus, g


N_PREP_PARAMS = 10


def _rwkv_prep_prompt_kernel(rc_ref, tail_ref, *refs):
    p = [x[...] for x in refs[:N_PREP_PARAMS]]
    outs = refs[N_PREP_PARAMS:]
    rc = rc_ref[0]
    tt = rc.shape[0]
    first = pl.program_id(1) == 0
    prev_row = jnp.where(first, 0.0, tail_ref[0, 7:8, :])
    rolled = pltpu.roll(rc, 1, axis=0)
    row = lax.broadcasted_iota(jnp.int32, (tt, 1), 0)
    prev = jnp.where(row == 0, prev_row, rolled)
    for o, val in zip(outs, _rwkv_prep_math(rc, prev, *p)):
        o[0] = val


def _rwkv_prep_sample_kernel(rc_ref, prev_ref, *refs):
    p = [x[...] for x in refs[:N_PREP_PARAMS]]
    outs = refs[N_PREP_PARAMS:]
    for o, val in zip(outs, _rwkv_prep_math(rc_ref[...], prev_ref[...], *p)):
        o[...] = val


def _head_seg():
    head = jnp.arange(R_WIDTH) // R_HEAD
    return (head[:, None] == head[None, :]).astype(BF16)


def _rwkv_prep_prompt(rc, params, *, tt):
    b, t, _ = rc.shape
    full = lambda a: pl.BlockSpec(a.shape, lambda i, j: (0,) * a.ndim)
    tile = lambda c: pl.BlockSpec((1, tt, c), lambda i, j: (i, j, 0))
    tail = pl.BlockSpec((1, 8, N_RWKV_COLS), lambda i, j: (i, jnp.maximum(j * (tt // 8) - 1, 0), 0))
    return pl.pallas_call(
        _rwkv_prep_prompt_kernel,
        grid=(b, t // tt),
        in_specs=[tile(N_RWKV_COLS), tail] + [full(a) for a in params],
        out_specs=[tile(R_WIDTH)] * 8,
        out_shape=[jax.ShapeDtypeStruct((b, t, R_WIDTH), F32)] * 8,
        compiler_params=_cparams("parallel", "parallel"),
        name="rwkv_prep_prompt",
    )(rc, rc, *params)


def _rwkv_prep_sample(rc, prev, params):
    n = rc.shape[0]
    full = lambda a: pl.BlockSpec(a.shape, lambda i: (0,) * a.ndim)
    return pl.pallas_call(
        _rwkv_prep_sample_kernel,
        grid=(1,),
        in_specs=[full(rc), full(prev)] + [full(a) for a in params],
        out_specs=[pl.BlockSpec((n, R_WIDTH), lambda i: (0, 0))] * 8,
        out_shape=[jax.ShapeDtypeStruct((n, R_WIDTH), F32)] * 8,
        compiler_params=_cparams("arbitrary"),
        name="rwkv_prep_sample",
    )(rc, prev, *params)


def _seg_lane_sum(x, lo_mask):
    lo = jnp.sum(jnp.where(lo_mask, x, 0.0), axis=-1, keepdims=True)
    hi = jnp.sum(jnp.where(lo_mask, 0.0, x), axis=-1, keepdims=True)
    return jnp.where(lo_mask, lo, hi)


def _rwkv_scan_kernel(r_ref, w_ref, k_ref, v_ref, kk_ref, ka_ref, s0_ref, y_ref, sT_ref, st_ref, *, bb, tc):
    pairs = R_HEADS // 2
    w2 = 2 * R_HEAD

    @pl.when(pl.program_id(1) == 0)
    def _():
        for b in range(bb):
            for p in range(pairs):
                st_ref[b, p] = jnp.concatenate([s0_ref[b, 2 * p], s0_ref[b, 2 * p + 1]], axis=1)

    lane = lax.broadcasted_iota(jnp.int32, (R_HEAD, w2), 1)
    sub = lax.broadcasted_iota(jnp.int32, (R_HEAD, w2), 0)
    lo_mask = lane < R_HEAD
    eye2 = (lane & (R_HEAD - 1)) == sub

    grp = min(8, tc)

    def group(tg, carry):
        rows = pl.ds(pl.multiple_of(tg * grp, grp), grp)
        for b in range(bb):
            for p in range(pairs):
                cols = slice(p * w2, (p + 1) * w2)
                r8, w8, k8, v8, kk8, ka8 = (ref[b, rows, cols] for ref in (r_ref, w_ref, k_ref, v_ref, kk_ref, ka_ref))
                s = st_ref[b, p]
                sb = _bf(s)
                ys = []
                for j in range(grp):
                    row = lambda a: a[j:j + 1, :]
                    sa = -_seg_lane_sum(sb * row(kk8), lo_mask)
                    vcol = _seg_lane_sum(jnp.where(eye2, jnp.broadcast_to(row(v8), (R_HEAD, w2)), 0.0), lo_mask)
                    s = s * row(w8) + sa * row(ka8) + vcol * row(k8)
                    sb = _bf(s)
                    yfull = _seg_lane_sum(sb * row(r8), lo_mask)
                    ys.append(jnp.sum(jnp.where(eye2, yfull, 0.0), axis=0, keepdims=True))
                st_ref[b, p] = s
                y_ref[b, rows, cols] = jnp.concatenate(ys, axis=0) if grp > 1 else ys[0]
        return carry

    lax.fori_loop(0, tc // grp, group, 0)

    @pl.when(pl.program_id(1) == pl.num_programs(1) - 1)
    def _():
        for b in range(bb):
            for p in range(pairs):
                s = st_ref[b, p]
                sT_ref[b, 2 * p] = s[:, :R_HEAD]
                sT_ref[b, 2 * p + 1] = s[:, R_HEAD:]


def _rwkv_scan(r, w, k, v, kk, ka, s0, *, bb, tc):
    b, t, _ = r.shape
    seq = pl.BlockSpec((bb, tc, R_WIDTH), lambda i, j: (i, j, 0))
    state = pl.BlockSpec((bb, R_HEADS, R_HEAD, R_HEAD), lambda i, j: (i, 0, 0, 0))
    return pl.pallas_call(
        functools.partial(_rwkv_scan_kernel, bb=bb, tc=tc),
        grid=(b // bb, t // tc),
        in_specs=[seq] * 6 + [state],
        out_specs=[seq, state],
        out_shape=[jax.ShapeDtypeStruct((b, t, R_WIDTH), F32),
                   jax.ShapeDtypeStruct((b, R_HEADS, R_HEAD, R_HEAD), F32)],
        scratch_shapes=[pltpu.VMEM((bb, R_HEADS // 2, R_HEAD, 2 * R_HEAD), F32)],
        compiler_params=_cparams("parallel", "arbitrary"),
        name="rwkv_scan",
    )(r, w, k, v, kk, ka, s0)


def _post_kernel(x_ref, y_ref, bonus_ref, g_ref, oa_ref, sg_ref, gnw_ref, gnb_ref, seg_ref, wba_ref, wbb_ref,
                 wout_ref, ln2_ref, wr_ref, br_ref, *rest, aliased, n_main):
    outs = rest[aliased:]
    x1_ref, h2_ref, ti_ref, tg_ref = outs

    @pl.when(pl.program_id(0) >= n_main)
    def _():
        for o in outs:
            o[...] = jnp.zeros(o.shape, o.dtype)

    @pl.when(pl.program_id(0) < n_main)
    def _():
        _post_body(x_ref, y_ref, bonus_ref, g_ref, oa_ref, sg_ref, gnw_ref, gnb_ref, seg_ref, wba_ref, wbb_ref,
                   wout_ref, ln2_ref, wr_ref, br_ref, x1_ref, h2_ref, ti_ref, tg_ref)


def _post_body(x_ref, y_ref, bonus_ref, g_ref, oa_ref, sg_ref, gnw_ref, gnb_ref, seg_ref, wba_ref, wbb_ref,
               wout_ref, ln2_ref, wr_ref, br_ref, x1_ref, h2_ref, ti_ref, tg_ref):
    y = y_ref[...]
    seg = seg_ref[...]
    mu = _segdot(y, seg) * (1.0 / R_HEAD)
    yc = y - mu
    var = _segdot(yc * yc, seg) * (1.0 / R_HEAD)
    yn = yc * lax.rsqrt(var + GN_EPS) * gnw_ref[...] + gnb_ref[...]
    o_b = (yn + bonus_ref[...]) * g_ref[...]
    mixed = (sg_ref[:, :D_MODEL] * _dot(oa_ref[...], wba_ref[...])
             + sg_ref[:, D_MODEL:] * _dot(o_b, wbb_ref[...]))
    x1 = x_ref[...] + _dot(mixed, wout_ref[...])
    x1_ref[...] = x1
    h2 = _rms(x1, ln2_ref[...])
    h2_ref[...] = h2
    logits = _dot(h2, wr_ref[...]) + br_ref[...]
    lane = lax.broadcasted_iota(jnp.int32, logits.shape, 1).astype(F32)
    work = logits
    vals, idxs = [], []
    for _ in range(TOP_K):
        m = jnp.max(work, axis=-1, keepdims=True)
        idx = jnp.min(jnp.where(work == m, lane, float(LANES)), axis=-1, keepdims=True)
        vals.append(m)
        idxs.append(idx)
        work = jnp.where(lane == idx, -jnp.inf, work)
    es = [jnp.exp(v - vals[0]) for v in vals]
    tot = es[0] + es[1] + es[2] + es[3]
    ti = jnp.zeros(logits.shape, F32)
    tg = jnp.zeros(logits.shape, F32)
    for kslot in range(TOP_K):
        ti = jnp.where(lane == float(kslot), idxs[kslot], ti)
        tg = jnp.where(lane == float(kslot), es[kslot] / tot, tg)
    ti_ref[...] = ti.astype(jnp.int32)
    tg_ref[...] = tg


def _post(x, y, bonus, g, oa, sg, consts, *, tm, n_total, row0=0, into=None):
    n = x.shape[0]
    blk0 = row0 // tm
    n_main = n // tm
    steps = n_main + (1 if into is None and n_total > n else 0)
    row = lambda c: pl.BlockSpec((tm, c), lambda i: (jnp.minimum(i, n_main - 1), 0))
    orow = lambda c: pl.BlockSpec((tm, c), lambda i: (i + blk0, 0))
    full = lambda a: pl.BlockSpec(a.shape, lambda i: (0,) * a.ndim)
    out_cols = ((D_MODEL, F32), (D_MODEL, F32), (LANES, jnp.int32), (LANES, F32))
    ins = [x, y, bonus, g, oa, sg, *consts]
    in_specs = [row(D_MODEL), row(R_WIDTH), row(R_WIDTH), row(R_WIDTH), row(A_WIDTH), row(2 * D_MODEL)]
    in_specs += [full(a) for a in consts]
    aliases = {}
    if into is not None:
        aliases = {len(ins) + i: i for i in range(len(into))}
        in_specs += [pl.BlockSpec(memory_space=pl.ANY)] * len(into)
        ins += list(into)
    return pl.pallas_call(
        functools.partial(_post_kernel, aliased=len(aliases), n_main=n_main),
        grid=(steps,),
        in_specs=in_specs,
        out_specs=[orow(c) for c, _ in out_cols],
        out_shape=[jax.ShapeDtypeStruct((n_total, c), dt) for c, dt in out_cols],
        input_output_aliases=aliases,
        compiler_params=_cparams("parallel"),
        name="post",
    )(*ins)


def _moe_kernel(be_ref, tok_ref, h_hbm, wgu_ref, bgu_ref, wd_ref, bd_ref, out_ref, xbuf, wgu_bf, wd_bf, sem, *, rows):
    i = pl.program_id(0)

    def gather(r):
        tok = tok_ref[0, 0, r]
        return pltpu.make_async_copy(h_hbm.at[pl.ds(tok, 1)], xbuf.at[pl.ds(r, 1)], sem)

    def issue(r, c):
        gather(r).start()
        return c

    lax.fori_loop(0, rows, issue, 0)

    changed = jnp.logical_or(i == 0, be_ref[i] != be_ref[jnp.maximum(i - 1, 0)])

    @pl.when(changed)
    def _():
        step = 128

        def cast(j, c):
            rs = pl.ds(pl.multiple_of(j * step, step), step)
            wgu_bf[rs, :] = wgu_ref[0, rs, :].astype(BF16)
            wd_bf[rs, :] = wd_ref[0, rs, :].astype(BF16)
            return c

        lax.fori_loop(0, D_MODEL // step, cast, 0)

    def wait(r, c):
        gather(r).wait()
        return c

    lax.fori_loop(0, rows, wait, 0)

    xb = xbuf[...].astype(BF16)
    acc = jnp.broadcast_to(bd_ref[0], (rows, D_MODEL))
    cw = 512
    for c in range(D_MODEL // cw):
        gs = slice(c * cw, (c + 1) * cw)
        us = slice(D_MODEL + c * cw, D_MODEL + (c + 1) * cw)
        gt = jnp.dot(xb, wgu_bf[:, gs], preferred_element_type=F32) + bgu_ref[0, :, gs]
        up = jnp.dot(xb, wgu_bf[:, us], preferred_element_type=F32) + bgu_ref[0, :, us]
        gt = jnp.minimum(gt, SWIGLU_LIMIT)
        up = jnp.clip(up, -SWIGLU_LIMIT, SWIGLU_LIMIT)
        act = (up + 1.0) * (gt * jax.nn.sigmoid(gt * SWIGLU_ALPHA))
        acc = acc + jnp.dot(act.astype(BF16), wd_bf[gs, :], preferred_element_type=F32)
    out_ref[...] = acc


def _moe_experts(block_e, slot_tok, h2, w_gate_up, b_gate_up, w_down, b_down, *, rows):
    n_slots = slot_tok.shape[0]
    n_blocks = n_slots // rows
    grid_spec = pltpu.PrefetchScalarGridSpec(
        num_scalar_prefetch=1,
        grid=(n_blocks,),
        in_specs=[
            pl.BlockSpec((1, 1, rows), lambda i, be: (i, 0, 0), memory_space=pltpu.SMEM),
            pl.BlockSpec(memory_space=pl.ANY),
            pl.BlockSpec((1, D_MODEL, 2 * D_MODEL), lambda i, be: (be[i], 0, 0)),
            pl.BlockSpec((1, 1, 2 * D_MODEL), lambda i, be: (be[i], 0, 0)),
            pl.BlockSpec((1, D_MODEL, D_MODEL), lambda i, be: (be[i], 0, 0)),
            pl.BlockSpec((1, 1, D_MODEL), lambda i, be: (be[i], 0, 0)),
        ],
        out_specs=pl.BlockSpec((rows, D_MODEL), lambda i, be: (i, 0)),
        scratch_shapes=[pltpu.VMEM((rows, D_MODEL), F32),
                        pltpu.VMEM((D_MODEL, 2 * D_MODEL), BF16),
                        pltpu.VMEM((D_MODEL, D_MODEL), BF16),
                        pltpu.SemaphoreType.DMA(())],
    )
    return pl.pallas_call(
        functools.partial(_moe_kernel, rows=rows),
        grid_spec=grid_spec,
        out_shape=jax.ShapeDtypeStruct((n_slots, D_MODEL), F32),
        compiler_params=_cparams("arbitrary"),
        name="moe_experts",
    )(block_e, slot_tok.reshape(n_blocks, 1, rows), h2, w_gate_up, b_gate_up[:, None], w_down, b_down[:, None])


def _combine_kernel(dest_ref, ys_hbm, x1_ref, gate_ref, lnf_ref, y_ref, buf, sem, *, tm):
    def gather(idx):
        d = dest_ref[0, 0, idx]
        return pltpu.make_async_copy(ys_hbm.at[pl.ds(d, 1)], buf.at[idx & (TOP_K - 1), pl.ds(idx >> 2, 1)], sem)

    def issue(idx, c):
        gather(idx).start()
        return c

    def wait(idx, c):
        gather(idx).wait()
        return c

    lax.fori_loop(0, tm * TOP_K, issue, 0)
    lax.fori_loop(0, tm * TOP_K, wait, 0)
    acc = x1_ref[...]
    for kslot in range(TOP_K):
        acc = acc + gate_ref[:, kslot:kslot + 1] * buf[kslot]
    y_ref[...] = _rms(acc, lnf_ref[...])


def _combine(dest, ys, x1, gates, ln_f, *, tm, n, row0):
    blk0 = row0 // tm
    nt = n // tm
    dest_blocks = dest.reshape(-1, 1, tm * TOP_K)
    return pl.pallas_call(
        functools.partial(_combine_kernel, tm=tm),
        grid=(nt,),
        in_specs=[
            pl.BlockSpec((1, 1, tm * TOP_K), lambda i: (i + blk0, 0, 0), memory_space=pltpu.SMEM),
            pl.BlockSpec(memory_space=pl.ANY),
            pl.BlockSpec((tm, D_MODEL), lambda i: (i + blk0, 0)),
            pl.BlockSpec((tm, LANES), lambda i: (i + blk0, 0)),
            pl.BlockSpec(ln_f.shape, lambda i: (0, 0)),
        ],
        out_specs=pl.BlockSpec((tm, D_MODEL), lambda i: (i, 0)),
        out_shape=jax.ShapeDtypeStruct((n, D_MODEL), F32),
        scratch_shapes=[pltpu.VMEM((TOP_K, tm, D_MODEL), F32), pltpu.SemaphoreType.DMA(())],
        compiler_params=_cparams("arbitrary"),
        name="combine",
    )(dest_blocks, ys, x1, gates, ln_f)


def _route(topi, rows):
    e_flat = topi.reshape(-1)
    nk = e_flat.shape[0]
    order = jnp.argsort(e_flat)
    e_s = e_flat[order]
    tok_s = (order // TOP_K).astype(jnp.int32)
    sizes = jnp.bincount(e_flat, length=N_EXPERTS)
    starts = jnp.cumsum(sizes) - sizes
    padded = (sizes + rows - 1) // rows * rows
    pends = jnp.cumsum(padded)
    pstarts = pends - padded
    dest_s = (pstarts[e_s] + jnp.arange(nk) - starts[e_s]).astype(jnp.int32)
    n_blocks = -(-(nk + N_EXPERTS * (rows - 1)) // rows)
    slot_tok = jnp.zeros((n_blocks * rows,), jnp.int32).at[dest_s].set(tok_s)
    dest = jnp.zeros((nk,), jnp.int32).at[order].set(dest_s)
    block_e = jnp.minimum(jnp.searchsorted(pends, jnp.arange(n_blocks) * rows, side='right'),
                          N_EXPERTS - 1).astype(jnp.int32)
    return block_e, slot_tok, dest


def kernel(x_prompt, x_sample, cache_k, cache_v, state_wkv, state_shift, rel_bias, ln1, w_in, rwkv_mu, w0,
           w_decay_up, a0, w_a_up, w_g_up, k_k, k_a, r_k, gn_w, gn_b, w_branch_a, w_branch_b, w_out, ln2,
           w_router, b_router, w_gate_up, b_gate_up, w_down, b_down, ln_f):
    bp, seq, _ = x_prompt.shape
    db = x_sample.shape[0]
    n_p = bp * seq
    n_tot = n_p + db
    l = 0

    row = lambda a: a.reshape(1, -1)
    seg = _head_seg()
    prep_params = (row(rwkv_mu[l]), row(w0[l]), w_decay_up[l], row(a0[l]), w_a_up[l], w_g_up[l],
                   row(k_k[l]), row(k_a[l]), row(r_k[l]), seg)
    wr_pad = jnp.pad(w_router[l], ((0, 0), (0, LANES - N_EXPERTS))).astype(BF16)
    br_pad = jnp.pad(b_router[l], (0, LANES - N_EXPERTS), constant_values=NEG).reshape(1, LANES)
    post_small = (row(gn_w[l]), row(gn_b[l]), seg)
    post_tail = (row(ln2[l]), wr_pad, br_pad)

    w_in_bf = w_in[l].astype(BF16)
    consts = (post_small + (w_branch_a[l].astype(BF16), w_branch_b[l].astype(BF16), w_out[l].astype(BF16))
              + post_tail)

    xp = x_prompt.reshape(n_p, D_MODEL)
    q_p, k_p, v_p, rc_p, sg_p = _inproj(xp, row(ln1[l]), w_in_bf, tm=TOKEN_TILE)
    as3 = lambda a: a.reshape(bp, seq, -1)
    oa_p = _attn_prompt(as3(q_p), as3(k_p), as3(v_p), _prompt_bias(rel_bias))
    prep_p = _rwkv_prep_prompt(as3(rc_p), prep_params, tt=TOKEN_TILE)
    r_, w_, k2_, vr_, kk_, ka_, bonus_p, g_p = prep_p
    s0_p = jnp.zeros((bp, R_HEADS, R_HEAD, R_HEAD), F32)
    y_p, wkv_p = _rwkv_scan(r_, w_, k2_, vr_, kk_, ka_, s0_p, bb=SCAN_BATCH, tc=SCAN_CHUNK)
    flat = lambda a: a.reshape(n_p, -1)
    bufs = _post(xp, flat(y_p), flat(bonus_p), flat(g_p), flat(oa_p), sg_p, consts,
                 tm=TOKEN_TILE, n_total=n_tot)

    xs = x_sample.reshape(db, D_MODEL)
    q_s, k_s, v_s, rc_s, sg_s = _inproj(xs, row(ln1[l]), w_in_bf, tm=db)
    bias_s, b0_s = _sample_bias(rel_bias)
    wbuf = cache_k.shape[2]
    oa_s = _attn_sample(q_s, k_s, v_s, cache_k[l].reshape(db, wbuf, A_WIDTH), cache_v[l].reshape(db, wbuf, A_WIDTH),
                        bias_s, b0_s)
    prep_s = _rwkv_prep_sample(rc_s, state_shift[l], prep_params)
    sr, sw, sk2, svr, skk, ska, bonus_s, g_s = [a[:, None] for a in prep_s]
    y_s, wkv_s = _rwkv_scan(sr, sw, sk2, svr, skk, ska, state_wkv[l], bb=SCAN_BATCH, tc=1)
    x1_all, h2_all, ti_all, tg_all = _post(xs, y_s[:, 0], bonus_s[:, 0], g_s[:, 0], oa_s, sg_s, consts,
                                           tm=db, n_total=n_tot, row0=n_p, into=bufs)

    block_e, slot_tok, dest = _route(ti_all[:, :TOP_K], MOE_ROWS)
    ys = _moe_experts(block_e, slot_tok, h2_all, w_gate_up[l], b_gate_up[l], w_down[l], b_down[l], rows=MOE_ROWS)
    lnf = row(ln_f)
    pad = (-dest.shape[0]) % (TOKEN_TILE * TOP_K)
    y_prompt = _combine(jnp.pad(dest, (0, pad)), ys, x1_all, tg_all, lnf, tm=TOKEN_TILE, n=n_p, row0=0)
    y_sample = _combine(dest, ys, x1_all, tg_all, lnf, tm=db, n=db, row0=n_p)

    heads = lambda a, b_: a.reshape(1, b_, -1, A_HEADS, HEAD_DIM)
    return (y_prompt.reshape(bp, seq, D_MODEL), y_sample.reshape(db, 1, D_MODEL),
            heads(k_p, bp), heads(v_p, bp), wkv_p[None], as3(rc_p)[:, -1][None],
            heads(k_s, db), heads(v_s, db), wkv_s[None], rc_s[None])
```

```python
import functools
import math

import jax
import jax.numpy as jnp
from jax import lax
from jax.experimental import pallas as pl
from jax.experimental.pallas import tpu as pltpu

F32 = jnp.float32
BF16 = jnp.bfloat16

D_MODEL = 1024
A_HEADS = 8
HEAD_DIM = 64
A_WIDTH = A_HEADS * HEAD_DIM
PATTERNS = ((128, 1), (512, 4), (2048, 16))
BLOCK = 128
NUM_BUCKETS = 32
MAX_DISTANCE = 2048
SCALE = HEAD_DIM ** -0.5
NEG = -1e30
R_HEADS = 8
R_HEAD = 64
R_WIDTH = R_HEADS * R_HEAD
DECAY_LORA = 64
AAA_LORA = 64
GATE_LORA = 128
GN_EPS = 64e-5
N_RWKV_COLS = 3 * R_WIDTH + DECAY_LORA + AAA_LORA + GATE_LORA
N_IN_COLS = 3 * A_WIDTH + N_RWKV_COLS + 2 * D_MODEL
N_EXPERTS = 32
TOP_K = 4
SWIGLU_LIMIT = 7.0
SWIGLU_ALPHA = 1.702
RMS_EPS = 1e-6

LANES = 128
VMEM_LIMIT = 56 * 1024 * 1024
TOKEN_TILE = 256
MOE_ROWS = 256
SCAN_BATCH = 2
SCAN_CHUNK = 64


def _cparams(*sem):
    return pltpu.CompilerParams(dimension_semantics=sem, vmem_limit_bytes=VMEM_LIMIT)


def _rms(x, g):
    return x * lax.rsqrt(jnp.mean(x * x, axis=-1, keepdims=True) + RMS_EPS) * g


def _dot(a, b):
    return jnp.dot(a.astype(BF16), b.astype(BF16), preferred_element_type=F32)


def _bf(x):
    return x.astype(BF16).astype(F32)


def _split3(x):
    hi = x.astype(BF16)
    r1 = x - hi.astype(F32)
    mid = r1.astype(BF16)
    lo = (r1 - mid.astype(F32)).astype(BF16)
    return hi, mid, lo


def _segdot(x, seg):
    hi, mid, lo = _split3(x)
    d = lambda p: jnp.dot(p, seg, preferred_element_type=F32)
    return d(hi) + d(mid) + d(lo)


def _inproj_kernel(x_ref, g_ref, w_ref, q_ref, k_ref, v_ref, rc_ref, sg_ref):
    h = _rms(x_ref[...], g_ref[...]).astype(BF16)

    def mm(c0, c1):
        return _dot(h, w_ref[:, c0:c1])

    c1 = 3 * A_WIDTH
    c2 = c1 + N_RWKV_COLS
    q_ref[...] = mm(0, A_WIDTH)
    k_ref[...] = mm(A_WIDTH, 2 * A_WIDTH)
    v_ref[...] = mm(2 * A_WIDTH, c1)
    rc_ref[...] = mm(c1, c2)
    sg_ref[:, :D_MODEL] = jax.nn.sigmoid(mm(c2, c2 + D_MODEL))
    sg_ref[:, D_MODEL:] = jax.nn.sigmoid(mm(c2 + D_MODEL, N_IN_COLS))


def _inproj(x, ln1, w, *, tm):
    n = x.shape[0]
    row = lambda c: pl.BlockSpec((tm, c), lambda i: (i, 0))
    full = lambda a: pl.BlockSpec(a.shape, lambda i: (0,) * a.ndim)
    out_cols = (A_WIDTH, A_WIDTH, A_WIDTH, N_RWKV_COLS, 2 * D_MODEL)
    return pl.pallas_call(
        _inproj_kernel,
        grid=(n // tm,),
        in_specs=[row(D_MODEL), full(ln1), full(w)],
        out_specs=[row(c) for c in out_cols],
        out_shape=[jax.ShapeDtypeStruct((n, c), F32) for c in out_cols],
        compiler_params=_cparams("parallel"),
        name="inproj",
    )(x, ln1, w)


def _t5_bucket(dist):
    max_exact = NUM_BUCKETS // 2
    d = jnp.maximum(dist, 1).astype(F32)
    large = max_exact + (jnp.log(d / max_exact) / math.log(MAX_DISTANCE / max_exact)
                         * (NUM_BUCKETS - max_exact)).astype(jnp.int32)
    large = jnp.minimum(large, NUM_BUCKETS - 1)
    return jnp.where(dist < max_exact, dist, large)


def _prompt_bias(rel_bias):
    i = jnp.arange(BLOCK)[:, None]
    j = jnp.arange(2 * BLOCK)[None, :]
    delta = i + BLOCK - j
    out = []
    for window, dil in PATTERNS:
        n = window // dil
        band = (delta >= 0) & (delta <= n)
        b = rel_bias.astype(F32)[_t5_bucket(jnp.clip(delta, 0, n) * dil)].transpose(2, 0, 1)
        out.append(jnp.where(band[None], b, NEG))
    return jnp.stack(out, 0)


def _sample_bias(rel_bias):
    rb = rel_bias.astype(F32)
    out = []
    for window, dil in PATTERNS:
        n = window // dil
        jj = n - jnp.arange(n)
        out.append(rb[_t5_bucket(jj * dil)])
    b = jnp.stack(out, 0)[..., None]
    b0 = rb[_t5_bucket(jnp.zeros((1,), jnp.int32))].reshape(A_HEADS, 1)
    return b, b0


def _attn_prompt_kernel(q_ref, k_ref, v_ref, bias_ref, o_ref, os_ref, ls_ref, *, seq):
    nt = (((1,), (1,)), ((), ()))
    npat = len(PATTERNS)
    w2 = 2 * HEAD_DIM
    lo = lax.broadcasted_iota(jnp.int32, (BLOCK, w2), 1) < HEAD_DIM
    key = lax.broadcasted_iota(jnp.int32, (2 * BLOCK, 2 * BLOCK), 1)

    for g, (window, dil) in enumerate(PATTERNS):
        nb = seq // (BLOCK * dil)
        sh = dil.bit_length() - 1

        def body(it, carry, g=g, dil=dil, nb=nb, sh=sh):
            r = it & (dil - 1)
            c = it >> sh
            start = c * (BLOCK * dil) + r
            rows = pl.ds(start, BLOCK, stride=dil) if dil > 1 else pl.ds(start, BLOCK)
            qb = q_ref[0, rows, :]
            kb = k_ref[0, rows, :]
            vb = v_ref[0, rows, :]
            if nb > 1:
                pstart = jnp.maximum(c - 1, 0) * (BLOCK * dil) + r
                prow = pl.ds(pstart, BLOCK, stride=dil) if dil > 1 else pl.ds(pstart, BLOCK)
                kp = k_ref[0, prow, :]
                vp = v_ref[0, prow, :]
            q2 = jnp.concatenate([jnp.where(lo, qb, 0.0), jnp.where(lo, 0.0, qb)], axis=0).astype(BF16)
            if nb > 1:
                k2 = jnp.concatenate([kp, kb], axis=0)
                v2 = jnp.concatenate([vp, vb], axis=0)
                bias = bias_ref[g, 0]
                valid = (bias > 0.5 * NEG) & ((key >= BLOCK) | (c > 0))
            else:
                k2, v2 = kb, vb
                bias = bias_ref[g, 0, :, BLOCK:]
                valid = bias > 0.5 * NEG
            s = lax.dot_general(q2, k2.astype(BF16), nt, preferred_element_type=F32)
            s = jnp.where(valid, s * SCALE + bias, NEG)
            m = jnp.max(s, axis=-1, keepdims=True)
            p = jnp.exp(s - m)
            l = jnp.sum(p, axis=-1, keepdims=True)
            o2 = jnp.dot((p * (1.0 / l)).astype(BF16), v2.astype(BF16), preferred_element_type=F32)
            lse = jnp.broadcast_to(m + jnp.log(l), (2 * BLOCK, w2))
            os_ref[g, rows, :] = jnp.where(lo, o2[:BLOCK], o2[BLOCK:])
            ls_ref[g, rows, :] = jnp.where(lo, lse[:BLOCK], lse[BLOCK:])
            return carry

        lax.fori_loop(0, dil * nb, body, 0, unroll=8)

    chunk = 256

    def merge(ch, carry):
        rows = pl.ds(pl.multiple_of(ch * chunk, chunk), chunk)
        lse = [ls_ref[g, rows, :] for g in range(npat)]
        top = functools.reduce(jnp.maximum, lse)
        e = [jnp.exp(x - top) for x in lse]
        inv = 1.0 / functools.reduce(lambda a, b_: a + b_, e)
        acc = jnp.zeros((chunk, 2 * HEAD_DIM), F32)
        for g in range(npat):
            acc = acc + (e[g] * inv) * os_ref[g, rows, :]
        o_ref[0, rows, :] = acc
        return carry

    lax.fori_loop(0, seq // chunk, merge, 0)


def _attn_prompt(q, k, v, bias):
    b, s, _ = q.shape
    w2 = 2 * HEAD_DIM
    qspec = pl.BlockSpec((1, s, w2), lambda i, p: (i, 0, p))
    return pl.pallas_call(
        functools.partial(_attn_prompt_kernel, seq=s),
        grid=(b, A_HEADS // 2),
        in_specs=[qspec, qspec, qspec,
                  pl.BlockSpec((len(PATTERNS), 1, 2 * BLOCK, 2 * BLOCK), lambda i, p: (0, p, 0, 0))],
        out_specs=qspec,
        out_shape=jax.ShapeDtypeStruct((b, s, A_WIDTH), F32),
        scratch_shapes=[pltpu.VMEM((len(PATTERNS), s, w2), F32)] * 2,
        compiler_params=_cparams("parallel", "parallel"),
        name="attn_prompt",
    )(q, k, v, bias.reshape(len(PATTERNS), A_HEADS // 2, 2 * BLOCK, 2 * BLOCK))


def _attn_sample_kernel(q_ref, kn_ref, vn_ref, *refs):
    npat = len(PATTERNS)
    k_refs, v_refs = refs[:npat], refs[npat:2 * npat]
    bias_ref, b0_ref, o_ref = refs[2 * npat:]
    q = _bf(q_ref[0])
    v_new = _bf(vn_ref[0])
    s0 = jnp.sum(q * _bf(kn_ref[0]), axis=-1, keepdims=True) * SCALE + b0_ref[...]
    outs, lses = [], []
    for g in range(npat):
        kg = _bf(k_refs[g][0, :, 0])
        s = jnp.sum(kg * q[None], axis=-1, keepdims=True) * SCALE + bias_ref[g]
        m = jnp.maximum(jnp.max(s, axis=0), s0)
        l = jnp.sum(jnp.exp(s - m[None]), axis=0) + jnp.exp(s0 - m)
        lse = m + jnp.log(l)
        p = _bf(jnp.exp(s - lse[None]))
        p0 = _bf(jnp.exp(s0 - lse))
        outs.append(jnp.sum(p * _bf(v_refs[g][0, :, 0]), axis=0) + p0 * v_new)
        lses.append(lse)
    top = functools.reduce(jnp.maximum, lses)
    e = [jnp.exp(x - top) for x in lses]
    inv = 1.0 / functools.reduce(lambda a, b_: a + b_, e)
    acc = jnp.zeros((A_HEADS, HEAD_DIM), F32)
    for g in range(npat):
        acc = acc + _bf(e[g] * inv) * _bf(outs[g])
    o_ref[0] = acc


def _attn_sample(q, k_new, v_new, cache_k, cache_v, bias, b0):
    db, wb = cache_k.shape[:2]
    heads = lambda a: a.reshape(db, A_HEADS, HEAD_DIM)
    vec = pl.BlockSpec((1, A_HEADS, HEAD_DIM), lambda i: (i, 0, 0))
    full = lambda a: pl.BlockSpec(a.shape, lambda i: (0,) * a.ndim)
    views, specs = [], []
    for window, dil in PATTERNS:
        start = wb - window
        assert window // dil == BLOCK and start >= 0 and start % (BLOCK * dil) == 0 and wb % dil == 0
        views.append(lambda a, dil=dil: a.reshape(db, wb // dil, dil, A_HEADS, HEAD_DIM))
        specs.append(pl.BlockSpec((1, BLOCK, 1, A_HEADS, HEAD_DIM),
                                  lambda i, blk=start // (BLOCK * dil): (i, blk, 0, 0, 0)))
    args = ([heads(q), heads(k_new), heads(v_new)] + [vw(cache_k) for vw in views] + [vw(cache_v) for vw in views]
            + [bias, b0])
    return pl.pallas_call(
        _attn_sample_kernel,
        grid=(db,),
        in_specs=[vec, vec, vec] + specs + specs + [full(bias), full(b0)],
        out_specs=vec,
        out_shape=jax.ShapeDtypeStruct((db, A_HEADS, HEAD_DIM), F32),
        compiler_params=_cparams("parallel"),
        name="attn_sample",
    )(*args).reshape(db, A_WIDTH)


def _rwkv_prep_math(rc, prev, mu, w0, wd, a0, wa, wg, k_k, k_a, r_k, seg):
    xr = rc + (prev - rc) * mu
    o1, o2, o3 = R_WIDTH, 2 * R_WIDTH, 3 * R_WIDTH
    r = xr[:, :o1]
    kr = xr[:, o1:o2]
    vr = xr[:, o2:o3]
    xw = xr[:, o3:o3 + DECAY_LORA]
    xa = xr[:, o3 + DECAY_LORA:o3 + DECAY_LORA + AAA_LORA]
    xg = xr[:, o3 + DECAY_LORA + AAA_LORA:]
    z = -(w0 + _dot(jnp.tanh(xw), wd))
    softplus = jnp.maximum(z, 0.0) + jnp.log(1.0 + jnp.exp(-jnp.abs(z)))
    decay = jnp.exp(-jnp.exp(-softplus - 0.5))
    a = jax.nn.sigmoid(a0 + _dot(xa, wa))
    g = _dot(jax.nn.sigmoid(xg), wg)
    kk = kr * k_k
    k2 = kr * (1.0 + (a - 1.0) * k_a)
    kk = kk / jnp.maximum(jnp.sqrt(_segdot(kk * kk, seg)), 1e-12)
    bonus = _segdot(r * k2 * r_k, seg) * vr
    return _bf(r), decay, k2, vr, _bf(kk), kk * a, bonus, g


N_PREP_PARAMS = 10


def _rwkv_prep_prompt_kernel(rc_ref, tail_ref, *refs):
    p = [x[...] for x in refs[:N_PREP_PARAMS]]
    outs = refs[N_PREP_PARAMS:]
    rc = rc_ref[0]
    tt = rc.shape[0]
    first = pl.program_id(1) == 0
    prev_row = jnp.where(first, 0.0, tail_ref[0, 7:8, :])
    rolled = pltpu.roll(rc, 1, axis=0)
    row = lax.broadcasted_iota(jnp.int32, (tt, 1), 0)
    prev = jnp.where(row == 0, prev_row, rolled)
    for o, val in zip(outs, _rwkv_prep_math(rc, prev, *p)):
        o[0] = val


def _rwkv_prep_sample_kernel(rc_ref, prev_ref, *refs):
    p = [x[...] for x in refs[:N_PREP_PARAMS]]
    outs = refs[N_PREP_PARAMS:]
    for o, val in zip(outs, _rwkv_prep_math(rc_ref[...], prev_ref[...], *p)):
        o[...] = val


def _head_seg():
    head = jnp.arange(R_WIDTH) // R_HEAD
    return (head[:, None] == head[None, :]).astype(BF16)


def _rwkv_prep_prompt(rc, params, *, tt):
    b, t, _ = rc.shape
    full = lambda a: pl.BlockSpec(a.shape, lambda i, j: (0,) * a.ndim)
    tile = lambda c: pl.BlockSpec((1, tt, c), lambda i, j: (i, j, 0))
    tail = pl.BlockSpec((1, 8, N_RWKV_COLS), lambda i, j: (i, jnp.maximum(j * (tt // 8) - 1, 0), 0))
    return pl.pallas_call(
        _rwkv_prep_prompt_kernel,
        grid=(b, t // tt),
        in_specs=[tile(N_RWKV_COLS), tail] + [full(a) for a in params],
        out_specs=[tile(R_WIDTH)] * 8,
        out_shape=[jax.ShapeDtypeStruct((b, t, R_WIDTH), F32)] * 8,
        compiler_params=_cparams("parallel", "parallel"),
        name="rwkv_prep_prompt",
    )(rc, rc, *params)


def _rwkv_prep_sample(rc, prev, params):
    n = rc.shape[0]
    full = lambda a: pl.BlockSpec(a.shape, lambda i: (0,) * a.ndim)
    return pl.pallas_call(
        _rwkv_prep_sample_kernel,
        grid=(1,),
        in_specs=[full(rc), full(prev)] + [full(a) for a in params],
        out_specs=[pl.BlockSpec((n, R_WIDTH), lambda i: (0, 0))] * 8,
        out_shape=[jax.ShapeDtypeStruct((n, R_WIDTH), F32)] * 8,
        compiler_params=_cparams("arbitrary"),
        name="rwkv_prep_sample",
    )(rc, prev, *params)


def _seg_lane_sum(x, lo_mask):
    lo = jnp.sum(jnp.where(lo_mask, x, 0.0), axis=-1, keepdims=True)
    hi = jnp.sum(jnp.where(lo_mask, 0.0, x), axis=-1, keepdims=True)
    return jnp.where(lo_mask, lo, hi)


def _rwkv_scan_kernel(r_ref, w_ref, k_ref, v_ref, kk_ref, ka_ref, s0_ref, y_ref, sT_ref, st_ref, *, bb, tc):
    pairs = R_HEADS // 2
    w2 = 2 * R_HEAD

    @pl.when(pl.program_id(1) == 0)
    def _():
        for b in range(bb):
            for p in range(pairs):
                st_ref[b, p] = jnp.concatenate([s0_ref[b, 2 * p], s0_ref[b, 2 * p + 1]], axis=1)

    lane = lax.broadcasted_iota(jnp.int32, (R_HEAD, w2), 1)
    sub = lax.broadcasted_iota(jnp.int32, (R_HEAD, w2), 0)
    lo_mask = lane < R_HEAD
    eye2 = (lane & (R_HEAD - 1)) == sub

    grp = min(8, tc)

    def group(tg, carry):
        rows = pl.ds(pl.multiple_of(tg * grp, grp), grp)
        for b in range(bb):
            for p in range(pairs):
                cols = slice(p * w2, (p + 1) * w2)
                r8, w8, k8, v8, kk8, ka8 = (ref[b, rows, cols] for ref in (r_ref, w_ref, k_ref, v_ref, kk_ref, ka_ref))
                s = st_ref[b, p]
                sb = _bf(s)
                ys = []
                for j in range(grp):
                    row = lambda a: a[j:j + 1, :]
                    sa = -_seg_lane_sum(sb * row(kk8), lo_mask)
                    vcol = _seg_lane_sum(jnp.where(eye2, jnp.broadcast_to(row(v8), (R_HEAD, w2)), 0.0), lo_mask)
                    s = s * row(w8) + sa * row(ka8) + vcol * row(k8)
                    sb = _bf(s)
                    yfull = _seg_lane_sum(sb * row(r8), lo_mask)
                    ys.append(jnp.sum(jnp.where(eye2, yfull, 0.0), axis=0, keepdims=True))
                st_ref[b, p] = s
                y_ref[b, rows, cols] = jnp.concatenate(ys, axis=0) if grp > 1 else ys[0]
        return carry

    lax.fori_loop(0, tc // grp, group, 0)

    @pl.when(pl.program_id(1) == pl.num_programs(1) - 1)
    def _():
        for b in range(bb):
            for p in range(pairs):
                s = st_ref[b, p]
                sT_ref[b, 2 * p] = s[:, :R_HEAD]
                sT_ref[b, 2 * p + 1] = s[:, R_HEAD:]


def _rwkv_scan(r, w, k, v, kk, ka, s0, *, bb, tc):
    b, t, _ = r.shape
    seq = pl.BlockSpec((bb, tc, R_WIDTH), lambda i, j: (i, j, 0))
    state = pl.BlockSpec((bb, R_HEADS, R_HEAD, R_HEAD), lambda i, j: (i, 0, 0, 0))
    return pl.pallas_call(
        functools.partial(_rwkv_scan_kernel, bb=bb, tc=tc),
        grid=(b // bb, t // tc),
        in_specs=[seq] * 6 + [state],
        out_specs=[seq, state],
        out_shape=[jax.ShapeDtypeStruct((b, t, R_WIDTH), F32),
                   jax.ShapeDtypeStruct((b, R_HEADS, R_HEAD, R_HEAD), F32)],
        scratch_shapes=[pltpu.VMEM((bb, R_HEADS // 2, R_HEAD, 2 * R_HEAD), F32)],
        compiler_params=_cparams("parallel", "arbitrary"),
        name="rwkv_scan",
    )(r, w, k, v, kk, ka, s0)


CHAINS = LANES // 2
K2 = R_HEAD // 2


def _chain_rows(x):
    b, t, _ = x.shape
    x = x.reshape(b, t, R_HEADS, K2, 2).transpose(1, 3, 4, 0, 2)
    return x.reshape(t, K2, LANES)


def _chain_vals(x):
    b, t, _ = x.shape
    x = x.reshape(b, t, R_HEADS, R_HEAD).transpose(1, 3, 0, 2).reshape(t, R_HEAD, CHAINS)
    return jnp.concatenate([x, x], axis=-1)


def _lane_scan_kernel(kk_ref, w_ref, ka_ref, kx_ref, r_ref, vv_ref, s0_ref, y_ref, sT_ref, s_ref, *, tc):
    @pl.when(pl.program_id(0) == 0)
    def _():
        for k2 in range(K2):
            s_ref[k2] = s0_ref[k2]

    def both_halves(x):
        return x + pltpu.roll(x, CHAINS, axis=1)

    nhalf = 2
    vh = R_HEAD // nhalf

    def row(ref, t, k2):
        return jnp.broadcast_to(ref[t, k2:k2 + 1, :], (vh, LANES))

    acc0 = []
    for hf in range(nhalf):
        vs = slice(hf * vh, (hf + 1) * vh)
        a = jnp.zeros((vh, LANES), F32)
        for k2 in range(K2):
            a = a + _bf(s_ref[k2, vs, :]) * row(kk_ref, 0, k2)
        acc0.append(a)

    def step(t, acc):
        tn = jnp.minimum(t + 1, tc - 1)
        nxt = []
        for hf in range(nhalf):
            vs = slice(hf * vh, (hf + 1) * vh)
            vv = vv_ref[t, vs, :]
            sa = -both_halves(acc[hf])
            yacc = jnp.zeros((vh, LANES), F32)
            nacc = jnp.zeros((vh, LANES), F32)
            for k2 in range(K2):
                s = s_ref[k2, vs, :] * row(w_ref, t, k2) + sa * row(ka_ref, t, k2) + vv * row(kx_ref, t, k2)
                s_ref[k2, vs, :] = s
                sb = _bf(s)
                yacc = yacc + sb * row(r_ref, t, k2)
                nacc = nacc + sb * row(kk_ref, tn, k2)
            y_ref[t, vs, :] = both_halves(yacc)
            nxt.append(nacc)
        return tuple(nxt)

    lax.fori_loop(0, tc, step, tuple(acc0))

    @pl.when(pl.program_id(0) == pl.num_programs(0) - 1)
    def _():
        for k2 in range(K2):
            sT_ref[k2] = s_ref[k2]


def _lane_scan(r, w, k, v, kk, ka, s0, *, tc):
    b, t, _ = r.shape
    assert b * R_HEADS == CHAINS
    ops = [_chain_rows(x) for x in (kk, w, ka, k, r)] + [_chain_vals(v)]
    s0c = s0.reshape(b, R_HEADS, R_HEAD, K2, 2).transpose(3, 2, 4, 0, 1).reshape(K2, R_HEAD, LANES)
    rows = pl.BlockSpec((tc, K2, LANES), lambda i: (i, 0, 0))
    vals = pl.BlockSpec((tc, R_HEAD, LANES), lambda i: (i, 0, 0))
    state = pl.BlockSpec((K2, R_HEAD, LANES), lambda i: (0, 0, 0))
    y, st = pl.pallas_call(
        functools.partial(_lane_scan_kernel, tc=tc),
        grid=(t // tc,),
        in_specs=[rows] * 5 + [vals, state],
        out_specs=[vals, state],
        out_shape=[jax.ShapeDtypeStruct((t, R_HEAD, LANES), F32), jax.ShapeDtypeStruct((K2, R_HEAD, LANES), F32)],
        scratch_shapes=[pltpu.VMEM((K2, R_HEAD, LANES), F32)],
        compiler_params=_cparams("arbitrary"),
        name="rwkv_lane_scan",
    )(*ops, s0c)
    y = y[:, :, :CHAINS].reshape(t, R_HEAD, b, R_HEADS).transpose(2, 0, 3, 1).reshape(b, t, R_WIDTH)
    st = st.reshape(K2, R_HEAD, 2, b, R_HEADS).transpose(3, 4, 1, 0, 2).reshape(b, R_HEADS, R_HEAD, R_HEAD)
    return y, st


def _post_kernel(x_ref, y_ref, bonus_ref, g_ref, oa_ref, sg_ref, gnw_ref, gnb_ref, seg_ref, wba_ref, wbb_ref,
                 wout_ref, ln2_ref, wr_ref, br_ref, *rest, aliased, n_main):
    outs = rest[aliased:]
    x1_ref, h2_ref, ti_ref, tg_ref = outs

    @pl.when(pl.program_id(0) >= n_main)
    def _():
        for o in outs:
            o[...] = jnp.zeros(o.shape, o.dtype)

    @pl.when(pl.program_id(0) < n_main)
    def _():
        _post_body(x_ref, y_ref, bonus_ref, g_ref, oa_ref, sg_ref, gnw_ref, gnb_ref, seg_ref, wba_ref, wbb_ref,
                   wout_ref, ln2_ref, wr_ref, br_ref, x1_ref, h2_ref, ti_ref, tg_ref)


def _post_body(x_ref, y_ref, bonus_ref, g_ref, oa_ref, sg_ref, gnw_ref, gnb_ref, seg_ref, wba_ref, wbb_ref,
               wout_ref, ln2_ref, wr_ref, br_ref, x1_ref, h2_ref, ti_ref, tg_ref):
    y = y_ref[...]
    seg = seg_ref[...]
    mu = _segdot(y, seg) * (1.0 / R_HEAD)
    yc = y - mu
    var = _segdot(yc * yc, seg) * (1.0 / R_HEAD)
    yn = yc * lax.rsqrt(var + GN_EPS) * gnw_ref[...] + gnb_ref[...]
    o_b = (yn + bonus_ref[...]) * g_ref[...]
    mixed = (sg_ref[:, :D_MODEL] * _dot(oa_ref[...], wba_ref[...])
             + sg_ref[:, D_MODEL:] * _dot(o_b, wbb_ref[...]))
    x1 = x_ref[...] + _dot(mixed, wout_ref[...])
    x1_ref[...] = x1
    h2 = _rms(x1, ln2_ref[...])
    h2_ref[...] = h2
    logits = _dot(h2, wr_ref[...]) + br_ref[...]
    lane = lax.broadcasted_iota(jnp.int32, logits.shape, 1).astype(F32)
    work = logits
    vals, idxs = [], []
    for _ in range(TOP_K):
        m = jnp.max(work, axis=-1, keepdims=True)
        idx = jnp.min(jnp.where(work == m, lane, float(LANES)), axis=-1, keepdims=True)
        vals.append(m)
        idxs.append(idx)
        work = jnp.where(lane == idx, -jnp.inf, work)
    es = [jnp.exp(v - vals[0]) for v in vals]
    tot = es[0] + es[1] + es[2] + es[3]
    ti = jnp.zeros(logits.shape, F32)
    tg = jnp.zeros(logits.shape, F32)
    for kslot in range(TOP_K):
        ti = jnp.where(lane == float(kslot), idxs[kslot], ti)
        tg = jnp.where(lane == float(kslot), es[kslot] / tot, tg)
    ti_ref[...] = ti.astype(jnp.int32)
    tg_ref[...] = tg


def _post(x, y, bonus, g, oa, sg, consts, *, tm, n_total, row0=0, into=None):
    n = x.shape[0]
    blk0 = row0 // tm
    n_main = n // tm
    steps = n_main + (1 if into is None and n_total > n else 0)
    row = lambda c: pl.BlockSpec((tm, c), lambda i: (jnp.minimum(i, n_main - 1), 0))
    orow = lambda c: pl.BlockSpec((tm, c), lambda i: (i + blk0, 0))
    full = lambda a: pl.BlockSpec(a.shape, lambda i: (0,) * a.ndim)
    out_cols = ((D_MODEL, F32), (D_MODEL, F32), (LANES, jnp.int32), (LANES, F32))
    ins = [x, y, bonus, g, oa, sg, *consts]
    in_specs = [row(D_MODEL), row(R_WIDTH), row(R_WIDTH), row(R_WIDTH), row(A_WIDTH), row(2 * D_MODEL)]
    in_specs += [full(a) for a in consts]
    aliases = {}
    if into is not None:
        aliases = {len(ins) + i: i for i in range(len(into))}
        in_specs += [pl.BlockSpec(memory_space=pl.ANY)] * len(into)
        ins += list(into)
    return pl.pallas_call(
        functools.partial(_post_kernel, aliased=len(aliases), n_main=n_main),
        grid=(steps,),
        in_specs=in_specs,
        out_specs=[orow(c) for c, _ in out_cols],
        out_shape=[jax.ShapeDtypeStruct((n_total, c), dt) for c, dt in out_cols],
        input_output_aliases=aliases,
        compiler_params=_cparams("parallel"),
        name="post",
    )(*ins)


def _moe_kernel(be_ref, tok_ref, h_hbm, wgu_ref, bgu_ref, wd_ref, bd_ref, out_ref, xbuf, wgu_bf, wd_bf, sem, *, rows):
    i = pl.program_id(0)

    def gather(r):
        tok = tok_ref[0, 0, r]
        return pltpu.make_async_copy(h_hbm.at[pl.ds(tok, 1)], xbuf.at[pl.ds(r, 1)], sem)

    def issue(r, c):
        gather(r).start()
        return c

    lax.fori_loop(0, rows, issue, 0)

    changed = jnp.logical_or(i == 0, be_ref[i] != be_ref[jnp.maximum(i - 1, 0)])

    @pl.when(changed)
    def _():
        step = 128

        def cast(j, c):
            rs = pl.ds(pl.multiple_of(j * step, step), step)
            wgu_bf[rs, :] = wgu_ref[0, rs, :].astype(BF16)
            wd_bf[rs, :] = wd_ref[0, rs, :].astype(BF16)
            return c

        lax.fori_loop(0, D_MODEL // step, cast, 0)

    def wait(r, c):
        gather(r).wait()
        return c

    lax.fori_loop(0, rows, wait, 0)

    xb = xbuf[...].astype(BF16)
    acc = jnp.broadcast_to(bd_ref[0], (rows, D_MODEL))
    cw = 512
    for c in range(D_MODEL // cw):
        gs = slice(c * cw, (c + 1) * cw)
        us = slice(D_MODEL + c * cw, D_MODEL + (c + 1) * cw)
        gt = jnp.dot(xb, wgu_bf[:, gs], preferred_element_type=F32) + bgu_ref[0, :, gs]
        up = jnp.dot(xb, wgu_bf[:, us], preferred_element_type=F32) + bgu_ref[0, :, us]
        gt = jnp.minimum(gt, SWIGLU_LIMIT)
        up = jnp.clip(up, -SWIGLU_LIMIT, SWIGLU_LIMIT)
        act = (up + 1.0) * (gt * jax.nn.sigmoid(gt * SWIGLU_ALPHA))
        acc = acc + jnp.dot(act.astype(BF16), wd_bf[gs, :], preferred_element_type=F32)
    out_ref[...] = acc


def _moe_experts(block_e, slot_tok, h2, w_gate_up, b_gate_up, w_down, b_down, *, rows):
    n_slots = slot_tok.shape[0]
    n_blocks = n_slots // rows
    grid_spec = pltpu.PrefetchScalarGridSpec(
        num_scalar_prefetch=1,
        grid=(n_blocks,),
        in_specs=[
            pl.BlockSpec((1, 1, rows), lambda i, be: (i, 0, 0), memory_space=pltpu.SMEM),
            pl.BlockSpec(memory_space=pl.ANY),
            pl.BlockSpec((1, D_MODEL, 2 * D_MODEL), lambda i, be: (be[i], 0, 0)),
            pl.BlockSpec((1, 1, 2 * D_MODEL), lambda i, be: (be[i], 0, 0)),
            pl.BlockSpec((1, D_MODEL, D_MODEL), lambda i, be: (be[i], 0, 0)),
            pl.BlockSpec((1, 1, D_MODEL), lambda i, be: (be[i], 0, 0)),
        ],
        out_specs=pl.BlockSpec((rows, D_MODEL), lambda i, be: (i, 0)),
        scratch_shapes=[pltpu.VMEM((rows, D_MODEL), F32),
                        pltpu.VMEM((D_MODEL, 2 * D_MODEL), BF16),
                        pltpu.VMEM((D_MODEL, D_MODEL), BF16),
                        pltpu.SemaphoreType.DMA(())],
    )
    return pl.pallas_call(
        functools.partial(_moe_kernel, rows=rows),
        grid_spec=grid_spec,
        out_shape=jax.ShapeDtypeStruct((n_slots, D_MODEL), F32),
        compiler_params=_cparams("arbitrary"),
        name="moe_experts",
    )(block_e, slot_tok.reshape(n_blocks, 1, rows), h2, w_gate_up, b_gate_up[:, None], w_down, b_down[:, None])


def _combine_kernel(dest_ref, ys_hbm, x1_ref, gate_ref, lnf_ref, y_ref, buf, sem, *, tm):
    def gather(idx):
        d = dest_ref[0, 0, idx]
        return pltpu.make_async_copy(ys_hbm.at[pl.ds(d, 1)], buf.at[idx & (TOP_K - 1), pl.ds(idx >> 2, 1)], sem)

    def issue(idx, c):
        gather(idx).start()
        return c

    def wait(idx, c):
        gather(idx).wait()
        return c

    lax.fori_loop(0, tm * TOP_K, issue, 0)
    lax.fori_loop(0, tm * TOP_K, wait, 0)
    acc = x1_ref[...]
    for kslot in range(TOP_K):
        acc = acc + gate_ref[:, kslot:kslot + 1] * buf[kslot]
    y_ref[...] = _rms(acc, lnf_ref[...])


def _combine(dest, ys, x1, gates, ln_f, *, tm, n, row0):
    blk0 = row0 // tm
    nt = n // tm
    dest_blocks = dest.reshape(-1, 1, tm * TOP_K)
    return pl.pallas_call(
        functools.partial(_combine_kernel, tm=tm),
        grid=(nt,),
        in_specs=[
            pl.BlockSpec((1, 1, tm * TOP_K), lambda i: (i + blk0, 0, 0), memory_space=pltpu.SMEM),
            pl.BlockSpec(memory_space=pl.ANY),
            pl.BlockSpec((tm, D_MODEL), lambda i: (i + blk0, 0)),
            pl.BlockSpec((tm, LANES), lambda i: (i + blk0, 0)),
            pl.BlockSpec(ln_f.shape, lambda i: (0, 0)),
        ],
        out_specs=pl.BlockSpec((tm, D_MODEL), lambda i: (i, 0)),
        out_shape=jax.ShapeDtypeStruct((n, D_MODEL), F32),
        scratch_shapes=[pltpu.VMEM((TOP_K, tm, D_MODEL), F32), pltpu.SemaphoreType.DMA(())],
        compiler_params=_cparams("arbitrary"),
        name="combine",
    )(dest_blocks, ys, x1, gates, ln_f)


def _route(topi, rows):
    e_flat = topi.reshape(-1)
    nk = e_flat.shape[0]
    order = jnp.argsort(e_flat)
    e_s = e_flat[order]
    tok_s = (order // TOP_K).astype(jnp.int32)
    sizes = jnp.bincount(e_flat, length=N_EXPERTS)
    starts = jnp.cumsum(sizes) - sizes
    padded = (sizes + rows - 1) // rows * rows
    pends = jnp.cumsum(padded)
    pstarts = pends - padded
    dest_s = (pstarts[e_s] + jnp.arange(nk) - starts[e_s]).astype(jnp.int32)
    n_blocks = -(-(nk + N_EXPERTS * (rows - 1)) // rows)
    slot_tok = jnp.zeros((n_blocks * rows,), jnp.int32).at[dest_s].set(tok_s)
    dest = jnp.zeros((nk,), jnp.int32).at[order].set(dest_s)
    block_e = jnp.minimum(jnp.searchsorted(pends, jnp.arange(n_blocks) * rows, side='right'),
                          N_EXPERTS - 1).astype(jnp.int32)
    return block_e, slot_tok, dest


def kernel(x_prompt, x_sample, cache_k, cache_v, state_wkv, state_shift, rel_bias, ln1, w_in, rwkv_mu, w0,
           w_decay_up, a0, w_a_up, w_g_up, k_k, k_a, r_k, gn_w, gn_b, w_branch_a, w_branch_b, w_out, ln2,
           w_router, b_router, w_gate_up, b_gate_up, w_down, b_down, ln_f):
    bp, seq, _ = x_prompt.shape
    db = x_sample.shape[0]
    n_p = bp * seq
    n_tot = n_p + db
    l = 0

    row = lambda a: a.reshape(1, -1)
    seg = _head_seg()
    prep_params = (row(rwkv_mu[l]), row(w0[l]), w_decay_up[l], row(a0[l]), w_a_up[l], w_g_up[l],
                   row(k_k[l]), row(k_a[l]), row(r_k[l]), seg)
    wr_pad = jnp.pad(w_router[l], ((0, 0), (0, LANES - N_EXPERTS))).astype(BF16)
    br_pad = jnp.pad(b_router[l], (0, LANES - N_EXPERTS), constant_values=NEG).reshape(1, LANES)
    post_small = (row(gn_w[l]), row(gn_b[l]), seg)
    post_tail = (row(ln2[l]), wr_pad, br_pad)

    w_in_bf = w_in[l].astype(BF16)
    consts = (post_small + (w_branch_a[l].astype(BF16), w_branch_b[l].astype(BF16), w_out[l].astype(BF16))
              + post_tail)

    xp = x_prompt.reshape(n_p, D_MODEL)
    q_p, k_p, v_p, rc_p, sg_p = _inproj(xp, row(ln1[l]), w_in_bf, tm=TOKEN_TILE)
    as3 = lambda a: a.reshape(bp, seq, -1)
    oa_p = _attn_prompt(as3(q_p), as3(k_p), as3(v_p), _prompt_bias(rel_bias))
    prep_p = _rwkv_prep_prompt(as3(rc_p), prep_params, tt=TOKEN_TILE)
    r_, w_, k2_, vr_, kk_, ka_, bonus_p, g_p = prep_p
    s0_p = jnp.zeros((bp, R_HEADS, R_HEAD, R_HEAD), F32)
    y_p, wkv_p = _lane_scan(r_, w_, k2_, vr_, kk_, ka_, s0_p, tc=SCAN_CHUNK)
    flat = lambda a: a.reshape(n_p, -1)
    bufs = _post(xp, flat(y_p), flat(bonus_p), flat(g_p), flat(oa_p), sg_p, consts,
                 tm=TOKEN_TILE, n_total=n_tot)

    xs = x_sample.reshape(db, D_MODEL)
    q_s, k_s, v_s, rc_s, sg_s = _inproj(xs, row(ln1[l]), w_in_bf, tm=db)
    bias_s, b0_s = _sample_bias(rel_bias)
    oa_s = _attn_sample(q_s, k_s, v_s, cache_k[l], cache_v[l], bias_s, b0_s)
    prep_s = _rwkv_prep_sample(rc_s, state_shift[l], prep_params)
    sr, sw, sk2, svr, skk, ska, bonus_s, g_s = [a[:, None] for a in prep_s]
    y_s, wkv_s = _rwkv_scan(sr, sw, sk2, svr, skk, ska, state_wkv[l], bb=SCAN_BATCH, tc=1)
    x1_all, h2_all, ti_all, tg_all = _post(xs, y_s[:, 0], bonus_s[:, 0], g_s[:, 0], oa_s, sg_s, consts,
                                           tm=db, n_total=n_tot, row0=n_p, into=bufs)

    block_e, slot_tok, dest = _route(ti_all[:, :TOP_K], MOE_ROWS)
    ys = _moe_experts(block_e, slot_tok, h2_all, w_gate_up[l], b_gate_up[l], w_down[l], b_down[l], rows=MOE_ROWS)
    lnf = row(ln_f)
    pad = (-dest.shape[0]) % (TOKEN_TILE * TOP_K)
    y_prompt = _combine(jnp.pad(dest, (0, pad)), ys, x1_all, tg_all, lnf, tm=TOKEN_TILE, n=n_p, row0=0)
    y_sample = _combine(dest, ys, x1_all, tg_all, lnf, tm=db, n=db, row0=n_p)

    heads = lambda a, b_: a.reshape(1, b_, -1, A_HEADS, HEAD_DIM)
    return (y_prompt.reshape(bp, seq, D_MODEL), y_sample.reshape(db, 1, D_MODEL),
            heads(k_p, bp), heads(v_p, bp), wkv_p[None], as3(rc_p)[:, -1][None],
            heads(k_s, db), heads(v_s, db), wkv_s[None], rc_s[None])
```

```python
import functools
import math

import jax
import jax.numpy as jnp
from jax import lax
from jax.experimental import pallas as pl
from jax.experimental.pallas import tpu as pltpu

F32 = jnp.float32
BF16 = jnp.bfloat16

D_MODEL = 1024
A_HEADS = 8
HEAD_DIM = 64
A_WIDTH = A_HEADS * HEAD_DIM
PATTERNS = ((128, 1), (512, 4), (2048, 16))
BLOCK = 128
NUM_BUCKETS = 32
MAX_DISTANCE = 2048
SCALE = HEAD_DIM ** -0.5
NEG = -1e30
R_HEADS = 8
R_HEAD = 64
R_WIDTH = R_HEADS * R_HEAD
DECAY_LORA = 64
AAA_LORA = 64
GATE_LORA = 128
GN_EPS = 64e-5
N_RWKV_COLS = 3 * R_WIDTH + DECAY_LORA + AAA_LORA + GATE_LORA
N_IN_COLS = 3 * A_WIDTH + N_RWKV_COLS + 2 * D_MODEL
N_EXPERTS = 32
TOP_K = 4
SWIGLU_LIMIT = 7.0
SWIGLU_ALPHA = 1.702
RMS_EPS = 1e-6

LANES = 128
VMEM_LIMIT = 56 * 1024 * 1024
TOKEN_TILE = 256
MOE_ROWS = 256
SCAN_BATCH = 2
SCAN_CHUNK = 64


def _cparams(*sem):
    return pltpu.CompilerParams(dimension_semantics=sem, vmem_limit_bytes=VMEM_LIMIT)


def _rms(x, g):
    return x * lax.rsqrt(jnp.mean(x * x, axis=-1, keepdims=True) + RMS_EPS) * g


def _dot(a, b):
    return jnp.dot(a.astype(BF16), b.astype(BF16), preferred_element_type=F32)


def _bf(x):
    return x.astype(BF16).astype(F32)


def _split3(x):
    hi = x.astype(BF16)
    r1 = x - hi.astype(F32)
    mid = r1.astype(BF16)
    lo = (r1 - mid.astype(F32)).astype(BF16)
    return hi, mid, lo


def _segdot(x, seg):
    hi, mid, lo = _split3(x)
    d = lambda p: jnp.dot(p, seg, preferred_element_type=F32)
    return d(hi) + d(mid) + d(lo)


def _inproj_kernel(x_ref, g_ref, w_ref, q_ref, k_ref, v_ref, rc_ref, sg_ref):
    h = _rms(x_ref[...], g_ref[...]).astype(BF16)

    def mm(c0, c1):
        return _dot(h, w_ref[:, c0:c1])

    c1 = 3 * A_WIDTH
    c2 = c1 + N_RWKV_COLS
    q_ref[...] = mm(0, A_WIDTH)
    k_ref[...] = mm(A_WIDTH, 2 * A_WIDTH)
    v_ref[...] = mm(2 * A_WIDTH, c1)
    rc_ref[...] = mm(c1, c2)
    sg_ref[:, :D_MODEL] = jax.nn.sigmoid(mm(c2, c2 + D_MODEL))
    sg_ref[:, D_MODEL:] = jax.nn.sigmoid(mm(c2 + D_MODEL, N_IN_COLS))


def _inproj(x, ln1, w, *, tm):
    n = x.shape[0]
    row = lambda c: pl.BlockSpec((tm, c), lambda i: (i, 0))
    full = lambda a: pl.BlockSpec(a.shape, lambda i: (0,) * a.ndim)
    out_cols = (A_WIDTH, A_WIDTH, A_WIDTH, N_RWKV_COLS, 2 * D_MODEL)
    return pl.pallas_call(
        _inproj_kernel,
        grid=(n // tm,),
        in_specs=[row(D_MODEL), full(ln1), full(w)],
        out_specs=[row(c) for c in out_cols],
        out_shape=[jax.ShapeDtypeStruct((n, c), F32) for c in out_cols],
        compiler_params=_cparams("parallel"),
        name="inproj",
    )(x, ln1, w)


def _t5_bucket(dist):
    max_exact = NUM_BUCKETS // 2
    d = jnp.maximum(dist, 1).astype(F32)
    large = max_exact + (jnp.log(d / max_exact) / math.log(MAX_DISTANCE / max_exact)
                         * (NUM_BUCKETS - max_exact)).astype(jnp.int32)
    large = jnp.minimum(large, NUM_BUCKETS - 1)
    return jnp.where(dist < max_exact, dist, large)


def _prompt_bias(rel_bias):
    i = jnp.arange(BLOCK)[:, None]
    j = jnp.arange(2 * BLOCK)[None, :]
    delta = i + BLOCK - j
    out = []
    for window, dil in PATTERNS:
        n = window // dil
        band = (delta >= 0) & (delta <= n)
        b = rel_bias.astype(F32)[_t5_bucket(jnp.clip(delta, 0, n) * dil)].transpose(2, 0, 1)
        out.append(jnp.where(band[None], b, NEG))
    return jnp.stack(out, 0)


def _sample_bias(rel_bias):
    rb = rel_bias.astype(F32)
    out = []
    for window, dil in PATTERNS:
        n = window // dil
        jj = n - jnp.arange(n)
        out.append(rb[_t5_bucket(jj * dil)])
    b = jnp.stack(out, 0)[..., None]
    b0 = rb[_t5_bucket(jnp.zeros((1,), jnp.int32))].reshape(A_HEADS, 1)
    return b, b0


def _attn_prompt_kernel(q_ref, k_ref, v_ref, bias_ref, o_ref, os_ref, ls_ref, *, seq):
    nt = (((1,), (1,)), ((), ()))
    npat = len(PATTERNS)
    w2 = 2 * HEAD_DIM
    lo = lax.broadcasted_iota(jnp.int32, (BLOCK, w2), 1) < HEAD_DIM
    key = lax.broadcasted_iota(jnp.int32, (2 * BLOCK, 2 * BLOCK), 1)

    for g, (window, dil) in enumerate(PATTERNS):
        nb = seq // (BLOCK * dil)
        sh = dil.bit_length() - 1

        def body(it, carry, g=g, dil=dil, nb=nb, sh=sh):
            r = it & (dil - 1)
            c = it >> sh
            start = c * (BLOCK * dil) + r
            rows = pl.ds(start, BLOCK, stride=dil) if dil > 1 else pl.ds(start, BLOCK)
            qb = q_ref[0, rows, :]
            kb = k_ref[0, rows, :]
            vb = v_ref[0, rows, :]
            if nb > 1:
                pstart = jnp.maximum(c - 1, 0) * (BLOCK * dil) + r
                prow = pl.ds(pstart, BLOCK, stride=dil) if dil > 1 else pl.ds(pstart, BLOCK)
                kp = k_ref[0, prow, :]
                vp = v_ref[0, prow, :]
            q2 = jnp.concatenate([jnp.where(lo, qb, 0.0), jnp.where(lo, 0.0, qb)], axis=0).astype(BF16)
            if nb > 1:
                k2 = jnp.concatenate([kp, kb], axis=0)
                v2 = jnp.concatenate([vp, vb], axis=0)
                bias = bias_ref[g, 0]
                valid = (bias > 0.5 * NEG) & ((key >= BLOCK) | (c > 0))
            else:
                k2, v2 = kb, vb
                bias = bias_ref[g, 0, :, BLOCK:]
                valid = bias > 0.5 * NEG
            s = lax.dot_general(q2, k2.astype(BF16), nt, preferred_element_type=F32)
            s = jnp.where(valid, s * SCALE + bias, NEG)
            m = jnp.max(s, axis=-1, keepdims=True)
            p = jnp.exp(s - m)
            l = jnp.sum(p, axis=-1, keepdims=True)
            o2 = jnp.dot((p * (1.0 / l)).astype(BF16), v2.astype(BF16), preferred_element_type=F32)
            lse = jnp.broadcast_to(m + jnp.log(l), (2 * BLOCK, w2))
            os_ref[g, rows, :] = jnp.where(lo, o2[:BLOCK], o2[BLOCK:])
            ls_ref[g, rows, :] = jnp.where(lo, lse[:BLOCK], lse[BLOCK:])
            return carry

        lax.fori_loop(0, dil * nb, body, 0, unroll=8)

    chunk = 256

    def merge(ch, carry):
        rows = pl.ds(pl.multiple_of(ch * chunk, chunk), chunk)
        lse = [ls_ref[g, rows, :] for g in range(npat)]
        top = functools.reduce(jnp.maximum, lse)
        e = [jnp.exp(x - top) for x in lse]
        inv = 1.0 / functools.reduce(lambda a, b_: a + b_, e)
        acc = jnp.zeros((chunk, 2 * HEAD_DIM), F32)
        for g in range(npat):
            acc = acc + (e[g] * inv) * os_ref[g, rows, :]
        o_ref[0, rows, :] = acc
        return carry

    lax.fori_loop(0, seq // chunk, merge, 0)


def _attn_prompt(q, k, v, bias):
    b, s, _ = q.shape
    w2 = 2 * HEAD_DIM
    qspec = pl.BlockSpec((1, s, w2), lambda i, p: (i, 0, p))
    return pl.pallas_call(
        functools.partial(_attn_prompt_kernel, seq=s),
        grid=(b, A_HEADS // 2),
        in_specs=[qspec, qspec, qspec,
                  pl.BlockSpec((len(PATTERNS), 1, 2 * BLOCK, 2 * BLOCK), lambda i, p: (0, p, 0, 0))],
        out_specs=qspec,
        out_shape=jax.ShapeDtypeStruct((b, s, A_WIDTH), F32),
        scratch_shapes=[pltpu.VMEM((len(PATTERNS), s, w2), F32)] * 2,
        compiler_params=_cparams("parallel", "parallel"),
        name="attn_prompt",
    )(q, k, v, bias.reshape(len(PATTERNS), A_HEADS // 2, 2 * BLOCK, 2 * BLOCK))


def _attn_sample_kernel(q_ref, kn_ref, vn_ref, *refs):
    npat = len(PATTERNS)
    k_refs, v_refs = refs[:npat], refs[npat:2 * npat]
    bias_ref, b0_ref, o_ref = refs[2 * npat:]
    q = _bf(q_ref[0])
    v_new = _bf(vn_ref[0])
    s0 = jnp.sum(q * _bf(kn_ref[0]), axis=-1, keepdims=True) * SCALE + b0_ref[...]
    outs, lses = [], []
    for g in range(npat):
        kg = _bf(k_refs[g][0, :, 0])
        s = jnp.sum(kg * q[None], axis=-1, keepdims=True) * SCALE + bias_ref[g]
        m = jnp.maximum(jnp.max(s, axis=0), s0)
        l = jnp.sum(jnp.exp(s - m[None]), axis=0) + jnp.exp(s0 - m)
        lse = m + jnp.log(l)
        p = _bf(jnp.exp(s - lse[None]))
        p0 = _bf(jnp.exp(s0 - lse))
        outs.append(jnp.sum(p * _bf(v_refs[g][0, :, 0]), axis=0) + p0 * v_new)
        lses.append(lse)
    top = functools.reduce(jnp.maximum, lses)
    e = [jnp.exp(x - top) for x in lses]
    inv = 1.0 / functools.reduce(lambda a, b_: a + b_, e)
    acc = jnp.zeros((A_HEADS, HEAD_DIM), F32)
    for g in range(npat):
        acc = acc + _bf(e[g] * inv) * _bf(outs[g])
    o_ref[0] = acc


def _attn_sample(q, k_new, v_new, cache_k, cache_v, bias, b0):
    db, wb = cache_k.shape[:2]
    heads = lambda a: a.reshape(db, A_HEADS, HEAD_DIM)
    vec = pl.BlockSpec((1, A_HEADS, HEAD_DIM), lambda i: (i, 0, 0))
    full = lambda a: pl.BlockSpec(a.shape, lambda i: (0,) * a.ndim)
    views, specs = [], []
    for window, dil in PATTERNS:
        start = wb - window
        assert window // dil == BLOCK and start >= 0 and start % (BLOCK * dil) == 0 and wb % dil == 0
        views.append(lambda a, dil=dil: a.reshape(db, wb // dil, dil, A_HEADS, HEAD_DIM))
        specs.append(pl.BlockSpec((1, BLOCK, 1, A_HEADS, HEAD_DIM),
                                  lambda i, blk=start // (BLOCK * dil): (i, blk, 0, 0, 0)))
    args = ([heads(q), heads(k_new), heads(v_new)] + [vw(cache_k) for vw in views] + [vw(cache_v) for vw in views]
            + [bias, b0])
    return pl.pallas_call(
        _attn_sample_kernel,
        grid=(db,),
        in_specs=[vec, vec, vec] + specs + specs + [full(bias), full(b0)],
        out_specs=vec,
        out_shape=jax.ShapeDtypeStruct((db, A_HEADS, HEAD_DIM), F32),
        compiler_params=_cparams("parallel"),
        name="attn_sample",
    )(*args).reshape(db, A_WIDTH)


def _rwkv_prep_math(rc, prev, mu, w0, wd, a0, wa, wg, k_k, k_a, r_k, seg):
    xr = rc + (prev - rc) * mu
    o1, o2, o3 = R_WIDTH, 2 * R_WIDTH, 3 * R_WIDTH
    r = xr[:, :o1]
    kr = xr[:, o1:o2]
    vr = xr[:, o2:o3]
    xw = xr[:, o3:o3 + DECAY_LORA]
    xa = xr[:, o3 + DECAY_LORA:o3 + DECAY_LORA + AAA_LORA]
    xg = xr[:, o3 + DECAY_LORA + AAA_LORA:]
    z = -(w0 + _dot(jnp.tanh(xw), wd))
    softplus = jnp.maximum(z, 0.0) + jnp.log(1.0 + jnp.exp(-jnp.abs(z)))
    decay = jnp.exp(-jnp.exp(-softplus - 0.5))
    a = jax.nn.sigmoid(a0 + _dot(xa, wa))
    g = _dot(jax.nn.sigmoid(xg), wg)
    kk = kr * k_k
    k2 = kr * (1.0 + (a - 1.0) * k_a)
    kk = kk / jnp.maximum(jnp.sqrt(_segdot(kk * kk, seg)), 1e-12)
    bonus = _segdot(r * k2 * r_k, seg) * vr
    return _bf(r), decay, k2, vr, _bf(kk), kk * a, bonus, g


N_PREP_PARAMS = 10


def _rwkv_prep_prompt_kernel(rc_ref, tail_ref, *refs):
    p = [x[...] for x in refs[:N_PREP_PARAMS]]
    outs = refs[N_PREP_PARAMS:]
    rc = rc_ref[0]
    tt = rc.shape[0]
    first = pl.program_id(1) == 0
    prev_row = jnp.where(first, 0.0, tail_ref[0, 7:8, :])
    rolled = pltpu.roll(rc, 1, axis=0)
    row = lax.broadcasted_iota(jnp.int32, (tt, 1), 0)
    prev = jnp.where(row == 0, prev_row, rolled)
    for o, val in zip(outs, _rwkv_prep_math(rc, prev, *p)):
        o[0] = val


def _rwkv_prep_sample_kernel(rc_ref, prev_ref, *refs):
    p = [x[...] for x in refs[:N_PREP_PARAMS]]
    outs = refs[N_PREP_PARAMS:]
    for o, val in zip(outs, _rwkv_prep_math(rc_ref[...], prev_ref[...], *p)):
        o[...] = val


def _head_seg():
    head = jnp.arange(R_WIDTH) // R_HEAD
    return (head[:, None] == head[None, :]).astype(BF16)


def _rwkv_prep_prompt(rc, params, *, tt):
    b, t, _ = rc.shape
    full = lambda a: pl.BlockSpec(a.shape, lambda i, j: (0,) * a.ndim)
    tile = lambda c: pl.BlockSpec((1, tt, c), lambda i, j: (i, j, 0))
    tail = pl.BlockSpec((1, 8, N_RWKV_COLS), lambda i, j: (i, jnp.maximum(j * (tt // 8) - 1, 0), 0))
    return pl.pallas_call(
        _rwkv_prep_prompt_kernel,
        grid=(b, t // tt),
        in_specs=[tile(N_RWKV_COLS), tail] + [full(a) for a in params],
        out_specs=[tile(R_WIDTH)] * 8,
        out_shape=[jax.ShapeDtypeStruct((b, t, R_WIDTH), F32)] * 8,
        compiler_params=_cparams("parallel", "parallel"),
        name="rwkv_prep_prompt",
    )(rc, rc, *params)


def _rwkv_prep_sample(rc, prev, params):
    n = rc.shape[0]
    full = lambda a: pl.BlockSpec(a.shape, lambda i: (0,) * a.ndim)
    return pl.pallas_call(
        _rwkv_prep_sample_kernel,
        grid=(1,),
        in_specs=[full(rc), full(prev)] + [full(a) for a in params],
        out_specs=[pl.BlockSpec((n, R_WIDTH), lambda i: (0, 0))] * 8,
        out_shape=[jax.ShapeDtypeStruct((n, R_WIDTH), F32)] * 8,
        compiler_params=_cparams("arbitrary"),
        name="rwkv_prep_sample",
    )(rc, prev, *params)


def _seg_lane_sum(x, lo_mask):
    lo = jnp.sum(jnp.where(lo_mask, x, 0.0), axis=-1, keepdims=True)
    hi = jnp.sum(jnp.where(lo_mask, 0.0, x), axis=-1, keepdims=True)
    return jnp.where(lo_mask, lo, hi)


def _rwkv_scan_kernel(r_ref, w_ref, k_ref, v_ref, kk_ref, ka_ref, s0_ref, y_ref, sT_ref, st_ref, *, bb, tc):
    pairs = R_HEADS // 2
    w2 = 2 * R_HEAD

    @pl.when(pl.program_id(1) == 0)
    def _():
        for b in range(bb):
            for p in range(pairs):
                st_ref[b, p] = jnp.concatenate([s0_ref[b, 2 * p], s0_ref[b, 2 * p + 1]], axis=1)

    lane = lax.broadcasted_iota(jnp.int32, (R_HEAD, w2), 1)
    sub = lax.broadcasted_iota(jnp.int32, (R_HEAD, w2), 0)
    lo_mask = lane < R_HEAD
    eye2 = (lane & (R_HEAD - 1)) == sub

    grp = min(8, tc)

    def group(tg, carry):
        rows = pl.ds(pl.multiple_of(tg * grp, grp), grp)
        for b in range(bb):
            for p in range(pairs):
                cols = slice(p * w2, (p + 1) * w2)
                r8, w8, k8, v8, kk8, ka8 = (ref[b, rows, cols] for ref in (r_ref, w_ref, k_ref, v_ref, kk_ref, ka_ref))
                s = st_ref[b, p]
                sb = _bf(s)
                ys = []
                for j in range(grp):
                    row = lambda a: a[j:j + 1, :]
                    sa = -_seg_lane_sum(sb * row(kk8), lo_mask)
                    vcol = _seg_lane_sum(jnp.where(eye2, jnp.broadcast_to(row(v8), (R_HEAD, w2)), 0.0), lo_mask)
                    s = s * row(w8) + sa * row(ka8) + vcol * row(k8)
                    sb = _bf(s)
                    yfull = _seg_lane_sum(sb * row(r8), lo_mask)
                    ys.append(jnp.sum(jnp.where(eye2, yfull, 0.0), axis=0, keepdims=True))
                st_ref[b, p] = s
                y_ref[b, rows, cols] = jnp.concatenate(ys, axis=0) if grp > 1 else ys[0]
        return carry

    lax.fori_loop(0, tc // grp, group, 0)

    @pl.when(pl.program_id(1) == pl.num_programs(1) - 1)
    def _():
        for b in range(bb):
            for p in range(pairs):
                s = st_ref[b, p]
                sT_ref[b, 2 * p] = s[:, :R_HEAD]
                sT_ref[b, 2 * p + 1] = s[:, R_HEAD:]


def _rwkv_scan(r, w, k, v, kk, ka, s0, *, bb, tc):
    b, t, _ = r.shape
    seq = pl.BlockSpec((bb, tc, R_WIDTH), lambda i, j: (i, j, 0))
    state = pl.BlockSpec((bb, R_HEADS, R_HEAD, R_HEAD), lambda i, j: (i, 0, 0, 0))
    return pl.pallas_call(
        functools.partial(_rwkv_scan_kernel, bb=bb, tc=tc),
        grid=(b // bb, t // tc),
        in_specs=[seq] * 6 + [state],
        out_specs=[seq, state],
        out_shape=[jax.ShapeDtypeStruct((b, t, R_WIDTH), F32),
                   jax.ShapeDtypeStruct((b, R_HEADS, R_HEAD, R_HEAD), F32)],
        scratch_shapes=[pltpu.VMEM((bb, R_HEADS // 2, R_HEAD, 2 * R_HEAD), F32)],
        compiler_params=_cparams("parallel", "arbitrary"),
        name="rwkv_scan",
    )(r, w, k, v, kk, ka, s0)


CHAINS = LANES // 2
K2 = R_HEAD // 2


def _chain_rows(x):
    b, t, _ = x.shape
    x = x.reshape(b, t, R_HEADS, K2, 2).transpose(1, 3, 4, 0, 2)
    return x.reshape(t, K2, LANES)


def _chain_vals(x):
    b, t, _ = x.shape
    x = x.reshape(b, t, R_HEADS, R_HEAD).transpose(1, 3, 0, 2).reshape(t, R_HEAD, CHAINS)
    return jnp.concatenate([x, x], axis=-1)


def _lane_scan_kernel(kk_ref, w_ref, ka_ref, kx_ref, r_ref, vv_ref, s0_ref, y_ref, sT_ref, s_ref, *, tc):
    @pl.when(pl.program_id(0) == 0)
    def _():
        for k2 in range(K2):
            s_ref[k2] = s0_ref[k2]

    def both_halves(x):
        return x + pltpu.roll(x, CHAINS, axis=1)

    nhalf = 2
    vh = R_HEAD // nhalf

    def row(ref, t, k2):
        return jnp.broadcast_to(ref[t, k2:k2 + 1, :], (vh, LANES))

    acc0 = []
    for hf in range(nhalf):
        vs = slice(hf * vh, (hf + 1) * vh)
        a = jnp.zeros((vh, LANES), F32)
        for k2 in range(K2):
            a = a + _bf(s_ref[k2, vs, :]) * row(kk_ref, 0, k2)
        acc0.append(a)

    def step(t, acc):
        tn = jnp.minimum(t + 1, tc - 1)
        nxt = []
        for hf in range(nhalf):
            vs = slice(hf * vh, (hf + 1) * vh)
            vv = vv_ref[t, vs, :]
            sa = -both_halves(acc[hf])
            yacc = jnp.zeros((vh, LANES), F32)
            nacc = jnp.zeros((vh, LANES), F32)
            for k2 in range(K2):
                s = s_ref[k2, vs, :] * row(w_ref, t, k2) + sa * row(ka_ref, t, k2) + vv * row(kx_ref, t, k2)
                s_ref[k2, vs, :] = s
                sb = _bf(s)
                yacc = yacc + sb * row(r_ref, t, k2)
                nacc = nacc + sb * row(kk_ref, tn, k2)
            y_ref[t, vs, :] = both_halves(yacc)
            nxt.append(nacc)
        return tuple(nxt)

    lax.fori_loop(0, tc, step, tuple(acc0))

    @pl.when(pl.program_id(0) == pl.num_programs(0) - 1)
    def _():
        for k2 in range(K2):
            sT_ref[k2] = s_ref[k2]


def _lane_scan(r, w, k, v, kk, ka, s0, *, tc):
    b, t, _ = r.shape
    assert b * R_HEADS == CHAINS
    ops = [_chain_rows(x) for x in (kk, w, ka, k, r)] + [_chain_vals(v)]
    s0c = s0.reshape(b, R_HEADS, R_HEAD, K2, 2).transpose(3, 2, 4, 0, 1).reshape(K2, R_HEAD, LANES)
    rows = pl.BlockSpec((tc, K2, LANES), lambda i: (i, 0, 0))
    vals = pl.BlockSpec((tc, R_HEAD, LANES), lambda i: (i, 0, 0))
    state = pl.BlockSpec((K2, R_HEAD, LANES), lambda i: (0, 0, 0))
    y, st = pl.pallas_call(
        functools.partial(_lane_scan_kernel, tc=tc),
        grid=(t // tc,),
        in_specs=[rows] * 5 + [vals, state],
        out_specs=[vals, state],
        out_shape=[jax.ShapeDtypeStruct((t, R_HEAD, LANES), F32), jax.ShapeDtypeStruct((K2, R_HEAD, LANES), F32)],
        scratch_shapes=[pltpu.VMEM((K2, R_HEAD, LANES), F32)],
        compiler_params=_cparams("arbitrary"),
        name="rwkv_lane_scan",
    )(*ops, s0c)
    y = y[:, :, :CHAINS].reshape(t, R_HEAD, b, R_HEADS).transpose(2, 0, 3, 1).reshape(b, t, R_WIDTH)
    st = st.reshape(K2, R_HEAD, 2, b, R_HEADS).transpose(3, 4, 1, 0, 2).reshape(b, R_HEADS, R_HEAD, R_HEAD)
    return y, st


def _post_kernel(x_ref, y_ref, bonus_ref, g_ref, oa_ref, sg_ref, gnw_ref, gnb_ref, seg_ref, wba_ref, wbb_ref,
                 wout_ref, ln2_ref, wr_ref, br_ref, *rest, aliased, n_main):
    outs = rest[aliased:]
    x1_ref, h2_ref, ti_ref, tg_ref = outs

    @pl.when(pl.program_id(0) >= n_main)
    def _():
        for o in outs:
            o[...] = jnp.zeros(o.shape, o.dtype)

    @pl.when(pl.program_id(0) < n_main)
    def _():
        _post_body(x_ref, y_ref, bonus_ref, g_ref, oa_ref, sg_ref, gnw_ref, gnb_ref, seg_ref, wba_ref, wbb_ref,
                   wout_ref, ln2_ref, wr_ref, br_ref, x1_ref, h2_ref, ti_ref, tg_ref)


def _post_body(x_ref, y_ref, bonus_ref, g_ref, oa_ref, sg_ref, gnw_ref, gnb_ref, seg_ref, wba_ref, wbb_ref,
               wout_ref, ln2_ref, wr_ref, br_ref, x1_ref, h2_ref, ti_ref, tg_ref):
    y = y_ref[...]
    seg = seg_ref[...]
    mu = _segdot(y, seg) * (1.0 / R_HEAD)
    yc = y - mu
    var = _segdot(yc * yc, seg) * (1.0 / R_HEAD)
    yn = yc * lax.rsqrt(var + GN_EPS) * gnw_ref[...] + gnb_ref[...]
    o_b = (yn + bonus_ref[...]) * g_ref[...]
    mixed = (sg_ref[:, :D_MODEL] * _dot(oa_ref[...], wba_ref[...])
             + sg_ref[:, D_MODEL:] * _dot(o_b, wbb_ref[...]))
    x1 = x_ref[...] + _dot(mixed, wout_ref[...])
    x1_ref[...] = x1
    h2 = _rms(x1, ln2_ref[...])
    h2_ref[...] = h2
    logits = _dot(h2, wr_ref[...]) + br_ref[...]
    lane = lax.broadcasted_iota(jnp.int32, logits.shape, 1).astype(F32)
    work = logits
    vals, idxs = [], []
    for _ in range(TOP_K):
        m = jnp.max(work, axis=-1, keepdims=True)
        idx = jnp.min(jnp.where(work == m, lane, float(LANES)), axis=-1, keepdims=True)
        vals.append(m)
        idxs.append(idx)
        work = jnp.where(lane == idx, -jnp.inf, work)
    es = [jnp.exp(v - vals[0]) for v in vals]
    tot = es[0] + es[1] + es[2] + es[3]
    ti = jnp.zeros(logits.shape, F32)
    tg = jnp.zeros(logits.shape, F32)
    for kslot in range(TOP_K):
        ti = jnp.where(lane == float(kslot), idxs[kslot], ti)
        tg = jnp.where(lane == float(kslot), es[kslot] / tot, tg)
    ti_ref[...] = ti.astype(jnp.int32)
    tg_ref[...] = tg


def _post(x, y, bonus, g, oa, sg, consts, *, tm, n_total, row0=0, into=None):
    n = x.shape[0]
    blk0 = row0 // tm
    n_main = n // tm
    steps = n_main + (1 if into is None and n_total > n else 0)
    row = lambda c: pl.BlockSpec((tm, c), lambda i: (jnp.minimum(i, n_main - 1), 0))
    orow = lambda c: pl.BlockSpec((tm, c), lambda i: (i + blk0, 0))
    full = lambda a: pl.BlockSpec(a.shape, lambda i: (0,) * a.ndim)
    out_cols = ((D_MODEL, F32), (D_MODEL, F32), (LANES, jnp.int32), (LANES, F32))
    ins = [x, y, bonus, g, oa, sg, *consts]
    in_specs = [row(D_MODEL), row(R_WIDTH), row(R_WIDTH), row(R_WIDTH), row(A_WIDTH), row(2 * D_MODEL)]
    in_specs += [full(a) for a in consts]
    aliases = {}
    if into is not None:
        aliases = {len(ins) + i: i for i in range(len(into))}
        in_specs += [pl.BlockSpec(memory_space=pl.ANY)] * len(into)
        ins += list(into)
    return pl.pallas_call(
        functools.partial(_post_kernel, aliased=len(aliases), n_main=n_main),
        grid=(steps,),
        in_specs=in_specs,
        out_specs=[orow(c) for c, _ in out_cols],
        out_shape=[jax.ShapeDtypeStruct((n_total, c), dt) for c, dt in out_cols],
        input_output_aliases=aliases,
        compiler_params=_cparams("parallel"),
        name="post",
    )(*ins)


def _start_rows(rows_ref, hbm, buf, sem, *, rows, to_hbm):
    for r in range(rows):
        at = pl.ds(rows_ref[0, 0, r], 1)
        if to_hbm:
            pltpu.make_async_copy(buf.at[pl.ds(r, 1)], hbm.at[at], sem).start()
        else:
            pltpu.make_async_copy(hbm.at[at], buf.at[pl.ds(r, 1)], sem).start()


def _wait_rows(hbm, buf, sem, *, rows, to_hbm):
    whole = hbm.at[pl.ds(0, rows)]
    (pltpu.make_async_copy(buf, whole, sem) if to_hbm else pltpu.make_async_copy(whole, buf, sem)).wait()


def _expert_mlp(xb, wgu_bf, wd_bf, bgu_ref, bd_ref, rows):
    acc = jnp.broadcast_to(bd_ref[0], (rows, D_MODEL))
    cw = 512
    for c in range(D_MODEL // cw):
        gs = slice(c * cw, (c + 1) * cw)
        us = slice(D_MODEL + c * cw, D_MODEL + (c + 1) * cw)
        gt = jnp.dot(xb, wgu_bf[:, gs], preferred_element_type=F32) + bgu_ref[0, :, gs]
        up = jnp.dot(xb, wgu_bf[:, us], preferred_element_type=F32) + bgu_ref[0, :, us]
        gt = jnp.minimum(gt, SWIGLU_LIMIT)
        up = jnp.clip(up, -SWIGLU_LIMIT, SWIGLU_LIMIT)
        act = (up + 1.0) * (gt * jax.nn.sigmoid(gt * SWIGLU_ALPHA))
        acc = acc + jnp.dot(act.astype(BF16), wd_bf[gs, :], preferred_element_type=F32)
    return acc


def _moe_kernel(be_ref, src_cur, src_nxt, dst_prv, dst_cur, h_hbm, wgu_ref, bgu_ref, wd_ref, bd_ref, ys_hbm,
                x0, x1, y0, y1, wgu_bf, wd_bf, gsem, ssem, *, rows, plane_rows, n_tok):
    i = pl.program_id(0)
    last = pl.num_programs(0) - 1

    changed = jnp.logical_or(i == 0, be_ref[i] != be_ref[jnp.maximum(i - 1, 0)])

    @pl.when(changed)
    def _():
        step = 128

        def cast(j, c):
            rs = pl.ds(pl.multiple_of(j * step, step), step)
            wgu_bf[rs, :] = wgu_ref[0, rs, :].astype(BF16)
            wd_bf[rs, :] = wd_ref[0, rs, :].astype(BF16)
            return c

        lax.fori_loop(0, D_MODEL // step, cast, 0)

    def run(par):
        xc, xn = (x0, x1) if par == 0 else (x1, x0)
        yc, yp = (y0, y1) if par == 0 else (y1, y0)
        gather = functools.partial(_start_rows, hbm=h_hbm, rows=rows, to_hbm=False)
        scatter = functools.partial(_start_rows, hbm=ys_hbm, rows=rows, to_hbm=True)
        gathered = functools.partial(_wait_rows, h_hbm, rows=rows, to_hbm=False)
        scattered = functools.partial(_wait_rows, ys_hbm, rows=rows, to_hbm=True)

        @pl.when(i == 0)
        def _():
            yp[...] = jnp.zeros(yp.shape, yp.dtype)
            spare = plane_rows - n_tok
            fills = [pltpu.make_async_copy(yp.at[pl.ds(0, spare)],
                                           ys_hbm.at[pl.ds(k * plane_rows + n_tok, spare)], ssem.at[1 - par])
                     for k in range(TOP_K)]
            for d in fills:
                d.start()
            for d in fills:
                d.wait()
            gather(src_cur, buf=xc, sem=gsem.at[par])

        gathered(xc, gsem.at[par])

        @pl.when(i >= 1)
        def _():
            scattered(yc, ssem.at[par])

        gather(src_nxt, buf=xn, sem=gsem.at[1 - par])
        scatter(dst_prv, buf=yp, sem=ssem.at[1 - par])
        yc[...] = _expert_mlp(xc[...].astype(BF16), wgu_bf, wd_bf, bgu_ref, bd_ref, rows)

        @pl.when(i == last)
        def _():
            scatter(dst_cur, buf=yc, sem=ssem.at[par])
            gathered(xn, gsem.at[1 - par])
            scattered(yp, ssem.at[1 - par])
            scattered(yc, ssem.at[par])

    for par in range(2):
        pl.when(lax.rem(i, 2) == par)(functools.partial(run, par))


def _moe_experts(block_e, ids, h2, w_gate_up, b_gate_up, w_down, b_down, *, rows, plane_rows, n_tok):
    n_blocks = ids.shape[0] - 2
    src = lax.shift_right_logical(ids, 2)
    dst = (ids & (TOP_K - 1)) * plane_rows + src
    idblk = lambda off: pl.BlockSpec((1, 1, rows), lambda i, be: (i + off, 0, 0), memory_space=pltpu.SMEM)
    grid_spec = pltpu.PrefetchScalarGridSpec(
        num_scalar_prefetch=1,
        grid=(n_blocks,),
        in_specs=[
            idblk(1), idblk(2), idblk(0), idblk(1),
            pl.BlockSpec(memory_space=pl.ANY),
            pl.BlockSpec((1, D_MODEL, 2 * D_MODEL), lambda i, be: (be[i], 0, 0)),
            pl.BlockSpec((1, 1, 2 * D_MODEL), lambda i, be: (be[i], 0, 0)),
            pl.BlockSpec((1, D_MODEL, D_MODEL), lambda i, be: (be[i], 0, 0)),
            pl.BlockSpec((1, 1, D_MODEL), lambda i, be: (be[i], 0, 0)),
        ],
        out_specs=pl.BlockSpec(memory_space=pl.ANY),
        scratch_shapes=[pltpu.VMEM((rows, D_MODEL), F32)] * 4 + [
            pltpu.VMEM((D_MODEL, 2 * D_MODEL), BF16),
            pltpu.VMEM((D_MODEL, D_MODEL), BF16),
            pltpu.SemaphoreType.DMA((2,)),
            pltpu.SemaphoreType.DMA((2,))],
    )
    return pl.pallas_call(
        functools.partial(_moe_kernel, rows=rows, plane_rows=plane_rows, n_tok=n_tok),
        grid_spec=grid_spec,
        out_shape=jax.ShapeDtypeStruct((TOP_K * plane_rows, D_MODEL), F32),
        compiler_params=_cparams("arbitrary"),
        name="moe_experts",
    )(block_e, src, src, dst, dst, h2, w_gate_up, b_gate_up[:, None], w_down, b_down[:, None])


def _combine_kernel(ys_ref, x1_ref, gate_ref, lnf_ref, y_ref):
    acc = x1_ref[...]
    for kslot in range(TOP_K):
        acc = acc + gate_ref[:, kslot:kslot + 1] * ys_ref[kslot]
    y_ref[...] = _rms(acc, lnf_ref[...])


def _combine(ys, x1, gates, ln_f, *, tm, n, row0):
    blk0 = row0 // tm
    return pl.pallas_call(
        _combine_kernel,
        grid=(n // tm,),
        in_specs=[
            pl.BlockSpec((TOP_K, tm, D_MODEL), lambda i: (0, i + blk0, 0)),
            pl.BlockSpec((tm, D_MODEL), lambda i: (i + blk0, 0)),
            pl.BlockSpec((tm, LANES), lambda i: (i + blk0, 0)),
            pl.BlockSpec(ln_f.shape, lambda i: (0, 0)),
        ],
        out_specs=pl.BlockSpec((tm, D_MODEL), lambda i: (i, 0)),
        out_shape=jax.ShapeDtypeStruct((n, D_MODEL), F32),
        compiler_params=_cparams("parallel"),
        name="combine",
    )(ys.reshape(TOP_K, -1, D_MODEL), x1, gates, ln_f)


def _route(topi, n_pad, rows):
    n_tok = topi.shape[0]
    e_flat = topi.reshape(-1)
    nk = e_flat.shape[0]
    n_blocks = -(-(nk + N_EXPERTS * (rows - 1)) // rows)
    spare = n_pad - n_tok
    assert 3 * rows <= TOP_K * spare
    experts = jnp.arange(N_EXPERTS, dtype=jnp.int32)
    order = jnp.argsort(e_flat).astype(jnp.int32)
    sizes = jnp.sum((e_flat[:, None] == experts[None, :]).astype(jnp.int32), axis=0)
    padded = (sizes + rows - 1) // rows * rows
    pends = jnp.cumsum(padded)
    slot = jnp.arange(n_blocks * rows, dtype=jnp.int32)
    past = (slot[:, None] >= pends[None, :]).astype(jnp.int32)
    e_slot = jnp.sum(past, axis=1)
    off = slot - jnp.sum(past * padded[None, :], axis=1)
    src = jnp.sum(past * sizes[None, :], axis=1) + off
    size_e = jnp.sum((e_slot[:, None] == experts[None, :]) * sizes[None, :], axis=1)

    def pad_ids(d):
        return (TOP_K * (n_tok + d % spare) + d // spare).astype(jnp.int32)

    ids = jnp.where(off < size_e, order[jnp.clip(src, 0, nk - 1)], pad_ids((slot // rows) % 2 * rows + slot % rows))
    end_ids = pad_ids(2 * rows + jnp.arange(rows, dtype=jnp.int32))
    ids = jnp.concatenate([end_ids, ids, end_ids]).reshape(n_blocks + 2, 1, rows)
    block_e = jnp.minimum(e_slot[::rows], N_EXPERTS - 1).astype(jnp.int32)
    return block_e, ids


def kernel(x_prompt, x_sample, cache_k, cache_v, state_wkv, state_shift, rel_bias, ln1, w_in, rwkv_mu, w0,
           w_decay_up, a0, w_a_up, w_g_up, k_k, k_a, r_k, gn_w, gn_b, w_branch_a, w_branch_b, w_out, ln2,
           w_router, b_router, w_gate_up, b_gate_up, w_down, b_down, ln_f):
    bp, seq, _ = x_prompt.shape
    db = x_sample.shape[0]
    n_p = bp * seq
    n_tot = n_p + db
    n_pad = -(-n_tot // TOKEN_TILE) * TOKEN_TILE
    l = 0

    row = lambda a: a.reshape(1, -1)
    seg = _head_seg()
    prep_params = (row(rwkv_mu[l]), row(w0[l]), w_decay_up[l], row(a0[l]), w_a_up[l], w_g_up[l],
                   row(k_k[l]), row(k_a[l]), row(r_k[l]), seg)
    wr_pad = jnp.pad(w_router[l], ((0, 0), (0, LANES - N_EXPERTS))).astype(BF16)
    br_pad = jnp.pad(b_router[l], (0, LANES - N_EXPERTS), constant_values=NEG).reshape(1, LANES)
    post_small = (row(gn_w[l]), row(gn_b[l]), seg)
    post_tail = (row(ln2[l]), wr_pad, br_pad)

    w_in_bf = w_in[l].astype(BF16)
    consts = (post_small + (w_branch_a[l].astype(BF16), w_branch_b[l].astype(BF16), w_out[l].astype(BF16))
              + post_tail)

    xp = x_prompt.reshape(n_p, D_MODEL)
    q_p, k_p, v_p, rc_p, sg_p = _inproj(xp, row(ln1[l]), w_in_bf, tm=TOKEN_TILE)
    as3 = lambda a: a.reshape(bp, seq, -1)
    oa_p = _attn_prompt(as3(q_p), as3(k_p), as3(v_p), _prompt_bias(rel_bias))
    prep_p = _rwkv_prep_prompt(as3(rc_p), prep_params, tt=TOKEN_TILE)
    r_, w_, k2_, vr_, kk_, ka_, bonus_p, g_p = prep_p
    s0_p = jnp.zeros((bp, R_HEADS, R_HEAD, R_HEAD), F32)
    y_p, wkv_p = _lane_scan(r_, w_, k2_, vr_, kk_, ka_, s0_p, tc=SCAN_CHUNK)
    flat = lambda a: a.reshape(n_p, -1)
    bufs = _post(xp, flat(y_p), flat(bonus_p), flat(g_p), flat(oa_p), sg_p, consts,
                 tm=TOKEN_TILE, n_total=n_pad)

    xs = x_sample.reshape(db, D_MODEL)
    q_s, k_s, v_s, rc_s, sg_s = _inproj(xs, row(ln1[l]), w_in_bf, tm=db)
    bias_s, b0_s = _sample_bias(rel_bias)
    oa_s = _attn_sample(q_s, k_s, v_s, cache_k[l], cache_v[l], bias_s, b0_s)
    prep_s = _rwkv_prep_sample(rc_s, state_shift[l], prep_params)
    sr, sw, sk2, svr, skk, ska, bonus_s, g_s = [a[:, None] for a in prep_s]
    y_s, wkv_s = _rwkv_scan(sr, sw, sk2, svr, skk, ska, state_wkv[l], bb=SCAN_BATCH, tc=1)
    x1_all, h2_all, ti_all, tg_all = _post(xs, y_s[:, 0], bonus_s[:, 0], g_s[:, 0], oa_s, sg_s, consts,
                                           tm=db, n_total=n_pad, row0=n_p, into=bufs)

    block_e, ids = _route(ti_all[:n_tot, :TOP_K], n_pad, MOE_ROWS)
    ys = _moe_experts(block_e, ids, h2_all, w_gate_up[l], b_gate_up[l], w_down[l], b_down[l],
                      rows=MOE_ROWS, plane_rows=n_pad, n_tok=n_tot)
    lnf = row(ln_f)
    y_prompt = _combine(ys, x1_all, tg_all, lnf, tm=TOKEN_TILE, n=n_p, row0=0)
    y_sample = _combine(ys, x1_all, tg_all, lnf, tm=db, n=db, row0=n_p)

    heads = lambda a, b_: a.reshape(1, b_, -1, A_HEADS, HEAD_DIM)
    return (y_prompt.reshape(bp, seq, D_MODEL), y_sample.reshape(db, 1, D_MODEL),
            heads(k_p, bp), heads(v_p, bp), wkv_p[None], as3(rc_p)[:, -1][None],
            heads(k_s, db), heads(v_s, db), wkv_s[None], rc_s[None])
```

```python
import functools
import math

import jax
import jax.numpy as jnp
from jax import lax
from jax.experimental import pallas as pl
from jax.experimental.pallas import tpu as pltpu

F32 = jnp.float32
BF16 = jnp.bfloat16

D_MODEL = 1024
A_HEADS = 8
HEAD_DIM = 64
A_WIDTH = A_HEADS * HEAD_DIM
PATTERNS = ((128, 1), (512, 4), (2048, 16))
BLOCK = 128
NUM_BUCKETS = 32
MAX_DISTANCE = 2048
SCALE = HEAD_DIM ** -0.5
NEG = -1e30
R_HEADS = 8
R_HEAD = 64
R_WIDTH = R_HEADS * R_HEAD
DECAY_LORA = 64
AAA_LORA = 64
GATE_LORA = 128
GN_EPS = 64e-5
N_RWKV_COLS = 3 * R_WIDTH + DECAY_LORA + AAA_LORA + GATE_LORA
N_IN_COLS = 3 * A_WIDTH + N_RWKV_COLS + 2 * D_MODEL
N_EXPERTS = 32
TOP_K = 4
SWIGLU_LIMIT = 7.0
SWIGLU_ALPHA = 1.702
RMS_EPS = 1e-6

LANES = 128
VMEM_LIMIT = 56 * 1024 * 1024
TOKEN_TILE = 256
MOE_ROWS = 256
SCAN_BATCH = 2
SCAN_CHUNK = 64


def _cparams(*sem):
    return pltpu.CompilerParams(dimension_semantics=sem, vmem_limit_bytes=VMEM_LIMIT)


def _rms(x, g):
    return x * lax.rsqrt(jnp.mean(x * x, axis=-1, keepdims=True) + RMS_EPS) * g


def _dot(a, b):
    return jnp.dot(a.astype(BF16), b.astype(BF16), preferred_element_type=F32)


def _bf(x):
    return x.astype(BF16).astype(F32)


def _split3(x):
    hi = x.astype(BF16)
    r1 = x - hi.astype(F32)
    mid = r1.astype(BF16)
    lo = (r1 - mid.astype(F32)).astype(BF16)
    return hi, mid, lo


def _segdot(x, seg):
    hi, mid, lo = _split3(x)
    d = lambda p: jnp.dot(p, seg, preferred_element_type=F32)
    return d(hi) + d(mid) + d(lo)


def _inproj_kernel(x_ref, g_ref, w_ref, q_ref, k_ref, v_ref, rc_ref, sg_ref):
    h = _rms(x_ref[...], g_ref[...]).astype(BF16)

    def mm(c0, c1):
        return _dot(h, w_ref[:, c0:c1])

    c1 = 3 * A_WIDTH
    c2 = c1 + N_RWKV_COLS
    q_ref[...] = mm(0, A_WIDTH)
    k_ref[...] = mm(A_WIDTH, 2 * A_WIDTH)
    v_ref[...] = mm(2 * A_WIDTH, c1)
    rc_ref[...] = mm(c1, c2)
    sg_ref[:, :D_MODEL] = jax.nn.sigmoid(mm(c2, c2 + D_MODEL))
    sg_ref[:, D_MODEL:] = jax.nn.sigmoid(mm(c2 + D_MODEL, N_IN_COLS))


def _inproj(x, ln1, w, *, tm):
    n = x.shape[0]
    row = lambda c: pl.BlockSpec((tm, c), lambda i: (i, 0))
    full = lambda a: pl.BlockSpec(a.shape, lambda i: (0,) * a.ndim)
    out_cols = (A_WIDTH, A_WIDTH, A_WIDTH, N_RWKV_COLS, 2 * D_MODEL)
    return pl.pallas_call(
        _inproj_kernel,
        grid=(n // tm,),
        in_specs=[row(D_MODEL), full(ln1), full(w)],
        out_specs=[row(c) for c in out_cols],
        out_shape=[jax.ShapeDtypeStruct((n, c), F32) for c in out_cols],
        compiler_params=_cparams("parallel"),
        name="inproj",
    )(x, ln1, w)


def _t5_bucket(dist):
    max_exact = NUM_BUCKETS // 2
    d = jnp.maximum(dist, 1).astype(F32)
    large = max_exact + (jnp.log(d / max_exact) / math.log(MAX_DISTANCE / max_exact)
                         * (NUM_BUCKETS - max_exact)).astype(jnp.int32)
    large = jnp.minimum(large, NUM_BUCKETS - 1)
    return jnp.where(dist < max_exact, dist, large)


def _prompt_bias(rel_bias):
    i = jnp.arange(BLOCK)[:, None]
    j = jnp.arange(2 * BLOCK)[None, :]
    delta = i + BLOCK - j
    out = []
    for window, dil in PATTERNS:
        n = window // dil
        band = (delta >= 0) & (delta <= n)
        b = rel_bias.astype(F32)[_t5_bucket(jnp.clip(delta, 0, n) * dil)].transpose(2, 0, 1)
        out.append(jnp.where(band[None], b, NEG))
    return jnp.stack(out, 0)


def _sample_bias(rel_bias):
    rb = rel_bias.astype(F32)
    out = []
    for window, dil in PATTERNS:
        n = window // dil
        jj = n - jnp.arange(n)
        out.append(rb[_t5_bucket(jj * dil)])
    b = jnp.stack(out, 0)[..., None]
    b0 = rb[_t5_bucket(jnp.zeros((1,), jnp.int32))].reshape(A_HEADS, 1)
    return b, b0


def _attn_prompt_kernel(q_ref, k_ref, v_ref, bias_ref, o_ref, os_ref, ls_ref, *, seq):
    nt = (((1,), (1,)), ((), ()))
    npat = len(PATTERNS)
    w2 = 2 * HEAD_DIM
    lo = lax.broadcasted_iota(jnp.int32, (BLOCK, w2), 1) < HEAD_DIM
    key = lax.broadcasted_iota(jnp.int32, (2 * BLOCK, 2 * BLOCK), 1)

    for g, (window, dil) in enumerate(PATTERNS):
        nb = seq // (BLOCK * dil)
        sh = dil.bit_length() - 1

        def body(it, carry, g=g, dil=dil, nb=nb, sh=sh):
            r = it & (dil - 1)
            c = it >> sh
            start = c * (BLOCK * dil) + r
            rows = pl.ds(start, BLOCK, stride=dil) if dil > 1 else pl.ds(start, BLOCK)
            qb = q_ref[0, rows, :]
            kb = k_ref[0, rows, :]
            vb = v_ref[0, rows, :]
            if nb > 1:
                pstart = jnp.maximum(c - 1, 0) * (BLOCK * dil) + r
                prow = pl.ds(pstart, BLOCK, stride=dil) if dil > 1 else pl.ds(pstart, BLOCK)
                kp = k_ref[0, prow, :]
                vp = v_ref[0, prow, :]
            q2 = jnp.concatenate([jnp.where(lo, qb, 0.0), jnp.where(lo, 0.0, qb)], axis=0).astype(BF16)
            if nb > 1:
                k2 = jnp.concatenate([kp, kb], axis=0)
                v2 = jnp.concatenate([vp, vb], axis=0)
                bias = bias_ref[g, 0]
                valid = (bias > 0.5 * NEG) & ((key >= BLOCK) | (c > 0))
            else:
                k2, v2 = kb, vb
                bias = bias_ref[g, 0, :, BLOCK:]
                valid = bias > 0.5 * NEG
            s = lax.dot_general(q2, k2.astype(BF16), nt, preferred_element_type=F32)
            s = jnp.where(valid, s * SCALE + bias, NEG)
            m = jnp.max(s, axis=-1, keepdims=True)
            p = jnp.exp(s - m)
            l = jnp.sum(p, axis=-1, keepdims=True)
            o2 = jnp.dot((p * (1.0 / l)).astype(BF16), v2.astype(BF16), preferred_element_type=F32)
            lse = jnp.broadcast_to(m + jnp.log(l), (2 * BLOCK, w2))
            os_ref[g, rows, :] = jnp.where(lo, o2[:BLOCK], o2[BLOCK:])
            ls_ref[g, rows, :] = jnp.where(lo, lse[:BLOCK], lse[BLOCK:])
            return carry

        lax.fori_loop(0, dil * nb, body, 0, unroll=8)

    chunk = 256

    def merge(ch, carry):
        rows = pl.ds(pl.multiple_of(ch * chunk, chunk), chunk)
        lse = [ls_ref[g, rows, :] for g in range(npat)]
        top = functools.reduce(jnp.maximum, lse)
        e = [jnp.exp(x - top) for x in lse]
        inv = 1.0 / functools.reduce(lambda a, b_: a + b_, e)
        acc = jnp.zeros((chunk, 2 * HEAD_DIM), F32)
        for g in range(npat):
            acc = acc + (e[g] * inv) * os_ref[g, rows, :]
        o_ref[0, rows, :] = acc
        return carry

    lax.fori_loop(0, seq // chunk, merge, 0)


def _attn_prompt(q, k, v, bias):
    b, s, _ = q.shape
    w2 = 2 * HEAD_DIM
    qspec = pl.BlockSpec((1, s, w2), lambda i, p: (i, 0, p))
    return pl.pallas_call(
        functools.partial(_attn_prompt_kernel, seq=s),
        grid=(b, A_HEADS // 2),
        in_specs=[qspec, qspec, qspec,
                  pl.BlockSpec((len(PATTERNS), 1, 2 * BLOCK, 2 * BLOCK), lambda i, p: (0, p, 0, 0))],
        out_specs=qspec,
        out_shape=jax.ShapeDtypeStruct((b, s, A_WIDTH), F32),
        scratch_shapes=[pltpu.VMEM((len(PATTERNS), s, w2), F32)] * 2,
        compiler_params=_cparams("parallel", "parallel"),
        name="attn_prompt",
    )(q, k, v, bias.reshape(len(PATTERNS), A_HEADS // 2, 2 * BLOCK, 2 * BLOCK))


def _attn_sample_kernel(q_ref, kn_ref, vn_ref, *refs):
    npat = len(PATTERNS)
    k_refs, v_refs = refs[:npat], refs[npat:2 * npat]
    bias_ref, b0_ref, o_ref = refs[2 * npat:]
    q = _bf(q_ref[0])
    v_new = _bf(vn_ref[0])
    s0 = jnp.sum(q * _bf(kn_ref[0]), axis=-1, keepdims=True) * SCALE + b0_ref[...]
    outs, lses = [], []
    for g in range(npat):
        kg = _bf(k_refs[g][0, :, 0])
        s = jnp.sum(kg * q[None], axis=-1, keepdims=True) * SCALE + bias_ref[g]
        m = jnp.maximum(jnp.max(s, axis=0), s0)
        l = jnp.sum(jnp.exp(s - m[None]), axis=0) + jnp.exp(s0 - m)
        lse = m + jnp.log(l)
        p = _bf(jnp.exp(s - lse[None]))
        p0 = _bf(jnp.exp(s0 - lse))
        outs.append(jnp.sum(p * _bf(v_refs[g][0, :, 0]), axis=0) + p0 * v_new)
        lses.append(lse)
    top = functools.reduce(jnp.maximum, lses)
    e = [jnp.exp(x - top) for x in lses]
    inv = 1.0 / functools.reduce(lambda a, b_: a + b_, e)
    acc = jnp.zeros((A_HEADS, HEAD_DIM), F32)
    for g in range(npat):
        acc = acc + _bf(e[g] * inv) * _bf(outs[g])
    o_ref[0] = acc


def _attn_sample(q, k_new, v_new, cache_k, cache_v, bias, b0):
    db, wb = cache_k.shape[:2]
    heads = lambda a: a.reshape(db, A_HEADS, HEAD_DIM)
    vec = pl.BlockSpec((1, A_HEADS, HEAD_DIM), lambda i: (i, 0, 0))
    full = lambda a: pl.BlockSpec(a.shape, lambda i: (0,) * a.ndim)
    views, specs = [], []
    for window, dil in PATTERNS:
        start = wb - window
        assert window // dil == BLOCK and start >= 0 and start % (BLOCK * dil) == 0 and wb % dil == 0
        views.append(lambda a, dil=dil: a.reshape(db, wb // dil, dil, A_HEADS, HEAD_DIM))
        specs.append(pl.BlockSpec((1, BLOCK, 1, A_HEADS, HEAD_DIM),
                                  lambda i, blk=start // (BLOCK * dil): (i, blk, 0, 0, 0)))
    args = ([heads(q), heads(k_new), heads(v_new)] + [vw(cache_k) for vw in views] + [vw(cache_v) for vw in views]
            + [bias, b0])
    return pl.pallas_call(
        _attn_sample_kernel,
        grid=(db,),
        in_specs=[vec, vec, vec] + specs + specs + [full(bias), full(b0)],
        out_specs=vec,
        out_shape=jax.ShapeDtypeStruct((db, A_HEADS, HEAD_DIM), F32),
        compiler_params=_cparams("parallel"),
        name="attn_sample",
    )(*args).reshape(db, A_WIDTH)


def _rwkv_prep_math(rc, prev, mu, w0, wd, a0, wa, wg, k_k, k_a, r_k, seg):
    xr = rc + (prev - rc) * mu
    o1, o2, o3 = R_WIDTH, 2 * R_WIDTH, 3 * R_WIDTH
    r = xr[:, :o1]
    kr = xr[:, o1:o2]
    vr = xr[:, o2:o3]
    xw = xr[:, o3:o3 + DECAY_LORA]
    xa = xr[:, o3 + DECAY_LORA:o3 + DECAY_LORA + AAA_LORA]
    xg = xr[:, o3 + DECAY_LORA + AAA_LORA:]
    z = -(w0 + _dot(jnp.tanh(xw), wd))
    softplus = jnp.maximum(z, 0.0) + jnp.log(1.0 + jnp.exp(-jnp.abs(z)))
    decay = jnp.exp(-jnp.exp(-softplus - 0.5))
    a = jax.nn.sigmoid(a0 + _dot(xa, wa))
    g = _dot(jax.nn.sigmoid(xg), wg)
    kk = kr * k_k
    k2 = kr * (1.0 + (a - 1.0) * k_a)
    kk = kk / jnp.maximum(jnp.sqrt(_segdot(kk * kk, seg)), 1e-12)
    bonus = _segdot(r * k2 * r_k, seg) * vr
    return _bf(r), decay, k2, vr, _bf(kk), kk * a, bonus, g


N_PREP_PARAMS = 10


def _rwkv_prep_prompt_kernel(rc_ref, tail_ref, *refs):
    p = [x[...] for x in refs[:N_PREP_PARAMS]]
    outs = refs[N_PREP_PARAMS:]
    rc = rc_ref[0]
    tt = rc.shape[0]
    first = pl.program_id(1) == 0
    prev_row = jnp.where(first, 0.0, tail_ref[0, 7:8, :])
    rolled = pltpu.roll(rc, 1, axis=0)
    row = lax.broadcasted_iota(jnp.int32, (tt, 1), 0)
    prev = jnp.where(row == 0, prev_row, rolled)
    for o, val in zip(outs, _rwkv_prep_math(rc, prev, *p)):
        o[0] = val


def _rwkv_prep_sample_kernel(rc_ref, prev_ref, *refs):
    p = [x[...] for x in refs[:N_PREP_PARAMS]]
    outs = refs[N_PREP_PARAMS:]
    for o, val in zip(outs, _rwkv_prep_math(rc_ref[...], prev_ref[...], *p)):
        o[...] = val


def _head_seg():
    head = jnp.arange(R_WIDTH) // R_HEAD
    return (head[:, None] == head[None, :]).astype(BF16)


def _rwkv_prep_prompt(rc, params, *, tt):
    b, t, _ = rc.shape
    full = lambda a: pl.BlockSpec(a.shape, lambda i, j: (0,) * a.ndim)
    tile = lambda c: pl.BlockSpec((1, tt, c), lambda i, j: (i, j, 0))
    tail = pl.BlockSpec((1, 8, N_RWKV_COLS), lambda i, j: (i, jnp.maximum(j * (tt // 8) - 1, 0), 0))
    return pl.pallas_call(
        _rwkv_prep_prompt_kernel,
        grid=(b, t // tt),
        in_specs=[tile(N_RWKV_COLS), tail] + [full(a) for a in params],
        out_specs=[tile(R_WIDTH)] * 8,
        out_shape=[jax.ShapeDtypeStruct((b, t, R_WIDTH), F32)] * 8,
        compiler_params=_cparams("parallel", "parallel"),
        name="rwkv_prep_prompt",
    )(rc, rc, *params)


def _rwkv_prep_sample(rc, prev, params):
    n = rc.shape[0]
    full = lambda a: pl.BlockSpec(a.shape, lambda i: (0,) * a.ndim)
    return pl.pallas_call(
        _rwkv_prep_sample_kernel,
        grid=(1,),
        in_specs=[full(rc), full(prev)] + [full(a) for a in params],
        out_specs=[pl.BlockSpec((n, R_WIDTH), lambda i: (0, 0))] * 8,
        out_shape=[jax.ShapeDtypeStruct((n, R_WIDTH), F32)] * 8,
        compiler_params=_cparams("arbitrary"),
        name="rwkv_prep_sample",
    )(rc, prev, *params)


def _seg_lane_sum(x, lo_mask):
    lo = jnp.sum(jnp.where(lo_mask, x, 0.0), axis=-1, keepdims=True)
    hi = jnp.sum(jnp.where(lo_mask, 0.0, x), axis=-1, keepdims=True)
    return jnp.where(lo_mask, lo, hi)


def _rwkv_scan_kernel(r_ref, w_ref, k_ref, v_ref, kk_ref, ka_ref, s0_ref, y_ref, sT_ref, st_ref, *, bb, tc):
    pairs = R_HEADS // 2
    w2 = 2 * R_HEAD

    @pl.when(pl.program_id(1) == 0)
    def _():
        for b in range(bb):
            for p in range(pairs):
                st_ref[b, p] = jnp.concatenate([s0_ref[b, 2 * p], s0_ref[b, 2 * p + 1]], axis=1)

    lane = lax.broadcasted_iota(jnp.int32, (R_HEAD, w2), 1)
    sub = lax.broadcasted_iota(jnp.int32, (R_HEAD, w2), 0)
    lo_mask = lane < R_HEAD
    eye2 = (lane & (R_HEAD - 1)) == sub

    grp = min(8, tc)

    def group(tg, carry):
        rows = pl.ds(pl.multiple_of(tg * grp, grp), grp)
        for b in range(bb):
            for p in range(pairs):
                cols = slice(p * w2, (p + 1) * w2)
                r8, w8, k8, v8, kk8, ka8 = (ref[b, rows, cols] for ref in (r_ref, w_ref, k_ref, v_ref, kk_ref, ka_ref))
                s = st_ref[b, p]
                sb = _bf(s)
                ys = []
                for j in range(grp):
                    row = lambda a: a[j:j + 1, :]
                    sa = -_seg_lane_sum(sb * row(kk8), lo_mask)
                    vcol = _seg_lane_sum(jnp.where(eye2, jnp.broadcast_to(row(v8), (R_HEAD, w2)), 0.0), lo_mask)
                    s = s * row(w8) + sa * row(ka8) + vcol * row(k8)
                    sb = _bf(s)
                    yfull = _seg_lane_sum(sb * row(r8), lo_mask)
                    ys.append(jnp.sum(jnp.where(eye2, yfull, 0.0), axis=0, keepdims=True))
                st_ref[b, p] = s
                y_ref[b, rows, cols] = jnp.concatenate(ys, axis=0) if grp > 1 else ys[0]
        return carry

    lax.fori_loop(0, tc // grp, group, 0)

    @pl.when(pl.program_id(1) == pl.num_programs(1) - 1)
    def _():
        for b in range(bb):
            for p in range(pairs):
                s = st_ref[b, p]
                sT_ref[b, 2 * p] = s[:, :R_HEAD]
                sT_ref[b, 2 * p + 1] = s[:, R_HEAD:]


def _rwkv_scan(r, w, k, v, kk, ka, s0, *, bb, tc):
    b, t, _ = r.shape
    seq = pl.BlockSpec((bb, tc, R_WIDTH), lambda i, j: (i, j, 0))
    state = pl.BlockSpec((bb, R_HEADS, R_HEAD, R_HEAD), lambda i, j: (i, 0, 0, 0))
    return pl.pallas_call(
        functools.partial(_rwkv_scan_kernel, bb=bb, tc=tc),
        grid=(b // bb, t // tc),
        in_specs=[seq] * 6 + [state],
        out_specs=[seq, state],
        out_shape=[jax.ShapeDtypeStruct((b, t, R_WIDTH), F32),
                   jax.ShapeDtypeStruct((b, R_HEADS, R_HEAD, R_HEAD), F32)],
        scratch_shapes=[pltpu.VMEM((bb, R_HEADS // 2, R_HEAD, 2 * R_HEAD), F32)],
        compiler_params=_cparams("parallel", "arbitrary"),
        name="rwkv_scan",
    )(r, w, k, v, kk, ka, s0)


CHAINS = LANES // 2
K2 = R_HEAD // 2


def _chain_rows(x):
    b, t, _ = x.shape
    x = x.reshape(b, t, R_HEADS, K2, 2).transpose(1, 3, 4, 0, 2)
    return x.reshape(t, K2, LANES)


def _lane_scan_kernel(kk_ref, w_ref, ka_ref, kx_ref, r_ref, v_ref, s0_ref, y_ref, sT_ref, s_ref, *, tc):
    lo = lax.broadcasted_iota(jnp.int32, (K2, LANES), 1) < CHAINS

    @pl.when(pl.program_id(0) == 0)
    def _():
        for k2 in range(K2):
            s_ref[k2] = s0_ref[k2]

    def both_halves(x):
        return x + pltpu.roll(x, CHAINS, axis=1)

    nhalf = 2
    vh = R_HEAD // nhalf

    def row(ref, t, k2):
        return jnp.broadcast_to(ref[t, k2:k2 + 1, :], (vh, LANES))

    acc0 = []
    for hf in range(nhalf):
        vs = slice(hf * vh, (hf + 1) * vh)
        a = jnp.zeros((vh, LANES), F32)
        for k2 in range(K2):
            a = a + _bf(s_ref[k2, vs, :]) * row(kk_ref, 0, k2)
        acc0.append(a)

    def step(t, acc):
        tn = jnp.minimum(t + 1, tc - 1)
        nxt, ys = [], []
        v_t = v_ref[t]
        v_sw = pltpu.roll(v_t, CHAINS, axis=1)
        for hf in range(nhalf):
            vs = slice(hf * vh, (hf + 1) * vh)
            vv = jnp.where(lo, v_t, v_sw) if hf == 0 else jnp.where(lo, v_sw, v_t)
            sa = -both_halves(acc[hf])
            yacc = jnp.zeros((vh, LANES), F32)
            nacc = jnp.zeros((vh, LANES), F32)
            for k2 in range(K2):
                s = s_ref[k2, vs, :] * row(w_ref, t, k2) + sa * row(ka_ref, t, k2) + vv * row(kx_ref, t, k2)
                s_ref[k2, vs, :] = s
                sb = _bf(s)
                yacc = yacc + sb * row(r_ref, t, k2)
                nacc = nacc + sb * row(kk_ref, tn, k2)
            ys.append(both_halves(yacc))
            nxt.append(nacc)
        y_ref[t] = jnp.where(lo, ys[0], ys[1])
        return tuple(nxt)

    lax.fori_loop(0, tc, step, tuple(acc0))

    @pl.when(pl.program_id(0) == pl.num_programs(0) - 1)
    def _():
        for k2 in range(K2):
            sT_ref[k2] = s_ref[k2]


def _lane_scan(r, w, k, v, kk, ka, s0, *, tc):
    b, t, _ = r.shape
    assert b * R_HEADS == CHAINS
    ops = [_chain_rows(x) for x in (kk, w, ka, k, r, v)]
    s0c = s0.reshape(b, R_HEADS, K2, 2, K2, 2).transpose(4, 3, 2, 5, 0, 1).reshape(K2, R_HEAD, LANES)
    rows = pl.BlockSpec((tc, K2, LANES), lambda i: (i, 0, 0))
    state = pl.BlockSpec((K2, R_HEAD, LANES), lambda i: (0, 0, 0))
    y, st = pl.pallas_call(
        functools.partial(_lane_scan_kernel, tc=tc),
        grid=(t // tc,),
        in_specs=[rows] * 6 + [state],
        out_specs=[rows, state],
        out_shape=[jax.ShapeDtypeStruct((t, K2, LANES), F32), jax.ShapeDtypeStruct((K2, R_HEAD, LANES), F32)],
        scratch_shapes=[pltpu.VMEM((K2, R_HEAD, LANES), F32)],
        compiler_params=_cparams("arbitrary"),
        name="rwkv_lane_scan",
    )(*ops, s0c)
    y = y.reshape(t, K2, 2, b, R_HEADS).transpose(3, 0, 4, 1, 2).reshape(b, t, R_WIDTH)
    st = st.reshape(K2, 2, K2, 2, b, R_HEADS).transpose(4, 5, 2, 1, 0, 3).reshape(b, R_HEADS, R_HEAD, R_HEAD)
    return y, st


def _post_kernel(x_ref, y_ref, bonus_ref, g_ref, oa_ref, sg_ref, gnw_ref, gnb_ref, seg_ref, wba_ref, wbb_ref,
                 wout_ref, ln2_ref, wr_ref, br_ref, *rest, aliased, n_main):
    outs = rest[aliased:]
    x1_ref, h2_ref, ti_ref, tg_ref = outs

    @pl.when(pl.program_id(0) >= n_main)
    def _():
        for o in outs:
            o[...] = jnp.zeros(o.shape, o.dtype)

    @pl.when(pl.program_id(0) < n_main)
    def _():
        _post_body(x_ref, y_ref, bonus_ref, g_ref, oa_ref, sg_ref, gnw_ref, gnb_ref, seg_ref, wba_ref, wbb_ref,
                   wout_ref, ln2_ref, wr_ref, br_ref, x1_ref, h2_ref, ti_ref, tg_ref)


def _post_body(x_ref, y_ref, bonus_ref, g_ref, oa_ref, sg_ref, gnw_ref, gnb_ref, seg_ref, wba_ref, wbb_ref,
               wout_ref, ln2_ref, wr_ref, br_ref, x1_ref, h2_ref, ti_ref, tg_ref):
    y = y_ref[...]
    seg = seg_ref[...]
    mu = _segdot(y, seg) * (1.0 / R_HEAD)
    yc = y - mu
    var = _segdot(yc * yc, seg) * (1.0 / R_HEAD)
    yn = yc * lax.rsqrt(var + GN_EPS) * gnw_ref[...] + gnb_ref[...]
    o_b = (yn + bonus_ref[...]) * g_ref[...]
    mixed = (sg_ref[:, :D_MODEL] * _dot(oa_ref[...], wba_ref[...])
             + sg_ref[:, D_MODEL:] * _dot(o_b, wbb_ref[...]))
    x1 = x_ref[...] + _dot(mixed, wout_ref[...])
    x1_ref[...] = x1
    h2 = _rms(x1, ln2_ref[...])
    h2_ref[...] = h2
    logits = _dot(h2, wr_ref[...]) + br_ref[...]
    lane = lax.broadcasted_iota(jnp.int32, logits.shape, 1).astype(F32)
    work = logits
    vals, idxs = [], []
    for _ in range(TOP_K):
        m = jnp.max(work, axis=-1, keepdims=True)
        idx = jnp.min(jnp.where(work == m, lane, float(LANES)), axis=-1, keepdims=True)
        vals.append(m)
        idxs.append(idx)
        work = jnp.where(lane == idx, -jnp.inf, work)
    es = [jnp.exp(v - vals[0]) for v in vals]
    tot = es[0] + es[1] + es[2] + es[3]
    ti = jnp.zeros(logits.shape, F32)
    tg = jnp.zeros(logits.shape, F32)
    for kslot in range(TOP_K):
        ti = jnp.where(lane == float(kslot), idxs[kslot], ti)
        tg = jnp.where(lane == float(kslot), es[kslot] / tot, tg)
    ti_ref[...] = ti.astype(jnp.int32)
    tg_ref[...] = tg


def _post(x, y, bonus, g, oa, sg, consts, *, tm, n_total, row0=0, into=None):
    n = x.shape[0]
    blk0 = row0 // tm
    n_main = n // tm
    steps = n_main + (1 if into is None and n_total > n else 0)
    row = lambda c: pl.BlockSpec((tm, c), lambda i: (jnp.minimum(i, n_main - 1), 0))
    orow = lambda c: pl.BlockSpec((tm, c), lambda i: (i + blk0, 0))
    full = lambda a: pl.BlockSpec(a.shape, lambda i: (0,) * a.ndim)
    out_cols = ((D_MODEL, F32), (D_MODEL, F32), (LANES, jnp.int32), (LANES, F32))
    ins = [x, y, bonus, g, oa, sg, *consts]
    in_specs = [row(D_MODEL), row(R_WIDTH), row(R_WIDTH), row(R_WIDTH), row(A_WIDTH), row(2 * D_MODEL)]
    in_specs += [full(a) for a in consts]
    aliases = {}
    if into is not None:
        aliases = {len(ins) + i: i for i in range(len(into))}
        in_specs += [pl.BlockSpec(memory_space=pl.ANY)] * len(into)
        ins += list(into)
    return pl.pallas_call(
        functools.partial(_post_kernel, aliased=len(aliases), n_main=n_main),
        grid=(steps,),
        in_specs=in_specs,
        out_specs=[orow(c) for c, _ in out_cols],
        out_shape=[jax.ShapeDtypeStruct((n_total, c), dt) for c, dt in out_cols],
        input_output_aliases=aliases,
        compiler_params=_cparams("parallel"),
        name="post",
    )(*ins)


def _start_rows(rows_ref, hbm, buf, sem, *, rows, to_hbm):
    for r in range(rows):
        at = pl.ds(rows_ref[0, 0, r], 1)
        prio = r % 2
        if to_hbm:
            pltpu.make_async_copy(buf.at[pl.ds(r, 1)], hbm.at[at], sem).start(priority=prio)
        else:
            pltpu.make_async_copy(hbm.at[at], buf.at[pl.ds(r, 1)], sem).start(priority=prio)


def _wait_rows(hbm, buf, sem, *, rows, to_hbm):
    whole = hbm.at[pl.ds(0, rows)]
    (pltpu.make_async_copy(buf, whole, sem) if to_hbm else pltpu.make_async_copy(whole, buf, sem)).wait()


def _expert_mlp(xb, wgu_bf, wd_bf, bgu_ref, bd_ref, rows):
    acc = jnp.broadcast_to(bd_ref[0], (rows, D_MODEL))
    cw = 512
    for c in range(D_MODEL // cw):
        gs = slice(c * cw, (c + 1) * cw)
        us = slice(D_MODEL + c * cw, D_MODEL + (c + 1) * cw)
        gt = jnp.dot(xb, wgu_bf[:, gs], preferred_element_type=F32) + bgu_ref[0, :, gs]
        up = jnp.dot(xb, wgu_bf[:, us], preferred_element_type=F32) + bgu_ref[0, :, us]
        gt = jnp.minimum(gt, SWIGLU_LIMIT)
        up = jnp.clip(up, -SWIGLU_LIMIT, SWIGLU_LIMIT)
        act = (up + 1.0) * (gt * jax.nn.sigmoid(gt * SWIGLU_ALPHA))
        acc = acc + jnp.dot(act.astype(BF16), wd_bf[gs, :], preferred_element_type=F32)
    return acc


def _moe_kernel(be_ref, src_cur, src_nxt, dst_prv, dst_cur, h_hbm, wgu_ref, bgu_ref, wd_ref, bd_ref, ys_hbm,
                x0, x1, y0, y1, wgu_bf, wd_bf, gsem, ssem, *, rows, plane_rows, n_tok):
    i = pl.program_id(0)
    last = pl.num_programs(0) - 1

    changed = jnp.logical_or(i == 0, be_ref[i] != be_ref[jnp.maximum(i - 1, 0)])

    @pl.when(changed)
    def _():
        step = 128

        def cast(j, c):
            rs = pl.ds(pl.multiple_of(j * step, step), step)
            wgu_bf[rs, :] = wgu_ref[0, rs, :].astype(BF16)
            wd_bf[rs, :] = wd_ref[0, rs, :].astype(BF16)
            return c

        lax.fori_loop(0, D_MODEL // step, cast, 0)

    def run(par):
        xc, xn = (x0, x1) if par == 0 else (x1, x0)
        yc, yp = (y0, y1) if par == 0 else (y1, y0)
        gather = functools.partial(_start_rows, hbm=h_hbm, rows=rows, to_hbm=False)
        scatter = functools.partial(_start_rows, hbm=ys_hbm, rows=rows, to_hbm=True)
        gathered = functools.partial(_wait_rows, h_hbm, rows=rows, to_hbm=False)
        scattered = functools.partial(_wait_rows, ys_hbm, rows=rows, to_hbm=True)

        @pl.when(i == 0)
        def _():
            yp[...] = jnp.zeros(yp.shape, yp.dtype)
            spare = plane_rows - n_tok
            fills = [pltpu.make_async_copy(yp.at[pl.ds(0, spare)],
                                           ys_hbm.at[pl.ds(k * plane_rows + n_tok, spare)], ssem.at[1 - par])
                     for k in range(TOP_K)]
            for d in fills:
                d.start()
            for d in fills:
                d.wait()
            gather(src_cur, buf=xc, sem=gsem.at[par])

        gathered(xc, gsem.at[par])

        @pl.when(i >= 1)
        def _():
            scattered(yc, ssem.at[par])

        gather(src_nxt, buf=xn, sem=gsem.at[1 - par])
        scatter(dst_prv, buf=yp, sem=ssem.at[1 - par])
        yc[...] = _expert_mlp(xc[...].astype(BF16), wgu_bf, wd_bf, bgu_ref, bd_ref, rows)

        @pl.when(i == last)
        def _():
            scatter(dst_cur, buf=yc, sem=ssem.at[par])
            gathered(xn, gsem.at[1 - par])
            scattered(yp, ssem.at[1 - par])
            scattered(yc, ssem.at[par])

    for par in range(2):
        pl.when(lax.rem(i, 2) == par)(functools.partial(run, par))


def _moe_experts(block_e, ids, h2, w_gate_up, b_gate_up, w_down, b_down, *, rows, plane_rows, n_tok):
    n_blocks = ids.shape[0] - 2
    ids = lax.optimization_barrier(ids)
    src = lax.shift_right_logical(ids, 2)
    dst = (ids & (TOP_K - 1)) * plane_rows + src
    idblk = lambda off: pl.BlockSpec((1, 1, rows), lambda i, be: (i + off, 0, 0), memory_space=pltpu.SMEM)
    grid_spec = pltpu.PrefetchScalarGridSpec(
        num_scalar_prefetch=1,
        grid=(n_blocks,),
        in_specs=[
            idblk(1), idblk(2), idblk(0), idblk(1),
            pl.BlockSpec(memory_space=pl.ANY),
            pl.BlockSpec((1, D_MODEL, 2 * D_MODEL), lambda i, be: (be[i], 0, 0)),
            pl.BlockSpec((1, 1, 2 * D_MODEL), lambda i, be: (be[i], 0, 0)),
            pl.BlockSpec((1, D_MODEL, D_MODEL), lambda i, be: (be[i], 0, 0)),
            pl.BlockSpec((1, 1, D_MODEL), lambda i, be: (be[i], 0, 0)),
        ],
        out_specs=pl.BlockSpec(memory_space=pl.ANY),
        scratch_shapes=[pltpu.VMEM((rows, D_MODEL), F32)] * 4 + [
            pltpu.VMEM((D_MODEL, 2 * D_MODEL), BF16),
            pltpu.VMEM((D_MODEL, D_MODEL), BF16),
            pltpu.SemaphoreType.DMA((2,)),
            pltpu.SemaphoreType.DMA((2,))],
    )
    return pl.pallas_call(
        functools.partial(_moe_kernel, rows=rows, plane_rows=plane_rows, n_tok=n_tok),
        grid_spec=grid_spec,
        out_shape=jax.ShapeDtypeStruct((TOP_K * plane_rows, D_MODEL), F32),
        compiler_params=_cparams("arbitrary"),
        name="moe_experts",
    )(block_e, src, src, dst, dst, h2, w_gate_up, b_gate_up[:, None], w_down, b_down[:, None])


def _combine_kernel(ys_ref, x1_ref, gate_ref, lnf_ref, y_ref):
    acc = x1_ref[...]
    for kslot in range(TOP_K):
        acc = acc + gate_ref[:, kslot:kslot + 1] * ys_ref[kslot]
    y_ref[...] = _rms(acc, lnf_ref[...])


def _combine(ys, x1, gates, ln_f, *, tm, n, row0):
    blk0 = row0 // tm
    return pl.pallas_call(
        _combine_kernel,
        grid=(n // tm,),
        in_specs=[
            pl.BlockSpec((TOP_K, tm, D_MODEL), lambda i: (0, i + blk0, 0)),
            pl.BlockSpec((tm, D_MODEL), lambda i: (i + blk0, 0)),
            pl.BlockSpec((tm, LANES), lambda i: (i + blk0, 0)),
            pl.BlockSpec(ln_f.shape, lambda i: (0, 0)),
        ],
        out_specs=pl.BlockSpec((tm, D_MODEL), lambda i: (i, 0)),
        out_shape=jax.ShapeDtypeStruct((n, D_MODEL), F32),
        compiler_params=_cparams("parallel"),
        name="combine",
    )(ys.reshape(TOP_K, -1, D_MODEL), x1, gates, ln_f)


def _route(topi, n_pad, rows):
    n_tok = topi.shape[0]
    e_flat = topi.reshape(-1)
    nk = e_flat.shape[0]
    n_blocks = -(-(nk + N_EXPERTS * (rows - 1)) // rows)
    spare = n_pad - n_tok
    assert 3 * rows <= TOP_K * spare
    experts = jnp.arange(N_EXPERTS, dtype=jnp.int32)
    order = jnp.argsort(e_flat).astype(jnp.int32)
    sizes = jnp.sum((e_flat[:, None] == experts[None, :]).astype(jnp.int32), axis=0)
    padded = (sizes + rows - 1) // rows * rows
    pends = jnp.cumsum(padded)
    slot = jnp.arange(n_blocks * rows, dtype=jnp.int32)
    past = (slot[:, None] >= pends[None, :]).astype(jnp.int32)
    e_slot = jnp.sum(past, axis=1)
    off = slot - jnp.sum(past * padded[None, :], axis=1)
    src = jnp.sum(past * sizes[None, :], axis=1) + off
    size_e = jnp.sum((e_slot[:, None] == experts[None, :]) * sizes[None, :], axis=1)

    def pad_ids(d):
        return (TOP_K * (n_tok + d % spare) + d // spare).astype(jnp.int32)

    ids = jnp.where(off < size_e, order[jnp.clip(src, 0, nk - 1)], pad_ids((slot // rows) % 2 * rows + slot % rows))
    end_ids = pad_ids(2 * rows + jnp.arange(rows, dtype=jnp.int32))
    ids = jnp.concatenate([end_ids, ids, end_ids]).reshape(n_blocks + 2, 1, rows)
    block_e = jnp.minimum(e_slot[::rows], N_EXPERTS - 1).astype(jnp.int32)
    return block_e, ids


def kernel(x_prompt, x_sample, cache_k, cache_v, state_wkv, state_shift, rel_bias, ln1, w_in, rwkv_mu, w0,
           w_decay_up, a0, w_a_up, w_g_up, k_k, k_a, r_k, gn_w, gn_b, w_branch_a, w_branch_b, w_out, ln2,
           w_router, b_router, w_gate_up, b_gate_up, w_down, b_down, ln_f):
    bp, seq, _ = x_prompt.shape
    db = x_sample.shape[0]
    n_p = bp * seq
    n_tot = n_p + db
    n_pad = -(-n_tot // TOKEN_TILE) * TOKEN_TILE
    l = 0

    row = lambda a: a.reshape(1, -1)
    seg = _head_seg()
    prep_params = (row(rwkv_mu[l]), row(w0[l]), w_decay_up[l], row(a0[l]), w_a_up[l], w_g_up[l],
                   row(k_k[l]), row(k_a[l]), row(r_k[l]), seg)
    wr_pad = jnp.pad(w_router[l], ((0, 0), (0, LANES - N_EXPERTS))).astype(BF16)
    br_pad = jnp.pad(b_router[l], (0, LANES - N_EXPERTS), constant_values=NEG).reshape(1, LANES)
    post_small = (row(gn_w[l]), row(gn_b[l]), seg)
    post_tail = (row(ln2[l]), wr_pad, br_pad)

    w_in_bf = w_in[l].astype(BF16)
    consts = (post_small + (w_branch_a[l].astype(BF16), w_branch_b[l].astype(BF16), w_out[l].astype(BF16))
              + post_tail)

    xp = x_prompt.reshape(n_p, D_MODEL)
    q_p, k_p, v_p, rc_p, sg_p = _inproj(xp, row(ln1[l]), w_in_bf, tm=TOKEN_TILE)
    as3 = lambda a: a.reshape(bp, seq, -1)
    oa_p = _attn_prompt(as3(q_p), as3(k_p), as3(v_p), _prompt_bias(rel_bias))
    prep_p = _rwkv_prep_prompt(as3(rc_p), prep_params, tt=TOKEN_TILE)
    r_, w_, k2_, vr_, kk_, ka_, bonus_p, g_p = prep_p
    s0_p = jnp.zeros((bp, R_HEADS, R_HEAD, R_HEAD), F32)
    y_p, wkv_p = _lane_scan(r_, w_, k2_, vr_, kk_, ka_, s0_p, tc=SCAN_CHUNK)
    flat = lambda a: a.reshape(n_p, -1)
    bufs = _post(xp, flat(y_p), flat(bonus_p), flat(g_p), flat(oa_p), sg_p, consts,
                 tm=TOKEN_TILE, n_total=n_pad)

    xs = x_sample.reshape(db, D_MODEL)
    q_s, k_s, v_s, rc_s, sg_s = _inproj(xs, row(ln1[l]), w_in_bf, tm=db)
    bias_s, b0_s = _sample_bias(rel_bias)
    oa_s = _attn_sample(q_s, k_s, v_s, cache_k[l], cache_v[l], bias_s, b0_s)
    prep_s = _rwkv_prep_sample(rc_s, state_shift[l], prep_params)
    sr, sw, sk2, svr, skk, ska, bonus_s, g_s = [a[:, None] for a in prep_s]
    y_s, wkv_s = _rwkv_scan(sr, sw, sk2, svr, skk, ska, state_wkv[l], bb=SCAN_BATCH, tc=1)
    x1_all, h2_all, ti_all, tg_all = _post(xs, y_s[:, 0], bonus_s[:, 0], g_s[:, 0], oa_s, sg_s, consts,
                                           tm=db, n_total=n_pad, row0=n_p, into=bufs)

    block_e, ids = _route(ti_all[:n_tot, :TOP_K], n_pad, MOE_ROWS)
    ys = _moe_experts(block_e, ids, h2_all, w_gate_up[l], b_gate_up[l], w_down[l], b_down[l],
                      rows=MOE_ROWS, plane_rows=n_pad, n_tok=n_tot)
    lnf = row(ln_f)
    y_prompt = _combine(ys, x1_all, tg_all, lnf, tm=TOKEN_TILE, n=n_p, row0=0)
    y_sample = _combine(ys, x1_all, tg_all, lnf, tm=db, n=db, row0=n_p)

    heads = lambda a, b_: a.reshape(1, b_, -1, A_HEADS, HEAD_DIM)
    return (y_prompt.reshape(bp, seq, D_MODEL), y_sample.reshape(db, 1, D_MODEL),
            heads(k_p, bp), heads(v_p, bp), wkv_p[None], as3(rc_p)[:, -1][None],
            heads(k_s, db), heads(v_s, db), wkv_s[None], rc_s[None])
```

```python
import functools
import math

import jax
import jax.numpy as jnp
from jax import lax
from jax.experimental import pallas as pl
from jax.experimental.pallas import tpu as pltpu

F32 = jnp.float32
BF16 = jnp.bfloat16

D_MODEL = 1024
A_HEADS = 8
HEAD_DIM = 64
A_WIDTH = A_HEADS * HEAD_DIM
PATTERNS = ((128, 1), (512, 4), (2048, 16))
BLOCK = 128
NUM_BUCKETS = 32
MAX_DISTANCE = 2048
SCALE = HEAD_DIM ** -0.5
NEG = -1e30
R_HEADS = 8
R_HEAD = 64
R_WIDTH = R_HEADS * R_HEAD
DECAY_LORA = 64
AAA_LORA = 64
GATE_LORA = 128
GN_EPS = 64e-5
N_RWKV_COLS = 3 * R_WIDTH + DECAY_LORA + AAA_LORA + GATE_LORA
N_IN_COLS = 3 * A_WIDTH + N_RWKV_COLS + 2 * D_MODEL
N_EXPERTS = 32
TOP_K = 4
SWIGLU_LIMIT = 7.0
SWIGLU_ALPHA = 1.702
RMS_EPS = 1e-6

LANES = 128
VMEM_LIMIT = 56 * 1024 * 1024
TOKEN_TILE = 256
MOE_ROWS = 256
SCAN_BATCH = 2
SCAN_CHUNK = 64


def _cparams(*sem):
    return pltpu.CompilerParams(dimension_semantics=sem, vmem_limit_bytes=VMEM_LIMIT)


def _rms(x, g):
    return x * lax.rsqrt(jnp.mean(x * x, axis=-1, keepdims=True) + RMS_EPS) * g


def _dot(a, b):
    return jnp.dot(a.astype(BF16), b.astype(BF16), preferred_element_type=F32)


def _bf(x):
    return x.astype(BF16).astype(F32)


def _split3(x):
    hi = x.astype(BF16)
    r1 = x - hi.astype(F32)
    mid = r1.astype(BF16)
    lo = (r1 - mid.astype(F32)).astype(BF16)
    return hi, mid, lo


def _segdot(x, seg):
    hi, mid, lo = _split3(x)
    d = lambda p: jnp.dot(p, seg, preferred_element_type=F32)
    return d(hi) + d(mid) + d(lo)


def _inproj_kernel(x_ref, g_ref, w_ref, q_ref, k_ref, v_ref, rc_ref, sg_ref):
    h = _rms(x_ref[...], g_ref[...]).astype(BF16)

    def mm(c0, c1):
        return _dot(h, w_ref[:, c0:c1])

    c1 = 3 * A_WIDTH
    c2 = c1 + N_RWKV_COLS
    q_ref[...] = mm(0, A_WIDTH)
    k_ref[...] = mm(A_WIDTH, 2 * A_WIDTH)
    v_ref[...] = mm(2 * A_WIDTH, c1)
    rc_ref[...] = mm(c1, c2)
    sg_ref[:, :D_MODEL] = jax.nn.sigmoid(mm(c2, c2 + D_MODEL))
    sg_ref[:, D_MODEL:] = jax.nn.sigmoid(mm(c2 + D_MODEL, N_IN_COLS))


def _inproj(x, ln1, w, *, tm):
    n = x.shape[0]
    row = lambda c: pl.BlockSpec((tm, c), lambda i: (i, 0))
    full = lambda a: pl.BlockSpec(a.shape, lambda i: (0,) * a.ndim)
    out_cols = (A_WIDTH, A_WIDTH, A_WIDTH, N_RWKV_COLS, 2 * D_MODEL)
    return pl.pallas_call(
        _inproj_kernel,
        grid=(n // tm,),
        in_specs=[row(D_MODEL), full(ln1), full(w)],
        out_specs=[row(c) for c in out_cols],
        out_shape=[jax.ShapeDtypeStruct((n, c), F32) for c in out_cols],
        compiler_params=_cparams("parallel"),
        name="inproj",
    )(x, ln1, w)


def _t5_bucket(dist):
    max_exact = NUM_BUCKETS // 2
    d = jnp.maximum(dist, 1).astype(F32)
    large = max_exact + (jnp.log(d / max_exact) / math.log(MAX_DISTANCE / max_exact)
                         * (NUM_BUCKETS - max_exact)).astype(jnp.int32)
    large = jnp.minimum(large, NUM_BUCKETS - 1)
    return jnp.where(dist < max_exact, dist, large)


def _bias_lookup(rel_bias, dist):
    onehot = (_t5_bucket(dist)[..., None] == jnp.arange(NUM_BUCKETS)).astype(F32)
    return jnp.moveaxis(jnp.dot(onehot, rel_bias.astype(F32), precision=lax.Precision.HIGHEST), -1, 0)


def _prompt_bias(rel_bias):
    i = jnp.arange(BLOCK)[:, None]
    j = jnp.arange(2 * BLOCK)[None, :]
    delta = i + BLOCK - j
    out = []
    for window, dil in PATTERNS:
        n = window // dil
        band = (delta >= 0) & (delta <= n)
        out.append(jnp.where(band[None], _bias_lookup(rel_bias, jnp.clip(delta, 0, n) * dil), NEG))
    return jnp.stack(out, 0)


def _sample_bias(rel_bias, wb):
    dist = wb - jnp.arange(wb)
    out = []
    for window, dil in PATTERNS:
        member = (dist % dil == 0) & (dist <= window)
        out.append(jnp.where(member[None], _bias_lookup(rel_bias, dist), NEG))
    b0 = _bias_lookup(rel_bias, jnp.zeros((1,), jnp.int32))
    return jnp.stack(out, 0), b0


def _attn_prompt_kernel(q_ref, k_ref, v_ref, bias_ref, o_ref, os_ref, ls_ref, *, seq):
    nt = (((1,), (1,)), ((), ()))
    npat = len(PATTERNS)
    w2 = 2 * HEAD_DIM
    lo = lax.broadcasted_iota(jnp.int32, (BLOCK, w2), 1) < HEAD_DIM
    key = lax.broadcasted_iota(jnp.int32, (2 * BLOCK, 2 * BLOCK), 1)

    for g, (window, dil) in enumerate(PATTERNS):
        nb = seq // (BLOCK * dil)
        sh = dil.bit_length() - 1

        def body(it, carry, g=g, dil=dil, nb=nb, sh=sh):
            r = it & (dil - 1)
            c = it >> sh
            start = c * (BLOCK * dil) + r
            rows = pl.ds(start, BLOCK, stride=dil) if dil > 1 else pl.ds(start, BLOCK)
            qb = q_ref[0, rows, :]
            kb = k_ref[0, rows, :]
            vb = v_ref[0, rows, :]
            if nb > 1:
                pstart = jnp.maximum(c - 1, 0) * (BLOCK * dil) + r
                prow = pl.ds(pstart, BLOCK, stride=dil) if dil > 1 else pl.ds(pstart, BLOCK)
                kp = k_ref[0, prow, :]
                vp = v_ref[0, prow, :]
            q2 = jnp.concatenate([jnp.where(lo, qb, 0.0), jnp.where(lo, 0.0, qb)], axis=0).astype(BF16)
            if nb > 1:
                k2 = jnp.concatenate([kp, kb], axis=0)
                v2 = jnp.concatenate([vp, vb], axis=0)
                bias = bias_ref[g, 0]
                valid = (bias > 0.5 * NEG) & ((key >= BLOCK) | (c > 0))
            else:
                k2, v2 = kb, vb
                bias = bias_ref[g, 0, :, BLOCK:]
                valid = bias > 0.5 * NEG
            s = lax.dot_general(q2, k2.astype(BF16), nt, preferred_element_type=F32)
            s = jnp.where(valid, s * SCALE + bias, NEG)
            m = jnp.max(s, axis=-1, keepdims=True)
            p = jnp.exp(s - m)
            l = jnp.sum(p, axis=-1, keepdims=True)
            o2 = jnp.dot((p * (1.0 / l)).astype(BF16), v2.astype(BF16), preferred_element_type=F32)
            lse = jnp.broadcast_to(m + jnp.log(l), (2 * BLOCK, w2))
            os_ref[g, rows, :] = jnp.where(lo, o2[:BLOCK], o2[BLOCK:])
            ls_ref[g, rows, :] = jnp.where(lo, lse[:BLOCK], lse[BLOCK:])
            return carry

        lax.fori_loop(0, dil * nb, body, 0, unroll=8)

    chunk = 256

    def merge(ch, carry):
        rows = pl.ds(pl.multiple_of(ch * chunk, chunk), chunk)
        lse = [ls_ref[g, rows, :] for g in range(npat)]
        top = functools.reduce(jnp.maximum, lse)
        e = [jnp.exp(x - top) for x in lse]
        inv = 1.0 / functools.reduce(lambda a, b_: a + b_, e)
        acc = jnp.zeros((chunk, 2 * HEAD_DIM), F32)
        for g in range(npat):
            acc = acc + (e[g] * inv) * os_ref[g, rows, :]
        o_ref[0, rows, :] = acc
        return carry

    lax.fori_loop(0, seq // chunk, merge, 0)


def _attn_prompt(q, k, v, bias):
    b, s, _ = q.shape
    w2 = 2 * HEAD_DIM
    qspec = pl.BlockSpec((1, s, w2), lambda i, p: (i, 0, p))
    return pl.pallas_call(
        functools.partial(_attn_prompt_kernel, seq=s),
        grid=(b, A_HEADS // 2),
        in_specs=[qspec, qspec, qspec,
                  pl.BlockSpec((len(PATTERNS), 1, 2 * BLOCK, 2 * BLOCK), lambda i, p: (0, p, 0, 0))],
        out_specs=qspec,
        out_shape=jax.ShapeDtypeStruct((b, s, A_WIDTH), F32),
        scratch_shapes=[pltpu.VMEM((len(PATTERNS), s, w2), F32)] * 2,
        compiler_params=_cparams("parallel", "parallel"),
        name="attn_prompt",
    )(q, k, v, bias.reshape(len(PATTERNS), A_HEADS // 2, 2 * BLOCK, 2 * BLOCK))


def _attn_sample_kernel(q_ref, kn_ref, vn_ref, kt_ref, vt_ref, bias_ref, b0_ref, o_ref):
    npat = len(PATTERNS)
    nt = (((1,), (1,)), ((), ()))
    q = q_ref[0]
    v_new = _bf(vn_ref[0])
    s0 = jnp.sum(_bf(q) * _bf(kn_ref[0]), axis=-1, keepdims=True) * SCALE + b0_ref[...]
    s = jnp.concatenate([_dot(q[h:h + 1], kt_ref[0, h]) for h in range(A_HEADS)], axis=0) * SCALE
    ps, p0s, lses = [], [], []
    for g in range(npat):
        bias = bias_ref[g]
        sg = jnp.where(bias > 0.5 * NEG, s + bias, NEG)
        m = jnp.maximum(jnp.max(sg, axis=-1, keepdims=True), s0)
        l = jnp.sum(jnp.exp(sg - m), axis=-1, keepdims=True) + jnp.exp(s0 - m)
        lse = m + jnp.log(l)
        ps.append(jnp.exp(sg - lse))
        p0s.append(_bf(jnp.exp(s0 - lse)))
        lses.append(lse)
    top = functools.reduce(jnp.maximum, lses)
    e = [jnp.exp(x - top) for x in lses]
    inv = 1.0 / functools.reduce(lambda a, b_: a + b_, e)
    w = [_bf(x * inv) for x in e]
    rows = []
    for h in range(A_HEADS):
        p_h = jnp.concatenate([p[h:h + 1] for p in ps], axis=0)
        o_h = lax.dot_general(p_h.astype(BF16), vt_ref[0, h].astype(BF16), nt, preferred_element_type=F32)
        acc = jnp.zeros((1, HEAD_DIM), F32)
        for g in range(npat):
            acc = acc + w[g][h:h + 1] * _bf(o_h[g:g + 1] + p0s[g][h:h + 1] * v_new[h:h + 1])
        rows.append(acc)
    o_ref[0] = jnp.concatenate(rows, axis=0)


def _attn_sample(q, k_new, v_new, cache_k, cache_v, bias, b0):
    db, wb = cache_k.shape[:2]
    heads = lambda a: a.reshape(db, A_HEADS, HEAD_DIM)
    vec = pl.BlockSpec((1, A_HEADS, HEAD_DIM), lambda i: (i, 0, 0))
    full = lambda a: pl.BlockSpec(a.shape, lambda i: (0,) * a.ndim)
    rows_minor = lambda a: a.transpose(0, 2, 3, 1)
    cache = pl.BlockSpec((1, A_HEADS, HEAD_DIM, wb), lambda i: (i, 0, 0, 0))
    args = [heads(q), heads(k_new), heads(v_new), rows_minor(cache_k), rows_minor(cache_v), bias, b0]
    return pl.pallas_call(
        _attn_sample_kernel,
        grid=(db,),
        in_specs=[vec, vec, vec, cache, cache, full(bias), full(b0)],
        out_specs=vec,
        out_shape=jax.ShapeDtypeStruct((db, A_HEADS, HEAD_DIM), F32),
        compiler_params=_cparams("parallel"),
        name="attn_sample",
    )(*args).reshape(db, A_WIDTH)


def _rwkv_prep_math(rc, prev, mu, w0, wd, a0, wa, wg, k_k, k_a, r_k, seg):
    xr = rc + (prev - rc) * mu
    o1, o2, o3 = R_WIDTH, 2 * R_WIDTH, 3 * R_WIDTH
    r = xr[:, :o1]
    kr = xr[:, o1:o2]
    vr = xr[:, o2:o3]
    xw = xr[:, o3:o3 + DECAY_LORA]
    xa = xr[:, o3 + DECAY_LORA:o3 + DECAY_LORA + AAA_LORA]
    xg = xr[:, o3 + DECAY_LORA + AAA_LORA:]
    z = -(w0 + _dot(jnp.tanh(xw), wd))
    softplus = jnp.maximum(z, 0.0) + jnp.log(1.0 + jnp.exp(-jnp.abs(z)))
    decay = jnp.exp(-jnp.exp(-softplus - 0.5))
    a = jax.nn.sigmoid(a0 + _dot(xa, wa))
    g = _dot(jax.nn.sigmoid(xg), wg)
    kk = kr * k_k
    k2 = kr * (1.0 + (a - 1.0) * k_a)
    kk = kk / jnp.maximum(jnp.sqrt(_segdot(kk * kk, seg)), 1e-12)
    bonus = _segdot(r * k2 * r_k, seg) * vr
    return _bf(r), decay, k2, vr, _bf(kk), kk * a, bonus, g


N_PREP_PARAMS = 10


def _rwkv_prep_prompt_kernel(rc_ref, tail_ref, *refs):
    p = [x[...] for x in refs[:N_PREP_PARAMS]]
    outs = refs[N_PREP_PARAMS:]
    rc = rc_ref[0]
    tt = rc.shape[0]
    first = pl.program_id(1) == 0
    prev_row = jnp.where(first, 0.0, tail_ref[0, 7:8, :])
    rolled = pltpu.roll(rc, 1, axis=0)
    row = lax.broadcasted_iota(jnp.int32, (tt, 1), 0)
    prev = jnp.where(row == 0, prev_row, rolled)
    for o, val in zip(outs, _rwkv_prep_math(rc, prev, *p)):
        o[0] = val


def _rwkv_prep_sample_kernel(rc_ref, prev_ref, *refs):
    p = [x[...] for x in refs[:N_PREP_PARAMS]]
    outs = refs[N_PREP_PARAMS:]
    for o, val in zip(outs, _rwkv_prep_math(rc_ref[...], prev_ref[...], *p)):
        o[...] = val


def _head_seg():
    head = jnp.arange(R_WIDTH) // R_HEAD
    return (head[:, None] == head[None, :]).astype(BF16)


def _rwkv_prep_prompt(rc, params, *, tt):
    b, t, _ = rc.shape
    full = lambda a: pl.BlockSpec(a.shape, lambda i, j: (0,) * a.ndim)
    tile = lambda c: pl.BlockSpec((1, tt, c), lambda i, j: (i, j, 0))
    tail = pl.BlockSpec((1, 8, N_RWKV_COLS), lambda i, j: (i, jnp.maximum(j * (tt // 8) - 1, 0), 0))
    return pl.pallas_call(
        _rwkv_prep_prompt_kernel,
        grid=(b, t // tt),
        in_specs=[tile(N_RWKV_COLS), tail] + [full(a) for a in params],
        out_specs=[tile(R_WIDTH)] * 8,
        out_shape=[jax.ShapeDtypeStruct((b, t, R_WIDTH), F32)] * 8,
        compiler_params=_cparams("parallel", "parallel"),
        name="rwkv_prep_prompt",
    )(rc, rc, *params)


def _rwkv_prep_sample(rc, prev, params):
    n = rc.shape[0]
    full = lambda a: pl.BlockSpec(a.shape, lambda i: (0,) * a.ndim)
    return pl.pallas_call(
        _rwkv_prep_sample_kernel,
        grid=(1,),
        in_specs=[full(rc), full(prev)] + [full(a) for a in params],
        out_specs=[pl.BlockSpec((n, R_WIDTH), lambda i: (0, 0))] * 8,
        out_shape=[jax.ShapeDtypeStruct((n, R_WIDTH), F32)] * 8,
        compiler_params=_cparams("arbitrary"),
        name="rwkv_prep_sample",
    )(rc, prev, *params)


def _seg_lane_sum(x, lo_mask):
    lo = jnp.sum(jnp.where(lo_mask, x, 0.0), axis=-1, keepdims=True)
    hi = jnp.sum(jnp.where(lo_mask, 0.0, x), axis=-1, keepdims=True)
    return jnp.where(lo_mask, lo, hi)


def _rwkv_scan_kernel(r_ref, w_ref, k_ref, v_ref, kk_ref, ka_ref, s0_ref, y_ref, sT_ref, st_ref, *, bb, tc):
    pairs = R_HEADS // 2
    w2 = 2 * R_HEAD

    @pl.when(pl.program_id(1) == 0)
    def _():
        for b in range(bb):
            for p in range(pairs):
                st_ref[b, p] = jnp.concatenate([s0_ref[b, 2 * p], s0_ref[b, 2 * p + 1]], axis=1)

    lane = lax.broadcasted_iota(jnp.int32, (R_HEAD, w2), 1)
    sub = lax.broadcasted_iota(jnp.int32, (R_HEAD, w2), 0)
    lo_mask = lane < R_HEAD
    eye2 = (lane & (R_HEAD - 1)) == sub

    grp = min(8, tc)

    def group(tg, carry):
        rows = pl.ds(pl.multiple_of(tg * grp, grp), grp)
        for b in range(bb):
            for p in range(pairs):
                cols = slice(p * w2, (p + 1) * w2)
                r8, w8, k8, v8, kk8, ka8 = (ref[b, rows, cols] for ref in (r_ref, w_ref, k_ref, v_ref, kk_ref, ka_ref))
                s = st_ref[b, p]
                sb = _bf(s)
                ys = []
                for j in range(grp):
                    row = lambda a: a[j:j + 1, :]
                    sa = -_seg_lane_sum(sb * row(kk8), lo_mask)
                    vcol = _seg_lane_sum(jnp.where(eye2, jnp.broadcast_to(row(v8), (R_HEAD, w2)), 0.0), lo_mask)
                    s = s * row(w8) + sa * row(ka8) + vcol * row(k8)
                    sb = _bf(s)
                    yfull = _seg_lane_sum(sb * row(r8), lo_mask)
                    ys.append(jnp.sum(jnp.where(eye2, yfull, 0.0), axis=0, keepdims=True))
                st_ref[b, p] = s
                y_ref[b, rows, cols] = jnp.concatenate(ys, axis=0) if grp > 1 else ys[0]
        return carry

    lax.fori_loop(0, tc // grp, group, 0)

    @pl.when(pl.program_id(1) == pl.num_programs(1) - 1)
    def _():
        for b in range(bb):
            for p in range(pairs):
                s = st_ref[b, p]
                sT_ref[b, 2 * p] = s[:, :R_HEAD]
                sT_ref[b, 2 * p + 1] = s[:, R_HEAD:]


def _rwkv_scan(r, w, k, v, kk, ka, s0, *, bb, tc):
    b, t, _ = r.shape
    seq = pl.BlockSpec((bb, tc, R_WIDTH), lambda i, j: (i, j, 0))
    state = pl.BlockSpec((bb, R_HEADS, R_HEAD, R_HEAD), lambda i, j: (i, 0, 0, 0))
    return pl.pallas_call(
        functools.partial(_rwkv_scan_kernel, bb=bb, tc=tc),
        grid=(b // bb, t // tc),
        in_specs=[seq] * 6 + [state],
        out_specs=[seq, state],
        out_shape=[jax.ShapeDtypeStruct((b, t, R_WIDTH), F32),
                   jax.ShapeDtypeStruct((b, R_HEADS, R_HEAD, R_HEAD), F32)],
        scratch_shapes=[pltpu.VMEM((bb, R_HEADS // 2, R_HEAD, 2 * R_HEAD), F32)],
        compiler_params=_cparams("parallel", "arbitrary"),
        name="rwkv_scan",
    )(r, w, k, v, kk, ka, s0)


CHAINS = LANES // 2
K2 = R_HEAD // 2


def _chain_rows(x):
    b, t, _ = x.shape
    x = x.reshape(b, t, R_HEADS, K2, 2).transpose(1, 3, 4, 0, 2)
    return x.reshape(t, K2, LANES)


def _lane_scan_kernel(kk_ref, w_ref, ka_ref, kx_ref, r_ref, v_ref, s0_ref, y_ref, sT_ref, s_ref, *, tc):
    lo = lax.broadcasted_iota(jnp.int32, (K2, LANES), 1) < CHAINS

    @pl.when(pl.program_id(0) == 0)
    def _():
        for k2 in range(K2):
            s_ref[k2] = s0_ref[k2]

    def both_halves(x):
        return x + pltpu.roll(x, CHAINS, axis=1)

    nhalf = 2
    vh = R_HEAD // nhalf

    def row(ref, t, k2):
        return jnp.broadcast_to(ref[t, k2:k2 + 1, :], (vh, LANES))

    acc0 = []
    for hf in range(nhalf):
        vs = slice(hf * vh, (hf + 1) * vh)
        a = jnp.zeros((vh, LANES), F32)
        for k2 in range(K2):
            a = a + _bf(s_ref[k2, vs, :]) * row(kk_ref, 0, k2)
        acc0.append(a)

    def step(t, acc):
        tn = jnp.minimum(t + 1, tc - 1)
        nxt, ys = [], []
        v_t = v_ref[t]
        v_sw = pltpu.roll(v_t, CHAINS, axis=1)
        for hf in range(nhalf):
            vs = slice(hf * vh, (hf + 1) * vh)
            vv = jnp.where(lo, v_t, v_sw) if hf == 0 else jnp.where(lo, v_sw, v_t)
            sa = -both_halves(acc[hf])
            yacc = jnp.zeros((vh, LANES), F32)
            nacc = jnp.zeros((vh, LANES), F32)
            for k2 in range(K2):
                s = s_ref[k2, vs, :] * row(w_ref, t, k2) + sa * row(ka_ref, t, k2) + vv * row(kx_ref, t, k2)
                s_ref[k2, vs, :] = s
                sb = _bf(s)
                yacc = yacc + sb * row(r_ref, t, k2)
                nacc = nacc + sb * row(kk_ref, tn, k2)
            ys.append(both_halves(yacc))
            nxt.append(nacc)
        y_ref[t] = jnp.where(lo, ys[0], ys[1])
        return tuple(nxt)

    lax.fori_loop(0, tc, step, tuple(acc0))

    @pl.when(pl.program_id(0) == pl.num_programs(0) - 1)
    def _():
        for k2 in range(K2):
            sT_ref[k2] = s_ref[k2]


def _lane_scan(r, w, k, v, kk, ka, s0, *, tc):
    b, t, _ = r.shape
    assert b * R_HEADS == CHAINS
    ops = [_chain_rows(x) for x in (kk, w, ka, k, r, v)]
    s0c = s0.reshape(b, R_HEADS, K2, 2, K2, 2).transpose(4, 3, 2, 5, 0, 1).reshape(K2, R_HEAD, LANES)
    rows = pl.BlockSpec((tc, K2, LANES), lambda i: (i, 0, 0))
    state = pl.BlockSpec((K2, R_HEAD, LANES), lambda i: (0, 0, 0))
    y, st = pl.pallas_call(
        functools.partial(_lane_scan_kernel, tc=tc),
        grid=(t // tc,),
        in_specs=[rows] * 6 + [state],
        out_specs=[rows, state],
        out_shape=[jax.ShapeDtypeStruct((t, K2, LANES), F32), jax.ShapeDtypeStruct((K2, R_HEAD, LANES), F32)],
        scratch_shapes=[pltpu.VMEM((K2, R_HEAD, LANES), F32)],
        compiler_params=_cparams("arbitrary"),
        name="rwkv_lane_scan",
    )(*ops, s0c)
    y = y.reshape(t, K2, 2, b, R_HEADS).transpose(3, 0, 4, 1, 2).reshape(b, t, R_WIDTH)
    st = st.reshape(K2, 2, K2, 2, b, R_HEADS).transpose(4, 5, 2, 1, 0, 3).reshape(b, R_HEADS, R_HEAD, R_HEAD)
    return y, st


def _post_kernel(x_ref, y_ref, bonus_ref, g_ref, oa_ref, sg_ref, gnw_ref, gnb_ref, seg_ref, wba_ref, wbb_ref,
                 wout_ref, ln2_ref, wr_ref, br_ref, *rest, aliased, n_main):
    outs = rest[aliased:]
    x1_ref, h2_ref, ti_ref, tg_ref = outs

    @pl.when(pl.program_id(0) >= n_main)
    def _():
        for o in outs:
            o[...] = jnp.zeros(o.shape, o.dtype)

    @pl.when(pl.program_id(0) < n_main)
    def _():
        _post_body(x_ref, y_ref, bonus_ref, g_ref, oa_ref, sg_ref, gnw_ref, gnb_ref, seg_ref, wba_ref, wbb_ref,
                   wout_ref, ln2_ref, wr_ref, br_ref, x1_ref, h2_ref, ti_ref, tg_ref)


def _post_body(x_ref, y_ref, bonus_ref, g_ref, oa_ref, sg_ref, gnw_ref, gnb_ref, seg_ref, wba_ref, wbb_ref,
               wout_ref, ln2_ref, wr_ref, br_ref, x1_ref, h2_ref, ti_ref, tg_ref):
    y = y_ref[...]
    seg = seg_ref[...]
    mu = _segdot(y, seg) * (1.0 / R_HEAD)
    yc = y - mu
    var = _segdot(yc * yc, seg) * (1.0 / R_HEAD)
    yn = yc * lax.rsqrt(var + GN_EPS) * gnw_ref[...] + gnb_ref[...]
    o_b = (yn + bonus_ref[...]) * g_ref[...]
    mixed = (sg_ref[:, :D_MODEL] * _dot(oa_ref[...], wba_ref[...])
             + sg_ref[:, D_MODEL:] * _dot(o_b, wbb_ref[...]))
    x1 = x_ref[...] + _dot(mixed, wout_ref[...])
    x1_ref[...] = x1
    h2 = _rms(x1, ln2_ref[...])
    h2_ref[...] = h2
    logits = _dot(h2, wr_ref[...]) + br_ref[...]
    lane = lax.broadcasted_iota(jnp.int32, logits.shape, 1).astype(F32)
    work = logits
    vals, idxs = [], []
    for _ in range(TOP_K):
        m = jnp.max(work, axis=-1, keepdims=True)
        idx = jnp.min(jnp.where(work == m, lane, float(LANES)), axis=-1, keepdims=True)
        vals.append(m)
        idxs.append(idx)
        work = jnp.where(lane == idx, -jnp.inf, work)
    es = [jnp.exp(v - vals[0]) for v in vals]
    tot = es[0] + es[1] + es[2] + es[3]
    ti = jnp.zeros(logits.shape, F32)
    tg = jnp.zeros(logits.shape, F32)
    for kslot in range(TOP_K):
        ti = jnp.where(lane == float(kslot), idxs[kslot], ti)
        tg = jnp.where(lane == float(kslot), es[kslot] / tot, tg)
    ti_ref[...] = ti.astype(jnp.int32)
    tg_ref[...] = tg


def _post(x, y, bonus, g, oa, sg, consts, *, tm, n_total, row0=0, into=None):
    n = x.shape[0]
    blk0 = row0 // tm
    n_main = n // tm
    steps = n_main + (1 if into is None and n_total > n else 0)
    row = lambda c: pl.BlockSpec((tm, c), lambda i: (jnp.minimum(i, n_main - 1), 0))
    orow = lambda c: pl.BlockSpec((tm, c), lambda i: (i + blk0, 0))
    full = lambda a: pl.BlockSpec(a.shape, lambda i: (0,) * a.ndim)
    out_cols = ((D_MODEL, F32), (D_MODEL, F32), (LANES, jnp.int32), (LANES, F32))
    ins = [x, y, bonus, g, oa, sg, *consts]
    in_specs = [row(D_MODEL), row(R_WIDTH), row(R_WIDTH), row(R_WIDTH), row(A_WIDTH), row(2 * D_MODEL)]
    in_specs += [full(a) for a in consts]
    aliases = {}
    if into is not None:
        aliases = {len(ins) + i: i for i in range(len(into))}
        in_specs += [pl.BlockSpec(memory_space=pl.ANY)] * len(into)
        ins += list(into)
    return pl.pallas_call(
        functools.partial(_post_kernel, aliased=len(aliases), n_main=n_main),
        grid=(steps,),
        in_specs=in_specs,
        out_specs=[orow(c) for c, _ in out_cols],
        out_shape=[jax.ShapeDtypeStruct((n_total, c), dt) for c, dt in out_cols],
        input_output_aliases=aliases,
        compiler_params=_cparams("parallel"),
        name="post",
    )(*ins)


def _start_rows(rows_ref, hbm, buf, sem, *, rows, to_hbm):
    for r in range(rows):
        at = pl.ds(rows_ref[0, 0, r], 1)
        prio = r % 2
        if to_hbm:
            pltpu.make_async_copy(buf.at[pl.ds(r, 1)], hbm.at[at], sem).start(priority=prio)
        else:
            pltpu.make_async_copy(hbm.at[at], buf.at[pl.ds(r, 1)], sem).start(priority=prio)


def _wait_rows(hbm, buf, sem, *, rows, to_hbm):
    whole = hbm.at[pl.ds(0, rows)]
    (pltpu.make_async_copy(buf, whole, sem) if to_hbm else pltpu.make_async_copy(whole, buf, sem)).wait()


def _expert_mlp(xb, wgu_bf, wd_bf, bgu_ref, bd_ref, rows):
    acc = jnp.broadcast_to(bd_ref[0], (rows, D_MODEL))
    cw = 512
    for c in range(D_MODEL // cw):
        gs = slice(c * cw, (c + 1) * cw)
        us = slice(D_MODEL + c * cw, D_MODEL + (c + 1) * cw)
        gt = jnp.dot(xb, wgu_bf[:, gs], preferred_element_type=F32) + bgu_ref[0, :, gs]
        up = jnp.dot(xb, wgu_bf[:, us], preferred_element_type=F32) + bgu_ref[0, :, us]
        gt = jnp.minimum(gt, SWIGLU_LIMIT)
        up = jnp.clip(up, -SWIGLU_LIMIT, SWIGLU_LIMIT)
        act = (up + 1.0) * (gt * jax.nn.sigmoid(gt * SWIGLU_ALPHA))
        acc = acc + jnp.dot(act.astype(BF16), wd_bf[gs, :], preferred_element_type=F32)
    return acc


def _moe_kernel(be_ref, src_cur, src_nxt, dst_prv, dst_cur, h_hbm, wgu_ref, bgu_ref, wd_ref, bd_ref, ys_hbm,
                x0, x1, y0, y1, wgu_bf, wd_bf, gsem, ssem, *, rows, plane_rows, n_tok):
    i = pl.program_id(0)
    last = pl.num_programs(0) - 1

    changed = jnp.logical_or(i == 0, be_ref[i] != be_ref[jnp.maximum(i - 1, 0)])

    @pl.when(changed)
    def _():
        step = 128

        def cast(j, c):
            rs = pl.ds(pl.multiple_of(j * step, step), step)
            wgu_bf[rs, :] = wgu_ref[0, rs, :].astype(BF16)
            wd_bf[rs, :] = wd_ref[0, rs, :].astype(BF16)
            return c

        lax.fori_loop(0, D_MODEL // step, cast, 0)

    def run(par):
        xc, xn = (x0, x1) if par == 0 else (x1, x0)
        yc, yp = (y0, y1) if par == 0 else (y1, y0)
        gather = functools.partial(_start_rows, hbm=h_hbm, rows=rows, to_hbm=False)
        scatter = functools.partial(_start_rows, hbm=ys_hbm, rows=rows, to_hbm=True)
        gathered = functools.partial(_wait_rows, h_hbm, rows=rows, to_hbm=False)
        scattered = functools.partial(_wait_rows, ys_hbm, rows=rows, to_hbm=True)

        @pl.when(i == 0)
        def _():
            yp[...] = jnp.zeros(yp.shape, yp.dtype)
            spare = plane_rows - n_tok
            fills = [pltpu.make_async_copy(yp.at[pl.ds(0, spare)],
                                           ys_hbm.at[pl.ds(k * plane_rows + n_tok, spare)], ssem.at[1 - par])
                     for k in range(TOP_K)]
            for d in fills:
                d.start()
            for d in fills:
                d.wait()
            gather(src_cur, buf=xc, sem=gsem.at[par])

        gathered(xc, gsem.at[par])

        @pl.when(i >= 1)
        def _():
            scattered(yc, ssem.at[par])

        gather(src_nxt, buf=xn, sem=gsem.at[1 - par])
        scatter(dst_prv, buf=yp, sem=ssem.at[1 - par])
        yc[...] = _expert_mlp(xc[...].astype(BF16), wgu_bf, wd_bf, bgu_ref, bd_ref, rows)

        @pl.when(i == last)
        def _():
            scatter(dst_cur, buf=yc, sem=ssem.at[par])
            gathered(xn, gsem.at[1 - par])
            scattered(yp, ssem.at[1 - par])
            scattered(yc, ssem.at[par])

    for par in range(2):
        pl.when(lax.rem(i, 2) == par)(functools.partial(run, par))


def _moe_experts(block_e, ids, h2, w_gate_up, b_gate_up, w_down, b_down, *, rows, plane_rows, n_tok):
    n_blocks = ids.shape[0] - 2
    ids = lax.optimization_barrier(ids)
    src = lax.shift_right_logical(ids, 2)
    dst = (ids & (TOP_K - 1)) * plane_rows + src
    idblk = lambda off: pl.BlockSpec((1, 1, rows), lambda i, be: (i + off, 0, 0), memory_space=pltpu.SMEM)
    grid_spec = pltpu.PrefetchScalarGridSpec(
        num_scalar_prefetch=1,
        grid=(n_blocks,),
        in_specs=[
            idblk(1), idblk(2), idblk(0), idblk(1),
            pl.BlockSpec(memory_space=pl.ANY),
            pl.BlockSpec((1, D_MODEL, 2 * D_MODEL), lambda i, be: (be[i], 0, 0)),
            pl.BlockSpec((1, 1, 2 * D_MODEL), lambda i, be: (be[i], 0, 0)),
            pl.BlockSpec((1, D_MODEL, D_MODEL), lambda i, be: (be[i], 0, 0)),
            pl.BlockSpec((1, 1, D_MODEL), lambda i, be: (be[i], 0, 0)),
        ],
        out_specs=pl.BlockSpec(memory_space=pl.ANY),
        scratch_shapes=[pltpu.VMEM((rows, D_MODEL), F32)] * 4 + [
            pltpu.VMEM((D_MODEL, 2 * D_MODEL), BF16),
            pltpu.VMEM((D_MODEL, D_MODEL), BF16),
            pltpu.SemaphoreType.DMA((2,)),
            pltpu.SemaphoreType.DMA((2,))],
    )
    return pl.pallas_call(
        functools.partial(_moe_kernel, rows=rows, plane_rows=plane_rows, n_tok=n_tok),
        grid_spec=grid_spec,
        out_shape=jax.ShapeDtypeStruct((TOP_K * plane_rows, D_MODEL), F32),
        compiler_params=_cparams("arbitrary"),
        name="moe_experts",
    )(block_e, src, src, dst, dst, h2, w_gate_up, b_gate_up[:, None], w_down, b_down[:, None])


def _combine_kernel(ys_ref, x1_ref, gate_ref, lnf_ref, y_ref):
    acc = x1_ref[...]
    for kslot in range(TOP_K):
        acc = acc + gate_ref[:, kslot:kslot + 1] * ys_ref[kslot]
    y_ref[...] = _rms(acc, lnf_ref[...])


def _combine(ys, x1, gates, ln_f, *, tm, n, row0):
    blk0 = row0 // tm
    return pl.pallas_call(
        _combine_kernel,
        grid=(n // tm,),
        in_specs=[
            pl.BlockSpec((TOP_K, tm, D_MODEL), lambda i: (0, i + blk0, 0)),
            pl.BlockSpec((tm, D_MODEL), lambda i: (i + blk0, 0)),
            pl.BlockSpec((tm, LANES), lambda i: (i + blk0, 0)),
            pl.BlockSpec(ln_f.shape, lambda i: (0, 0)),
        ],
        out_specs=pl.BlockSpec((tm, D_MODEL), lambda i: (i, 0)),
        out_shape=jax.ShapeDtypeStruct((n, D_MODEL), F32),
        compiler_params=_cparams("parallel"),
        name="combine",
    )(ys.reshape(TOP_K, -1, D_MODEL), x1, gates, ln_f)


def _route(topi, n_pad, rows):
    n_tok = topi.shape[0]
    e_flat = topi.reshape(-1)
    nk = e_flat.shape[0]
    n_blocks = -(-(nk + N_EXPERTS * (rows - 1)) // rows)
    spare = n_pad - n_tok
    assert 3 * rows <= TOP_K * spare
    experts = jnp.arange(N_EXPERTS, dtype=jnp.int32)
    order = jnp.argsort(e_flat).astype(jnp.int32)
    sizes = jnp.sum((e_flat[:, None] == experts[None, :]).astype(jnp.int32), axis=0)
    padded = (sizes + rows - 1) // rows * rows
    pends = jnp.cumsum(padded)
    slot = jnp.arange(n_blocks * rows, dtype=jnp.int32)
    past = (slot[:, None] >= pends[None, :]).astype(jnp.int32)
    e_slot = jnp.sum(past, axis=1)
    off = slot - jnp.sum(past * padded[None, :], axis=1)
    src = jnp.sum(past * sizes[None, :], axis=1) + off
    size_e = jnp.sum((e_slot[:, None] == experts[None, :]) * sizes[None, :], axis=1)

    def pad_ids(d):
        return (TOP_K * (n_tok + d % spare) + d // spare).astype(jnp.int32)

    ids = jnp.where(off < size_e, order[jnp.clip(src, 0, nk - 1)], pad_ids((slot // rows) % 2 * rows + slot % rows))
    end_ids = pad_ids(2 * rows + jnp.arange(rows, dtype=jnp.int32))
    ids = jnp.concatenate([end_ids, ids, end_ids]).reshape(n_blocks + 2, 1, rows)
    block_e = jnp.minimum(e_slot[::rows], N_EXPERTS - 1).astype(jnp.int32)
    return block_e, ids


def kernel(x_prompt, x_sample, cache_k, cache_v, state_wkv, state_shift, rel_bias, ln1, w_in, rwkv_mu, w0,
           w_decay_up, a0, w_a_up, w_g_up, k_k, k_a, r_k, gn_w, gn_b, w_branch_a, w_branch_b, w_out, ln2,
           w_router, b_router, w_gate_up, b_gate_up, w_down, b_down, ln_f):
    bp, seq, _ = x_prompt.shape
    db = x_sample.shape[0]
    n_p = bp * seq
    n_tot = n_p + db
    n_pad = -(-n_tot // TOKEN_TILE) * TOKEN_TILE
    l = 0

    row = lambda a: a.reshape(1, -1)
    seg = _head_seg()
    prep_params = (row(rwkv_mu[l]), row(w0[l]), w_decay_up[l], row(a0[l]), w_a_up[l], w_g_up[l],
                   row(k_k[l]), row(k_a[l]), row(r_k[l]), seg)
    wr_pad = jnp.pad(w_router[l], ((0, 0), (0, LANES - N_EXPERTS))).astype(BF16)
    br_pad = jnp.pad(b_router[l], (0, LANES - N_EXPERTS), constant_values=NEG).reshape(1, LANES)
    post_small = (row(gn_w[l]), row(gn_b[l]), seg)
    post_tail = (row(ln2[l]), wr_pad, br_pad)

    w_in_bf = w_in[l].astype(BF16)
    consts = (post_small + (w_branch_a[l].astype(BF16), w_branch_b[l].astype(BF16), w_out[l].astype(BF16))
              + post_tail)

    xp = x_prompt.reshape(n_p, D_MODEL)
    q_p, k_p, v_p, rc_p, sg_p = _inproj(xp, row(ln1[l]), w_in_bf, tm=TOKEN_TILE)
    as3 = lambda a: a.reshape(bp, seq, -1)
    oa_p = _attn_prompt(as3(q_p), as3(k_p), as3(v_p), _prompt_bias(rel_bias))
    prep_p = _rwkv_prep_prompt(as3(rc_p), prep_params, tt=TOKEN_TILE)
    r_, w_, k2_, vr_, kk_, ka_, bonus_p, g_p = prep_p
    s0_p = jnp.zeros((bp, R_HEADS, R_HEAD, R_HEAD), F32)
    y_p, wkv_p = _lane_scan(r_, w_, k2_, vr_, kk_, ka_, s0_p, tc=SCAN_CHUNK)
    flat = lambda a: a.reshape(n_p, -1)
    bufs = _post(xp, flat(y_p), flat(bonus_p), flat(g_p), flat(oa_p), sg_p, consts,
                 tm=TOKEN_TILE, n_total=n_pad)

    xs = x_sample.reshape(db, D_MODEL)
    q_s, k_s, v_s, rc_s, sg_s = _inproj(xs, row(ln1[l]), w_in_bf, tm=db)
    bias_s, b0_s = _sample_bias(rel_bias, cache_k.shape[2])
    oa_s = _attn_sample(q_s, k_s, v_s, cache_k[l], cache_v[l], bias_s, b0_s)
    prep_s = _rwkv_prep_sample(rc_s, state_shift[l], prep_params)
    sr, sw, sk2, svr, skk, ska, bonus_s, g_s = [a[:, None] for a in prep_s]
    y_s, wkv_s = _rwkv_scan(sr, sw, sk2, svr, skk, ska, state_wkv[l], bb=SCAN_BATCH, tc=1)
    x1_all, h2_all, ti_all, tg_all = _post(xs, y_s[:, 0], bonus_s[:, 0], g_s[:, 0], oa_s, sg_s, consts,
                                           tm=db, n_total=n_pad, row0=n_p, into=bufs)

    block_e, ids = _route(ti_all[:n_tot, :TOP_K], n_pad, MOE_ROWS)
    ys = _moe_experts(block_e, ids, h2_all, w_gate_up[l], b_gate_up[l], w_down[l], b_down[l],
                      rows=MOE_ROWS, plane_rows=n_pad, n_tok=n_tot)
    lnf = row(ln_f)
    y_prompt = _combine(ys, x1_all, tg_all, lnf, tm=TOKEN_TILE, n=n_p, row0=0)
    y_sample = _combine(ys, x1_all, tg_all, lnf, tm=db, n=db, row0=n_p)

    heads = lambda a, b_: a.reshape(1, b_, -1, A_HEADS, HEAD_DIM)
    return (y_prompt.reshape(bp, seq, D_MODEL), y_sample.reshape(db, 1, D_MODEL),
            heads(k_p, bp), heads(v_p, bp), wkv_p[None], as3(rc_p)[:, -1][None],
            heads(k_s, db), heads(v_s, db), wkv_s[None], rc_s[None])
```

```python
import functools
import math

import jax
import jax.numpy as jnp
from jax import lax
from jax.experimental import pallas as pl
from jax.experimental.pallas import tpu as pltpu

F32 = jnp.float32
BF16 = jnp.bfloat16

D_MODEL = 1024
A_HEADS = 8
HEAD_DIM = 64
A_WIDTH = A_HEADS * HEAD_DIM
PATTERNS = ((128, 1), (512, 4), (2048, 16))
BLOCK = 128
NUM_BUCKETS = 32
MAX_DISTANCE = 2048
SCALE = HEAD_DIM ** -0.5
NEG = -1e30
R_HEADS = 8
R_HEAD = 64
R_WIDTH = R_HEADS * R_HEAD
DECAY_LORA = 64
AAA_LORA = 64
GATE_LORA = 128
GN_EPS = 64e-5
N_RWKV_COLS = 3 * R_WIDTH + DECAY_LORA + AAA_LORA + GATE_LORA
N_IN_COLS = 3 * A_WIDTH + N_RWKV_COLS + 2 * D_MODEL
N_EXPERTS = 32
TOP_K = 4
SWIGLU_LIMIT = 7.0
SWIGLU_ALPHA = 1.702
RMS_EPS = 1e-6

LANES = 128
VMEM_LIMIT = 56 * 1024 * 1024
TOKEN_TILE = 256
MOE_ROWS = 512
SCAN_BATCH = 2
SCAN_CHUNK = 64


def _cparams(*sem):
    return pltpu.CompilerParams(dimension_semantics=sem, vmem_limit_bytes=VMEM_LIMIT)


def _rms(x, g):
    return x * lax.rsqrt(jnp.mean(x * x, axis=-1, keepdims=True) + RMS_EPS) * g


def _dot(a, b):
    return jnp.dot(a.astype(BF16), b.astype(BF16), preferred_element_type=F32)


def _bf(x):
    return x.astype(BF16).astype(F32)


def _split3(x):
    hi = x.astype(BF16)
    r1 = x - hi.astype(F32)
    mid = r1.astype(BF16)
    lo = (r1 - mid.astype(F32)).astype(BF16)
    return hi, mid, lo


def _segdot(x, seg):
    hi, mid, lo = _split3(x)
    d = lambda p: jnp.dot(p, seg, preferred_element_type=F32)
    return d(hi) + d(mid) + d(lo)


def _inproj_kernel(x_ref, g_ref, w_ref, q_ref, k_ref, v_ref, rc_ref, sg_ref):
    h = _rms(x_ref[...], g_ref[...]).astype(BF16)

    def mm(c0, c1):
        return _dot(h, w_ref[:, c0:c1])

    c1 = 3 * A_WIDTH
    c2 = c1 + N_RWKV_COLS
    q_ref[...] = mm(0, A_WIDTH)
    k_ref[...] = mm(A_WIDTH, 2 * A_WIDTH)
    v_ref[...] = mm(2 * A_WIDTH, c1)
    rc_ref[...] = mm(c1, c2)
    sg_ref[:, :D_MODEL] = jax.nn.sigmoid(mm(c2, c2 + D_MODEL))
    sg_ref[:, D_MODEL:] = jax.nn.sigmoid(mm(c2 + D_MODEL, N_IN_COLS))


def _inproj(x, ln1, w, *, tm):
    n = x.shape[0]
    row = lambda c: pl.BlockSpec((tm, c), lambda i: (i, 0))
    full = lambda a: pl.BlockSpec(a.shape, lambda i: (0,) * a.ndim)
    out_cols = (A_WIDTH, A_WIDTH, A_WIDTH, N_RWKV_COLS, 2 * D_MODEL)
    return pl.pallas_call(
        _inproj_kernel,
        grid=(n // tm,),
        in_specs=[row(D_MODEL), full(ln1), full(w)],
        out_specs=[row(c) for c in out_cols],
        out_shape=[jax.ShapeDtypeStruct((n, c), F32) for c in out_cols],
        compiler_params=_cparams("parallel"),
        name="inproj",
    )(x, ln1, w)


def _t5_bucket(dist):
    max_exact = NUM_BUCKETS // 2
    d = jnp.maximum(dist, 1).astype(F32)
    large = max_exact + (jnp.log(d / max_exact) / math.log(MAX_DISTANCE / max_exact)
                         * (NUM_BUCKETS - max_exact)).astype(jnp.int32)
    large = jnp.minimum(large, NUM_BUCKETS - 1)
    return jnp.where(dist < max_exact, dist, large)


def _bias_lookup(rel_bias, dist):
    onehot = (_t5_bucket(dist)[..., None] == jnp.arange(NUM_BUCKETS)).astype(F32)
    return jnp.moveaxis(jnp.dot(onehot, rel_bias.astype(F32), precision=lax.Precision.HIGHEST), -1, 0)


def _prompt_bias(rel_bias):
    i = jnp.arange(BLOCK)[:, None]
    j = jnp.arange(2 * BLOCK)[None, :]
    delta = i + BLOCK - j
    out = []
    for window, dil in PATTERNS:
        n = window // dil
        band = (delta >= 0) & (delta <= n)
        out.append(jnp.where(band[None], _bias_lookup(rel_bias, jnp.clip(delta, 0, n) * dil), NEG))
    return jnp.stack(out, 0)


def _sample_bias(rel_bias, wb):
    dist = wb - jnp.arange(wb)
    out = []
    for window, dil in PATTERNS:
        member = (dist % dil == 0) & (dist <= window)
        out.append(jnp.where(member[None], _bias_lookup(rel_bias, dist), NEG))
    b0 = _bias_lookup(rel_bias, jnp.zeros((1,), jnp.int32))
    return jnp.stack(out, 0), b0


def _attn_prompt_kernel(q_ref, k_ref, v_ref, bias_ref, o_ref, os_ref, ls_ref, *, seq):
    nt = (((1,), (1,)), ((), ()))
    npat = len(PATTERNS)
    w2 = 2 * HEAD_DIM
    lo = lax.broadcasted_iota(jnp.int32, (BLOCK, w2), 1) < HEAD_DIM
    key = lax.broadcasted_iota(jnp.int32, (2 * BLOCK, 2 * BLOCK), 1)

    for g, (window, dil) in enumerate(PATTERNS):
        nb = seq // (BLOCK * dil)
        sh = dil.bit_length() - 1

        def body(it, carry, g=g, dil=dil, nb=nb, sh=sh):
            r = it & (dil - 1)
            c = it >> sh
            start = c * (BLOCK * dil) + r
            rows = pl.ds(start, BLOCK, stride=dil) if dil > 1 else pl.ds(start, BLOCK)
            qb = q_ref[0, rows, :]
            kb = k_ref[0, rows, :]
            vb = v_ref[0, rows, :]
            if nb > 1:
                pstart = jnp.maximum(c - 1, 0) * (BLOCK * dil) + r
                prow = pl.ds(pstart, BLOCK, stride=dil) if dil > 1 else pl.ds(pstart, BLOCK)
                kp = k_ref[0, prow, :]
                vp = v_ref[0, prow, :]
            q2 = jnp.concatenate([jnp.where(lo, qb, 0.0), jnp.where(lo, 0.0, qb)], axis=0).astype(BF16)
            if nb > 1:
                k2 = jnp.concatenate([kp, kb], axis=0)
                v2 = jnp.concatenate([vp, vb], axis=0)
                bias = bias_ref[g, 0]
                valid = (bias > 0.5 * NEG) & ((key >= BLOCK) | (c > 0))
            else:
                k2, v2 = kb, vb
                bias = bias_ref[g, 0, :, BLOCK:]
                valid = bias > 0.5 * NEG
            s = lax.dot_general(q2, k2.astype(BF16), nt, preferred_element_type=F32)
            s = jnp.where(valid, s * SCALE + bias, NEG)
            m = jnp.max(s, axis=-1, keepdims=True)
            p = jnp.exp(s - m)
            l = jnp.sum(p, axis=-1, keepdims=True)
            o2 = jnp.dot((p * (1.0 / l)).astype(BF16), v2.astype(BF16), preferred_element_type=F32)
            lse = jnp.broadcast_to(m + jnp.log(l), (2 * BLOCK, w2))
            os_ref[g, rows, :] = jnp.where(lo, o2[:BLOCK], o2[BLOCK:])
            ls_ref[g, rows, :] = jnp.where(lo, lse[:BLOCK], lse[BLOCK:])
            return carry

        lax.fori_loop(0, dil * nb, body, 0, unroll=8)

    chunk = 256

    def merge(ch, carry):
        rows = pl.ds(pl.multiple_of(ch * chunk, chunk), chunk)
        lse = [ls_ref[g, rows, :] for g in range(npat)]
        top = functools.reduce(jnp.maximum, lse)
        e = [jnp.exp(x - top) for x in lse]
        inv = 1.0 / functools.reduce(lambda a, b_: a + b_, e)
        acc = jnp.zeros((chunk, 2 * HEAD_DIM), F32)
        for g in range(npat):
            acc = acc + (e[g] * inv) * os_ref[g, rows, :]
        o_ref[0, rows, :] = acc
        return carry

    lax.fori_loop(0, seq // chunk, merge, 0)


def _attn_prompt(q, k, v, bias):
    b, s, _ = q.shape
    w2 = 2 * HEAD_DIM
    qspec = pl.BlockSpec((1, s, w2), lambda i, p: (i, 0, p))
    return pl.pallas_call(
        functools.partial(_attn_prompt_kernel, seq=s),
        grid=(b, A_HEADS // 2),
        in_specs=[qspec, qspec, qspec,
                  pl.BlockSpec((len(PATTERNS), 1, 2 * BLOCK, 2 * BLOCK), lambda i, p: (0, p, 0, 0))],
        out_specs=qspec,
        out_shape=jax.ShapeDtypeStruct((b, s, A_WIDTH), F32),
        scratch_shapes=[pltpu.VMEM((len(PATTERNS), s, w2), F32)] * 2,
        compiler_params=_cparams("parallel", "parallel"),
        name="attn_prompt",
    )(q, k, v, bias.reshape(len(PATTERNS), A_HEADS // 2, 2 * BLOCK, 2 * BLOCK))


def _attn_sample_kernel(q_ref, kn_ref, vn_ref, kt_ref, vt_ref, bias_ref, b0_ref, o_ref):
    npat = len(PATTERNS)
    nt = (((1,), (1,)), ((), ()))
    q = q_ref[0]
    v_new = _bf(vn_ref[0])
    s0 = jnp.sum(_bf(q) * _bf(kn_ref[0]), axis=-1, keepdims=True) * SCALE + b0_ref[...]
    s = jnp.concatenate([_dot(q[h:h + 1], kt_ref[0, h]) for h in range(A_HEADS)], axis=0) * SCALE
    ps, p0s, lses = [], [], []
    for g in range(npat):
        bias = bias_ref[g]
        sg = jnp.where(bias > 0.5 * NEG, s + bias, NEG)
        m = jnp.maximum(jnp.max(sg, axis=-1, keepdims=True), s0)
        l = jnp.sum(jnp.exp(sg - m), axis=-1, keepdims=True) + jnp.exp(s0 - m)
        lse = m + jnp.log(l)
        ps.append(jnp.exp(sg - lse))
        p0s.append(_bf(jnp.exp(s0 - lse)))
        lses.append(lse)
    top = functools.reduce(jnp.maximum, lses)
    e = [jnp.exp(x - top) for x in lses]
    inv = 1.0 / functools.reduce(lambda a, b_: a + b_, e)
    w = [_bf(x * inv) for x in e]
    rows = []
    for h in range(A_HEADS):
        p_h = jnp.concatenate([p[h:h + 1] for p in ps], axis=0)
        o_h = lax.dot_general(p_h.astype(BF16), vt_ref[0, h].astype(BF16), nt, preferred_element_type=F32)
        acc = jnp.zeros((1, HEAD_DIM), F32)
        for g in range(npat):
            acc = acc + w[g][h:h + 1] * _bf(o_h[g:g + 1] + p0s[g][h:h + 1] * v_new[h:h + 1])
        rows.append(acc)
    o_ref[0] = jnp.concatenate(rows, axis=0)


def _attn_sample(q, k_new, v_new, cache_k, cache_v, bias, b0):
    db, wb = cache_k.shape[:2]
    heads = lambda a: a.reshape(db, A_HEADS, HEAD_DIM)
    vec = pl.BlockSpec((1, A_HEADS, HEAD_DIM), lambda i: (i, 0, 0))
    full = lambda a: pl.BlockSpec(a.shape, lambda i: (0,) * a.ndim)
    rows_minor = lambda a: a.transpose(0, 2, 3, 1)
    cache = pl.BlockSpec((1, A_HEADS, HEAD_DIM, wb), lambda i: (i, 0, 0, 0))
    args = [heads(q), heads(k_new), heads(v_new), rows_minor(cache_k), rows_minor(cache_v), bias, b0]
    return pl.pallas_call(
        _attn_sample_kernel,
        grid=(db,),
        in_specs=[vec, vec, vec, cache, cache, full(bias), full(b0)],
        out_specs=vec,
        out_shape=jax.ShapeDtypeStruct((db, A_HEADS, HEAD_DIM), F32),
        compiler_params=_cparams("parallel"),
        name="attn_sample",
    )(*args).reshape(db, A_WIDTH)


def _rwkv_prep_math(rc, prev, mu, w0, wd, a0, wa, wg, k_k, k_a, r_k, seg):
    xr = rc + (prev - rc) * mu
    o1, o2, o3 = R_WIDTH, 2 * R_WIDTH, 3 * R_WIDTH
    r = xr[:, :o1]
    kr = xr[:, o1:o2]
    vr = xr[:, o2:o3]
    xw = xr[:, o3:o3 + DECAY_LORA]
    xa = xr[:, o3 + DECAY_LORA:o3 + DECAY_LORA + AAA_LORA]
    xg = xr[:, o3 + DECAY_LORA + AAA_LORA:]
    z = -(w0 + _dot(jnp.tanh(xw), wd))
    softplus = jnp.maximum(z, 0.0) + jnp.log(1.0 + jnp.exp(-jnp.abs(z)))
    decay = jnp.exp(-jnp.exp(-softplus - 0.5))
    a = jax.nn.sigmoid(a0 + _dot(xa, wa))
    g = _dot(jax.nn.sigmoid(xg), wg)
    kk = kr * k_k
    k2 = kr * (1.0 + (a - 1.0) * k_a)
    kk = kk / jnp.maximum(jnp.sqrt(_segdot(kk * kk, seg)), 1e-12)
    bonus = _segdot(r * k2 * r_k, seg) * vr
    return _bf(r), decay, k2, vr, _bf(kk), kk * a, bonus, g


N_PREP_PARAMS = 10


def _rwkv_prep_prompt_kernel(rc_ref, tail_ref, *refs):
    p = [x[...] for x in refs[:N_PREP_PARAMS]]
    outs = refs[N_PREP_PARAMS:]
    rc = rc_ref[0]
    tt = rc.shape[0]
    first = pl.program_id(1) == 0
    prev_row = jnp.where(first, 0.0, tail_ref[0, 7:8, :])
    rolled = pltpu.roll(rc, 1, axis=0)
    row = lax.broadcasted_iota(jnp.int32, (tt, 1), 0)
    prev = jnp.where(row == 0, prev_row, rolled)
    for o, val in zip(outs, _rwkv_prep_math(rc, prev, *p)):
        o[0] = val


def _rwkv_prep_sample_kernel(rc_ref, prev_ref, *refs):
    p = [x[...] for x in refs[:N_PREP_PARAMS]]
    outs = refs[N_PREP_PARAMS:]
    for o, val in zip(outs, _rwkv_prep_math(rc_ref[...], prev_ref[...], *p)):
        o[...] = val


def _head_seg():
    head = jnp.arange(R_WIDTH) // R_HEAD
    return (head[:, None] == head[None, :]).astype(BF16)


def _rwkv_prep_prompt(rc, params, *, tt):
    b, t, _ = rc.shape
    full = lambda a: pl.BlockSpec(a.shape, lambda i, j: (0,) * a.ndim)
    tile = lambda c: pl.BlockSpec((1, tt, c), lambda i, j: (i, j, 0))
    tail = pl.BlockSpec((1, 8, N_RWKV_COLS), lambda i, j: (i, jnp.maximum(j * (tt // 8) - 1, 0), 0))
    return pl.pallas_call(
        _rwkv_prep_prompt_kernel,
        grid=(b, t // tt),
        in_specs=[tile(N_RWKV_COLS), tail] + [full(a) for a in params],
        out_specs=[tile(R_WIDTH)] * 8,
        out_shape=[jax.ShapeDtypeStruct((b, t, R_WIDTH), F32)] * 8,
        compiler_params=_cparams("parallel", "parallel"),
        name="rwkv_prep_prompt",
    )(rc, rc, *params)


def _rwkv_prep_sample(rc, prev, params):
    n = rc.shape[0]
    full = lambda a: pl.BlockSpec(a.shape, lambda i: (0,) * a.ndim)
    return pl.pallas_call(
        _rwkv_prep_sample_kernel,
        grid=(1,),
        in_specs=[full(rc), full(prev)] + [full(a) for a in params],
        out_specs=[pl.BlockSpec((n, R_WIDTH), lambda i: (0, 0))] * 8,
        out_shape=[jax.ShapeDtypeStruct((n, R_WIDTH), F32)] * 8,
        compiler_params=_cparams("arbitrary"),
        name="rwkv_prep_sample",
    )(rc, prev, *params)


def _seg_lane_sum(x, lo_mask):
    lo = jnp.sum(jnp.where(lo_mask, x, 0.0), axis=-1, keepdims=True)
    hi = jnp.sum(jnp.where(lo_mask, 0.0, x), axis=-1, keepdims=True)
    return jnp.where(lo_mask, lo, hi)


def _rwkv_scan_kernel(r_ref, w_ref, k_ref, v_ref, kk_ref, ka_ref, s0_ref, y_ref, sT_ref, st_ref, *, bb, tc):
    pairs = R_HEADS // 2
    w2 = 2 * R_HEAD

    @pl.when(pl.program_id(1) == 0)
    def _():
        for b in range(bb):
            for p in range(pairs):
                st_ref[b, p] = jnp.concatenate([s0_ref[b, 2 * p], s0_ref[b, 2 * p + 1]], axis=1)

    lane = lax.broadcasted_iota(jnp.int32, (R_HEAD, w2), 1)
    sub = lax.broadcasted_iota(jnp.int32, (R_HEAD, w2), 0)
    lo_mask = lane < R_HEAD
    eye2 = (lane & (R_HEAD - 1)) == sub

    grp = min(8, tc)

    def group(tg, carry):
        rows = pl.ds(pl.multiple_of(tg * grp, grp), grp)
        for b in range(bb):
            for p in range(pairs):
                cols = slice(p * w2, (p + 1) * w2)
                r8, w8, k8, v8, kk8, ka8 = (ref[b, rows, cols] for ref in (r_ref, w_ref, k_ref, v_ref, kk_ref, ka_ref))
                s = st_ref[b, p]
                sb = _bf(s)
                ys = []
                for j in range(grp):
                    row = lambda a: a[j:j + 1, :]
                    sa = -_seg_lane_sum(sb * row(kk8), lo_mask)
                    vcol = _seg_lane_sum(jnp.where(eye2, jnp.broadcast_to(row(v8), (R_HEAD, w2)), 0.0), lo_mask)
                    s = s * row(w8) + sa * row(ka8) + vcol * row(k8)
                    sb = _bf(s)
                    yfull = _seg_lane_sum(sb * row(r8), lo_mask)
                    ys.append(jnp.sum(jnp.where(eye2, yfull, 0.0), axis=0, keepdims=True))
                st_ref[b, p] = s
                y_ref[b, rows, cols] = jnp.concatenate(ys, axis=0) if grp > 1 else ys[0]
        return carry

    lax.fori_loop(0, tc // grp, group, 0)

    @pl.when(pl.program_id(1) == pl.num_programs(1) - 1)
    def _():
        for b in range(bb):
            for p in range(pairs):
                s = st_ref[b, p]
                sT_ref[b, 2 * p] = s[:, :R_HEAD]
                sT_ref[b, 2 * p + 1] = s[:, R_HEAD:]


def _rwkv_scan(r, w, k, v, kk, ka, s0, *, bb, tc):
    b, t, _ = r.shape
    seq = pl.BlockSpec((bb, tc, R_WIDTH), lambda i, j: (i, j, 0))
    state = pl.BlockSpec((bb, R_HEADS, R_HEAD, R_HEAD), lambda i, j: (i, 0, 0, 0))
    return pl.pallas_call(
        functools.partial(_rwkv_scan_kernel, bb=bb, tc=tc),
        grid=(b // bb, t // tc),
        in_specs=[seq] * 6 + [state],
        out_specs=[seq, state],
        out_shape=[jax.ShapeDtypeStruct((b, t, R_WIDTH), F32),
                   jax.ShapeDtypeStruct((b, R_HEADS, R_HEAD, R_HEAD), F32)],
        scratch_shapes=[pltpu.VMEM((bb, R_HEADS // 2, R_HEAD, 2 * R_HEAD), F32)],
        compiler_params=_cparams("parallel", "arbitrary"),
        name="rwkv_scan",
    )(r, w, k, v, kk, ka, s0)


CHAINS = LANES // 2
K2 = R_HEAD // 2


def _chain_rows(x):
    b, t, _ = x.shape
    x = x.reshape(b, t, R_HEADS, 2, K2).transpose(1, 4, 3, 0, 2)
    return x.reshape(t, K2, LANES)


def _lane_scan_kernel(kk_ref, w_ref, ka_ref, kx_ref, r_ref, v_ref, s0_ref, y_ref, sT_ref, s_ref, *, tc):
    lo = lax.broadcasted_iota(jnp.int32, (K2, LANES), 1) < CHAINS

    @pl.when(pl.program_id(0) == 0)
    def _():
        for k2 in range(K2):
            s_ref[k2] = s0_ref[k2]

    def both_halves(x):
        return x + pltpu.roll(x, CHAINS, axis=1)

    nhalf = 2
    vh = R_HEAD // nhalf

    def row(ref, t, k2):
        return jnp.broadcast_to(ref[t, k2:k2 + 1, :], (vh, LANES))

    acc0 = []
    for hf in range(nhalf):
        vs = slice(hf * vh, (hf + 1) * vh)
        a = jnp.zeros((vh, LANES), F32)
        for k2 in range(K2):
            a = a + _bf(s_ref[k2, vs, :]) * row(kk_ref, 0, k2)
        acc0.append(a)

    def step(t, acc):
        tn = jnp.minimum(t + 1, tc - 1)
        nxt, ys = [], []
        v_t = v_ref[t]
        v_sw = pltpu.roll(v_t, CHAINS, axis=1)
        for hf in range(nhalf):
            vs = slice(hf * vh, (hf + 1) * vh)
            vv = jnp.where(lo, v_t, v_sw) if hf == 0 else jnp.where(lo, v_sw, v_t)
            sa = -both_halves(acc[hf])
            yacc = jnp.zeros((vh, LANES), F32)
            nacc = jnp.zeros((vh, LANES), F32)
            for k2 in range(K2):
                s = s_ref[k2, vs, :] * row(w_ref, t, k2) + sa * row(ka_ref, t, k2) + vv * row(kx_ref, t, k2)
                s_ref[k2, vs, :] = s
                sb = _bf(s)
                yacc = yacc + sb * row(r_ref, t, k2)
                nacc = nacc + sb * row(kk_ref, tn, k2)
            ys.append(both_halves(yacc))
            nxt.append(nacc)
        y_ref[t] = jnp.where(lo, ys[0], ys[1])
        return tuple(nxt)

    lax.fori_loop(0, tc, step, tuple(acc0))

    @pl.when(pl.program_id(0) == pl.num_programs(0) - 1)
    def _():
        for k2 in range(K2):
            sT_ref[k2] = s_ref[k2]


def _lane_scan(r, w, k, v, kk, ka, s0, *, tc):
    b, t, _ = r.shape
    assert b * R_HEADS == CHAINS
    ops = [_chain_rows(x) for x in (kk, w, ka, k, r, v)]
    s0c = s0.reshape(b, R_HEADS, 2, K2, 2, K2).transpose(5, 2, 3, 4, 0, 1).reshape(K2, R_HEAD, LANES)
    rows = pl.BlockSpec((tc, K2, LANES), lambda i: (i, 0, 0))
    state = pl.BlockSpec((K2, R_HEAD, LANES), lambda i: (0, 0, 0))
    y, st = pl.pallas_call(
        functools.partial(_lane_scan_kernel, tc=tc),
        grid=(t // tc,),
        in_specs=[rows] * 6 + [state],
        out_specs=[rows, state],
        out_shape=[jax.ShapeDtypeStruct((t, K2, LANES), F32), jax.ShapeDtypeStruct((K2, R_HEAD, LANES), F32)],
        scratch_shapes=[pltpu.VMEM((K2, R_HEAD, LANES), F32)],
        compiler_params=_cparams("arbitrary"),
        name="rwkv_lane_scan",
    )(*ops, s0c)
    y = y.reshape(t, K2, 2, b, R_HEADS).transpose(3, 0, 4, 2, 1).reshape(b, t, R_WIDTH)
    st = st.reshape(K2, 2, K2, 2, b, R_HEADS).transpose(4, 5, 1, 2, 3, 0).reshape(b, R_HEADS, R_HEAD, R_HEAD)
    return y, st


def _post_kernel(x_ref, y_ref, bonus_ref, g_ref, oa_ref, sg_ref, gnw_ref, gnb_ref, seg_ref, wba_ref, wbb_ref,
                 wout_ref, ln2_ref, wr_ref, br_ref, *rest, aliased, n_main):
    outs = rest[aliased:]
    x1_ref, h2_ref, ti_ref, tg_ref = outs

    @pl.when(pl.program_id(0) >= n_main)
    def _():
        for o in outs:
            o[...] = jnp.zeros(o.shape, o.dtype)

    @pl.when(pl.program_id(0) < n_main)
    def _():
        _post_body(x_ref, y_ref, bonus_ref, g_ref, oa_ref, sg_ref, gnw_ref, gnb_ref, seg_ref, wba_ref, wbb_ref,
                   wout_ref, ln2_ref, wr_ref, br_ref, x1_ref, h2_ref, ti_ref, tg_ref)


def _post_body(x_ref, y_ref, bonus_ref, g_ref, oa_ref, sg_ref, gnw_ref, gnb_ref, seg_ref, wba_ref, wbb_ref,
               wout_ref, ln2_ref, wr_ref, br_ref, x1_ref, h2_ref, ti_ref, tg_ref):
    y = y_ref[...]
    seg = seg_ref[...]
    mu = _segdot(y, seg) * (1.0 / R_HEAD)
    yc = y - mu
    var = _segdot(yc * yc, seg) * (1.0 / R_HEAD)
    yn = yc * lax.rsqrt(var + GN_EPS) * gnw_ref[...] + gnb_ref[...]
    o_b = (yn + bonus_ref[...]) * g_ref[...]
    mixed = (sg_ref[:, :D_MODEL] * _dot(oa_ref[...], wba_ref[...])
             + sg_ref[:, D_MODEL:] * _dot(o_b, wbb_ref[...]))
    x1 = x_ref[...] + _dot(mixed, wout_ref[...])
    x1_ref[...] = x1
    h2 = _rms(x1, ln2_ref[...])
    h2_ref[...] = h2
    logits = _dot(h2, wr_ref[...]) + br_ref[...]
    lane = lax.broadcasted_iota(jnp.int32, logits.shape, 1).astype(F32)
    work = logits
    vals, idxs = [], []
    for _ in range(TOP_K):
        m = jnp.max(work, axis=-1, keepdims=True)
        idx = jnp.min(jnp.where(work == m, lane, float(LANES)), axis=-1, keepdims=True)
        vals.append(m)
        idxs.append(idx)
        work = jnp.where(lane == idx, -jnp.inf, work)
    es = [jnp.exp(v - vals[0]) for v in vals]
    tot = es[0] + es[1] + es[2] + es[3]
    ti = jnp.zeros(logits.shape, F32)
    tg = jnp.zeros(logits.shape, F32)
    for kslot in range(TOP_K):
        ti = jnp.where(lane == float(kslot), idxs[kslot], ti)
        tg = jnp.where(lane == float(kslot), es[kslot] / tot, tg)
    ti_ref[...] = ti.astype(jnp.int32)
    tg_ref[...] = tg


def _post(x, y, bonus, g, oa, sg, consts, *, tm, n_total, row0=0, into=None):
    n = x.shape[0]
    blk0 = row0 // tm
    n_main = n // tm
    steps = n_total // tm if into is None else n_main
    row = lambda c: pl.BlockSpec((tm, c), lambda i: (jnp.minimum(i, n_main - 1), 0))
    orow = lambda c: pl.BlockSpec((tm, c), lambda i: (i + blk0, 0))
    full = lambda a: pl.BlockSpec(a.shape, lambda i: (0,) * a.ndim)
    out_cols = ((D_MODEL, F32), (D_MODEL, F32), (LANES, jnp.int32), (LANES, F32))
    ins = [x, y, bonus, g, oa, sg, *consts]
    in_specs = [row(D_MODEL), row(R_WIDTH), row(R_WIDTH), row(R_WIDTH), row(A_WIDTH), row(2 * D_MODEL)]
    in_specs += [full(a) for a in consts]
    aliases = {}
    if into is not None:
        aliases = {len(ins) + i: i for i in range(len(into))}
        in_specs += [pl.BlockSpec(memory_space=pl.ANY)] * len(into)
        ins += list(into)
    return pl.pallas_call(
        functools.partial(_post_kernel, aliased=len(aliases), n_main=n_main),
        grid=(steps,),
        in_specs=in_specs,
        out_specs=[orow(c) for c, _ in out_cols],
        out_shape=[jax.ShapeDtypeStruct((n_total, c), dt) for c, dt in out_cols],
        input_output_aliases=aliases,
        compiler_params=_cparams("parallel"),
        name="post",
    )(*ins)


def _start_rows(rows_ref, hbm, buf, sem, *, rows, to_hbm):
    for r in range(rows):
        at = pl.ds(rows_ref[0, 0, r], 1)
        prio = r % 2
        if to_hbm:
            pltpu.make_async_copy(buf.at[pl.ds(r, 1)], hbm.at[at], sem).start(priority=prio)
        else:
            pltpu.make_async_copy(hbm.at[at], buf.at[pl.ds(r, 1)], sem).start(priority=prio)


def _wait_rows(hbm, buf, sem, *, rows, to_hbm):
    whole = hbm.at[pl.ds(0, rows)]
    (pltpu.make_async_copy(buf, whole, sem) if to_hbm else pltpu.make_async_copy(whole, buf, sem)).wait()


def _expert_mlp(xb, wgu_bf, wd_bf, bgu_ref, bd_ref, rows):
    acc = jnp.broadcast_to(bd_ref[0], (rows, D_MODEL))
    cw = 512
    for c in range(D_MODEL // cw):
        gs = slice(c * cw, (c + 1) * cw)
        us = slice(D_MODEL + c * cw, D_MODEL + (c + 1) * cw)
        gt = jnp.dot(xb, wgu_bf[:, gs], preferred_element_type=F32) + bgu_ref[0, :, gs]
        up = jnp.dot(xb, wgu_bf[:, us], preferred_element_type=F32) + bgu_ref[0, :, us]
        gt = jnp.minimum(gt, SWIGLU_LIMIT)
        up = jnp.clip(up, -SWIGLU_LIMIT, SWIGLU_LIMIT)
        act = (up + 1.0) * (gt * jax.nn.sigmoid(gt * SWIGLU_ALPHA))
        acc = acc + jnp.dot(act.astype(BF16), wd_bf[gs, :], preferred_element_type=F32)
    return acc


def _moe_kernel(be_ref, src_cur, src_nxt, dst_prv, dst_cur, h_hbm, wgu_ref, bgu_ref, wd_ref, bd_ref, ys_hbm,
                x0, x1, y0, y1, wgu_bf, wd_bf, gsem, ssem, *, rows, plane_rows, n_tok):
    i = pl.program_id(0)
    last = pl.num_programs(0) - 1

    changed = jnp.logical_or(i == 0, be_ref[i] != be_ref[jnp.maximum(i - 1, 0)])

    @pl.when(changed)
    def _():
        step = 128

        def cast(j, c):
            rs = pl.ds(pl.multiple_of(j * step, step), step)
            wgu_bf[rs, :] = wgu_ref[0, rs, :].astype(BF16)
            wd_bf[rs, :] = wd_ref[0, rs, :].astype(BF16)
            return c

        lax.fori_loop(0, D_MODEL // step, cast, 0)

    def run(par):
        xc, xn = (x0, x1) if par == 0 else (x1, x0)
        yc, yp = (y0, y1) if par == 0 else (y1, y0)
        gather = functools.partial(_start_rows, hbm=h_hbm, rows=rows, to_hbm=False)
        scatter = functools.partial(_start_rows, hbm=ys_hbm, rows=rows, to_hbm=True)
        gathered = functools.partial(_wait_rows, h_hbm, rows=rows, to_hbm=False)
        scattered = functools.partial(_wait_rows, ys_hbm, rows=rows, to_hbm=True)

        @pl.when(i == 0)
        def _():
            yp[...] = jnp.zeros(yp.shape, yp.dtype)
            spare = plane_rows - n_tok
            fills = [pltpu.make_async_copy(yp.at[pl.ds(0, spare)],
                                           ys_hbm.at[pl.ds(k * plane_rows + n_tok, spare)], ssem.at[1 - par])
                     for k in range(TOP_K)]
            for d in fills:
                d.start()
            for d in fills:
                d.wait()
            gather(src_cur, buf=xc, sem=gsem.at[par])

        gathered(xc, gsem.at[par])
        gather(src_nxt, buf=xn, sem=gsem.at[1 - par])

        @pl.when(i >= 1)
        def _():
            scattered(yc, ssem.at[par])

        scatter(dst_prv, buf=yp, sem=ssem.at[1 - par])
        yc[...] = _expert_mlp(xc[...].astype(BF16), wgu_bf, wd_bf, bgu_ref, bd_ref, rows)

        @pl.when(i == last)
        def _():
            scatter(dst_cur, buf=yc, sem=ssem.at[par])
            gathered(xn, gsem.at[1 - par])
            scattered(yp, ssem.at[1 - par])
            scattered(yc, ssem.at[par])

    for par in range(2):
        pl.when(lax.rem(i, 2) == par)(functools.partial(run, par))


def _moe_experts(block_e, ids, h2, w_gate_up, b_gate_up, w_down, b_down, *, rows, plane_rows, n_tok):
    n_blocks = ids.shape[0] - 2
    ids = lax.optimization_barrier(ids)
    src = lax.shift_right_logical(ids, 2)
    dst = (ids & (TOP_K - 1)) * plane_rows + src
    idblk = lambda off: pl.BlockSpec((1, 1, rows), lambda i, be: (i + off, 0, 0), memory_space=pltpu.SMEM)
    grid_spec = pltpu.PrefetchScalarGridSpec(
        num_scalar_prefetch=1,
        grid=(n_blocks,),
        in_specs=[
            idblk(1), idblk(2), idblk(0), idblk(1),
            pl.BlockSpec(memory_space=pl.ANY),
            pl.BlockSpec((1, D_MODEL, 2 * D_MODEL), lambda i, be: (be[i], 0, 0)),
            pl.BlockSpec((1, 1, 2 * D_MODEL), lambda i, be: (be[i], 0, 0)),
            pl.BlockSpec((1, D_MODEL, D_MODEL), lambda i, be: (be[i], 0, 0)),
            pl.BlockSpec((1, 1, D_MODEL), lambda i, be: (be[i], 0, 0)),
        ],
        out_specs=pl.BlockSpec(memory_space=pl.ANY),
        scratch_shapes=[pltpu.VMEM((rows, D_MODEL), F32)] * 4 + [
            pltpu.VMEM((D_MODEL, 2 * D_MODEL), BF16),
            pltpu.VMEM((D_MODEL, D_MODEL), BF16),
            pltpu.SemaphoreType.DMA((2,)),
            pltpu.SemaphoreType.DMA((2,))],
    )
    return pl.pallas_call(
        functools.partial(_moe_kernel, rows=rows, plane_rows=plane_rows, n_tok=n_tok),
        grid_spec=grid_spec,
        out_shape=jax.ShapeDtypeStruct((TOP_K * plane_rows, D_MODEL), F32),
        compiler_params=_cparams("arbitrary"),
        name="moe_experts",
    )(block_e, src, src, dst, dst, h2, w_gate_up, b_gate_up[:, None], w_down, b_down[:, None])


def _combine_kernel(ys_ref, x1_ref, gate_ref, lnf_ref, y_ref):
    acc = x1_ref[...]
    for kslot in range(TOP_K):
        acc = acc + gate_ref[:, kslot:kslot + 1] * ys_ref[kslot]
    y_ref[...] = _rms(acc, lnf_ref[...])


def _combine(ys, x1, gates, ln_f, *, tm, n, row0):
    blk0 = row0 // tm
    return pl.pallas_call(
        _combine_kernel,
        grid=(n // tm,),
        in_specs=[
            pl.BlockSpec((TOP_K, tm, D_MODEL), lambda i: (0, i + blk0, 0)),
            pl.BlockSpec((tm, D_MODEL), lambda i: (i + blk0, 0)),
            pl.BlockSpec((tm, LANES), lambda i: (i + blk0, 0)),
            pl.BlockSpec(ln_f.shape, lambda i: (0, 0)),
        ],
        out_specs=pl.BlockSpec((tm, D_MODEL), lambda i: (i, 0)),
        out_shape=jax.ShapeDtypeStruct((n, D_MODEL), F32),
        compiler_params=_cparams("parallel"),
        name="combine",
    )(ys.reshape(TOP_K, -1, D_MODEL), x1, gates, ln_f)


def _route(topi, n_pad, rows):
    n_tok = topi.shape[0]
    e_flat = topi.reshape(-1)
    nk = e_flat.shape[0]
    n_blocks = -(-(nk + N_EXPERTS * (rows - 1)) // rows)
    spare = n_pad - n_tok
    assert 3 * rows <= TOP_K * spare
    experts = jnp.arange(N_EXPERTS, dtype=jnp.int32)
    order = jnp.argsort(e_flat).astype(jnp.int32)
    sizes = jnp.sum((e_flat[:, None] == experts[None, :]).astype(jnp.int32), axis=0)
    padded = (sizes + rows - 1) // rows * rows
    pends = jnp.cumsum(padded)
    slot = jnp.arange(n_blocks * rows, dtype=jnp.int32)
    past = (slot[:, None] >= pends[None, :]).astype(jnp.int32)
    e_slot = jnp.sum(past, axis=1)
    off = slot - jnp.sum(past * padded[None, :], axis=1)
    src = jnp.sum(past * sizes[None, :], axis=1) + off
    size_e = jnp.sum((e_slot[:, None] == experts[None, :]) * sizes[None, :], axis=1)

    def pad_ids(d):
        return (TOP_K * (n_tok + d % spare) + d // spare).astype(jnp.int32)

    ids = jnp.where(off < size_e, order[jnp.clip(src, 0, nk - 1)], pad_ids((slot // rows) % 2 * rows + slot % rows))
    end_ids = pad_ids(2 * rows + jnp.arange(rows, dtype=jnp.int32))
    ids = jnp.concatenate([end_ids, ids, end_ids]).reshape(n_blocks + 2, 1, rows)
    block_e = jnp.minimum(e_slot[::rows], N_EXPERTS - 1).astype(jnp.int32)
    return block_e, ids


def kernel(x_prompt, x_sample, cache_k, cache_v, state_wkv, state_shift, rel_bias, ln1, w_in, rwkv_mu, w0,
           w_decay_up, a0, w_a_up, w_g_up, k_k, k_a, r_k, gn_w, gn_b, w_branch_a, w_branch_b, w_out, ln2,
           w_router, b_router, w_gate_up, b_gate_up, w_down, b_down, ln_f):
    bp, seq, _ = x_prompt.shape
    db = x_sample.shape[0]
    n_p = bp * seq
    n_tot = n_p + db
    n_pad = -(-(n_tot + -(-3 * MOE_ROWS // TOP_K)) // TOKEN_TILE) * TOKEN_TILE
    l = 0

    row = lambda a: a.reshape(1, -1)
    seg = _head_seg()
    prep_params = (row(rwkv_mu[l]), row(w0[l]), w_decay_up[l], row(a0[l]), w_a_up[l], w_g_up[l],
                   row(k_k[l]), row(k_a[l]), row(r_k[l]), seg)
    wr_pad = jnp.pad(w_router[l], ((0, 0), (0, LANES - N_EXPERTS))).astype(BF16)
    br_pad = jnp.pad(b_router[l], (0, LANES - N_EXPERTS), constant_values=NEG).reshape(1, LANES)
    post_small = (row(gn_w[l]), row(gn_b[l]), seg)
    post_tail = (row(ln2[l]), wr_pad, br_pad)

    w_in_bf = w_in[l].astype(BF16)
    consts = (post_small + (w_branch_a[l].astype(BF16), w_branch_b[l].astype(BF16), w_out[l].astype(BF16))
              + post_tail)

    xp = x_prompt.reshape(n_p, D_MODEL)
    q_p, k_p, v_p, rc_p, sg_p = _inproj(xp, row(ln1[l]), w_in_bf, tm=TOKEN_TILE)
    as3 = lambda a: a.reshape(bp, seq, -1)
    oa_p = _attn_prompt(as3(q_p), as3(k_p), as3(v_p), _prompt_bias(rel_bias))
    prep_p = _rwkv_prep_prompt(as3(rc_p), prep_params, tt=TOKEN_TILE)
    r_, w_, k2_, vr_, kk_, ka_, bonus_p, g_p = prep_p
    s0_p = jnp.zeros((bp, R_HEADS, R_HEAD, R_HEAD), F32)
    y_p, wkv_p = _lane_scan(r_, w_, k2_, vr_, kk_, ka_, s0_p, tc=SCAN_CHUNK)
    flat = lambda a: a.reshape(n_p, -1)
    bufs = _post(xp, flat(y_p), flat(bonus_p), flat(g_p), flat(oa_p), sg_p, consts,
                 tm=TOKEN_TILE, n_total=n_pad)

    xs = x_sample.reshape(db, D_MODEL)
    q_s, k_s, v_s, rc_s, sg_s = _inproj(xs, row(ln1[l]), w_in_bf, tm=db)
    bias_s, b0_s = _sample_bias(rel_bias, cache_k.shape[2])
    oa_s = _attn_sample(q_s, k_s, v_s, cache_k[l], cache_v[l], bias_s, b0_s)
    prep_s = _rwkv_prep_sample(rc_s, state_shift[l], prep_params)
    sr, sw, sk2, svr, skk, ska, bonus_s, g_s = [a[:, None] for a in prep_s]
    y_s, wkv_s = _rwkv_scan(sr, sw, sk2, svr, skk, ska, state_wkv[l], bb=SCAN_BATCH, tc=1)
    x1_all, h2_all, ti_all, tg_all = _post(xs, y_s[:, 0], bonus_s[:, 0], g_s[:, 0], oa_s, sg_s, consts,
                                           tm=db, n_total=n_pad, row0=n_p, into=bufs)

    block_e, ids = _route(ti_all[:n_tot, :TOP_K], n_pad, MOE_ROWS)
    ys = _moe_experts(block_e, ids, h2_all, w_gate_up[l], b_gate_up[l], w_down[l], b_down[l],
                      rows=MOE_ROWS, plane_rows=n_pad, n_tok=n_tot)
    lnf = row(ln_f)
    y_prompt = _combine(ys, x1_all, tg_all, lnf, tm=TOKEN_TILE, n=n_p, row0=0)
    y_sample = _combine(ys, x1_all, tg_all, lnf, tm=db, n=db, row0=n_p)

    heads = lambda a, b_: a.reshape(1, b_, -1, A_HEADS, HEAD_DIM)
    return (y_prompt.reshape(bp, seq, D_MODEL), y_sample.reshape(db, 1, D_MODEL),
            heads(k_p, bp), heads(v_p, bp), wkv_p[None], as3(rc_p)[:, -1][None],
            heads(k_s, db), heads(v_s, db), wkv_s[None], rc_s[None])
```

```python
import functools
import math

import jax
import jax.numpy as jnp
from jax import lax
from jax.experimental import pallas as pl
from jax.experimental.pallas import tpu as pltpu

F32 = jnp.float32
BF16 = jnp.bfloat16

D_MODEL = 1024
A_HEADS = 8
HEAD_DIM = 64
A_WIDTH = A_HEADS * HEAD_DIM
PATTERNS = ((128, 1), (512, 4), (2048, 16))
BLOCK = 128
NUM_BUCKETS = 32
MAX_DISTANCE = 2048
SCALE = HEAD_DIM ** -0.5
NEG = -1e30
R_HEADS = 8
R_HEAD = 64
R_WIDTH = R_HEADS * R_HEAD
DECAY_LORA = 64
AAA_LORA = 64
GATE_LORA = 128
GN_EPS = 64e-5
N_RWKV_COLS = 3 * R_WIDTH + DECAY_LORA + AAA_LORA + GATE_LORA
N_IN_COLS = 3 * A_WIDTH + N_RWKV_COLS + 2 * D_MODEL
N_EXPERTS = 32
TOP_K = 4
SWIGLU_LIMIT = 7.0
SWIGLU_ALPHA = 1.702
RMS_EPS = 1e-6

LANES = 128
SUBLANES = 8
VMEM_LIMIT = 56 * 1024 * 1024
TOKEN_TILE = 256
MOE_ROWS = 512
SCAN_BATCH = 2
SCAN_CHUNK = 64


def _cparams(*sem):
    return pltpu.CompilerParams(dimension_semantics=sem, vmem_limit_bytes=VMEM_LIMIT)


def _rms(x, g):
    return x * lax.rsqrt(jnp.mean(x * x, axis=-1, keepdims=True) + RMS_EPS) * g


def _dot(a, b):
    return jnp.dot(a.astype(BF16), b.astype(BF16), preferred_element_type=F32)


def _bf(x):
    return x.astype(BF16).astype(F32)


def _split3(x):
    hi = x.astype(BF16)
    r1 = x - hi.astype(F32)
    mid = r1.astype(BF16)
    lo = (r1 - mid.astype(F32)).astype(BF16)
    return hi, mid, lo


def _segdot(x, seg):
    hi, mid, lo = _split3(x)
    d = lambda p: jnp.dot(p, seg, preferred_element_type=F32)
    return d(hi) + d(mid) + d(lo)


TILE_ROWS = D_MODEL // LANES
assert TILE_ROWS == SUBLANES


def _load_token_tiles(ref, n, *lead):
    return jnp.concatenate([ref[(*lead, pl.ds(p, n, stride=TILE_ROWS), slice(None))] for p in range(TILE_ROWS)], axis=1)


def _store_token_tiles(ref, x):
    n = x.shape[0]
    for p in range(TILE_ROWS):
        ref[pl.ds(p, n, stride=TILE_ROWS), :] = x[:, p * LANES:(p + 1) * LANES]


def _inproj_kernel(x_ref, g_ref, w_ref, q_ref, k_ref, v_ref, rc_ref, sg_ref):
    h = _rms(x_ref[...], g_ref[...]).astype(BF16)

    def mm(c0, c1):
        return _dot(h, w_ref[:, c0:c1])

    c1 = 3 * A_WIDTH
    c2 = c1 + N_RWKV_COLS
    q_ref[...] = mm(0, A_WIDTH)
    k_ref[...] = mm(A_WIDTH, 2 * A_WIDTH)
    v_ref[...] = mm(2 * A_WIDTH, c1)
    rc_ref[...] = mm(c1, c2)
    sg_ref[:, :D_MODEL] = jax.nn.sigmoid(mm(c2, c2 + D_MODEL))
    sg_ref[:, D_MODEL:] = jax.nn.sigmoid(mm(c2 + D_MODEL, N_IN_COLS))


def _inproj(x, ln1, w, *, tm):
    n = x.shape[0]
    row = lambda c: pl.BlockSpec((tm, c), lambda i: (i, 0))
    full = lambda a: pl.BlockSpec(a.shape, lambda i: (0,) * a.ndim)
    out_cols = (A_WIDTH, A_WIDTH, A_WIDTH, N_RWKV_COLS, 2 * D_MODEL)
    return pl.pallas_call(
        _inproj_kernel,
        grid=(n // tm,),
        in_specs=[row(D_MODEL), full(ln1), full(w)],
        out_specs=[row(c) for c in out_cols],
        out_shape=[jax.ShapeDtypeStruct((n, c), F32) for c in out_cols],
        compiler_params=_cparams("parallel"),
        name="inproj",
    )(x, ln1, w)


def _t5_bucket(dist):
    max_exact = NUM_BUCKETS // 2
    d = jnp.maximum(dist, 1).astype(F32)
    large = max_exact + (jnp.log(d / max_exact) / math.log(MAX_DISTANCE / max_exact)
                         * (NUM_BUCKETS - max_exact)).astype(jnp.int32)
    large = jnp.minimum(large, NUM_BUCKETS - 1)
    return jnp.where(dist < max_exact, dist, large)


def _bias_lookup(rel_bias, dist):
    onehot = (_t5_bucket(dist)[..., None] == jnp.arange(NUM_BUCKETS)).astype(F32)
    return jnp.moveaxis(jnp.dot(onehot, rel_bias.astype(F32), precision=lax.Precision.HIGHEST), -1, 0)


def _prompt_bias(rel_bias):
    i = jnp.arange(BLOCK)[:, None]
    j = jnp.arange(2 * BLOCK)[None, :]
    delta = i + BLOCK - j
    out = []
    for window, dil in PATTERNS:
        n = window // dil
        band = (delta >= 0) & (delta <= n)
        out.append(jnp.where(band[None], _bias_lookup(rel_bias, jnp.clip(delta, 0, n) * dil), NEG))
    return jnp.stack(out, 0)


def _sample_bias(rel_bias, wb):
    dist = wb - jnp.arange(wb)
    out = []
    for window, dil in PATTERNS:
        member = (dist % dil == 0) & (dist <= window)
        out.append(jnp.where(member[None], _bias_lookup(rel_bias, dist), NEG))
    b0 = _bias_lookup(rel_bias, jnp.zeros((1,), jnp.int32))
    return jnp.stack(out, 0), b0


def _attn_prompt_kernel(q_ref, k_ref, v_ref, bias_ref, o_ref, os_ref, ls_ref, *, seq):
    nt = (((1,), (1,)), ((), ()))
    npat = len(PATTERNS)
    w2 = 2 * HEAD_DIM
    lo = lax.broadcasted_iota(jnp.int32, (BLOCK, w2), 1) < HEAD_DIM
    key = lax.broadcasted_iota(jnp.int32, (2 * BLOCK, 2 * BLOCK), 1)

    for g, (window, dil) in enumerate(PATTERNS):
        nb = seq // (BLOCK * dil)
        sh = dil.bit_length() - 1

        def body(it, carry, g=g, dil=dil, nb=nb, sh=sh):
            r = it & (dil - 1)
            c = it >> sh
            start = c * (BLOCK * dil) + r
            rows = pl.ds(start, BLOCK, stride=dil) if dil > 1 else pl.ds(start, BLOCK)
            qb = q_ref[0, rows, :]
            kb = k_ref[0, rows, :]
            vb = v_ref[0, rows, :]
            if nb > 1:
                pstart = jnp.maximum(c - 1, 0) * (BLOCK * dil) + r
                prow = pl.ds(pstart, BLOCK, stride=dil) if dil > 1 else pl.ds(pstart, BLOCK)
                kp = k_ref[0, prow, :]
                vp = v_ref[0, prow, :]
            q2 = jnp.concatenate([jnp.where(lo, qb, 0.0), jnp.where(lo, 0.0, qb)], axis=0).astype(BF16)
            if nb > 1:
                k2 = jnp.concatenate([kp, kb], axis=0)
                v2 = jnp.concatenate([vp, vb], axis=0)
                bias = bias_ref[g, 0]
                valid = (bias > 0.5 * NEG) & ((key >= BLOCK) | (c > 0))
            else:
                k2, v2 = kb, vb
                bias = bias_ref[g, 0, :, BLOCK:]
                valid = bias > 0.5 * NEG
            s = lax.dot_general(q2, k2.astype(BF16), nt, preferred_element_type=F32)
            s = jnp.where(valid, s * SCALE + bias, NEG)
            m = jnp.max(s, axis=-1, keepdims=True)
            p = jnp.exp(s - m)
            l = jnp.sum(p, axis=-1, keepdims=True)
            o2 = jnp.dot((p * (1.0 / l)).astype(BF16), v2.astype(BF16), preferred_element_type=F32)
            lse = jnp.broadcast_to(m + jnp.log(l), (2 * BLOCK, w2))
            os_ref[g, rows, :] = jnp.where(lo, o2[:BLOCK], o2[BLOCK:])
            ls_ref[g, rows, :] = jnp.where(lo, lse[:BLOCK], lse[BLOCK:])
            return carry

        lax.fori_loop(0, dil * nb, body, 0, unroll=8)

    chunk = 256

    def merge(ch, carry):
        rows = pl.ds(pl.multiple_of(ch * chunk, chunk), chunk)
        lse = [ls_ref[g, rows, :] for g in range(npat)]
        top = functools.reduce(jnp.maximum, lse)
        e = [jnp.exp(x - top) for x in lse]
        inv = 1.0 / functools.reduce(lambda a, b_: a + b_, e)
        acc = jnp.zeros((chunk, 2 * HEAD_DIM), F32)
        for g in range(npat):
            acc = acc + (e[g] * inv) * os_ref[g, rows, :]
        o_ref[0, rows, :] = acc
        return carry

    lax.fori_loop(0, seq // chunk, merge, 0)


def _attn_prompt(q, k, v, bias):
    b, s, _ = q.shape
    w2 = 2 * HEAD_DIM
    qspec = pl.BlockSpec((1, s, w2), lambda i, p: (i, 0, p))
    return pl.pallas_call(
        functools.partial(_attn_prompt_kernel, seq=s),
        grid=(b, A_HEADS // 2),
        in_specs=[qspec, qspec, qspec,
                  pl.BlockSpec((len(PATTERNS), 1, 2 * BLOCK, 2 * BLOCK), lambda i, p: (0, p, 0, 0))],
        out_specs=qspec,
        out_shape=jax.ShapeDtypeStruct((b, s, A_WIDTH), F32),
        scratch_shapes=[pltpu.VMEM((len(PATTERNS), s, w2), F32)] * 2,
        compiler_params=_cparams("parallel", "parallel"),
        name="attn_prompt",
    )(q, k, v, bias.reshape(len(PATTERNS), A_HEADS // 2, 2 * BLOCK, 2 * BLOCK))


def _attn_sample_kernel(q_ref, kn_ref, vn_ref, kt_ref, vt_ref, bias_ref, b0_ref, o_ref):
    npat = len(PATTERNS)
    nt = (((1,), (1,)), ((), ()))
    q = q_ref[0]
    v_new = _bf(vn_ref[0])
    s0 = jnp.sum(_bf(q) * _bf(kn_ref[0]), axis=-1, keepdims=True) * SCALE + b0_ref[...]
    s = jnp.concatenate([_dot(q[h:h + 1], kt_ref[0, h]) for h in range(A_HEADS)], axis=0) * SCALE
    ps, p0s, lses = [], [], []
    for g in range(npat):
        bias = bias_ref[g]
        sg = jnp.where(bias > 0.5 * NEG, s + bias, NEG)
        m = jnp.maximum(jnp.max(sg, axis=-1, keepdims=True), s0)
        l = jnp.sum(jnp.exp(sg - m), axis=-1, keepdims=True) + jnp.exp(s0 - m)
        lse = m + jnp.log(l)
        ps.append(jnp.exp(sg - lse))
        p0s.append(_bf(jnp.exp(s0 - lse)))
        lses.append(lse)
    top = functools.reduce(jnp.maximum, lses)
    e = [jnp.exp(x - top) for x in lses]
    inv = 1.0 / functools.reduce(lambda a, b_: a + b_, e)
    w = [_bf(x * inv) for x in e]
    rows = []
    for h in range(A_HEADS):
        p_h = jnp.concatenate([p[h:h + 1] for p in ps], axis=0)
        o_h = lax.dot_general(p_h.astype(BF16), vt_ref[0, h].astype(BF16), nt, preferred_element_type=F32)
        acc = jnp.zeros((1, HEAD_DIM), F32)
        for g in range(npat):
            acc = acc + w[g][h:h + 1] * _bf(o_h[g:g + 1] + p0s[g][h:h + 1] * v_new[h:h + 1])
        rows.append(acc)
    o_ref[0] = jnp.concatenate(rows, axis=0)


def _attn_sample(q, k_new, v_new, cache_k, cache_v, bias, b0):
    db, wb = cache_k.shape[:2]
    heads = lambda a: a.reshape(db, A_HEADS, HEAD_DIM)
    vec = pl.BlockSpec((1, A_HEADS, HEAD_DIM), lambda i: (i, 0, 0))
    full = lambda a: pl.BlockSpec(a.shape, lambda i: (0,) * a.ndim)
    rows_minor = lambda a: a.transpose(0, 2, 3, 1)
    cache = pl.BlockSpec((1, A_HEADS, HEAD_DIM, wb), lambda i: (i, 0, 0, 0))
    args = [heads(q), heads(k_new), heads(v_new), rows_minor(cache_k), rows_minor(cache_v), bias, b0]
    return pl.pallas_call(
        _attn_sample_kernel,
        grid=(db,),
        in_specs=[vec, vec, vec, cache, cache, full(bias), full(b0)],
        out_specs=vec,
        out_shape=jax.ShapeDtypeStruct((db, A_HEADS, HEAD_DIM), F32),
        compiler_params=_cparams("parallel"),
        name="attn_sample",
    )(*args).reshape(db, A_WIDTH)


def _rwkv_prep_math(rc, prev, mu, w0, wd, a0, wa, wg, k_k, k_a, r_k, seg):
    xr = rc + (prev - rc) * mu
    o1, o2, o3 = R_WIDTH, 2 * R_WIDTH, 3 * R_WIDTH
    r = xr[:, :o1]
    kr = xr[:, o1:o2]
    vr = xr[:, o2:o3]
    xw = xr[:, o3:o3 + DECAY_LORA]
    xa = xr[:, o3 + DECAY_LORA:o3 + DECAY_LORA + AAA_LORA]
    xg = xr[:, o3 + DECAY_LORA + AAA_LORA:]
    z = -(w0 + _dot(jnp.tanh(xw), wd))
    softplus = jnp.maximum(z, 0.0) + jnp.log(1.0 + jnp.exp(-jnp.abs(z)))
    decay = jnp.exp(-jnp.exp(-softplus - 0.5))
    a = jax.nn.sigmoid(a0 + _dot(xa, wa))
    g = _dot(jax.nn.sigmoid(xg), wg)
    kk = kr * k_k
    k2 = kr * (1.0 + (a - 1.0) * k_a)
    kk = kk / jnp.maximum(jnp.sqrt(_segdot(kk * kk, seg)), 1e-12)
    bonus = _segdot(r * k2 * r_k, seg) * vr
    return _bf(r), decay, k2, vr, _bf(kk), kk * a, bonus, g


N_PREP_PARAMS = 10


def _rwkv_prep_prompt_kernel(rc_ref, tail_ref, *refs):
    p = [x[...] for x in refs[:N_PREP_PARAMS]]
    outs = refs[N_PREP_PARAMS:]
    rc = rc_ref[0]
    tt = rc.shape[0]
    first = pl.program_id(1) == 0
    prev_row = jnp.where(first, 0.0, tail_ref[0, 7:8, :])
    rolled = pltpu.roll(rc, 1, axis=0)
    row = lax.broadcasted_iota(jnp.int32, (tt, 1), 0)
    prev = jnp.where(row == 0, prev_row, rolled)
    for o, val in zip(outs, _rwkv_prep_math(rc, prev, *p)):
        o[0] = val


def _rwkv_prep_sample_kernel(rc_ref, prev_ref, *refs):
    p = [x[...] for x in refs[:N_PREP_PARAMS]]
    outs = refs[N_PREP_PARAMS:]
    for o, val in zip(outs, _rwkv_prep_math(rc_ref[...], prev_ref[...], *p)):
        o[...] = val


def _head_seg():
    head = jnp.arange(R_WIDTH) // R_HEAD
    return (head[:, None] == head[None, :]).astype(BF16)


def _rwkv_prep_prompt(rc, params, *, tt):
    b, t, _ = rc.shape
    full = lambda a: pl.BlockSpec(a.shape, lambda i, j: (0,) * a.ndim)
    tile = lambda c: pl.BlockSpec((1, tt, c), lambda i, j: (i, j, 0))
    tail = pl.BlockSpec((1, 8, N_RWKV_COLS), lambda i, j: (i, jnp.maximum(j * (tt // 8) - 1, 0), 0))
    return pl.pallas_call(
        _rwkv_prep_prompt_kernel,
        grid=(b, t // tt),
        in_specs=[tile(N_RWKV_COLS), tail] + [full(a) for a in params],
        out_specs=[tile(R_WIDTH)] * 8,
        out_shape=[jax.ShapeDtypeStruct((b, t, R_WIDTH), F32)] * 8,
        compiler_params=_cparams("parallel", "parallel"),
        name="rwkv_prep_prompt",
    )(rc, rc, *params)


def _rwkv_prep_sample(rc, prev, params):
    n = rc.shape[0]
    full = lambda a: pl.BlockSpec(a.shape, lambda i: (0,) * a.ndim)
    return pl.pallas_call(
        _rwkv_prep_sample_kernel,
        grid=(1,),
        in_specs=[full(rc), full(prev)] + [full(a) for a in params],
        out_specs=[pl.BlockSpec((n, R_WIDTH), lambda i: (0, 0))] * 8,
        out_shape=[jax.ShapeDtypeStruct((n, R_WIDTH), F32)] * 8,
        compiler_params=_cparams("arbitrary"),
        name="rwkv_prep_sample",
    )(rc, prev, *params)


def _seg_lane_sum(x, lo_mask):
    lo = jnp.sum(jnp.where(lo_mask, x, 0.0), axis=-1, keepdims=True)
    hi = jnp.sum(jnp.where(lo_mask, 0.0, x), axis=-1, keepdims=True)
    return jnp.where(lo_mask, lo, hi)


def _rwkv_scan_kernel(r_ref, w_ref, k_ref, v_ref, kk_ref, ka_ref, s0_ref, y_ref, sT_ref, st_ref, *, bb, tc):
    pairs = R_HEADS // 2
    w2 = 2 * R_HEAD

    @pl.when(pl.program_id(1) == 0)
    def _():
        for b in range(bb):
            for p in range(pairs):
                st_ref[b, p] = jnp.concatenate([s0_ref[b, 2 * p], s0_ref[b, 2 * p + 1]], axis=1)

    lane = lax.broadcasted_iota(jnp.int32, (R_HEAD, w2), 1)
    sub = lax.broadcasted_iota(jnp.int32, (R_HEAD, w2), 0)
    lo_mask = lane < R_HEAD
    eye2 = (lane & (R_HEAD - 1)) == sub

    grp = min(8, tc)

    def group(tg, carry):
        rows = pl.ds(pl.multiple_of(tg * grp, grp), grp)
        for b in range(bb):
            for p in range(pairs):
                cols = slice(p * w2, (p + 1) * w2)
                r8, w8, k8, v8, kk8, ka8 = (ref[b, rows, cols] for ref in (r_ref, w_ref, k_ref, v_ref, kk_ref, ka_ref))
                s = st_ref[b, p]
                sb = _bf(s)
                ys = []
                for j in range(grp):
                    row = lambda a: a[j:j + 1, :]
                    sa = -_seg_lane_sum(sb * row(kk8), lo_mask)
                    vcol = _seg_lane_sum(jnp.where(eye2, jnp.broadcast_to(row(v8), (R_HEAD, w2)), 0.0), lo_mask)
                    s = s * row(w8) + sa * row(ka8) + vcol * row(k8)
                    sb = _bf(s)
                    yfull = _seg_lane_sum(sb * row(r8), lo_mask)
                    ys.append(jnp.sum(jnp.where(eye2, yfull, 0.0), axis=0, keepdims=True))
                st_ref[b, p] = s
                y_ref[b, rows, cols] = jnp.concatenate(ys, axis=0) if grp > 1 else ys[0]
        return carry

    lax.fori_loop(0, tc // grp, group, 0)

    @pl.when(pl.program_id(1) == pl.num_programs(1) - 1)
    def _():
        for b in range(bb):
            for p in range(pairs):
                s = st_ref[b, p]
                sT_ref[b, 2 * p] = s[:, :R_HEAD]
                sT_ref[b, 2 * p + 1] = s[:, R_HEAD:]


def _rwkv_scan(r, w, k, v, kk, ka, s0, *, bb, tc):
    b, t, _ = r.shape
    seq = pl.BlockSpec((bb, tc, R_WIDTH), lambda i, j: (i, j, 0))
    state = pl.BlockSpec((bb, R_HEADS, R_HEAD, R_HEAD), lambda i, j: (i, 0, 0, 0))
    return pl.pallas_call(
        functools.partial(_rwkv_scan_kernel, bb=bb, tc=tc),
        grid=(b // bb, t // tc),
        in_specs=[seq] * 6 + [state],
        out_specs=[seq, state],
        out_shape=[jax.ShapeDtypeStruct((b, t, R_WIDTH), F32),
                   jax.ShapeDtypeStruct((b, R_HEADS, R_HEAD, R_HEAD), F32)],
        scratch_shapes=[pltpu.VMEM((bb, R_HEADS // 2, R_HEAD, 2 * R_HEAD), F32)],
        compiler_params=_cparams("parallel", "arbitrary"),
        name="rwkv_scan",
    )(r, w, k, v, kk, ka, s0)


CHAINS = LANES // 2
K2 = R_HEAD // 2


def _chain_rows(x):
    b, t, _ = x.shape
    x = x.reshape(b, t, R_HEADS, 2, K2).transpose(1, 4, 3, 0, 2)
    return x.reshape(t, K2, LANES)


def _lane_scan_kernel(kk_ref, w_ref, ka_ref, kx_ref, r_ref, v_ref, s0_ref, y_ref, sT_ref, s_ref, *, tc):
    lo = lax.broadcasted_iota(jnp.int32, (K2, LANES), 1) < CHAINS

    @pl.when(pl.program_id(0) == 0)
    def _():
        for k2 in range(K2):
            s_ref[k2] = s0_ref[k2]

    def both_halves(x):
        return x + pltpu.roll(x, CHAINS, axis=1)

    nhalf = 2
    vh = R_HEAD // nhalf

    def row(ref, t, k2):
        return jnp.broadcast_to(ref[t, k2:k2 + 1, :], (vh, LANES))

    acc0 = []
    for hf in range(nhalf):
        vs = slice(hf * vh, (hf + 1) * vh)
        a = jnp.zeros((vh, LANES), F32)
        for k2 in range(K2):
            a = a + _bf(s_ref[k2, vs, :]) * row(kk_ref, 0, k2)
        acc0.append(a)

    def step(t, acc):
        tn = jnp.minimum(t + 1, tc - 1)
        nxt, ys = [], []
        v_t = v_ref[t]
        v_sw = pltpu.roll(v_t, CHAINS, axis=1)
        for hf in range(nhalf):
            vs = slice(hf * vh, (hf + 1) * vh)
            vv = jnp.where(lo, v_t, v_sw) if hf == 0 else jnp.where(lo, v_sw, v_t)
            sa = -both_halves(acc[hf])
            yacc = jnp.zeros((vh, LANES), F32)
            nacc = jnp.zeros((vh, LANES), F32)
            for k2 in range(K2):
                s = s_ref[k2, vs, :] * row(w_ref, t, k2) + sa * row(ka_ref, t, k2) + vv * row(kx_ref, t, k2)
                s_ref[k2, vs, :] = s
                sb = _bf(s)
                yacc = yacc + sb * row(r_ref, t, k2)
                nacc = nacc + sb * row(kk_ref, tn, k2)
            ys.append(both_halves(yacc))
            nxt.append(nacc)
        y_ref[t] = jnp.where(lo, ys[0], ys[1])
        return tuple(nxt)

    lax.fori_loop(0, tc, step, tuple(acc0))

    @pl.when(pl.program_id(0) == pl.num_programs(0) - 1)
    def _():
        for k2 in range(K2):
            sT_ref[k2] = s_ref[k2]


def _lane_scan(r, w, k, v, kk, ka, s0, *, tc):
    b, t, _ = r.shape
    assert b * R_HEADS == CHAINS
    ops = [_chain_rows(x) for x in (kk, w, ka, k, r, v)]
    s0c = s0.reshape(b, R_HEADS, 2, K2, 2, K2).transpose(5, 2, 3, 4, 0, 1).reshape(K2, R_HEAD, LANES)
    rows = pl.BlockSpec((tc, K2, LANES), lambda i: (i, 0, 0))
    state = pl.BlockSpec((K2, R_HEAD, LANES), lambda i: (0, 0, 0))
    y, st = pl.pallas_call(
        functools.partial(_lane_scan_kernel, tc=tc),
        grid=(t // tc,),
        in_specs=[rows] * 6 + [state],
        out_specs=[rows, state],
        out_shape=[jax.ShapeDtypeStruct((t, K2, LANES), F32), jax.ShapeDtypeStruct((K2, R_HEAD, LANES), F32)],
        scratch_shapes=[pltpu.VMEM((K2, R_HEAD, LANES), F32)],
        compiler_params=_cparams("arbitrary"),
        name="rwkv_lane_scan",
    )(*ops, s0c)
    y = y.reshape(t, K2, 2, b, R_HEADS).transpose(3, 0, 4, 2, 1).reshape(b, t, R_WIDTH)
    st = st.reshape(K2, 2, K2, 2, b, R_HEADS).transpose(4, 5, 1, 2, 3, 0).reshape(b, R_HEADS, R_HEAD, R_HEAD)
    return y, st


def _post_kernel(x_ref, y_ref, bonus_ref, g_ref, oa_ref, sg_ref, gnw_ref, gnb_ref, seg_ref, wba_ref, wbb_ref,
                 wout_ref, ln2_ref, wr_ref, br_ref, *rest, aliased, n_main):
    outs = rest[aliased:]
    x1_ref, h2_ref, ti_ref, tg_ref = outs

    @pl.when(pl.program_id(0) >= n_main)
    def _():
        for o in outs:
            o[...] = jnp.zeros(o.shape, o.dtype)

    @pl.when(pl.program_id(0) < n_main)
    def _():
        _post_body(x_ref, y_ref, bonus_ref, g_ref, oa_ref, sg_ref, gnw_ref, gnb_ref, seg_ref, wba_ref, wbb_ref,
                   wout_ref, ln2_ref, wr_ref, br_ref, x1_ref, h2_ref, ti_ref, tg_ref)


def _post_body(x_ref, y_ref, bonus_ref, g_ref, oa_ref, sg_ref, gnw_ref, gnb_ref, seg_ref, wba_ref, wbb_ref,
               wout_ref, ln2_ref, wr_ref, br_ref, x1_ref, h2_ref, ti_ref, tg_ref):
    y = y_ref[...]
    seg = seg_ref[...]
    mu = _segdot(y, seg) * (1.0 / R_HEAD)
    yc = y - mu
    var = _segdot(yc * yc, seg) * (1.0 / R_HEAD)
    yn = yc * lax.rsqrt(var + GN_EPS) * gnw_ref[...] + gnb_ref[...]
    o_b = (yn + bonus_ref[...]) * g_ref[...]
    mixed = (sg_ref[:, :D_MODEL] * _dot(oa_ref[...], wba_ref[...])
             + sg_ref[:, D_MODEL:] * _dot(o_b, wbb_ref[...]))
    x1 = x_ref[...] + _dot(mixed, wout_ref[...])
    x1_ref[...] = x1
    h2 = _rms(x1, ln2_ref[...])
    _store_token_tiles(h2_ref, h2)
    logits = _dot(h2, wr_ref[...]) + br_ref[...]
    lane = lax.broadcasted_iota(jnp.int32, logits.shape, 1).astype(F32)
    work = logits
    vals, idxs = [], []
    for _ in range(TOP_K):
        m = jnp.max(work, axis=-1, keepdims=True)
        idx = jnp.min(jnp.where(work == m, lane, float(LANES)), axis=-1, keepdims=True)
        vals.append(m)
        idxs.append(idx)
        work = jnp.where(lane == idx, -jnp.inf, work)
    es = [jnp.exp(v - vals[0]) for v in vals]
    tot = es[0] + es[1] + es[2] + es[3]
    ti = jnp.zeros(logits.shape, F32)
    tg = jnp.zeros(logits.shape, F32)
    for kslot in range(TOP_K):
        ti = jnp.where(lane == float(kslot), idxs[kslot], ti)
        tg = jnp.where(lane == float(kslot), es[kslot] / tot, tg)
    ti_ref[...] = ti.astype(jnp.int32)
    tg_ref[...] = tg


def _post(x, y, bonus, g, oa, sg, consts, *, tm, n_total, row0=0, into=None):
    n = x.shape[0]
    blk0 = row0 // tm
    n_main = n // tm
    steps = n_total // tm if into is None else n_main
    row = lambda c: pl.BlockSpec((tm, c), lambda i: (jnp.minimum(i, n_main - 1), 0))
    orow = lambda c: pl.BlockSpec((tm, c), lambda i: (i + blk0, 0))
    otile = pl.BlockSpec((tm * TILE_ROWS, LANES), lambda i: (i + blk0, 0))
    full = lambda a: pl.BlockSpec(a.shape, lambda i: (0,) * a.ndim)
    ins = [x, y, bonus, g, oa, sg, *consts]
    in_specs = [row(D_MODEL), row(R_WIDTH), row(R_WIDTH), row(R_WIDTH), row(A_WIDTH), row(2 * D_MODEL)]
    in_specs += [full(a) for a in consts]
    aliases = {}
    if into is not None:
        aliases = {len(ins) + i: i for i in range(len(into))}
        in_specs += [pl.BlockSpec(memory_space=pl.ANY)] * len(into)
        ins += list(into)
    return pl.pallas_call(
        functools.partial(_post_kernel, aliased=len(aliases), n_main=n_main),
        grid=(steps,),
        in_specs=in_specs,
        out_specs=[orow(D_MODEL), otile, orow(LANES), orow(LANES)],
        out_shape=[jax.ShapeDtypeStruct((n_total, D_MODEL), F32),
                   jax.ShapeDtypeStruct((n_total * TILE_ROWS, LANES), F32),
                   jax.ShapeDtypeStruct((n_total, LANES), jnp.int32),
                   jax.ShapeDtypeStruct((n_total, LANES), F32)],
        input_output_aliases=aliases,
        compiler_params=_cparams("parallel"),
        name="post",
    )(*ins)


def _start_rows(rows_ref, hbm, buf, sem, *, rows, to_hbm):
    for r in range(rows):
        at = pl.ds(pl.multiple_of(rows_ref[0, 0, r], TILE_ROWS), TILE_ROWS)
        here = pl.ds(r * TILE_ROWS, TILE_ROWS)
        prio = r % 2
        if to_hbm:
            pltpu.make_async_copy(buf.at[here], hbm.at[at], sem).start(priority=prio)
        else:
            pltpu.make_async_copy(hbm.at[at], buf.at[here], sem).start(priority=prio)


def _wait_rows(hbm, buf, sem, *, rows, to_hbm):
    whole = hbm.at[pl.ds(0, rows * TILE_ROWS)]
    (pltpu.make_async_copy(buf, whole, sem) if to_hbm else pltpu.make_async_copy(whole, buf, sem)).wait()


def _expert_mlp(xb, wgu_bf, wd_bf, bgu_ref, bd_ref, rows):
    acc = jnp.broadcast_to(bd_ref[0], (rows, D_MODEL))
    cw = 512
    for c in range(D_MODEL // cw):
        gs = slice(c * cw, (c + 1) * cw)
        us = slice(D_MODEL + c * cw, D_MODEL + (c + 1) * cw)
        gt = jnp.dot(xb, wgu_bf[:, gs], preferred_element_type=F32) + bgu_ref[0, :, gs]
        up = jnp.dot(xb, wgu_bf[:, us], preferred_element_type=F32) + bgu_ref[0, :, us]
        gt = jnp.minimum(gt, SWIGLU_LIMIT)
        up = jnp.clip(up, -SWIGLU_LIMIT, SWIGLU_LIMIT)
        act = (up + 1.0) * (gt * jax.nn.sigmoid(gt * SWIGLU_ALPHA))
        acc = acc + jnp.dot(act.astype(BF16), wd_bf[gs, :], preferred_element_type=F32)
    return acc


def _moe_kernel(be_ref, src_cur, src_nxt, dst_prv, dst_cur, h_hbm, wgu_ref, bgu_ref, wd_ref, bd_ref, ys_hbm,
                x0, x1, y0, y1, wgu_bf, wd_bf, gsem, ssem, *, rows, plane_rows, n_tok):
    i = pl.program_id(0)
    last = pl.num_programs(0) - 1

    changed = jnp.logical_or(i == 0, be_ref[i] != be_ref[jnp.maximum(i - 1, 0)])

    @pl.when(changed)
    def _():
        step = 128

        def cast(j, c):
            rs = pl.ds(pl.multiple_of(j * step, step), step)
            wgu_bf[rs, :] = wgu_ref[0, rs, :].astype(BF16)
            wd_bf[rs, :] = wd_ref[0, rs, :].astype(BF16)
            return c

        lax.fori_loop(0, D_MODEL // step, cast, 0)

    def run(par):
        xc, xn = (x0, x1) if par == 0 else (x1, x0)
        yc, yp = (y0, y1) if par == 0 else (y1, y0)
        gather = functools.partial(_start_rows, hbm=h_hbm, rows=rows, to_hbm=False)
        scatter = functools.partial(_start_rows, hbm=ys_hbm, rows=rows, to_hbm=True)
        gathered = functools.partial(_wait_rows, h_hbm, rows=rows, to_hbm=False)
        scattered = functools.partial(_wait_rows, ys_hbm, rows=rows, to_hbm=True)

        @pl.when(i == 0)
        def _():
            yp[...] = jnp.zeros(yp.shape, yp.dtype)
            spare = plane_rows - n_tok
            fills = [pltpu.make_async_copy(yp.at[pl.ds(0, spare * TILE_ROWS)],
                                           ys_hbm.at[pl.ds((k * plane_rows + n_tok) * TILE_ROWS, spare * TILE_ROWS)],
                                           ssem.at[1 - par])
                     for k in range(TOP_K)]
            for d in fills:
                d.start()
            for d in fills:
                d.wait()
            gather(src_cur, buf=xc, sem=gsem.at[par])

        gathered(xc, gsem.at[par])
        gather(src_nxt, buf=xn, sem=gsem.at[1 - par])

        @pl.when(i >= 1)
        def _():
            scattered(yc, ssem.at[par])

        scatter(dst_prv, buf=yp, sem=ssem.at[1 - par])
        xb = _load_token_tiles(xc, rows).astype(BF16)
        _store_token_tiles(yc, _expert_mlp(xb, wgu_bf, wd_bf, bgu_ref, bd_ref, rows))

        @pl.when(i == last)
        def _():
            scatter(dst_cur, buf=yc, sem=ssem.at[par])
            gathered(xn, gsem.at[1 - par])
            scattered(yp, ssem.at[1 - par])
            scattered(yc, ssem.at[par])

    for par in range(2):
        pl.when(lax.rem(i, 2) == par)(functools.partial(run, par))


def _moe_experts(block_e, ids, h2, w_gate_up, b_gate_up, w_down, b_down, *, rows, plane_rows, n_tok):
    n_blocks = ids.shape[0] - 2
    ids = lax.optimization_barrier(ids)
    tok = lax.shift_right_logical(ids, 2)
    src = tok * TILE_ROWS
    dst = ((ids & (TOP_K - 1)) * plane_rows + tok) * TILE_ROWS
    idblk = lambda off: pl.BlockSpec((1, 1, rows), lambda i, be: (i + off, 0, 0), memory_space=pltpu.SMEM)
    grid_spec = pltpu.PrefetchScalarGridSpec(
        num_scalar_prefetch=1,
        grid=(n_blocks,),
        in_specs=[
            idblk(1), idblk(2), idblk(0), idblk(1),
            pl.BlockSpec(memory_space=pl.ANY),
            pl.BlockSpec((1, D_MODEL, 2 * D_MODEL), lambda i, be: (be[i], 0, 0)),
            pl.BlockSpec((1, 1, 2 * D_MODEL), lambda i, be: (be[i], 0, 0)),
            pl.BlockSpec((1, D_MODEL, D_MODEL), lambda i, be: (be[i], 0, 0)),
            pl.BlockSpec((1, 1, D_MODEL), lambda i, be: (be[i], 0, 0)),
        ],
        out_specs=pl.BlockSpec(memory_space=pl.ANY),
        scratch_shapes=[pltpu.VMEM((rows * TILE_ROWS, LANES), F32)] * 4 + [
            pltpu.VMEM((D_MODEL, 2 * D_MODEL), BF16),
            pltpu.VMEM((D_MODEL, D_MODEL), BF16),
            pltpu.SemaphoreType.DMA((2,)),
            pltpu.SemaphoreType.DMA((2,))],
    )
    return pl.pallas_call(
        functools.partial(_moe_kernel, rows=rows, plane_rows=plane_rows, n_tok=n_tok),
        grid_spec=grid_spec,
        out_shape=jax.ShapeDtypeStruct((TOP_K * plane_rows * TILE_ROWS, LANES), F32),
        compiler_params=_cparams("arbitrary"),
        name="moe_experts",
    )(block_e, src, src, dst, dst, h2, w_gate_up, b_gate_up[:, None], w_down, b_down[:, None])


def _combine_kernel(ys_ref, x1_ref, gate_ref, lnf_ref, y_ref):
    acc = x1_ref[...]
    tm = acc.shape[0]
    for kslot in range(TOP_K):
        acc = acc + gate_ref[:, kslot:kslot + 1] * _load_token_tiles(ys_ref, tm, kslot)
    y_ref[...] = _rms(acc, lnf_ref[...])


def _combine(ys, x1, gates, ln_f, *, tm, n, row0):
    blk0 = row0 // tm
    return pl.pallas_call(
        _combine_kernel,
        grid=(n // tm,),
        in_specs=[
            pl.BlockSpec((TOP_K, tm * TILE_ROWS, LANES), lambda i: (0, i + blk0, 0)),
            pl.BlockSpec((tm, D_MODEL), lambda i: (i + blk0, 0)),
            pl.BlockSpec((tm, LANES), lambda i: (i + blk0, 0)),
            pl.BlockSpec(ln_f.shape, lambda i: (0, 0)),
        ],
        out_specs=pl.BlockSpec((tm, D_MODEL), lambda i: (i, 0)),
        out_shape=jax.ShapeDtypeStruct((n, D_MODEL), F32),
        compiler_params=_cparams("parallel"),
        name="combine",
    )(ys.reshape(TOP_K, -1, LANES), x1, gates, ln_f)


def _route(topi, n_pad, rows):
    n_tok = topi.shape[0]
    e_flat = topi.reshape(-1)
    nk = e_flat.shape[0]
    n_blocks = -(-(nk + N_EXPERTS * (rows - 1)) // rows)
    spare = n_pad - n_tok
    assert 3 * rows <= TOP_K * spare
    experts = jnp.arange(N_EXPERTS, dtype=jnp.int32)
    order = jnp.argsort(e_flat).astype(jnp.int32)
    sizes = jnp.sum((e_flat[:, None] == experts[None, :]).astype(jnp.int32), axis=0)
    padded = (sizes + rows - 1) // rows * rows
    pends = jnp.cumsum(padded)
    slot = jnp.arange(n_blocks * rows, dtype=jnp.int32)
    past = (slot[:, None] >= pends[None, :]).astype(jnp.int32)
    e_slot = jnp.sum(past, axis=1)
    off = slot - jnp.sum(past * padded[None, :], axis=1)
    src = jnp.sum(past * sizes[None, :], axis=1) + off
    size_e = jnp.sum((e_slot[:, None] == experts[None, :]) * sizes[None, :], axis=1)

    def pad_ids(d):
        return (TOP_K * (n_tok + d % spare) + d // spare).astype(jnp.int32)

    ids = jnp.where(off < size_e, order[jnp.clip(src, 0, nk - 1)], pad_ids((slot // rows) % 2 * rows + slot % rows))
    end_ids = pad_ids(2 * rows + jnp.arange(rows, dtype=jnp.int32))
    ids = jnp.concatenate([end_ids, ids, end_ids]).reshape(n_blocks + 2, 1, rows)
    block_e = jnp.minimum(e_slot[::rows], N_EXPERTS - 1).astype(jnp.int32)
    return block_e, ids


def kernel(x_prompt, x_sample, cache_k, cache_v, state_wkv, state_shift, rel_bias, ln1, w_in, rwkv_mu, w0,
           w_decay_up, a0, w_a_up, w_g_up, k_k, k_a, r_k, gn_w, gn_b, w_branch_a, w_branch_b, w_out, ln2,
           w_router, b_router, w_gate_up, b_gate_up, w_down, b_down, ln_f):
    bp, seq, _ = x_prompt.shape
    db = x_sample.shape[0]
    n_p = bp * seq
    n_tot = n_p + db
    n_pad = -(-(n_tot + -(-3 * MOE_ROWS // TOP_K)) // TOKEN_TILE) * TOKEN_TILE
    l = 0

    row = lambda a: a.reshape(1, -1)
    seg = _head_seg()
    prep_params = (row(rwkv_mu[l]), row(w0[l]), w_decay_up[l], row(a0[l]), w_a_up[l], w_g_up[l],
                   row(k_k[l]), row(k_a[l]), row(r_k[l]), seg)
    wr_pad = jnp.pad(w_router[l], ((0, 0), (0, LANES - N_EXPERTS))).astype(BF16)
    br_pad = jnp.pad(b_router[l], (0, LANES - N_EXPERTS), constant_values=NEG).reshape(1, LANES)
    post_small = (row(gn_w[l]), row(gn_b[l]), seg)
    post_tail = (row(ln2[l]), wr_pad, br_pad)

    w_in_bf = w_in[l].astype(BF16)
    consts = (post_small + (w_branch_a[l].astype(BF16), w_branch_b[l].astype(BF16), w_out[l].astype(BF16))
              + post_tail)

    xp = x_prompt.reshape(n_p, D_MODEL)
    q_p, k_p, v_p, rc_p, sg_p = _inproj(xp, row(ln1[l]), w_in_bf, tm=TOKEN_TILE)
    as3 = lambda a: a.reshape(bp, seq, -1)
    oa_p = _attn_prompt(as3(q_p), as3(k_p), as3(v_p), _prompt_bias(rel_bias))
    prep_p = _rwkv_prep_prompt(as3(rc_p), prep_params, tt=TOKEN_TILE)
    r_, w_, k2_, vr_, kk_, ka_, bonus_p, g_p = prep_p
    s0_p = jnp.zeros((bp, R_HEADS, R_HEAD, R_HEAD), F32)
    y_p, wkv_p = _lane_scan(r_, w_, k2_, vr_, kk_, ka_, s0_p, tc=SCAN_CHUNK)
    flat = lambda a: a.reshape(n_p, -1)
    bufs = _post(xp, flat(y_p), flat(bonus_p), flat(g_p), flat(oa_p), sg_p, consts,
                 tm=TOKEN_TILE, n_total=n_pad)

    xs = x_sample.reshape(db, D_MODEL)
    q_s, k_s, v_s, rc_s, sg_s = _inproj(xs, row(ln1[l]), w_in_bf, tm=db)
    bias_s, b0_s = _sample_bias(rel_bias, cache_k.shape[2])
    oa_s = _attn_sample(q_s, k_s, v_s, cache_k[l], cache_v[l], bias_s, b0_s)
    prep_s = _rwkv_prep_sample(rc_s, state_shift[l], prep_params)
    sr, sw, sk2, svr, skk, ska, bonus_s, g_s = [a[:, None] for a in prep_s]
    y_s, wkv_s = _rwkv_scan(sr, sw, sk2, svr, skk, ska, state_wkv[l], bb=SCAN_BATCH, tc=1)
    x1_all, h2_all, ti_all, tg_all = _post(xs, y_s[:, 0], bonus_s[:, 0], g_s[:, 0], oa_s, sg_s, consts,
                                           tm=db, n_total=n_pad, row0=n_p, into=bufs)

    block_e, ids = _route(ti_all[:n_tot, :TOP_K], n_pad, MOE_ROWS)
    ys = _moe_experts(block_e, ids, h2_all, w_gate_up[l], b_gate_up[l], w_down[l], b_down[l],
                      rows=MOE_ROWS, plane_rows=n_pad, n_tok=n_tot)
    lnf = row(ln_f)
    y_prompt = _combine(ys, x1_all, tg_all, lnf, tm=TOKEN_TILE, n=n_p, row0=0)
    y_sample = _combine(ys, x1_all, tg_all, lnf, tm=db, n=db, row0=n_p)

    heads = lambda a, b_: a.reshape(1, b_, -1, A_HEADS, HEAD_DIM)
    return (y_prompt.reshape(bp, seq, D_MODEL), y_sample.reshape(db, 1, D_MODEL),
            heads(k_p, bp), heads(v_p, bp), wkv_p[None], as3(rc_p)[:, -1][None],
            heads(k_s, db), heads(v_s, db), wkv_s[None], rc_s[None])
```

```python
import functools
import math

import jax
import jax.numpy as jnp
from jax import lax
from jax.experimental import pallas as pl
from jax.experimental.pallas import tpu as pltpu

F32 = jnp.float32
BF16 = jnp.bfloat16

D_MODEL = 1024
A_HEADS = 8
HEAD_DIM = 64
A_WIDTH = A_HEADS * HEAD_DIM
PATTERNS = ((128, 1), (512, 4), (2048, 16))
BLOCK = 128
NUM_BUCKETS = 32
MAX_DISTANCE = 2048
SCALE = HEAD_DIM ** -0.5
NEG = -1e30
R_HEADS = 8
R_HEAD = 64
R_WIDTH = R_HEADS * R_HEAD
DECAY_LORA = 64
AAA_LORA = 64
GATE_LORA = 128
GN_EPS = 64e-5
N_RWKV_COLS = 3 * R_WIDTH + DECAY_LORA + AAA_LORA + GATE_LORA
N_IN_COLS = 3 * A_WIDTH + N_RWKV_COLS + 2 * D_MODEL
N_EXPERTS = 32
TOP_K = 4
SWIGLU_LIMIT = 7.0
SWIGLU_ALPHA = 1.702
RMS_EPS = 1e-6

LANES = 128
SUBLANES = 8
VMEM_LIMIT = 56 * 1024 * 1024
TOKEN_TILE = 256
MOE_ROWS = 512
SCAN_BATCH = 2
SCAN_CHUNK = 64


def _cparams(*sem):
    return pltpu.CompilerParams(dimension_semantics=sem, vmem_limit_bytes=VMEM_LIMIT)


def _rms(x, g):
    return x * lax.rsqrt(jnp.mean(x * x, axis=-1, keepdims=True) + RMS_EPS) * g


def _dot(a, b):
    return jnp.dot(a.astype(BF16), b.astype(BF16), preferred_element_type=F32)


def _bf(x):
    return x.astype(BF16).astype(F32)


def _split3(x):
    hi = x.astype(BF16)
    r1 = x - hi.astype(F32)
    mid = r1.astype(BF16)
    lo = (r1 - mid.astype(F32)).astype(BF16)
    return hi, mid, lo


def _segdot(x, seg):
    hi, mid, lo = _split3(x)
    d = lambda p: jnp.dot(p, seg, preferred_element_type=F32)
    return d(hi) + d(mid) + d(lo)


TILE_ROWS = D_MODEL // LANES
assert TILE_ROWS == SUBLANES


def _load_token_tiles(ref, n, *lead):
    return jnp.concatenate([ref[(*lead, pl.ds(p, n, stride=TILE_ROWS), slice(None))] for p in range(TILE_ROWS)], axis=1)


def _store_token_tiles(ref, x):
    n = x.shape[0]
    for p in range(TILE_ROWS):
        ref[pl.ds(p, n, stride=TILE_ROWS), :] = x[:, p * LANES:(p + 1) * LANES]


def _inproj_kernel(x_ref, g_ref, w_ref, q_ref, k_ref, v_ref, rc_ref, sg_ref, *t_refs):
    h = _rms(x_ref[...], g_ref[...]).astype(BF16)

    def mm(c0, c1):
        return _dot(h, w_ref[:, c0:c1])

    c1 = 3 * A_WIDTH
    c2 = c1 + N_RWKV_COLS
    q_ref[...] = mm(0, A_WIDTH)
    k = mm(A_WIDTH, 2 * A_WIDTH)
    v = mm(2 * A_WIDTH, c1)
    k_ref[...] = k
    v_ref[...] = v
    for t_ref, val in zip(t_refs, (k, v)):
        t_ref[0] = val.T.reshape(A_HEADS, HEAD_DIM, val.shape[0])
    rc_ref[...] = mm(c1, c2)
    sg_ref[:, :D_MODEL] = jax.nn.sigmoid(mm(c2, c2 + D_MODEL))
    sg_ref[:, D_MODEL:] = jax.nn.sigmoid(mm(c2 + D_MODEL, N_IN_COLS))


def _inproj(x, ln1, w, *, tm, seq=None):
    n = x.shape[0]
    row = lambda c: pl.BlockSpec((tm, c), lambda i: (i, 0))
    full = lambda a: pl.BlockSpec(a.shape, lambda i: (0,) * a.ndim)
    out_cols = (A_WIDTH, A_WIDTH, A_WIDTH, N_RWKV_COLS, 2 * D_MODEL)
    out_specs = [row(c) for c in out_cols]
    out_shape = [jax.ShapeDtypeStruct((n, c), F32) for c in out_cols]
    if seq is not None:
        per = seq // tm
        t_spec = pl.BlockSpec((1, A_HEADS, HEAD_DIM, tm), lambda i: (i // per, 0, 0, i % per))
        out_specs += [t_spec] * 2
        out_shape += [jax.ShapeDtypeStruct((n // seq, A_HEADS, HEAD_DIM, seq), F32)] * 2
    return pl.pallas_call(
        _inproj_kernel,
        grid=(n // tm,),
        in_specs=[row(D_MODEL), full(ln1), full(w)],
        out_specs=out_specs,
        out_shape=out_shape,
        compiler_params=_cparams("parallel"),
        name="inproj",
    )(x, ln1, w)


def _t5_bucket(dist):
    max_exact = NUM_BUCKETS // 2
    d = jnp.maximum(dist, 1).astype(F32)
    large = max_exact + (jnp.log(d / max_exact) / math.log(MAX_DISTANCE / max_exact)
                         * (NUM_BUCKETS - max_exact)).astype(jnp.int32)
    large = jnp.minimum(large, NUM_BUCKETS - 1)
    return jnp.where(dist < max_exact, dist, large)


def _bias_lookup(rel_bias, dist):
    onehot = (_t5_bucket(dist)[..., None] == jnp.arange(NUM_BUCKETS)).astype(F32)
    return jnp.moveaxis(jnp.dot(onehot, rel_bias.astype(F32), precision=lax.Precision.HIGHEST), -1, 0)


def _prompt_bias(rel_bias):
    i = jnp.arange(BLOCK)[:, None]
    j = jnp.arange(2 * BLOCK)[None, :]
    delta = i + BLOCK - j
    out = []
    for window, dil in PATTERNS:
        n = window // dil
        band = (delta >= 0) & (delta <= n)
        out.append(jnp.where(band[None], _bias_lookup(rel_bias, jnp.clip(delta, 0, n) * dil), NEG))
    return jnp.stack(out, 0)


def _sample_bias(rel_bias, wb):
    dist = wb - jnp.arange(wb)
    out = []
    for window, dil in PATTERNS:
        member = (dist % dil == 0) & (dist <= window)
        out.append(jnp.where(member[None], _bias_lookup(rel_bias, dist), NEG))
    b0 = _bias_lookup(rel_bias, jnp.zeros((1,), jnp.int32))
    return jnp.stack(out, 0), b0


def _attn_prompt_kernel(q_ref, k_ref, v_ref, bias_ref, o_ref, os_ref, ls_ref, *, seq):
    nt = (((1,), (1,)), ((), ()))
    npat = len(PATTERNS)
    w2 = 2 * HEAD_DIM
    lo = lax.broadcasted_iota(jnp.int32, (BLOCK, w2), 1) < HEAD_DIM
    key = lax.broadcasted_iota(jnp.int32, (2 * BLOCK, 2 * BLOCK), 1)

    for g, (window, dil) in enumerate(PATTERNS):
        nb = seq // (BLOCK * dil)
        sh = dil.bit_length() - 1

        def body(it, carry, g=g, dil=dil, nb=nb, sh=sh):
            r = it & (dil - 1)
            c = it >> sh
            start = c * (BLOCK * dil) + r
            rows = pl.ds(start, BLOCK, stride=dil) if dil > 1 else pl.ds(start, BLOCK)
            qb = q_ref[0, rows, :]
            kb = k_ref[0, rows, :]
            vb = v_ref[0, rows, :]
            if nb > 1:
                pstart = jnp.maximum(c - 1, 0) * (BLOCK * dil) + r
                prow = pl.ds(pstart, BLOCK, stride=dil) if dil > 1 else pl.ds(pstart, BLOCK)
                kp = k_ref[0, prow, :]
                vp = v_ref[0, prow, :]
            q2 = jnp.concatenate([jnp.where(lo, qb, 0.0), jnp.where(lo, 0.0, qb)], axis=0).astype(BF16)
            if nb > 1:
                k2 = jnp.concatenate([kp, kb], axis=0)
                v2 = jnp.concatenate([vp, vb], axis=0)
                bias = bias_ref[g, 0]
                valid = (bias > 0.5 * NEG) & ((key >= BLOCK) | (c > 0))
            else:
                k2, v2 = kb, vb
                bias = bias_ref[g, 0, :, BLOCK:]
                valid = bias > 0.5 * NEG
            s = lax.dot_general(q2, k2.astype(BF16), nt, preferred_element_type=F32)
            s = jnp.where(valid, s * SCALE + bias, NEG)
            m = jnp.max(s, axis=-1, keepdims=True)
            p = jnp.exp(s - m)
            l = jnp.sum(p, axis=-1, keepdims=True)
            o2 = jnp.dot((p * (1.0 / l)).astype(BF16), v2.astype(BF16), preferred_element_type=F32)
            lse = jnp.broadcast_to(m + jnp.log(l), (2 * BLOCK, w2))
            os_ref[g, rows, :] = jnp.where(lo, o2[:BLOCK], o2[BLOCK:])
            ls_ref[g, rows, :] = jnp.where(lo, lse[:BLOCK], lse[BLOCK:])
            return carry

        lax.fori_loop(0, dil * nb, body, 0, unroll=8)

    chunk = 256

    def merge(ch, carry):
        rows = pl.ds(pl.multiple_of(ch * chunk, chunk), chunk)
        lse = [ls_ref[g, rows, :] for g in range(npat)]
        top = functools.reduce(jnp.maximum, lse)
        e = [jnp.exp(x - top) for x in lse]
        inv = 1.0 / functools.reduce(lambda a, b_: a + b_, e)
        acc = jnp.zeros((chunk, 2 * HEAD_DIM), F32)
        for g in range(npat):
            acc = acc + (e[g] * inv) * os_ref[g, rows, :]
        o_ref[0, rows, :] = acc
        return carry

    lax.fori_loop(0, seq // chunk, merge, 0)


def _attn_prompt(q, k, v, bias):
    b, s, _ = q.shape
    w2 = 2 * HEAD_DIM
    qspec = pl.BlockSpec((1, s, w2), lambda i, p: (i, 0, p))
    return pl.pallas_call(
        functools.partial(_attn_prompt_kernel, seq=s),
        grid=(b, A_HEADS // 2),
        in_specs=[qspec, qspec, qspec,
                  pl.BlockSpec((len(PATTERNS), 1, 2 * BLOCK, 2 * BLOCK), lambda i, p: (0, p, 0, 0))],
        out_specs=qspec,
        out_shape=jax.ShapeDtypeStruct((b, s, A_WIDTH), F32),
        scratch_shapes=[pltpu.VMEM((len(PATTERNS), s, w2), F32)] * 2,
        compiler_params=_cparams("parallel", "parallel"),
        name="attn_prompt",
    )(q, k, v, bias.reshape(len(PATTERNS), A_HEADS // 2, 2 * BLOCK, 2 * BLOCK))


def _attn_sample_kernel(q_ref, kn_ref, vn_ref, kt_ref, vt_ref, bias_ref, b0_ref, o_ref):
    npat = len(PATTERNS)
    nt = (((1,), (1,)), ((), ()))
    q = q_ref[0]
    v_new = _bf(vn_ref[0])
    s0 = jnp.sum(_bf(q) * _bf(kn_ref[0]), axis=-1, keepdims=True) * SCALE + b0_ref[...]
    s = jnp.concatenate([_dot(q[h:h + 1], kt_ref[0, h]) for h in range(A_HEADS)], axis=0) * SCALE
    ps, p0s, lses = [], [], []
    for g in range(npat):
        bias = bias_ref[g]
        sg = jnp.where(bias > 0.5 * NEG, s + bias, NEG)
        m = jnp.maximum(jnp.max(sg, axis=-1, keepdims=True), s0)
        l = jnp.sum(jnp.exp(sg - m), axis=-1, keepdims=True) + jnp.exp(s0 - m)
        lse = m + jnp.log(l)
        ps.append(jnp.exp(sg - lse))
        p0s.append(_bf(jnp.exp(s0 - lse)))
        lses.append(lse)
    top = functools.reduce(jnp.maximum, lses)
    e = [jnp.exp(x - top) for x in lses]
    inv = 1.0 / functools.reduce(lambda a, b_: a + b_, e)
    w = [_bf(x * inv) for x in e]
    rows = []
    for h in range(A_HEADS):
        p_h = jnp.concatenate([p[h:h + 1] for p in ps], axis=0)
        o_h = lax.dot_general(p_h.astype(BF16), vt_ref[0, h].astype(BF16), nt, preferred_element_type=F32)
        acc = jnp.zeros((1, HEAD_DIM), F32)
        for g in range(npat):
            acc = acc + w[g][h:h + 1] * _bf(o_h[g:g + 1] + p0s[g][h:h + 1] * v_new[h:h + 1])
        rows.append(acc)
    o_ref[0] = jnp.concatenate(rows, axis=0)


def _attn_sample(q, k_new, v_new, cache_k, cache_v, bias, b0):
    db, wb = cache_k.shape[:2]
    heads = lambda a: a.reshape(db, A_HEADS, HEAD_DIM)
    vec = pl.BlockSpec((1, A_HEADS, HEAD_DIM), lambda i: (i, 0, 0))
    full = lambda a: pl.BlockSpec(a.shape, lambda i: (0,) * a.ndim)
    rows_minor = lambda a: a.transpose(0, 2, 3, 1)
    cache = pl.BlockSpec((1, A_HEADS, HEAD_DIM, wb), lambda i: (i, 0, 0, 0))
    args = [heads(q), heads(k_new), heads(v_new), rows_minor(cache_k), rows_minor(cache_v), bias, b0]
    return pl.pallas_call(
        _attn_sample_kernel,
        grid=(db,),
        in_specs=[vec, vec, vec, cache, cache, full(bias), full(b0)],
        out_specs=vec,
        out_shape=jax.ShapeDtypeStruct((db, A_HEADS, HEAD_DIM), F32),
        compiler_params=_cparams("parallel"),
        name="attn_sample",
    )(*args).reshape(db, A_WIDTH)


def _rwkv_prep_math(rc, prev, mu, w0, wd, a0, wa, wg, k_k, k_a, r_k, seg):
    xr = rc + (prev - rc) * mu
    o1, o2, o3 = R_WIDTH, 2 * R_WIDTH, 3 * R_WIDTH
    r = xr[:, :o1]
    kr = xr[:, o1:o2]
    vr = xr[:, o2:o3]
    xw = xr[:, o3:o3 + DECAY_LORA]
    xa = xr[:, o3 + DECAY_LORA:o3 + DECAY_LORA + AAA_LORA]
    xg = xr[:, o3 + DECAY_LORA + AAA_LORA:]
    z = -(w0 + _dot(jnp.tanh(xw), wd))
    softplus = jnp.maximum(z, 0.0) + jnp.log(1.0 + jnp.exp(-jnp.abs(z)))
    decay = jnp.exp(-jnp.exp(-softplus - 0.5))
    a = jax.nn.sigmoid(a0 + _dot(xa, wa))
    g = _dot(jax.nn.sigmoid(xg), wg)
    kk = kr * k_k
    k2 = kr * (1.0 + (a - 1.0) * k_a)
    kk = kk / jnp.maximum(jnp.sqrt(_segdot(kk * kk, seg)), 1e-12)
    bonus = _segdot(r * k2 * r_k, seg) * vr
    return _bf(r), decay, k2, vr, _bf(kk), kk * a, bonus, g


N_PREP_PARAMS = 10


def _rwkv_prep_prompt_kernel(rc_ref, tail_ref, *refs):
    p = [x[...] for x in refs[:N_PREP_PARAMS]]
    outs = refs[N_PREP_PARAMS:]
    rc = rc_ref[0]
    tt = rc.shape[0]
    first = pl.program_id(1) == 0
    prev_row = jnp.where(first, 0.0, tail_ref[0, 7:8, :])
    rolled = pltpu.roll(rc, 1, axis=0)
    row = lax.broadcasted_iota(jnp.int32, (tt, 1), 0)
    prev = jnp.where(row == 0, prev_row, rolled)
    for o, val in zip(outs, _rwkv_prep_math(rc, prev, *p)):
        o[0] = val


def _rwkv_prep_sample_kernel(rc_ref, prev_ref, *refs):
    p = [x[...] for x in refs[:N_PREP_PARAMS]]
    outs = refs[N_PREP_PARAMS:]
    for o, val in zip(outs, _rwkv_prep_math(rc_ref[...], prev_ref[...], *p)):
        o[...] = val


def _head_seg():
    head = jnp.arange(R_WIDTH) // R_HEAD
    return (head[:, None] == head[None, :]).astype(BF16)


def _rwkv_prep_prompt(rc, params, *, tt):
    b, t, _ = rc.shape
    full = lambda a: pl.BlockSpec(a.shape, lambda i, j: (0,) * a.ndim)
    tile = lambda c: pl.BlockSpec((1, tt, c), lambda i, j: (i, j, 0))
    tail = pl.BlockSpec((1, 8, N_RWKV_COLS), lambda i, j: (i, jnp.maximum(j * (tt // 8) - 1, 0), 0))
    return pl.pallas_call(
        _rwkv_prep_prompt_kernel,
        grid=(b, t // tt),
        in_specs=[tile(N_RWKV_COLS), tail] + [full(a) for a in params],
        out_specs=[tile(R_WIDTH)] * 8,
        out_shape=[jax.ShapeDtypeStruct((b, t, R_WIDTH), F32)] * 8,
        compiler_params=_cparams("parallel", "parallel"),
        name="rwkv_prep_prompt",
    )(rc, rc, *params)


def _rwkv_prep_sample(rc, prev, params):
    n = rc.shape[0]
    full = lambda a: pl.BlockSpec(a.shape, lambda i: (0,) * a.ndim)
    return pl.pallas_call(
        _rwkv_prep_sample_kernel,
        grid=(1,),
        in_specs=[full(rc), full(prev)] + [full(a) for a in params],
        out_specs=[pl.BlockSpec((n, R_WIDTH), lambda i: (0, 0))] * 8,
        out_shape=[jax.ShapeDtypeStruct((n, R_WIDTH), F32)] * 8,
        compiler_params=_cparams("arbitrary"),
        name="rwkv_prep_sample",
    )(rc, prev, *params)


def _seg_lane_sum(x, lo_mask):
    lo = jnp.sum(jnp.where(lo_mask, x, 0.0), axis=-1, keepdims=True)
    hi = jnp.sum(jnp.where(lo_mask, 0.0, x), axis=-1, keepdims=True)
    return jnp.where(lo_mask, lo, hi)


def _rwkv_scan_kernel(r_ref, w_ref, k_ref, v_ref, kk_ref, ka_ref, s0_ref, y_ref, sT_ref, st_ref, *, bb, tc):
    pairs = R_HEADS // 2
    w2 = 2 * R_HEAD

    @pl.when(pl.program_id(1) == 0)
    def _():
        for b in range(bb):
            for p in range(pairs):
                st_ref[b, p] = jnp.concatenate([s0_ref[b, 2 * p], s0_ref[b, 2 * p + 1]], axis=1)

    lane = lax.broadcasted_iota(jnp.int32, (R_HEAD, w2), 1)
    sub = lax.broadcasted_iota(jnp.int32, (R_HEAD, w2), 0)
    lo_mask = lane < R_HEAD
    eye2 = (lane & (R_HEAD - 1)) == sub

    grp = min(8, tc)

    def group(tg, carry):
        rows = pl.ds(pl.multiple_of(tg * grp, grp), grp)
        for b in range(bb):
            for p in range(pairs):
                cols = slice(p * w2, (p + 1) * w2)
                r8, w8, k8, v8, kk8, ka8 = (ref[b, rows, cols] for ref in (r_ref, w_ref, k_ref, v_ref, kk_ref, ka_ref))
                s = st_ref[b, p]
                sb = _bf(s)
                ys = []
                for j in range(grp):
                    row = lambda a: a[j:j + 1, :]
                    sa = -_seg_lane_sum(sb * row(kk8), lo_mask)
                    vcol = _seg_lane_sum(jnp.where(eye2, jnp.broadcast_to(row(v8), (R_HEAD, w2)), 0.0), lo_mask)
                    s = s * row(w8) + sa * row(ka8) + vcol * row(k8)
                    sb = _bf(s)
                    yfull = _seg_lane_sum(sb * row(r8), lo_mask)
                    ys.append(jnp.sum(jnp.where(eye2, yfull, 0.0), axis=0, keepdims=True))
                st_ref[b, p] = s
                y_ref[b, rows, cols] = jnp.concatenate(ys, axis=0) if grp > 1 else ys[0]
        return carry

    lax.fori_loop(0, tc // grp, group, 0)

    @pl.when(pl.program_id(1) == pl.num_programs(1) - 1)
    def _():
        for b in range(bb):
            for p in range(pairs):
                s = st_ref[b, p]
                sT_ref[b, 2 * p] = s[:, :R_HEAD]
                sT_ref[b, 2 * p + 1] = s[:, R_HEAD:]


def _rwkv_scan(r, w, k, v, kk, ka, s0, *, bb, tc):
    b, t, _ = r.shape
    seq = pl.BlockSpec((bb, tc, R_WIDTH), lambda i, j: (i, j, 0))
    state = pl.BlockSpec((bb, R_HEADS, R_HEAD, R_HEAD), lambda i, j: (i, 0, 0, 0))
    return pl.pallas_call(
        functools.partial(_rwkv_scan_kernel, bb=bb, tc=tc),
        grid=(b // bb, t // tc),
        in_specs=[seq] * 6 + [state],
        out_specs=[seq, state],
        out_shape=[jax.ShapeDtypeStruct((b, t, R_WIDTH), F32),
                   jax.ShapeDtypeStruct((b, R_HEADS, R_HEAD, R_HEAD), F32)],
        scratch_shapes=[pltpu.VMEM((bb, R_HEADS // 2, R_HEAD, 2 * R_HEAD), F32)],
        compiler_params=_cparams("parallel", "arbitrary"),
        name="rwkv_scan",
    )(r, w, k, v, kk, ka, s0)


CHAINS = LANES // 2
K2 = R_HEAD // 2


def _chain_rows(x):
    b, t, _ = x.shape
    x = x.reshape(b, t, R_HEADS, 2, K2).transpose(1, 4, 3, 0, 2)
    return x.reshape(t, K2, LANES)


def _lane_scan_kernel(kk_ref, w_ref, ka_ref, kx_ref, r_ref, v_ref, s0_ref, y_ref, sT_ref, s_ref, *, tc):
    lo = lax.broadcasted_iota(jnp.int32, (K2, LANES), 1) < CHAINS

    @pl.when(pl.program_id(0) == 0)
    def _():
        for k2 in range(K2):
            s_ref[k2] = s0_ref[k2]

    def both_halves(x):
        return x + pltpu.roll(x, CHAINS, axis=1)

    nhalf = 2
    vh = R_HEAD // nhalf

    def row(ref, t, k2):
        return jnp.broadcast_to(ref[t, k2:k2 + 1, :], (vh, LANES))

    acc0 = []
    for hf in range(nhalf):
        vs = slice(hf * vh, (hf + 1) * vh)
        a = jnp.zeros((vh, LANES), F32)
        for k2 in range(K2):
            a = a + _bf(s_ref[k2, vs, :]) * row(kk_ref, 0, k2)
        acc0.append(a)

    def step(t, acc):
        tn = jnp.minimum(t + 1, tc - 1)
        nxt, ys = [], []
        v_t = v_ref[t]
        v_sw = pltpu.roll(v_t, CHAINS, axis=1)
        for hf in range(nhalf):
            vs = slice(hf * vh, (hf + 1) * vh)
            vv = jnp.where(lo, v_t, v_sw) if hf == 0 else jnp.where(lo, v_sw, v_t)
            sa = -both_halves(acc[hf])
            yacc = jnp.zeros((vh, LANES), F32)
            nacc = jnp.zeros((vh, LANES), F32)
            for k2 in range(K2):
                s = s_ref[k2, vs, :] * row(w_ref, t, k2) + sa * row(ka_ref, t, k2) + vv * row(kx_ref, t, k2)
                s_ref[k2, vs, :] = s
                sb = _bf(s)
                yacc = yacc + sb * row(r_ref, t, k2)
                nacc = nacc + sb * row(kk_ref, tn, k2)
            ys.append(both_halves(yacc))
            nxt.append(nacc)
        y_ref[t] = jnp.where(lo, ys[0], ys[1])
        return tuple(nxt)

    lax.fori_loop(0, tc, step, tuple(acc0))

    @pl.when(pl.program_id(0) == pl.num_programs(0) - 1)
    def _():
        for k2 in range(K2):
            sT_ref[k2] = s_ref[k2]


def _lane_scan(r, w, k, v, kk, ka, s0, *, tc):
    b, t, _ = r.shape
    assert b * R_HEADS == CHAINS
    ops = [_chain_rows(x) for x in (kk, w, ka, k, r, v)]
    s0c = s0.reshape(b, R_HEADS, 2, K2, 2, K2).transpose(5, 2, 3, 4, 0, 1).reshape(K2, R_HEAD, LANES)
    rows = pl.BlockSpec((tc, K2, LANES), lambda i: (i, 0, 0))
    state = pl.BlockSpec((K2, R_HEAD, LANES), lambda i: (0, 0, 0))
    y, st = pl.pallas_call(
        functools.partial(_lane_scan_kernel, tc=tc),
        grid=(t // tc,),
        in_specs=[rows] * 6 + [state],
        out_specs=[rows, state],
        out_shape=[jax.ShapeDtypeStruct((t, K2, LANES), F32), jax.ShapeDtypeStruct((K2, R_HEAD, LANES), F32)],
        scratch_shapes=[pltpu.VMEM((K2, R_HEAD, LANES), F32)],
        compiler_params=_cparams("arbitrary"),
        name="rwkv_lane_scan",
    )(*ops, s0c)
    y = y.reshape(t, K2, 2, b, R_HEADS).transpose(3, 0, 4, 2, 1).reshape(b, t, R_WIDTH)
    st = st.reshape(K2, 2, K2, 2, b, R_HEADS).transpose(4, 5, 1, 2, 3, 0).reshape(b, R_HEADS, R_HEAD, R_HEAD)
    return y, st


def _post_kernel(x_ref, y_ref, bonus_ref, g_ref, oa_ref, sg_ref, gnw_ref, gnb_ref, seg_ref, wba_ref, wbb_ref,
                 wout_ref, ln2_ref, wr_ref, br_ref, *rest, aliased, n_main):
    outs = rest[aliased:]
    x1_ref, h2_ref, ti_ref, tg_ref = outs

    @pl.when(pl.program_id(0) >= n_main)
    def _():
        for o in outs:
            o[...] = jnp.zeros(o.shape, o.dtype)

    @pl.when(pl.program_id(0) < n_main)
    def _():
        _post_body(x_ref, y_ref, bonus_ref, g_ref, oa_ref, sg_ref, gnw_ref, gnb_ref, seg_ref, wba_ref, wbb_ref,
                   wout_ref, ln2_ref, wr_ref, br_ref, x1_ref, h2_ref, ti_ref, tg_ref)


def _post_body(x_ref, y_ref, bonus_ref, g_ref, oa_ref, sg_ref, gnw_ref, gnb_ref, seg_ref, wba_ref, wbb_ref,
               wout_ref, ln2_ref, wr_ref, br_ref, x1_ref, h2_ref, ti_ref, tg_ref):
    y = y_ref[...]
    seg = seg_ref[...]
    mu = _segdot(y, seg) * (1.0 / R_HEAD)
    yc = y - mu
    var = _segdot(yc * yc, seg) * (1.0 / R_HEAD)
    yn = yc * lax.rsqrt(var + GN_EPS) * gnw_ref[...] + gnb_ref[...]
    o_b = (yn + bonus_ref[...]) * g_ref[...]
    mixed = (sg_ref[:, :D_MODEL] * _dot(oa_ref[...], wba_ref[...])
             + sg_ref[:, D_MODEL:] * _dot(o_b, wbb_ref[...]))
    x1 = x_ref[...] + _dot(mixed, wout_ref[...])
    x1_ref[...] = x1
    h2 = _rms(x1, ln2_ref[...])
    _store_token_tiles(h2_ref, h2)
    logits = _dot(h2, wr_ref[...]) + br_ref[...]
    lane = lax.broadcasted_iota(jnp.int32, logits.shape, 1).astype(F32)
    work = logits
    vals, idxs = [], []
    for _ in range(TOP_K):
        m = jnp.max(work, axis=-1, keepdims=True)
        idx = jnp.min(jnp.where(work == m, lane, float(LANES)), axis=-1, keepdims=True)
        vals.append(m)
        idxs.append(idx)
        work = jnp.where(lane == idx, -jnp.inf, work)
    es = [jnp.exp(v - vals[0]) for v in vals]
    tot = es[0] + es[1] + es[2] + es[3]
    ti = jnp.zeros(logits.shape, F32)
    tg = jnp.zeros(logits.shape, F32)
    for kslot in range(TOP_K):
        ti = jnp.where(lane == float(kslot), idxs[kslot], ti)
        tg = jnp.where(lane == float(kslot), es[kslot] / tot, tg)
    ti_ref[...] = ti.astype(jnp.int32)
    tg_ref[...] = tg


def _post(x, y, bonus, g, oa, sg, consts, *, tm, n_total, row0=0, into=None):
    n = x.shape[0]
    blk0 = row0 // tm
    n_main = n // tm
    steps = n_total // tm if into is None else n_main
    row = lambda c: pl.BlockSpec((tm, c), lambda i: (jnp.minimum(i, n_main - 1), 0))
    orow = lambda c: pl.BlockSpec((tm, c), lambda i: (i + blk0, 0))
    otile = pl.BlockSpec((tm * TILE_ROWS, LANES), lambda i: (i + blk0, 0))
    full = lambda a: pl.BlockSpec(a.shape, lambda i: (0,) * a.ndim)
    ins = [x, y, bonus, g, oa, sg, *consts]
    in_specs = [row(D_MODEL), row(R_WIDTH), row(R_WIDTH), row(R_WIDTH), row(A_WIDTH), row(2 * D_MODEL)]
    in_specs += [full(a) for a in consts]
    aliases = {}
    if into is not None:
        aliases = {len(ins) + i: i for i in range(len(into))}
        in_specs += [pl.BlockSpec(memory_space=pl.ANY)] * len(into)
        ins += list(into)
    return pl.pallas_call(
        functools.partial(_post_kernel, aliased=len(aliases), n_main=n_main),
        grid=(steps,),
        in_specs=in_specs,
        out_specs=[orow(D_MODEL), otile, orow(LANES), orow(LANES)],
        out_shape=[jax.ShapeDtypeStruct((n_total, D_MODEL), F32),
                   jax.ShapeDtypeStruct((n_total * TILE_ROWS, LANES), F32),
                   jax.ShapeDtypeStruct((n_total, LANES), jnp.int32),
                   jax.ShapeDtypeStruct((n_total, LANES), F32)],
        input_output_aliases=aliases,
        compiler_params=_cparams("parallel"),
        name="post",
    )(*ins)


def _start_rows(rows_ref, hbm, buf, sem, *, rows, to_hbm):
    for r in range(rows):
        at = pl.ds(pl.multiple_of(rows_ref[0, 0, r], TILE_ROWS), TILE_ROWS)
        here = pl.ds(r * TILE_ROWS, TILE_ROWS)
        prio = r % 2
        if to_hbm:
            pltpu.make_async_copy(buf.at[here], hbm.at[at], sem).start(priority=prio)
        else:
            pltpu.make_async_copy(hbm.at[at], buf.at[here], sem).start(priority=prio)


def _wait_rows(hbm, buf, sem, *, rows, to_hbm):
    whole = hbm.at[pl.ds(0, rows * TILE_ROWS)]
    (pltpu.make_async_copy(buf, whole, sem) if to_hbm else pltpu.make_async_copy(whole, buf, sem)).wait()


def _expert_mlp(xb, wgu_bf, wd_bf, bgu_ref, bd_ref, rows):
    acc = jnp.broadcast_to(bd_ref[0], (rows, D_MODEL))
    cw = 512
    for c in range(D_MODEL // cw):
        gs = slice(c * cw, (c + 1) * cw)
        us = slice(D_MODEL + c * cw, D_MODEL + (c + 1) * cw)
        gt = jnp.dot(xb, wgu_bf[:, gs], preferred_element_type=F32) + bgu_ref[0, :, gs]
        up = jnp.dot(xb, wgu_bf[:, us], preferred_element_type=F32) + bgu_ref[0, :, us]
        gt = jnp.minimum(gt, SWIGLU_LIMIT)
        up = jnp.clip(up, -SWIGLU_LIMIT, SWIGLU_LIMIT)
        act = (up + 1.0) * (gt * jax.nn.sigmoid(gt * SWIGLU_ALPHA))
        acc = acc + jnp.dot(act.astype(BF16), wd_bf[gs, :], preferred_element_type=F32)
    return acc


def _moe_kernel(be_ref, src_cur, src_nxt, dst_prv, dst_cur, h_hbm, wgu_ref, bgu_ref, wd_ref, bd_ref, ys_hbm,
                x0, x1, y0, y1, wgu_bf, wd_bf, gsem, ssem, *, rows, plane_rows, n_tok):
    i = pl.program_id(0)
    last = pl.num_programs(0) - 1

    changed = jnp.logical_or(i == 0, be_ref[i] != be_ref[jnp.maximum(i - 1, 0)])

    @pl.when(changed)
    def _():
        step = 128

        def cast(j, c):
            rs = pl.ds(pl.multiple_of(j * step, step), step)
            wgu_bf[rs, :] = wgu_ref[0, rs, :].astype(BF16)
            wd_bf[rs, :] = wd_ref[0, rs, :].astype(BF16)
            return c

        lax.fori_loop(0, D_MODEL // step, cast, 0)

    def run(par):
        xc, xn = (x0, x1) if par == 0 else (x1, x0)
        yc, yp = (y0, y1) if par == 0 else (y1, y0)
        gather = functools.partial(_start_rows, hbm=h_hbm, rows=rows, to_hbm=False)
        scatter = functools.partial(_start_rows, hbm=ys_hbm, rows=rows, to_hbm=True)
        gathered = functools.partial(_wait_rows, h_hbm, rows=rows, to_hbm=False)
        scattered = functools.partial(_wait_rows, ys_hbm, rows=rows, to_hbm=True)

        @pl.when(i == 0)
        def _():
            yp[...] = jnp.zeros(yp.shape, yp.dtype)
            spare = plane_rows - n_tok
            fills = [pltpu.make_async_copy(yp.at[pl.ds(0, spare * TILE_ROWS)],
                                           ys_hbm.at[pl.ds((k * plane_rows + n_tok) * TILE_ROWS, spare * TILE_ROWS)],
                                           ssem.at[1 - par])
                     for k in range(TOP_K)]
            for d in fills:
                d.start()
            for d in fills:
                d.wait()
            gather(src_cur, buf=xc, sem=gsem.at[par])

        gathered(xc, gsem.at[par])
        gather(src_nxt, buf=xn, sem=gsem.at[1 - par])

        @pl.when(i >= 1)
        def _():
            scattered(yc, ssem.at[par])

        used = i < be_ref[last + 1]

        @pl.when(used)
        def _():
            scatter(dst_prv, buf=yp, sem=ssem.at[1 - par])
            xb = _load_token_tiles(xc, rows).astype(BF16)
            _store_token_tiles(yc, _expert_mlp(xb, wgu_bf, wd_bf, bgu_ref, bd_ref, rows))

        @pl.when(jnp.logical_not(used))
        def _():
            scatter(dst_prv, buf=yp, sem=ssem.at[1 - par])

        @pl.when(i == last)
        def _():
            scatter(dst_cur, buf=yc, sem=ssem.at[par])
            gathered(xn, gsem.at[1 - par])
            scattered(yp, ssem.at[1 - par])
            scattered(yc, ssem.at[par])

    for par in range(2):
        pl.when(lax.rem(i, 2) == par)(functools.partial(run, par))


def _moe_experts(block_e, ids, h2, w_gate_up, b_gate_up, w_down, b_down, *, rows, plane_rows, n_tok):
    n_blocks = ids.shape[0] - 2
    ids = lax.optimization_barrier(ids)
    tok = lax.shift_right_logical(ids, 2)
    src = tok * TILE_ROWS
    dst = ((ids & (TOP_K - 1)) * plane_rows + tok) * TILE_ROWS
    idblk = lambda off: pl.BlockSpec((1, 1, rows), lambda i, be: (i + off, 0, 0), memory_space=pltpu.SMEM)
    grid_spec = pltpu.PrefetchScalarGridSpec(
        num_scalar_prefetch=1,
        grid=(n_blocks,),
        in_specs=[
            idblk(1), idblk(2), idblk(0), idblk(1),
            pl.BlockSpec(memory_space=pl.ANY),
            pl.BlockSpec((1, D_MODEL, 2 * D_MODEL), lambda i, be: (be[i], 0, 0)),
            pl.BlockSpec((1, 1, 2 * D_MODEL), lambda i, be: (be[i], 0, 0)),
            pl.BlockSpec((1, D_MODEL, D_MODEL), lambda i, be: (be[i], 0, 0)),
            pl.BlockSpec((1, 1, D_MODEL), lambda i, be: (be[i], 0, 0)),
        ],
        out_specs=pl.BlockSpec(memory_space=pl.ANY),
        scratch_shapes=[pltpu.VMEM((rows * TILE_ROWS, LANES), F32)] * 4 + [
            pltpu.VMEM((D_MODEL, 2 * D_MODEL), BF16),
            pltpu.VMEM((D_MODEL, D_MODEL), BF16),
            pltpu.SemaphoreType.DMA((2,)),
            pltpu.SemaphoreType.DMA((2,))],
    )
    return pl.pallas_call(
        functools.partial(_moe_kernel, rows=rows, plane_rows=plane_rows, n_tok=n_tok),
        grid_spec=grid_spec,
        out_shape=jax.ShapeDtypeStruct((TOP_K * plane_rows * TILE_ROWS, LANES), F32),
        compiler_params=_cparams("arbitrary"),
        name="moe_experts",
    )(block_e, src, src, dst, dst, h2, w_gate_up, b_gate_up[:, None], w_down, b_down[:, None])


def _combine_kernel(ys_ref, x1_ref, gate_ref, lnf_ref, y_ref):
    acc = x1_ref[...]
    tm = acc.shape[0]
    for kslot in range(TOP_K):
        acc = acc + gate_ref[:, kslot:kslot + 1] * _load_token_tiles(ys_ref, tm, kslot)
    y_ref[...] = _rms(acc, lnf_ref[...])


def _combine(ys, x1, gates, ln_f, *, tm, n, row0):
    blk0 = row0 // tm
    return pl.pallas_call(
        _combine_kernel,
        grid=(n // tm,),
        in_specs=[
            pl.BlockSpec((TOP_K, tm * TILE_ROWS, LANES), lambda i: (0, i + blk0, 0)),
            pl.BlockSpec((tm, D_MODEL), lambda i: (i + blk0, 0)),
            pl.BlockSpec((tm, LANES), lambda i: (i + blk0, 0)),
            pl.BlockSpec(ln_f.shape, lambda i: (0, 0)),
        ],
        out_specs=pl.BlockSpec((tm, D_MODEL), lambda i: (i, 0)),
        out_shape=jax.ShapeDtypeStruct((n, D_MODEL), F32),
        compiler_params=_cparams("parallel"),
        name="combine",
    )(ys.reshape(TOP_K, -1, LANES), x1, gates, ln_f)


def _route(topi, n_pad, rows):
    n_tok = topi.shape[0]
    e_flat = topi.reshape(-1)
    nk = e_flat.shape[0]
    n_blocks = -(-(nk + N_EXPERTS * (rows - 1)) // rows)
    spare = n_pad - n_tok
    assert 3 * rows <= TOP_K * spare
    experts = jnp.arange(N_EXPERTS, dtype=jnp.int32)
    order = jnp.argsort(e_flat).astype(jnp.int32)
    sizes = jnp.sum((e_flat[:, None] == experts[None, :]).astype(jnp.int32), axis=0)
    padded = (sizes + rows - 1) // rows * rows
    pends = jnp.cumsum(padded)
    slot = jnp.arange(n_blocks * rows, dtype=jnp.int32)
    past = (slot[:, None] >= pends[None, :]).astype(jnp.int32)
    e_slot = jnp.sum(past, axis=1)
    off = slot - jnp.sum(past * padded[None, :], axis=1)
    src = jnp.sum(past * sizes[None, :], axis=1) + off
    size_e = jnp.sum((e_slot[:, None] == experts[None, :]) * sizes[None, :], axis=1)

    def pad_ids(d):
        return (TOP_K * (n_tok + d % spare) + d // spare).astype(jnp.int32)

    ids = jnp.where(off < size_e, order[jnp.clip(src, 0, nk - 1)], pad_ids((slot // rows) % 2 * rows + slot % rows))
    end_ids = pad_ids(2 * rows + jnp.arange(rows, dtype=jnp.int32))
    ids = jnp.concatenate([end_ids, ids, end_ids]).reshape(n_blocks + 2, 1, rows)
    block_e = jnp.minimum(e_slot[::rows], N_EXPERTS - 1).astype(jnp.int32)
    n_used = (pends[-1] // rows).astype(jnp.int32)
    return jnp.concatenate([block_e, n_used[None]]), ids


def kernel(x_prompt, x_sample, cache_k, cache_v, state_wkv, state_shift, rel_bias, ln1, w_in, rwkv_mu, w0,
           w_decay_up, a0, w_a_up, w_g_up, k_k, k_a, r_k, gn_w, gn_b, w_branch_a, w_branch_b, w_out, ln2,
           w_router, b_router, w_gate_up, b_gate_up, w_down, b_down, ln_f):
    bp, seq, _ = x_prompt.shape
    db = x_sample.shape[0]
    n_p = bp * seq
    n_tot = n_p + db
    n_pad = -(-(n_tot + -(-3 * MOE_ROWS // TOP_K)) // TOKEN_TILE) * TOKEN_TILE
    l = 0

    row = lambda a: a.reshape(1, -1)
    seg = _head_seg()
    prep_params = (row(rwkv_mu[l]), row(w0[l]), w_decay_up[l], row(a0[l]), w_a_up[l], w_g_up[l],
                   row(k_k[l]), row(k_a[l]), row(r_k[l]), seg)
    wr_pad = jnp.pad(w_router[l], ((0, 0), (0, LANES - N_EXPERTS))).astype(BF16)
    br_pad = jnp.pad(b_router[l], (0, LANES - N_EXPERTS), constant_values=NEG).reshape(1, LANES)
    post_small = (row(gn_w[l]), row(gn_b[l]), seg)
    post_tail = (row(ln2[l]), wr_pad, br_pad)

    w_in_bf = w_in[l].astype(BF16)
    consts = (post_small + (w_branch_a[l].astype(BF16), w_branch_b[l].astype(BF16), w_out[l].astype(BF16))
              + post_tail)

    xp = x_prompt.reshape(n_p, D_MODEL)
    q_p, k_p, v_p, rc_p, sg_p, kt_p, vt_p = _inproj(xp, row(ln1[l]), w_in_bf, tm=TOKEN_TILE, seq=seq)
    as3 = lambda a: a.reshape(bp, seq, -1)
    oa_p = _attn_prompt(as3(q_p), as3(k_p), as3(v_p), _prompt_bias(rel_bias))
    prep_p = _rwkv_prep_prompt(as3(rc_p), prep_params, tt=TOKEN_TILE)
    r_, w_, k2_, vr_, kk_, ka_, bonus_p, g_p = prep_p
    s0_p = jnp.zeros((bp, R_HEADS, R_HEAD, R_HEAD), F32)
    y_p, wkv_p = _lane_scan(r_, w_, k2_, vr_, kk_, ka_, s0_p, tc=SCAN_CHUNK)
    flat = lambda a: a.reshape(n_p, -1)
    bufs = _post(xp, flat(y_p), flat(bonus_p), flat(g_p), flat(oa_p), sg_p, consts,
                 tm=TOKEN_TILE, n_total=n_pad)

    xs = x_sample.reshape(db, D_MODEL)
    q_s, k_s, v_s, rc_s, sg_s = _inproj(xs, row(ln1[l]), w_in_bf, tm=db)
    bias_s, b0_s = _sample_bias(rel_bias, cache_k.shape[2])
    oa_s = _attn_sample(q_s, k_s, v_s, cache_k[l], cache_v[l], bias_s, b0_s)
    prep_s = _rwkv_prep_sample(rc_s, state_shift[l], prep_params)
    sr, sw, sk2, svr, skk, ska, bonus_s, g_s = [a[:, None] for a in prep_s]
    y_s, wkv_s = _rwkv_scan(sr, sw, sk2, svr, skk, ska, state_wkv[l], bb=SCAN_BATCH, tc=1)
    x1_all, h2_all, ti_all, tg_all = _post(xs, y_s[:, 0], bonus_s[:, 0], g_s[:, 0], oa_s, sg_s, consts,
                                           tm=db, n_total=n_pad, row0=n_p, into=bufs)

    block_e, ids = _route(ti_all[:n_tot, :TOP_K], n_pad, MOE_ROWS)
    ys = _moe_experts(block_e, ids, h2_all, w_gate_up[l], b_gate_up[l], w_down[l], b_down[l],
                      rows=MOE_ROWS, plane_rows=n_pad, n_tok=n_tot)
    lnf = row(ln_f)
    y_prompt = _combine(ys, x1_all, tg_all, lnf, tm=TOKEN_TILE, n=n_p, row0=0)
    y_sample = _combine(ys, x1_all, tg_all, lnf, tm=db, n=db, row0=n_p)

    heads = lambda a, b_: a.reshape(1, b_, -1, A_HEADS, HEAD_DIM)
    return (y_prompt.reshape(bp, seq, D_MODEL), y_sample.reshape(db, 1, D_MODEL),
            kt_p.transpose(0, 3, 1, 2)[None], vt_p.transpose(0, 3, 1, 2)[None], wkv_p[None], as3(rc_p)[:, -1][None],
            heads(k_s, db), heads(v_s, db), wkv_s[None], rc_s[None])
```

```python
import functools
import math

import jax
import jax.numpy as jnp
from jax import lax
from jax.experimental import pallas as pl
from jax.experimental.pallas import tpu as pltpu

F32 = jnp.float32
BF16 = jnp.bfloat16

D_MODEL = 1024
A_HEADS = 8
HEAD_DIM = 64
A_WIDTH = A_HEADS * HEAD_DIM
PATTERNS = ((128, 1), (512, 4), (2048, 16))
BLOCK = 128
NUM_BUCKETS = 32
MAX_DISTANCE = 2048
SCALE = HEAD_DIM ** -0.5
NEG = -1e30
R_HEADS = 8
R_HEAD = 64
R_WIDTH = R_HEADS * R_HEAD
DECAY_LORA = 64
AAA_LORA = 64
GATE_LORA = 128
GN_EPS = 64e-5
N_RWKV_COLS = 3 * R_WIDTH + DECAY_LORA + AAA_LORA + GATE_LORA
N_IN_COLS = 3 * A_WIDTH + N_RWKV_COLS + 2 * D_MODEL
N_EXPERTS = 32
TOP_K = 4
SWIGLU_LIMIT = 7.0
SWIGLU_ALPHA = 1.702
RMS_EPS = 1e-6

LANES = 128
SUBLANES = 8
VMEM_LIMIT = 56 * 1024 * 1024
TOKEN_TILE = 256
MOE_ROWS = 512
SCAN_BATCH = 2
SCAN_CHUNK = 64


def _cparams(*sem):
    return pltpu.CompilerParams(dimension_semantics=sem, vmem_limit_bytes=VMEM_LIMIT)


def _rms(x, g):
    return x * lax.rsqrt(jnp.mean(x * x, axis=-1, keepdims=True) + RMS_EPS) * g


def _dot(a, b):
    return jnp.dot(a.astype(BF16), b.astype(BF16), preferred_element_type=F32)


def _bf(x):
    return x.astype(BF16).astype(F32)


def _split3(x):
    hi = x.astype(BF16)
    r1 = x - hi.astype(F32)
    mid = r1.astype(BF16)
    lo = (r1 - mid.astype(F32)).astype(BF16)
    return hi, mid, lo


def _segdot(x, seg):
    hi, mid, lo = _split3(x)
    d = lambda p: jnp.dot(p, seg, preferred_element_type=F32)
    return d(hi) + d(mid) + d(lo)


TILE_ROWS = D_MODEL // LANES
assert TILE_ROWS == SUBLANES


def _load_token_tiles(ref, n, *lead):
    return jnp.concatenate([ref[(*lead, pl.ds(p, n, stride=TILE_ROWS), slice(None))] for p in range(TILE_ROWS)], axis=1)


def _store_token_tiles(ref, x):
    n = x.shape[0]
    for p in range(TILE_ROWS):
        ref[pl.ds(p, n, stride=TILE_ROWS), :] = x[:, p * LANES:(p + 1) * LANES]


def _project(x_ref, g_ref, w_ref):
    h = _rms(x_ref[...], g_ref[...]).astype(BF16)
    return lambda c0, c1: _dot(h, w_ref[:, c0:c1])


C_RWKV = 3 * A_WIDTH
C_GATE = C_RWKV + N_RWKV_COLS


def _inproj_kernel(x_ref, g_ref, w_ref, q_ref, k_ref, v_ref, rc_ref, sg_ref):
    mm = _project(x_ref, g_ref, w_ref)
    q_ref[...] = mm(0, A_WIDTH)
    k_ref[...] = mm(A_WIDTH, 2 * A_WIDTH)
    v_ref[...] = mm(2 * A_WIDTH, C_RWKV)
    rc_ref[...] = mm(C_RWKV, C_GATE)
    sg_ref[:, :D_MODEL] = jax.nn.sigmoid(mm(C_GATE, C_GATE + D_MODEL))
    sg_ref[:, D_MODEL:] = jax.nn.sigmoid(mm(C_GATE + D_MODEL, N_IN_COLS))


def _inproj(x, ln1, w, *, tm):
    n = x.shape[0]
    row = lambda c: pl.BlockSpec((tm, c), lambda i: (i, 0))
    full = lambda a: pl.BlockSpec(a.shape, lambda i: (0,) * a.ndim)
    out_cols = (A_WIDTH, A_WIDTH, A_WIDTH, N_RWKV_COLS, 2 * D_MODEL)
    return pl.pallas_call(
        _inproj_kernel,
        grid=(n // tm,),
        in_specs=[row(D_MODEL), full(ln1), full(w)],
        out_specs=[row(c) for c in out_cols],
        out_shape=[jax.ShapeDtypeStruct((n, c), F32) for c in out_cols],
        compiler_params=_cparams("parallel"),
        name="inproj",
    )(x, ln1, w)


N_PREP_PARAMS = 10
N_PREP_OUTS = 8


def _inproj_prompt_kernel(x_ref, g_ref, w_ref, *refs, per):
    params = [r[...] for r in refs[:N_PREP_PARAMS]]
    q_ref, k_ref, v_ref, sg_ref, kt_ref, vt_ref, shift_ref = refs[N_PREP_PARAMS:N_PREP_PARAMS + 7]
    prep_refs = refs[N_PREP_PARAMS + 7:-1]
    carry = refs[-1]
    mm = _project(x_ref, g_ref, w_ref)
    q_ref[...] = mm(0, A_WIDTH)
    k = mm(A_WIDTH, 2 * A_WIDTH)
    v = mm(2 * A_WIDTH, C_RWKV)
    k_ref[...] = k
    v_ref[...] = v
    tm = k.shape[0]
    kt_ref[0] = k.T.reshape(A_HEADS, HEAD_DIM, tm)
    vt_ref[0] = v.T.reshape(A_HEADS, HEAD_DIM, tm)
    sg_ref[:, :D_MODEL] = jax.nn.sigmoid(mm(C_GATE, C_GATE + D_MODEL))
    sg_ref[:, D_MODEL:] = jax.nn.sigmoid(mm(C_GATE + D_MODEL, N_IN_COLS))

    @pl.when(lax.rem(pl.program_id(0), per) == 0)
    def _():
        carry[...] = jnp.zeros(carry.shape, carry.dtype)

    rc = mm(C_RWKV, C_GATE)
    row = lax.broadcasted_iota(jnp.int32, (tm, 1), 0)
    prev = jnp.where(row == 0, carry[...], pltpu.roll(rc, 1, axis=0))
    for o, val in zip(prep_refs, _rwkv_prep_math(rc, prev, *params)):
        o[...] = val
    last = rc[tm - 1:tm, :]
    carry[...] = last
    shift_ref[0] = last


def _inproj_prompt(x, ln1, w, prep_params, *, tm, seq):
    n = x.shape[0]
    per = seq // tm
    row = lambda c: pl.BlockSpec((tm, c), lambda i: (i, 0))
    full = lambda a: pl.BlockSpec(a.shape, lambda i: (0,) * a.ndim)
    t_spec = pl.BlockSpec((1, A_HEADS, HEAD_DIM, tm), lambda i: (i // per, 0, 0, i % per))
    t_shape = jax.ShapeDtypeStruct((n // seq, A_HEADS, HEAD_DIM, seq), F32)
    cols = (A_WIDTH, A_WIDTH, A_WIDTH, 2 * D_MODEL)
    out_specs = ([row(c) for c in cols] + [t_spec, t_spec, pl.BlockSpec((1, 1, N_RWKV_COLS), lambda i: (i // per, 0, 0))]
                 + [row(R_WIDTH)] * N_PREP_OUTS)
    out_shape = ([jax.ShapeDtypeStruct((n, c), F32) for c in cols]
                 + [t_shape, t_shape, jax.ShapeDtypeStruct((n // seq, 1, N_RWKV_COLS), F32)]
                 + [jax.ShapeDtypeStruct((n, R_WIDTH), F32)] * N_PREP_OUTS)
    return pl.pallas_call(
        functools.partial(_inproj_prompt_kernel, per=per),
        grid=(n // tm,),
        in_specs=[row(D_MODEL), full(ln1), full(w)] + [full(a) for a in prep_params],
        out_specs=out_specs,
        out_shape=out_shape,
        scratch_shapes=[pltpu.VMEM((1, N_RWKV_COLS), F32)],
        compiler_params=_cparams("arbitrary"),
        name="inproj_prompt",
    )(x, ln1, w, *prep_params)


def _t5_bucket(dist):
    max_exact = NUM_BUCKETS // 2
    d = jnp.maximum(dist, 1).astype(F32)
    large = max_exact + (jnp.log(d / max_exact) / math.log(MAX_DISTANCE / max_exact)
                         * (NUM_BUCKETS - max_exact)).astype(jnp.int32)
    large = jnp.minimum(large, NUM_BUCKETS - 1)
    return jnp.where(dist < max_exact, dist, large)


def _bias_lookup(rel_bias, dist):
    onehot = (_t5_bucket(dist)[..., None] == jnp.arange(NUM_BUCKETS)).astype(F32)
    return jnp.moveaxis(jnp.dot(onehot, rel_bias.astype(F32), precision=lax.Precision.HIGHEST), -1, 0)


def _prompt_bias(rel_bias):
    i = jnp.arange(BLOCK)[:, None]
    j = jnp.arange(2 * BLOCK)[None, :]
    delta = i + BLOCK - j
    out = []
    for window, dil in PATTERNS:
        n = window // dil
        band = (delta >= 0) & (delta <= n)
        out.append(jnp.where(band[None], _bias_lookup(rel_bias, jnp.clip(delta, 0, n) * dil), NEG))
    return jnp.stack(out, 0)


def _sample_bias(rel_bias, wb):
    dist = wb - jnp.arange(wb)
    out = []
    for window, dil in PATTERNS:
        member = (dist % dil == 0) & (dist <= window)
        out.append(jnp.where(member[None], _bias_lookup(rel_bias, dist), NEG))
    b0 = _bias_lookup(rel_bias, jnp.zeros((1,), jnp.int32))
    return jnp.stack(out, 0), b0


def _attn_prompt_kernel(q_ref, k_ref, v_ref, bias_ref, o_ref, os_ref, ls_ref, *, seq):
    nt = (((1,), (1,)), ((), ()))
    npat = len(PATTERNS)
    w2 = 2 * HEAD_DIM
    lo = lax.broadcasted_iota(jnp.int32, (BLOCK, w2), 1) < HEAD_DIM
    key = lax.broadcasted_iota(jnp.int32, (2 * BLOCK, 2 * BLOCK), 1)

    for g, (window, dil) in enumerate(PATTERNS):
        nb = seq // (BLOCK * dil)
        sh = dil.bit_length() - 1

        def body(it, carry, g=g, dil=dil, nb=nb, sh=sh):
            r = it & (dil - 1)
            c = it >> sh
            start = c * (BLOCK * dil) + r
            rows = pl.ds(start, BLOCK, stride=dil) if dil > 1 else pl.ds(start, BLOCK)
            qb = q_ref[0, rows, :]
            kb = k_ref[0, rows, :]
            vb = v_ref[0, rows, :]
            if nb > 1:
                pstart = jnp.maximum(c - 1, 0) * (BLOCK * dil) + r
                prow = pl.ds(pstart, BLOCK, stride=dil) if dil > 1 else pl.ds(pstart, BLOCK)
                kp = k_ref[0, prow, :]
                vp = v_ref[0, prow, :]
            q2 = jnp.concatenate([jnp.where(lo, qb, 0.0), jnp.where(lo, 0.0, qb)], axis=0).astype(BF16)
            if nb > 1:
                k2 = jnp.concatenate([kp, kb], axis=0)
                v2 = jnp.concatenate([vp, vb], axis=0)
                bias = bias_ref[g, 0]
                valid = (bias > 0.5 * NEG) & ((key >= BLOCK) | (c > 0))
            else:
                k2, v2 = kb, vb
                bias = bias_ref[g, 0, :, BLOCK:]
                valid = bias > 0.5 * NEG
            s = lax.dot_general(q2, k2.astype(BF16), nt, preferred_element_type=F32)
            s = jnp.where(valid, s * SCALE + bias, NEG)
            m = jnp.max(s, axis=-1, keepdims=True)
            p = jnp.exp(s - m)
            l = jnp.sum(p, axis=-1, keepdims=True)
            o2 = jnp.dot((p * (1.0 / l)).astype(BF16), v2.astype(BF16), preferred_element_type=F32)
            lse = jnp.broadcast_to(m + jnp.log(l), (2 * BLOCK, w2))
            os_ref[g, rows, :] = jnp.where(lo, o2[:BLOCK], o2[BLOCK:])
            ls_ref[g, rows, :] = jnp.where(lo, lse[:BLOCK], lse[BLOCK:])
            return carry

        lax.fori_loop(0, dil * nb, body, 0, unroll=8)

    chunk = 256

    def merge(ch, carry):
        rows = pl.ds(pl.multiple_of(ch * chunk, chunk), chunk)
        lse = [ls_ref[g, rows, :] for g in range(npat)]
        top = functools.reduce(jnp.maximum, lse)
        e = [jnp.exp(x - top) for x in lse]
        inv = 1.0 / functools.reduce(lambda a, b_: a + b_, e)
        acc = jnp.zeros((chunk, 2 * HEAD_DIM), F32)
        for g in range(npat):
            acc = acc + (e[g] * inv) * os_ref[g, rows, :]
        o_ref[0, rows, :] = acc
        return carry

    lax.fori_loop(0, seq // chunk, merge, 0)


def _attn_prompt(q, k, v, bias):
    b, s, _ = q.shape
    w2 = 2 * HEAD_DIM
    qspec = pl.BlockSpec((1, s, w2), lambda i, p: (i, 0, p))
    return pl.pallas_call(
        functools.partial(_attn_prompt_kernel, seq=s),
        grid=(b, A_HEADS // 2),
        in_specs=[qspec, qspec, qspec,
                  pl.BlockSpec((len(PATTERNS), 1, 2 * BLOCK, 2 * BLOCK), lambda i, p: (0, p, 0, 0))],
        out_specs=qspec,
        out_shape=jax.ShapeDtypeStruct((b, s, A_WIDTH), F32),
        scratch_shapes=[pltpu.VMEM((len(PATTERNS), s, w2), F32)] * 2,
        compiler_params=_cparams("parallel", "parallel"),
        name="attn_prompt",
    )(q, k, v, bias.reshape(len(PATTERNS), A_HEADS // 2, 2 * BLOCK, 2 * BLOCK))


def _attn_sample_kernel(q_ref, kn_ref, vn_ref, kt_ref, vt_ref, bias_ref, b0_ref, o_ref):
    npat = len(PATTERNS)
    nt = (((1,), (1,)), ((), ()))
    q = q_ref[0]
    v_new = _bf(vn_ref[0])
    s0 = jnp.sum(_bf(q) * _bf(kn_ref[0]), axis=-1, keepdims=True) * SCALE + b0_ref[...]
    s = jnp.concatenate([_dot(q[h:h + 1], kt_ref[0, h]) for h in range(A_HEADS)], axis=0) * SCALE
    ps, p0s, lses = [], [], []
    for g in range(npat):
        bias = bias_ref[g]
        sg = jnp.where(bias > 0.5 * NEG, s + bias, NEG)
        m = jnp.maximum(jnp.max(sg, axis=-1, keepdims=True), s0)
        l = jnp.sum(jnp.exp(sg - m), axis=-1, keepdims=True) + jnp.exp(s0 - m)
        lse = m + jnp.log(l)
        ps.append(jnp.exp(sg - lse))
        p0s.append(_bf(jnp.exp(s0 - lse)))
        lses.append(lse)
    top = functools.reduce(jnp.maximum, lses)
    e = [jnp.exp(x - top) for x in lses]
    inv = 1.0 / functools.reduce(lambda a, b_: a + b_, e)
    w = [_bf(x * inv) for x in e]
    rows = []
    for h in range(A_HEADS):
        p_h = jnp.concatenate([p[h:h + 1] for p in ps], axis=0)
        o_h = lax.dot_general(p_h.astype(BF16), vt_ref[0, h].astype(BF16), nt, preferred_element_type=F32)
        acc = jnp.zeros((1, HEAD_DIM), F32)
        for g in range(npat):
            acc = acc + w[g][h:h + 1] * _bf(o_h[g:g + 1] + p0s[g][h:h + 1] * v_new[h:h + 1])
        rows.append(acc)
    o_ref[0] = jnp.concatenate(rows, axis=0)


def _attn_sample(q, k_new, v_new, cache_k, cache_v, bias, b0):
    db, wb = cache_k.shape[:2]
    heads = lambda a: a.reshape(db, A_HEADS, HEAD_DIM)
    vec = pl.BlockSpec((1, A_HEADS, HEAD_DIM), lambda i: (i, 0, 0))
    full = lambda a: pl.BlockSpec(a.shape, lambda i: (0,) * a.ndim)
    rows_minor = lambda a: a.transpose(0, 2, 3, 1)
    cache = pl.BlockSpec((1, A_HEADS, HEAD_DIM, wb), lambda i: (i, 0, 0, 0))
    args = [heads(q), heads(k_new), heads(v_new), rows_minor(cache_k), rows_minor(cache_v), bias, b0]
    return pl.pallas_call(
        _attn_sample_kernel,
        grid=(db,),
        in_specs=[vec, vec, vec, cache, cache, full(bias), full(b0)],
        out_specs=vec,
        out_shape=jax.ShapeDtypeStruct((db, A_HEADS, HEAD_DIM), F32),
        compiler_params=_cparams("parallel"),
        name="attn_sample",
    )(*args).reshape(db, A_WIDTH)


def _rwkv_prep_math(rc, prev, mu, w0, wd, a0, wa, wg, k_k, k_a, r_k, seg):
    xr = rc + (prev - rc) * mu
    o1, o2, o3 = R_WIDTH, 2 * R_WIDTH, 3 * R_WIDTH
    r = xr[:, :o1]
    kr = xr[:, o1:o2]
    vr = xr[:, o2:o3]
    xw = xr[:, o3:o3 + DECAY_LORA]
    xa = xr[:, o3 + DECAY_LORA:o3 + DECAY_LORA + AAA_LORA]
    xg = xr[:, o3 + DECAY_LORA + AAA_LORA:]
    z = -(w0 + _dot(jnp.tanh(xw), wd))
    softplus = jnp.maximum(z, 0.0) + jnp.log(1.0 + jnp.exp(-jnp.abs(z)))
    decay = jnp.exp(-jnp.exp(-softplus - 0.5))
    a = jax.nn.sigmoid(a0 + _dot(xa, wa))
    g = _dot(jax.nn.sigmoid(xg), wg)
    kk = kr * k_k
    k2 = kr * (1.0 + (a - 1.0) * k_a)
    kk = kk / jnp.maximum(jnp.sqrt(_segdot(kk * kk, seg)), 1e-12)
    bonus = _segdot(r * k2 * r_k, seg) * vr
    return _bf(r), decay, k2, vr, _bf(kk), kk * a, bonus, g


def _rwkv_prep_sample_kernel(rc_ref, prev_ref, *refs):
    p = [x[...] for x in refs[:N_PREP_PARAMS]]
    outs = refs[N_PREP_PARAMS:]
    for o, val in zip(outs, _rwkv_prep_math(rc_ref[...], prev_ref[...], *p)):
        o[...] = val


def _head_seg():
    head = jnp.arange(R_WIDTH) // R_HEAD
    return (head[:, None] == head[None, :]).astype(BF16)


def _rwkv_prep_sample(rc, prev, params):
    n = rc.shape[0]
    full = lambda a: pl.BlockSpec(a.shape, lambda i: (0,) * a.ndim)
    return pl.pallas_call(
        _rwkv_prep_sample_kernel,
        grid=(1,),
        in_specs=[full(rc), full(prev)] + [full(a) for a in params],
        out_specs=[pl.BlockSpec((n, R_WIDTH), lambda i: (0, 0))] * 8,
        out_shape=[jax.ShapeDtypeStruct((n, R_WIDTH), F32)] * 8,
        compiler_params=_cparams("arbitrary"),
        name="rwkv_prep_sample",
    )(rc, prev, *params)


def _seg_lane_sum(x, lo_mask):
    lo = jnp.sum(jnp.where(lo_mask, x, 0.0), axis=-1, keepdims=True)
    hi = jnp.sum(jnp.where(lo_mask, 0.0, x), axis=-1, keepdims=True)
    return jnp.where(lo_mask, lo, hi)


def _rwkv_scan_kernel(r_ref, w_ref, k_ref, v_ref, kk_ref, ka_ref, s0_ref, y_ref, sT_ref, st_ref, *, bb, tc):
    pairs = R_HEADS // 2
    w2 = 2 * R_HEAD

    @pl.when(pl.program_id(1) == 0)
    def _():
        for b in range(bb):
            for p in range(pairs):
                st_ref[b, p] = jnp.concatenate([s0_ref[b, 2 * p], s0_ref[b, 2 * p + 1]], axis=1)

    lane = lax.broadcasted_iota(jnp.int32, (R_HEAD, w2), 1)
    sub = lax.broadcasted_iota(jnp.int32, (R_HEAD, w2), 0)
    lo_mask = lane < R_HEAD
    eye2 = (lane & (R_HEAD - 1)) == sub

    grp = min(8, tc)

    def group(tg, carry):
        rows = pl.ds(pl.multiple_of(tg * grp, grp), grp)
        for b in range(bb):
            for p in range(pairs):
                cols = slice(p * w2, (p + 1) * w2)
                r8, w8, k8, v8, kk8, ka8 = (ref[b, rows, cols] for ref in (r_ref, w_ref, k_ref, v_ref, kk_ref, ka_ref))
                s = st_ref[b, p]
                sb = _bf(s)
                ys = []
                for j in range(grp):
                    row = lambda a: a[j:j + 1, :]
                    sa = -_seg_lane_sum(sb * row(kk8), lo_mask)
                    vcol = _seg_lane_sum(jnp.where(eye2, jnp.broadcast_to(row(v8), (R_HEAD, w2)), 0.0), lo_mask)
                    s = s * row(w8) + sa * row(ka8) + vcol * row(k8)
                    sb = _bf(s)
                    yfull = _seg_lane_sum(sb * row(r8), lo_mask)
                    ys.append(jnp.sum(jnp.where(eye2, yfull, 0.0), axis=0, keepdims=True))
                st_ref[b, p] = s
                y_ref[b, rows, cols] = jnp.concatenate(ys, axis=0) if grp > 1 else ys[0]
        return carry

    lax.fori_loop(0, tc // grp, group, 0)

    @pl.when(pl.program_id(1) == pl.num_programs(1) - 1)
    def _():
        for b in range(bb):
            for p in range(pairs):
                s = st_ref[b, p]
                sT_ref[b, 2 * p] = s[:, :R_HEAD]
                sT_ref[b, 2 * p + 1] = s[:, R_HEAD:]


def _rwkv_scan(r, w, k, v, kk, ka, s0, *, bb, tc):
    b, t, _ = r.shape
    seq = pl.BlockSpec((bb, tc, R_WIDTH), lambda i, j: (i, j, 0))
    state = pl.BlockSpec((bb, R_HEADS, R_HEAD, R_HEAD), lambda i, j: (i, 0, 0, 0))
    return pl.pallas_call(
        functools.partial(_rwkv_scan_kernel, bb=bb, tc=tc),
        grid=(b // bb, t // tc),
        in_specs=[seq] * 6 + [state],
        out_specs=[seq, state],
        out_shape=[jax.ShapeDtypeStruct((b, t, R_WIDTH), F32),
                   jax.ShapeDtypeStruct((b, R_HEADS, R_HEAD, R_HEAD), F32)],
        scratch_shapes=[pltpu.VMEM((bb, R_HEADS // 2, R_HEAD, 2 * R_HEAD), F32)],
        compiler_params=_cparams("parallel", "arbitrary"),
        name="rwkv_scan",
    )(r, w, k, v, kk, ka, s0)


CHAINS = LANES // 2
K2 = R_HEAD // 2


def _chain_rows(x):
    b, t, _ = x.shape
    x = x.reshape(b, t, R_HEADS, 2, K2).transpose(1, 4, 3, 0, 2)
    return x.reshape(t, K2, LANES)


def _lane_scan_kernel(kk_ref, w_ref, ka_ref, kx_ref, r_ref, v_ref, s0_ref, y_ref, sT_ref, s_ref, *, tc):
    lo = lax.broadcasted_iota(jnp.int32, (K2, LANES), 1) < CHAINS

    @pl.when(pl.program_id(0) == 0)
    def _():
        for k2 in range(K2):
            s_ref[k2] = s0_ref[k2]

    def both_halves(x):
        return x + pltpu.roll(x, CHAINS, axis=1)

    nhalf = 2
    vh = R_HEAD // nhalf

    def row(ref, t, k2):
        return jnp.broadcast_to(ref[t, k2:k2 + 1, :], (vh, LANES))

    acc0 = []
    for hf in range(nhalf):
        vs = slice(hf * vh, (hf + 1) * vh)
        a = jnp.zeros((vh, LANES), F32)
        for k2 in range(K2):
            a = a + _bf(s_ref[k2, vs, :]) * row(kk_ref, 0, k2)
        acc0.append(a)

    def step(t, acc):
        tn = jnp.minimum(t + 1, tc - 1)
        nxt, ys = [], []
        v_t = v_ref[t]
        v_sw = pltpu.roll(v_t, CHAINS, axis=1)
        for hf in range(nhalf):
            vs = slice(hf * vh, (hf + 1) * vh)
            vv = jnp.where(lo, v_t, v_sw) if hf == 0 else jnp.where(lo, v_sw, v_t)
            sa = -both_halves(acc[hf])
            yacc = jnp.zeros((vh, LANES), F32)
            nacc = jnp.zeros((vh, LANES), F32)
            for k2 in range(K2):
                s = s_ref[k2, vs, :] * row(w_ref, t, k2) + sa * row(ka_ref, t, k2) + vv * row(kx_ref, t, k2)
                s_ref[k2, vs, :] = s
                sb = _bf(s)
                yacc = yacc + sb * row(r_ref, t, k2)
                nacc = nacc + sb * row(kk_ref, tn, k2)
            ys.append(both_halves(yacc))
            nxt.append(nacc)
        y_ref[t] = jnp.where(lo, ys[0], ys[1])
        return tuple(nxt)

    lax.fori_loop(0, tc, step, tuple(acc0))

    @pl.when(pl.program_id(0) == pl.num_programs(0) - 1)
    def _():
        for k2 in range(K2):
            sT_ref[k2] = s_ref[k2]


def _lane_scan(r, w, k, v, kk, ka, s0, *, tc):
    b, t, _ = r.shape
    assert b * R_HEADS == CHAINS
    ops = [_chain_rows(x) for x in (kk, w, ka, k, r, v)]
    s0c = s0.reshape(b, R_HEADS, 2, K2, 2, K2).transpose(5, 2, 3, 4, 0, 1).reshape(K2, R_HEAD, LANES)
    rows = pl.BlockSpec((tc, K2, LANES), lambda i: (i, 0, 0))
    state = pl.BlockSpec((K2, R_HEAD, LANES), lambda i: (0, 0, 0))
    y, st = pl.pallas_call(
        functools.partial(_lane_scan_kernel, tc=tc),
        grid=(t // tc,),
        in_specs=[rows] * 6 + [state],
        out_specs=[rows, state],
        out_shape=[jax.ShapeDtypeStruct((t, K2, LANES), F32), jax.ShapeDtypeStruct((K2, R_HEAD, LANES), F32)],
        scratch_shapes=[pltpu.VMEM((K2, R_HEAD, LANES), F32)],
        compiler_params=_cparams("arbitrary"),
        name="rwkv_lane_scan",
    )(*ops, s0c)
    y = y.reshape(t, K2, 2, b, R_HEADS).transpose(3, 0, 4, 2, 1).reshape(b, t, R_WIDTH)
    st = st.reshape(K2, 2, K2, 2, b, R_HEADS).transpose(4, 5, 1, 2, 3, 0).reshape(b, R_HEADS, R_HEAD, R_HEAD)
    return y, st


def _post_kernel(x_ref, y_ref, bonus_ref, g_ref, oa_ref, sg_ref, gnw_ref, gnb_ref, seg_ref, wba_ref, wbb_ref,
                 wout_ref, ln2_ref, wr_ref, br_ref, *rest, aliased, n_main):
    outs = rest[aliased:]
    x1_ref, h2_ref, ti_ref, tg_ref = outs

    @pl.when(pl.program_id(0) >= n_main)
    def _():
        for o in outs:
            o[...] = jnp.zeros(o.shape, o.dtype)

    @pl.when(pl.program_id(0) < n_main)
    def _():
        _post_body(x_ref, y_ref, bonus_ref, g_ref, oa_ref, sg_ref, gnw_ref, gnb_ref, seg_ref, wba_ref, wbb_ref,
                   wout_ref, ln2_ref, wr_ref, br_ref, x1_ref, h2_ref, ti_ref, tg_ref)


def _post_body(x_ref, y_ref, bonus_ref, g_ref, oa_ref, sg_ref, gnw_ref, gnb_ref, seg_ref, wba_ref, wbb_ref,
               wout_ref, ln2_ref, wr_ref, br_ref, x1_ref, h2_ref, ti_ref, tg_ref):
    y = y_ref[...]
    seg = seg_ref[...]
    mu = _segdot(y, seg) * (1.0 / R_HEAD)
    yc = y - mu
    var = _segdot(yc * yc, seg) * (1.0 / R_HEAD)
    yn = yc * lax.rsqrt(var + GN_EPS) * gnw_ref[...] + gnb_ref[...]
    o_b = (yn + bonus_ref[...]) * g_ref[...]
    mixed = (sg_ref[:, :D_MODEL] * _dot(oa_ref[...], wba_ref[...])
             + sg_ref[:, D_MODEL:] * _dot(o_b, wbb_ref[...]))
    x1 = x_ref[...] + _dot(mixed, wout_ref[...])
    x1_ref[...] = x1
    h2 = _rms(x1, ln2_ref[...])
    _store_token_tiles(h2_ref, h2)
    logits = _dot(h2, wr_ref[...]) + br_ref[...]
    lane = lax.broadcasted_iota(jnp.int32, logits.shape, 1).astype(F32)
    work = logits
    vals, idxs = [], []
    for _ in range(TOP_K):
        m = jnp.max(work, axis=-1, keepdims=True)
        idx = jnp.min(jnp.where(work == m, lane, float(LANES)), axis=-1, keepdims=True)
        vals.append(m)
        idxs.append(idx)
        work = jnp.where(lane == idx, -jnp.inf, work)
    es = [jnp.exp(v - vals[0]) for v in vals]
    tot = es[0] + es[1] + es[2] + es[3]
    ti = jnp.zeros(logits.shape, F32)
    tg = jnp.zeros(logits.shape, F32)
    for kslot in range(TOP_K):
        ti = jnp.where(lane == float(kslot), idxs[kslot], ti)
        tg = jnp.where(lane == float(kslot), es[kslot] / tot, tg)
    ti_ref[...] = ti.astype(jnp.int32)
    tg_ref[...] = tg


def _post(x, y, bonus, g, oa, sg, consts, *, tm, n_total, row0=0, into=None):
    n = x.shape[0]
    blk0 = row0 // tm
    n_main = n // tm
    steps = n_total // tm if into is None else n_main
    row = lambda c: pl.BlockSpec((tm, c), lambda i: (jnp.minimum(i, n_main - 1), 0))
    orow = lambda c: pl.BlockSpec((tm, c), lambda i: (i + blk0, 0))
    otile = pl.BlockSpec((tm * TILE_ROWS, LANES), lambda i: (i + blk0, 0))
    full = lambda a: pl.BlockSpec(a.shape, lambda i: (0,) * a.ndim)
    ins = [x, y, bonus, g, oa, sg, *consts]
    in_specs = [row(D_MODEL), row(R_WIDTH), row(R_WIDTH), row(R_WIDTH), row(A_WIDTH), row(2 * D_MODEL)]
    in_specs += [full(a) for a in consts]
    aliases = {}
    if into is not None:
        aliases = {len(ins) + i: i for i in range(len(into))}
        in_specs += [pl.BlockSpec(memory_space=pl.ANY)] * len(into)
        ins += list(into)
    return pl.pallas_call(
        functools.partial(_post_kernel, aliased=len(aliases), n_main=n_main),
        grid=(steps,),
        in_specs=in_specs,
        out_specs=[orow(D_MODEL), otile, orow(LANES), orow(LANES)],
        out_shape=[jax.ShapeDtypeStruct((n_total, D_MODEL), F32),
                   jax.ShapeDtypeStruct((n_total * TILE_ROWS, LANES), F32),
                   jax.ShapeDtypeStruct((n_total, LANES), jnp.int32),
                   jax.ShapeDtypeStruct((n_total, LANES), F32)],
        input_output_aliases=aliases,
        compiler_params=_cparams("parallel"),
        name="post",
    )(*ins)


def _start_rows(rows_ref, hbm, buf, sem, *, rows, to_hbm):
    for r in range(rows):
        at = pl.ds(pl.multiple_of(rows_ref[0, 0, r], TILE_ROWS), TILE_ROWS)
        here = pl.ds(r * TILE_ROWS, TILE_ROWS)
        prio = r % 2
        if to_hbm:
            pltpu.make_async_copy(buf.at[here], hbm.at[at], sem).start(priority=prio)
        else:
            pltpu.make_async_copy(hbm.at[at], buf.at[here], sem).start(priority=prio)


def _wait_rows(hbm, buf, sem, *, rows, to_hbm):
    whole = hbm.at[pl.ds(0, rows * TILE_ROWS)]
    (pltpu.make_async_copy(buf, whole, sem) if to_hbm else pltpu.make_async_copy(whole, buf, sem)).wait()


def _expert_mlp(xb, wgu_bf, wd_bf, bgu_ref, bd_ref, rows):
    acc = jnp.broadcast_to(bd_ref[0], (rows, D_MODEL))
    cw = 512
    for c in range(D_MODEL // cw):
        gs = slice(c * cw, (c + 1) * cw)
        us = slice(D_MODEL + c * cw, D_MODEL + (c + 1) * cw)
        gt = jnp.dot(xb, wgu_bf[:, gs], preferred_element_type=F32) + bgu_ref[0, :, gs]
        up = jnp.dot(xb, wgu_bf[:, us], preferred_element_type=F32) + bgu_ref[0, :, us]
        gt = jnp.minimum(gt, SWIGLU_LIMIT)
        up = jnp.clip(up, -SWIGLU_LIMIT, SWIGLU_LIMIT)
        act = (up + 1.0) * (gt * jax.nn.sigmoid(gt * SWIGLU_ALPHA))
        acc = acc + jnp.dot(act.astype(BF16), wd_bf[gs, :], preferred_element_type=F32)
    return acc


def _moe_kernel(be_ref, src_cur, src_nxt, dst_prv, dst_cur, h_hbm, wgu_ref, bgu_ref, wd_ref, bd_ref, ys_hbm,
                x0, x1, y0, y1, wgu_bf, wd_bf, gsem, ssem, *, rows, plane_rows, n_tok):
    i = pl.program_id(0)
    last = pl.num_programs(0) - 1

    changed = jnp.logical_or(i == 0, be_ref[i] != be_ref[jnp.maximum(i - 1, 0)])

    @pl.when(changed)
    def _():
        step = 128

        def cast(j, c):
            rs = pl.ds(pl.multiple_of(j * step, step), step)
            wgu_bf[rs, :] = wgu_ref[0, rs, :].astype(BF16)
            wd_bf[rs, :] = wd_ref[0, rs, :].astype(BF16)
            return c

        lax.fori_loop(0, D_MODEL // step, cast, 0)

    def run(par):
        xc, xn = (x0, x1) if par == 0 else (x1, x0)
        yc, yp = (y0, y1) if par == 0 else (y1, y0)
        gather = functools.partial(_start_rows, hbm=h_hbm, rows=rows, to_hbm=False)
        scatter = functools.partial(_start_rows, hbm=ys_hbm, rows=rows, to_hbm=True)
        gathered = functools.partial(_wait_rows, h_hbm, rows=rows, to_hbm=False)
        scattered = functools.partial(_wait_rows, ys_hbm, rows=rows, to_hbm=True)

        @pl.when(i == 0)
        def _():
            yp[...] = jnp.zeros(yp.shape, yp.dtype)
            spare = plane_rows - n_tok
            fills = [pltpu.make_async_copy(yp.at[pl.ds(0, spare * TILE_ROWS)],
                                           ys_hbm.at[pl.ds((k * plane_rows + n_tok) * TILE_ROWS, spare * TILE_ROWS)],
                                           ssem.at[1 - par])
                     for k in range(TOP_K)]
            for d in fills:
                d.start()
            for d in fills:
                d.wait()
            gather(src_cur, buf=xc, sem=gsem.at[par])

        gathered(xc, gsem.at[par])
        gather(src_nxt, buf=xn, sem=gsem.at[1 - par])

        @pl.when(i >= 1)
        def _():
            scattered(yc, ssem.at[par])

        used = i < be_ref[last + 1]

        @pl.when(used)
        def _():
            scatter(dst_prv, buf=yp, sem=ssem.at[1 - par])
            xb = _load_token_tiles(xc, rows).astype(BF16)
            _store_token_tiles(yc, _expert_mlp(xb, wgu_bf, wd_bf, bgu_ref, bd_ref, rows))

        @pl.when(jnp.logical_not(used))
        def _():
            scatter(dst_prv, buf=yp, sem=ssem.at[1 - par])

        @pl.when(i == last)
        def _():
            scatter(dst_cur, buf=yc, sem=ssem.at[par])
            gathered(xn, gsem.at[1 - par])
            scattered(yp, ssem.at[1 - par])
            scattered(yc, ssem.at[par])

    for par in range(2):
        pl.when(lax.rem(i, 2) == par)(functools.partial(run, par))


def _moe_experts(block_e, ids, h2, w_gate_up, b_gate_up, w_down, b_down, *, rows, plane_rows, n_tok):
    n_blocks = ids.shape[0] - 2
    ids = lax.optimization_barrier(ids)
    tok = lax.shift_right_logical(ids, 2)
    src = tok * TILE_ROWS
    dst = ((ids & (TOP_K - 1)) * plane_rows + tok) * TILE_ROWS
    idblk = lambda off: pl.BlockSpec((1, 1, rows), lambda i, be: (i + off, 0, 0), memory_space=pltpu.SMEM)
    grid_spec = pltpu.PrefetchScalarGridSpec(
        num_scalar_prefetch=1,
        grid=(n_blocks,),
        in_specs=[
            idblk(1), idblk(2), idblk(0), idblk(1),
            pl.BlockSpec(memory_space=pl.ANY),
            pl.BlockSpec((1, D_MODEL, 2 * D_MODEL), lambda i, be: (be[i], 0, 0)),
            pl.BlockSpec((1, 1, 2 * D_MODEL), lambda i, be: (be[i], 0, 0)),
            pl.BlockSpec((1, D_MODEL, D_MODEL), lambda i, be: (be[i], 0, 0)),
            pl.BlockSpec((1, 1, D_MODEL), lambda i, be: (be[i], 0, 0)),
        ],
        out_specs=pl.BlockSpec(memory_space=pl.ANY),
        scratch_shapes=[pltpu.VMEM((rows * TILE_ROWS, LANES), F32)] * 4 + [
            pltpu.VMEM((D_MODEL, 2 * D_MODEL), BF16),
            pltpu.VMEM((D_MODEL, D_MODEL), BF16),
            pltpu.SemaphoreType.DMA((2,)),
            pltpu.SemaphoreType.DMA((2,))],
    )
    return pl.pallas_call(
        functools.partial(_moe_kernel, rows=rows, plane_rows=plane_rows, n_tok=n_tok),
        grid_spec=grid_spec,
        out_shape=jax.ShapeDtypeStruct((TOP_K * plane_rows * TILE_ROWS, LANES), F32),
        compiler_params=_cparams("arbitrary"),
        name="moe_experts",
    )(block_e, src, src, dst, dst, h2, w_gate_up, b_gate_up[:, None], w_down, b_down[:, None])


def _combine_kernel(ys_ref, x1_ref, gate_ref, lnf_ref, y_ref):
    acc = x1_ref[...]
    tm = acc.shape[0]
    for kslot in range(TOP_K):
        acc = acc + gate_ref[:, kslot:kslot + 1] * _load_token_tiles(ys_ref, tm, kslot)
    y_ref[...] = _rms(acc, lnf_ref[...])


def _combine(ys, x1, gates, ln_f, *, tm, n, row0):
    blk0 = row0 // tm
    return pl.pallas_call(
        _combine_kernel,
        grid=(n // tm,),
        in_specs=[
            pl.BlockSpec((TOP_K, tm * TILE_ROWS, LANES), lambda i: (0, i + blk0, 0)),
            pl.BlockSpec((tm, D_MODEL), lambda i: (i + blk0, 0)),
            pl.BlockSpec((tm, LANES), lambda i: (i + blk0, 0)),
            pl.BlockSpec(ln_f.shape, lambda i: (0, 0)),
        ],
        out_specs=pl.BlockSpec((tm, D_MODEL), lambda i: (i, 0)),
        out_shape=jax.ShapeDtypeStruct((n, D_MODEL), F32),
        compiler_params=_cparams("parallel"),
        name="combine",
    )(ys.reshape(TOP_K, -1, LANES), x1, gates, ln_f)


def _route(topi, n_pad, rows):
    n_tok = topi.shape[0]
    e_flat = topi.reshape(-1)
    nk = e_flat.shape[0]
    n_blocks = -(-(nk + N_EXPERTS * (rows - 1)) // rows)
    spare = n_pad - n_tok
    assert 3 * rows <= TOP_K * spare
    experts = jnp.arange(N_EXPERTS, dtype=jnp.int32)
    order = jnp.argsort(e_flat).astype(jnp.int32)
    sizes = jnp.sum((e_flat[:, None] == experts[None, :]).astype(jnp.int32), axis=0)
    padded = (sizes + rows - 1) // rows * rows
    pends = jnp.cumsum(padded)
    slot = jnp.arange(n_blocks * rows, dtype=jnp.int32)
    past = (slot[:, None] >= pends[None, :]).astype(jnp.int32)
    e_slot = jnp.sum(past, axis=1)
    off = slot - jnp.sum(past * padded[None, :], axis=1)
    src = jnp.sum(past * sizes[None, :], axis=1) + off
    size_e = jnp.sum((e_slot[:, None] == experts[None, :]) * sizes[None, :], axis=1)

    def pad_ids(d):
        return (TOP_K * (n_tok + d % spare) + d // spare).astype(jnp.int32)

    ids = jnp.where(off < size_e, order[jnp.clip(src, 0, nk - 1)], pad_ids((slot // rows) % 2 * rows + slot % rows))
    end_ids = pad_ids(2 * rows + jnp.arange(rows, dtype=jnp.int32))
    ids = jnp.concatenate([end_ids, ids, end_ids]).reshape(n_blocks + 2, 1, rows)
    block_e = jnp.minimum(e_slot[::rows], N_EXPERTS - 1).astype(jnp.int32)
    n_used = (pends[-1] // rows).astype(jnp.int32)
    return jnp.concatenate([block_e, n_used[None]]), ids


def kernel(x_prompt, x_sample, cache_k, cache_v, state_wkv, state_shift, rel_bias, ln1, w_in, rwkv_mu, w0,
           w_decay_up, a0, w_a_up, w_g_up, k_k, k_a, r_k, gn_w, gn_b, w_branch_a, w_branch_b, w_out, ln2,
           w_router, b_router, w_gate_up, b_gate_up, w_down, b_down, ln_f):
    bp, seq, _ = x_prompt.shape
    db = x_sample.shape[0]
    n_p = bp * seq
    n_tot = n_p + db
    n_pad = -(-(n_tot + -(-3 * MOE_ROWS // TOP_K)) // TOKEN_TILE) * TOKEN_TILE
    l = 0

    row = lambda a: a.reshape(1, -1)
    seg = _head_seg()
    prep_params = (row(rwkv_mu[l]), row(w0[l]), w_decay_up[l], row(a0[l]), w_a_up[l], w_g_up[l],
                   row(k_k[l]), row(k_a[l]), row(r_k[l]), seg)
    wr_pad = jnp.pad(w_router[l], ((0, 0), (0, LANES - N_EXPERTS))).astype(BF16)
    br_pad = jnp.pad(b_router[l], (0, LANES - N_EXPERTS), constant_values=NEG).reshape(1, LANES)
    post_small = (row(gn_w[l]), row(gn_b[l]), seg)
    post_tail = (row(ln2[l]), wr_pad, br_pad)

    w_in_bf = w_in[l].astype(BF16)
    consts = (post_small + (w_branch_a[l].astype(BF16), w_branch_b[l].astype(BF16), w_out[l].astype(BF16))
              + post_tail)

    xp = x_prompt.reshape(n_p, D_MODEL)
    q_p, k_p, v_p, sg_p, kt_p, vt_p, shift_p, *prep_p = _inproj_prompt(xp, row(ln1[l]), w_in_bf, prep_params,
                                                                     tm=TOKEN_TILE, seq=seq)
    as3 = lambda a: a.reshape(bp, seq, -1)
    oa_p = _attn_prompt(as3(q_p), as3(k_p), as3(v_p), _prompt_bias(rel_bias))
    r_, w_, k2_, vr_, kk_, ka_, bonus_p, g_p = prep_p
    s0_p = jnp.zeros((bp, R_HEADS, R_HEAD, R_HEAD), F32)
    y_p, wkv_p = _lane_scan(*(as3(a) for a in (r_, w_, k2_, vr_, kk_, ka_)), s0_p, tc=SCAN_CHUNK)
    flat = lambda a: a.reshape(n_p, -1)
    bufs = _post(xp, flat(y_p), bonus_p, g_p, flat(oa_p), sg_p, consts,
                 tm=TOKEN_TILE, n_total=n_pad)

    xs = x_sample.reshape(db, D_MODEL)
    q_s, k_s, v_s, rc_s, sg_s = _inproj(xs, row(ln1[l]), w_in_bf, tm=db)
    bias_s, b0_s = _sample_bias(rel_bias, cache_k.shape[2])
    oa_s = _attn_sample(q_s, k_s, v_s, cache_k[l], cache_v[l], bias_s, b0_s)
    prep_s = _rwkv_prep_sample(rc_s, state_shift[l], prep_params)
    sr, sw, sk2, svr, skk, ska, bonus_s, g_s = [a[:, None] for a in prep_s]
    y_s, wkv_s = _rwkv_scan(sr, sw, sk2, svr, skk, ska, state_wkv[l], bb=SCAN_BATCH, tc=1)
    x1_all, h2_all, ti_all, tg_all = _post(xs, y_s[:, 0], bonus_s[:, 0], g_s[:, 0], oa_s, sg_s, consts,
                                           tm=db, n_total=n_pad, row0=n_p, into=bufs)

    block_e, ids = _route(ti_all[:n_tot, :TOP_K], n_pad, MOE_ROWS)
    ys = _moe_experts(block_e, ids, h2_all, w_gate_up[l], b_gate_up[l], w_down[l], b_down[l],
                      rows=MOE_ROWS, plane_rows=n_pad, n_tok=n_tot)
    lnf = row(ln_f)
    y_prompt = _combine(ys, x1_all, tg_all, lnf, tm=TOKEN_TILE, n=n_p, row0=0)
    y_sample = _combine(ys, x1_all, tg_all, lnf, tm=db, n=db, row0=n_p)

    heads = lambda a, b_: a.reshape(1, b_, -1, A_HEADS, HEAD_DIM)
    return (y_prompt.reshape(bp, seq, D_MODEL), y_sample.reshape(db, 1, D_MODEL),
            kt_p.transpose(0, 3, 1, 2)[None], vt_p.transpose(0, 3, 1, 2)[None], wkv_p[None], shift_p[:, 0][None],
            heads(k_s, db), heads(v_s, db), wkv_s[None], rc_s[None])
```

```python
import functools
import math

import jax
import jax.numpy as jnp
from jax import lax
from jax.experimental import pallas as pl
from jax.experimental.pallas import tpu as pltpu

F32 = jnp.float32
BF16 = jnp.bfloat16

D_MODEL = 1024
A_HEADS = 8
HEAD_DIM = 64
A_WIDTH = A_HEADS * HEAD_DIM
PATTERNS = ((128, 1), (512, 4), (2048, 16))
BLOCK = 128
NUM_BUCKETS = 32
MAX_DISTANCE = 2048
SCALE = HEAD_DIM ** -0.5
NEG = -1e30
R_HEADS = 8
R_HEAD = 64
R_WIDTH = R_HEADS * R_HEAD
DECAY_LORA = 64
AAA_LORA = 64
GATE_LORA = 128
GN_EPS = 64e-5
N_RWKV_COLS = 3 * R_WIDTH + DECAY_LORA + AAA_LORA + GATE_LORA
N_IN_COLS = 3 * A_WIDTH + N_RWKV_COLS + 2 * D_MODEL
N_EXPERTS = 32
TOP_K = 4
SWIGLU_LIMIT = 7.0
SWIGLU_ALPHA = 1.702
RMS_EPS = 1e-6

LANES = 128
SUBLANES = 8
VMEM_LIMIT = 56 * 1024 * 1024
TOKEN_TILE = 256
MOE_ROWS = 512
SCAN_BATCH = 2
SCAN_CHUNK = 64


def _cparams(*sem):
    return pltpu.CompilerParams(dimension_semantics=sem, vmem_limit_bytes=VMEM_LIMIT)


def _rms(x, g):
    return x * lax.rsqrt(jnp.mean(x * x, axis=-1, keepdims=True) + RMS_EPS) * g


def _dot(a, b):
    return jnp.dot(a.astype(BF16), b.astype(BF16), preferred_element_type=F32)


def _bf(x):
    return x.astype(BF16).astype(F32)


def _split3(x):
    hi = x.astype(BF16)
    r1 = x - hi.astype(F32)
    mid = r1.astype(BF16)
    lo = (r1 - mid.astype(F32)).astype(BF16)
    return hi, mid, lo


def _segdot(x, seg):
    hi, mid, lo = _split3(x)
    d = lambda p: jnp.dot(p, seg, preferred_element_type=F32)
    return d(hi) + d(mid) + d(lo)


TILE_ROWS = D_MODEL // LANES
assert TILE_ROWS == SUBLANES


def _load_token_tiles(ref, n, *lead):
    return jnp.concatenate([ref[(*lead, pl.ds(p, n, stride=TILE_ROWS), slice(None))] for p in range(TILE_ROWS)], axis=1)


def _store_token_tiles(ref, x):
    n = x.shape[0]
    for p in range(TILE_ROWS):
        ref[pl.ds(p, n, stride=TILE_ROWS), :] = x[:, p * LANES:(p + 1) * LANES]


def _project(x_ref, g_ref, w_ref):
    h = _rms(x_ref[...], g_ref[...]).astype(BF16)
    return lambda c0, c1: _dot(h, w_ref[:, c0:c1])


C_RWKV = 3 * A_WIDTH
C_GATE = C_RWKV + N_RWKV_COLS


def _inproj_kernel(x_ref, g_ref, w_ref, q_ref, k_ref, v_ref, rc_ref, sg_ref):
    mm = _project(x_ref, g_ref, w_ref)
    q_ref[...] = mm(0, A_WIDTH)
    k_ref[...] = mm(A_WIDTH, 2 * A_WIDTH)
    v_ref[...] = mm(2 * A_WIDTH, C_RWKV)
    rc_ref[...] = mm(C_RWKV, C_GATE)
    sg_ref[:, :D_MODEL] = jax.nn.sigmoid(mm(C_GATE, C_GATE + D_MODEL))
    sg_ref[:, D_MODEL:] = jax.nn.sigmoid(mm(C_GATE + D_MODEL, N_IN_COLS))


def _inproj(x, ln1, w, *, tm):
    n = x.shape[0]
    row = lambda c: pl.BlockSpec((tm, c), lambda i: (i, 0))
    full = lambda a: pl.BlockSpec(a.shape, lambda i: (0,) * a.ndim)
    out_cols = (A_WIDTH, A_WIDTH, A_WIDTH, N_RWKV_COLS, 2 * D_MODEL)
    return pl.pallas_call(
        _inproj_kernel,
        grid=(n // tm,),
        in_specs=[row(D_MODEL), full(ln1), full(w)],
        out_specs=[row(c) for c in out_cols],
        out_shape=[jax.ShapeDtypeStruct((n, c), F32) for c in out_cols],
        compiler_params=_cparams("parallel"),
        name="inproj",
    )(x, ln1, w)


N_PREP_PARAMS = 10
N_PREP_OUTS = 8


def _inproj_prompt_kernel(x_ref, g_ref, w_ref, *refs, per):
    params = [r[...] for r in refs[:N_PREP_PARAMS]]
    q_ref, k_ref, v_ref, sg_ref, kt_ref, vt_ref, shift_ref = refs[N_PREP_PARAMS:N_PREP_PARAMS + 7]
    prep_refs = refs[N_PREP_PARAMS + 7:-1]
    carry = refs[-1]
    mm = _project(x_ref, g_ref, w_ref)
    q_ref[...] = mm(0, A_WIDTH)
    k = mm(A_WIDTH, 2 * A_WIDTH)
    v = mm(2 * A_WIDTH, C_RWKV)
    k_ref[...] = k
    v_ref[...] = v
    tm = k.shape[0]
    kt_ref[0] = k.T.reshape(A_HEADS, HEAD_DIM, tm)
    vt_ref[0] = v.T.reshape(A_HEADS, HEAD_DIM, tm)
    sg_ref[:, :D_MODEL] = jax.nn.sigmoid(mm(C_GATE, C_GATE + D_MODEL))
    sg_ref[:, D_MODEL:] = jax.nn.sigmoid(mm(C_GATE + D_MODEL, N_IN_COLS))

    @pl.when(lax.rem(pl.program_id(0), per) == 0)
    def _():
        carry[...] = jnp.zeros(carry.shape, carry.dtype)

    rc = mm(C_RWKV, C_GATE)
    row = lax.broadcasted_iota(jnp.int32, (tm, 1), 0)
    prev = jnp.where(row == 0, carry[...], pltpu.roll(rc, 1, axis=0))
    for o, val in zip(prep_refs, _rwkv_prep_math(rc, prev, *params)):
        o[0] = val
    last = rc[tm - 1:tm, :]
    carry[...] = last
    shift_ref[0] = last


def _inproj_prompt(x, ln1, w, prep_params, *, tm, seq):
    n = x.shape[0]
    per = seq // tm
    row = lambda c: pl.BlockSpec((tm, c), lambda i: (i, 0))
    full = lambda a: pl.BlockSpec(a.shape, lambda i: (0,) * a.ndim)
    t_spec = pl.BlockSpec((1, A_HEADS, HEAD_DIM, tm), lambda i: (i // per, 0, 0, i % per))
    t_shape = jax.ShapeDtypeStruct((n // seq, A_HEADS, HEAD_DIM, seq), F32)
    cols = (A_WIDTH, A_WIDTH, A_WIDTH, 2 * D_MODEL)
    out_specs = ([row(c) for c in cols] + [t_spec, t_spec, pl.BlockSpec((1, 1, N_RWKV_COLS), lambda i: (i // per, 0, 0))]
                 + [pl.BlockSpec((1, tm, R_WIDTH), lambda i: (i // per, i % per, 0))] * N_PREP_OUTS)
    out_shape = ([jax.ShapeDtypeStruct((n, c), F32) for c in cols]
                 + [t_shape, t_shape, jax.ShapeDtypeStruct((n // seq, 1, N_RWKV_COLS), F32)]
                 + [jax.ShapeDtypeStruct((n // seq, seq, R_WIDTH), F32)] * N_PREP_OUTS)
    return pl.pallas_call(
        functools.partial(_inproj_prompt_kernel, per=per),
        grid=(n // tm,),
        in_specs=[row(D_MODEL), full(ln1), full(w)] + [full(a) for a in prep_params],
        out_specs=out_specs,
        out_shape=out_shape,
        scratch_shapes=[pltpu.VMEM((1, N_RWKV_COLS), F32)],
        compiler_params=_cparams("arbitrary"),
        name="inproj_prompt",
    )(x, ln1, w, *prep_params)


def _t5_bucket(dist):
    max_exact = NUM_BUCKETS // 2
    d = jnp.maximum(dist, 1).astype(F32)
    large = max_exact + (jnp.log(d / max_exact) / math.log(MAX_DISTANCE / max_exact)
                         * (NUM_BUCKETS - max_exact)).astype(jnp.int32)
    large = jnp.minimum(large, NUM_BUCKETS - 1)
    return jnp.where(dist < max_exact, dist, large)


def _bias_lookup(rel_bias, dist):
    onehot = (_t5_bucket(dist)[..., None] == jnp.arange(NUM_BUCKETS)).astype(F32)
    return jnp.moveaxis(jnp.dot(onehot, rel_bias.astype(F32), precision=lax.Precision.HIGHEST), -1, 0)


def _prompt_bias(rel_bias):
    i = jnp.arange(BLOCK)[:, None]
    j = jnp.arange(2 * BLOCK)[None, :]
    delta = i + BLOCK - j
    out = []
    for window, dil in PATTERNS:
        n = window // dil
        band = (delta >= 0) & (delta <= n)
        out.append(jnp.where(band[None], _bias_lookup(rel_bias, jnp.clip(delta, 0, n) * dil), NEG))
    return jnp.stack(out, 0)


def _sample_bias(rel_bias, wb):
    dist = wb - jnp.arange(wb)
    out = []
    for window, dil in PATTERNS:
        member = (dist % dil == 0) & (dist <= window)
        out.append(jnp.where(member[None], _bias_lookup(rel_bias, dist), NEG))
    b0 = _bias_lookup(rel_bias, jnp.zeros((1,), jnp.int32))
    return jnp.stack(out, 0), b0


def _attn_prompt_kernel(q_ref, k_ref, v_ref, bias_ref, o_ref, os_ref, ls_ref, *, seq):
    nt = (((1,), (1,)), ((), ()))
    npat = len(PATTERNS)
    w2 = 2 * HEAD_DIM
    lo = lax.broadcasted_iota(jnp.int32, (BLOCK, w2), 1) < HEAD_DIM
    key = lax.broadcasted_iota(jnp.int32, (2 * BLOCK, 2 * BLOCK), 1)

    for g, (window, dil) in enumerate(PATTERNS):
        nb = seq // (BLOCK * dil)
        sh = dil.bit_length() - 1

        def body(it, carry, g=g, dil=dil, nb=nb, sh=sh):
            r = it & (dil - 1)
            c = it >> sh
            start = c * (BLOCK * dil) + r
            rows = pl.ds(start, BLOCK, stride=dil) if dil > 1 else pl.ds(start, BLOCK)
            qb = q_ref[0, rows, :]
            kb = k_ref[0, rows, :]
            vb = v_ref[0, rows, :]
            if nb > 1:
                pstart = jnp.maximum(c - 1, 0) * (BLOCK * dil) + r
                prow = pl.ds(pstart, BLOCK, stride=dil) if dil > 1 else pl.ds(pstart, BLOCK)
                kp = k_ref[0, prow, :]
                vp = v_ref[0, prow, :]
            q2 = jnp.concatenate([jnp.where(lo, qb, 0.0), jnp.where(lo, 0.0, qb)], axis=0).astype(BF16)
            if nb > 1:
                k2 = jnp.concatenate([kp, kb], axis=0)
                v2 = jnp.concatenate([vp, vb], axis=0)
                bias = bias_ref[g, 0]
                valid = (bias > 0.5 * NEG) & ((key >= BLOCK) | (c > 0))
            else:
                k2, v2 = kb, vb
                bias = bias_ref[g, 0, :, BLOCK:]
                valid = bias > 0.5 * NEG
            s = lax.dot_general(q2, k2.astype(BF16), nt, preferred_element_type=F32)
            s = jnp.where(valid, s * SCALE + bias, NEG)
            m = jnp.max(s, axis=-1, keepdims=True)
            p = jnp.exp(s - m)
            l = jnp.sum(p, axis=-1, keepdims=True)
            o2 = jnp.dot((p * (1.0 / l)).astype(BF16), v2.astype(BF16), preferred_element_type=F32)
            lse = jnp.broadcast_to(m + jnp.log(l), (2 * BLOCK, w2))
            os_ref[g, rows, :] = jnp.where(lo, o2[:BLOCK], o2[BLOCK:])
            ls_ref[g, rows, :] = jnp.where(lo, lse[:BLOCK], lse[BLOCK:])
            return carry

        lax.fori_loop(0, dil * nb, body, 0, unroll=8)

    chunk = 256

    def merge(ch, carry):
        rows = pl.ds(pl.multiple_of(ch * chunk, chunk), chunk)
        lse = [ls_ref[g, rows, :] for g in range(npat)]
        top = functools.reduce(jnp.maximum, lse)
        e = [jnp.exp(x - top) for x in lse]
        inv = 1.0 / functools.reduce(lambda a, b_: a + b_, e)
        acc = jnp.zeros((chunk, 2 * HEAD_DIM), F32)
        for g in range(npat):
            acc = acc + (e[g] * inv) * os_ref[g, rows, :]
        o_ref[0, rows, :] = acc
        return carry

    lax.fori_loop(0, seq // chunk, merge, 0)


def _attn_prompt(q, k, v, bias):
    b, s, _ = q.shape
    w2 = 2 * HEAD_DIM
    qspec = pl.BlockSpec((1, s, w2), lambda i, p: (i, 0, p))
    return pl.pallas_call(
        functools.partial(_attn_prompt_kernel, seq=s),
        grid=(b, A_HEADS // 2),
        in_specs=[qspec, qspec, qspec,
                  pl.BlockSpec((len(PATTERNS), 1, 2 * BLOCK, 2 * BLOCK), lambda i, p: (0, p, 0, 0))],
        out_specs=qspec,
        out_shape=jax.ShapeDtypeStruct((b, s, A_WIDTH), F32),
        scratch_shapes=[pltpu.VMEM((len(PATTERNS), s, w2), F32)] * 2,
        compiler_params=_cparams("parallel", "parallel"),
        name="attn_prompt",
    )(q, k, v, bias.reshape(len(PATTERNS), A_HEADS // 2, 2 * BLOCK, 2 * BLOCK))


def _attn_sample_kernel(q_ref, kn_ref, vn_ref, kt_ref, vt_ref, bias_ref, b0_ref, o_ref):
    npat = len(PATTERNS)
    nt = (((1,), (1,)), ((), ()))
    q = q_ref[0]
    v_new = _bf(vn_ref[0])
    s0 = jnp.sum(_bf(q) * _bf(kn_ref[0]), axis=-1, keepdims=True) * SCALE + b0_ref[...]
    s = jnp.concatenate([_dot(q[h:h + 1], kt_ref[0, h]) for h in range(A_HEADS)], axis=0) * SCALE
    ps, p0s, lses = [], [], []
    for g in range(npat):
        bias = bias_ref[g]
        sg = jnp.where(bias > 0.5 * NEG, s + bias, NEG)
        m = jnp.maximum(jnp.max(sg, axis=-1, keepdims=True), s0)
        l = jnp.sum(jnp.exp(sg - m), axis=-1, keepdims=True) + jnp.exp(s0 - m)
        lse = m + jnp.log(l)
        ps.append(jnp.exp(sg - lse))
        p0s.append(_bf(jnp.exp(s0 - lse)))
        lses.append(lse)
    top = functools.reduce(jnp.maximum, lses)
    e = [jnp.exp(x - top) for x in lses]
    inv = 1.0 / functools.reduce(lambda a, b_: a + b_, e)
    w = [_bf(x * inv) for x in e]
    rows = []
    for h in range(A_HEADS):
        p_h = jnp.concatenate([p[h:h + 1] for p in ps], axis=0)
        o_h = lax.dot_general(p_h.astype(BF16), vt_ref[0, h].astype(BF16), nt, preferred_element_type=F32)
        acc = jnp.zeros((1, HEAD_DIM), F32)
        for g in range(npat):
            acc = acc + w[g][h:h + 1] * _bf(o_h[g:g + 1] + p0s[g][h:h + 1] * v_new[h:h + 1])
        rows.append(acc)
    o_ref[0] = jnp.concatenate(rows, axis=0)


def _attn_sample(q, k_new, v_new, cache_k, cache_v, bias, b0):
    db, wb = cache_k.shape[:2]
    heads = lambda a: a.reshape(db, A_HEADS, HEAD_DIM)
    vec = pl.BlockSpec((1, A_HEADS, HEAD_DIM), lambda i: (i, 0, 0))
    full = lambda a: pl.BlockSpec(a.shape, lambda i: (0,) * a.ndim)
    rows_minor = lambda a: a.transpose(0, 2, 3, 1)
    cache = pl.BlockSpec((1, A_HEADS, HEAD_DIM, wb), lambda i: (i, 0, 0, 0))
    args = [heads(q), heads(k_new), heads(v_new), rows_minor(cache_k), rows_minor(cache_v), bias, b0]
    return pl.pallas_call(
        _attn_sample_kernel,
        grid=(db,),
        in_specs=[vec, vec, vec, cache, cache, full(bias), full(b0)],
        out_specs=vec,
        out_shape=jax.ShapeDtypeStruct((db, A_HEADS, HEAD_DIM), F32),
        compiler_params=_cparams("parallel"),
        name="attn_sample",
    )(*args).reshape(db, A_WIDTH)


def _rwkv_prep_math(rc, prev, mu, w0, wd, a0, wa, wg, k_k, k_a, r_k, seg):
    xr = rc + (prev - rc) * mu
    o1, o2, o3 = R_WIDTH, 2 * R_WIDTH, 3 * R_WIDTH
    r = xr[:, :o1]
    kr = xr[:, o1:o2]
    vr = xr[:, o2:o3]
    xw = xr[:, o3:o3 + DECAY_LORA]
    xa = xr[:, o3 + DECAY_LORA:o3 + DECAY_LORA + AAA_LORA]
    xg = xr[:, o3 + DECAY_LORA + AAA_LORA:]
    z = -(w0 + _dot(jnp.tanh(xw), wd))
    softplus = jnp.maximum(z, 0.0) + jnp.log(1.0 + jnp.exp(-jnp.abs(z)))
    decay = jnp.exp(-jnp.exp(-softplus - 0.5))
    a = jax.nn.sigmoid(a0 + _dot(xa, wa))
    g = _dot(jax.nn.sigmoid(xg), wg)
    kk = kr * k_k
    k2 = kr * (1.0 + (a - 1.0) * k_a)
    kk = kk / jnp.maximum(jnp.sqrt(_segdot(kk * kk, seg)), 1e-12)
    bonus = _segdot(r * k2 * r_k, seg) * vr
    return _bf(r), decay, k2, vr, _bf(kk), kk * a, bonus, g


def _rwkv_prep_sample_kernel(rc_ref, prev_ref, *refs):
    p = [x[...] for x in refs[:N_PREP_PARAMS]]
    outs = refs[N_PREP_PARAMS:]
    for o, val in zip(outs, _rwkv_prep_math(rc_ref[...], prev_ref[...], *p)):
        o[...] = val


def _head_seg():
    head = jnp.arange(R_WIDTH) // R_HEAD
    return (head[:, None] == head[None, :]).astype(BF16)


def _rwkv_prep_sample(rc, prev, params):
    n = rc.shape[0]
    full = lambda a: pl.BlockSpec(a.shape, lambda i: (0,) * a.ndim)
    return pl.pallas_call(
        _rwkv_prep_sample_kernel,
        grid=(1,),
        in_specs=[full(rc), full(prev)] + [full(a) for a in params],
        out_specs=[pl.BlockSpec((n, R_WIDTH), lambda i: (0, 0))] * 8,
        out_shape=[jax.ShapeDtypeStruct((n, R_WIDTH), F32)] * 8,
        compiler_params=_cparams("arbitrary"),
        name="rwkv_prep_sample",
    )(rc, prev, *params)


def _seg_lane_sum(x, lo_mask):
    lo = jnp.sum(jnp.where(lo_mask, x, 0.0), axis=-1, keepdims=True)
    hi = jnp.sum(jnp.where(lo_mask, 0.0, x), axis=-1, keepdims=True)
    return jnp.where(lo_mask, lo, hi)


def _rwkv_scan_kernel(r_ref, w_ref, k_ref, v_ref, kk_ref, ka_ref, s0_ref, y_ref, sT_ref, st_ref, *, bb, tc):
    pairs = R_HEADS // 2
    w2 = 2 * R_HEAD

    @pl.when(pl.program_id(1) == 0)
    def _():
        for b in range(bb):
            for p in range(pairs):
                st_ref[b, p] = jnp.concatenate([s0_ref[b, 2 * p], s0_ref[b, 2 * p + 1]], axis=1)

    lane = lax.broadcasted_iota(jnp.int32, (R_HEAD, w2), 1)
    sub = lax.broadcasted_iota(jnp.int32, (R_HEAD, w2), 0)
    lo_mask = lane < R_HEAD
    eye2 = (lane & (R_HEAD - 1)) == sub

    grp = min(8, tc)

    def group(tg, carry):
        rows = pl.ds(pl.multiple_of(tg * grp, grp), grp)
        for b in range(bb):
            for p in range(pairs):
                cols = slice(p * w2, (p + 1) * w2)
                r8, w8, k8, v8, kk8, ka8 = (ref[b, rows, cols] for ref in (r_ref, w_ref, k_ref, v_ref, kk_ref, ka_ref))
                s = st_ref[b, p]
                sb = _bf(s)
                ys = []
                for j in range(grp):
                    row = lambda a: a[j:j + 1, :]
                    sa = -_seg_lane_sum(sb * row(kk8), lo_mask)
                    vcol = _seg_lane_sum(jnp.where(eye2, jnp.broadcast_to(row(v8), (R_HEAD, w2)), 0.0), lo_mask)
                    s = s * row(w8) + sa * row(ka8) + vcol * row(k8)
                    sb = _bf(s)
                    yfull = _seg_lane_sum(sb * row(r8), lo_mask)
                    ys.append(jnp.sum(jnp.where(eye2, yfull, 0.0), axis=0, keepdims=True))
                st_ref[b, p] = s
                y_ref[b, rows, cols] = jnp.concatenate(ys, axis=0) if grp > 1 else ys[0]
        return carry

    lax.fori_loop(0, tc // grp, group, 0)

    @pl.when(pl.program_id(1) == pl.num_programs(1) - 1)
    def _():
        for b in range(bb):
            for p in range(pairs):
                s = st_ref[b, p]
                sT_ref[b, 2 * p] = s[:, :R_HEAD]
                sT_ref[b, 2 * p + 1] = s[:, R_HEAD:]


def _rwkv_scan(r, w, k, v, kk, ka, s0, *, bb, tc):
    b, t, _ = r.shape
    seq = pl.BlockSpec((bb, tc, R_WIDTH), lambda i, j: (i, j, 0))
    state = pl.BlockSpec((bb, R_HEADS, R_HEAD, R_HEAD), lambda i, j: (i, 0, 0, 0))
    return pl.pallas_call(
        functools.partial(_rwkv_scan_kernel, bb=bb, tc=tc),
        grid=(b // bb, t // tc),
        in_specs=[seq] * 6 + [state],
        out_specs=[seq, state],
        out_shape=[jax.ShapeDtypeStruct((b, t, R_WIDTH), F32),
                   jax.ShapeDtypeStruct((b, R_HEADS, R_HEAD, R_HEAD), F32)],
        scratch_shapes=[pltpu.VMEM((bb, R_HEADS // 2, R_HEAD, 2 * R_HEAD), F32)],
        compiler_params=_cparams("parallel", "arbitrary"),
        name="rwkv_scan",
    )(r, w, k, v, kk, ka, s0)


CHAINS = LANES // 2
K2 = R_HEAD // 2


def _chain_rows(x):
    b, t, _ = x.shape
    x = x.reshape(b, t, R_HEADS, 2, K2).transpose(1, 4, 3, 0, 2)
    return x.reshape(t, K2, LANES)


def _lane_scan_kernel(kk_ref, w_ref, ka_ref, kx_ref, r_ref, v_ref, s0_ref, y_ref, sT_ref, s_ref, *, tc):
    lo = lax.broadcasted_iota(jnp.int32, (K2, LANES), 1) < CHAINS

    @pl.when(pl.program_id(0) == 0)
    def _():
        for k2 in range(K2):
            s_ref[k2] = s0_ref[k2]

    def both_halves(x):
        return x + pltpu.roll(x, CHAINS, axis=1)

    nhalf = 2
    vh = R_HEAD // nhalf

    def row(ref, t, k2):
        return jnp.broadcast_to(ref[t, k2:k2 + 1, :], (vh, LANES))

    acc0 = []
    for hf in range(nhalf):
        vs = slice(hf * vh, (hf + 1) * vh)
        a = jnp.zeros((vh, LANES), F32)
        for k2 in range(K2):
            a = a + _bf(s_ref[k2, vs, :]) * row(kk_ref, 0, k2)
        acc0.append(a)

    def step(t, acc):
        tn = jnp.minimum(t + 1, tc - 1)
        nxt, ys = [], []
        v_t = v_ref[t]
        v_sw = pltpu.roll(v_t, CHAINS, axis=1)
        for hf in range(nhalf):
            vs = slice(hf * vh, (hf + 1) * vh)
            vv = jnp.where(lo, v_t, v_sw) if hf == 0 else jnp.where(lo, v_sw, v_t)
            sa = -both_halves(acc[hf])
            yacc = jnp.zeros((vh, LANES), F32)
            nacc = jnp.zeros((vh, LANES), F32)
            for k2 in range(K2):
                s = s_ref[k2, vs, :] * row(w_ref, t, k2) + sa * row(ka_ref, t, k2) + vv * row(kx_ref, t, k2)
                s_ref[k2, vs, :] = s
                sb = _bf(s)
                yacc = yacc + sb * row(r_ref, t, k2)
                nacc = nacc + sb * row(kk_ref, tn, k2)
            ys.append(both_halves(yacc))
            nxt.append(nacc)
        y_ref[t] = jnp.where(lo, ys[0], ys[1])
        return tuple(nxt)

    lax.fori_loop(0, tc, step, tuple(acc0))

    @pl.when(pl.program_id(0) == pl.num_programs(0) - 1)
    def _():
        for k2 in range(K2):
            sT_ref[k2] = s_ref[k2]


def _lane_scan(r, w, k, v, kk, ka, s0, *, tc):
    b, t, _ = r.shape
    assert b * R_HEADS == CHAINS
    ops = [_chain_rows(x) for x in (kk, w, ka, k, r, v)]
    s0c = s0.reshape(b, R_HEADS, 2, K2, 2, K2).transpose(5, 2, 3, 4, 0, 1).reshape(K2, R_HEAD, LANES)
    rows = pl.BlockSpec((tc, K2, LANES), lambda i: (i, 0, 0))
    state = pl.BlockSpec((K2, R_HEAD, LANES), lambda i: (0, 0, 0))
    y, st = pl.pallas_call(
        functools.partial(_lane_scan_kernel, tc=tc),
        grid=(t // tc,),
        in_specs=[rows] * 6 + [state],
        out_specs=[rows, state],
        out_shape=[jax.ShapeDtypeStruct((t, K2, LANES), F32), jax.ShapeDtypeStruct((K2, R_HEAD, LANES), F32)],
        scratch_shapes=[pltpu.VMEM((K2, R_HEAD, LANES), F32)],
        compiler_params=_cparams("arbitrary"),
        name="rwkv_lane_scan",
    )(*ops, s0c)
    y = y.reshape(t, K2, 2, b, R_HEADS).transpose(3, 0, 4, 2, 1).reshape(b, t, R_WIDTH)
    st = st.reshape(K2, 2, K2, 2, b, R_HEADS).transpose(4, 5, 1, 2, 3, 0).reshape(b, R_HEADS, R_HEAD, R_HEAD)
    return y, st


def _post_kernel(x_ref, y_ref, bonus_ref, g_ref, oa_ref, sg_ref, gnw_ref, gnb_ref, seg_ref, wba_ref, wbb_ref,
                 wout_ref, ln2_ref, wr_ref, br_ref, *rest, aliased, n_main):
    outs = rest[aliased:]
    x1_ref, h2_ref, ti_ref, tg_ref = outs

    @pl.when(pl.program_id(0) >= n_main)
    def _():
        for o in outs:
            o[...] = jnp.zeros(o.shape, o.dtype)

    @pl.when(pl.program_id(0) < n_main)
    def _():
        _post_body(x_ref, y_ref, bonus_ref, g_ref, oa_ref, sg_ref, gnw_ref, gnb_ref, seg_ref, wba_ref, wbb_ref,
                   wout_ref, ln2_ref, wr_ref, br_ref, x1_ref, h2_ref, ti_ref, tg_ref)


def _post_body(x_ref, y_ref, bonus_ref, g_ref, oa_ref, sg_ref, gnw_ref, gnb_ref, seg_ref, wba_ref, wbb_ref,
               wout_ref, ln2_ref, wr_ref, br_ref, x1_ref, h2_ref, ti_ref, tg_ref):
    y = y_ref[...]
    seg = seg_ref[...]
    mu = _segdot(y, seg) * (1.0 / R_HEAD)
    yc = y - mu
    var = _segdot(yc * yc, seg) * (1.0 / R_HEAD)
    yn = yc * lax.rsqrt(var + GN_EPS) * gnw_ref[...] + gnb_ref[...]
    o_b = (yn + bonus_ref[...]) * g_ref[...]
    mixed = (sg_ref[:, :D_MODEL] * _dot(oa_ref[...], wba_ref[...])
             + sg_ref[:, D_MODEL:] * _dot(o_b, wbb_ref[...]))
    x1 = x_ref[...] + _dot(mixed, wout_ref[...])
    x1_ref[...] = x1
    h2 = _rms(x1, ln2_ref[...])
    _store_token_tiles(h2_ref, h2)
    logits = _dot(h2, wr_ref[...]) + br_ref[...]
    lane = lax.broadcasted_iota(jnp.int32, logits.shape, 1).astype(F32)
    work = logits
    vals, idxs = [], []
    for _ in range(TOP_K):
        m = jnp.max(work, axis=-1, keepdims=True)
        idx = jnp.min(jnp.where(work == m, lane, float(LANES)), axis=-1, keepdims=True)
        vals.append(m)
        idxs.append(idx)
        work = jnp.where(lane == idx, -jnp.inf, work)
    es = [jnp.exp(v - vals[0]) for v in vals]
    tot = es[0] + es[1] + es[2] + es[3]
    ti = jnp.zeros(logits.shape, F32)
    tg = jnp.zeros(logits.shape, F32)
    for kslot in range(TOP_K):
        ti = jnp.where(lane == float(kslot), idxs[kslot], ti)
        tg = jnp.where(lane == float(kslot), es[kslot] / tot, tg)
    ti_ref[...] = ti.astype(jnp.int32)
    tg_ref[...] = tg


def _post(x, y, bonus, g, oa, sg, consts, *, tm, n_total, row0=0, into=None):
    n = x.shape[0]
    blk0 = row0 // tm
    n_main = n // tm
    steps = n_total // tm if into is None else n_main
    row = lambda c: pl.BlockSpec((tm, c), lambda i: (jnp.minimum(i, n_main - 1), 0))
    orow = lambda c: pl.BlockSpec((tm, c), lambda i: (i + blk0, 0))
    otile = pl.BlockSpec((tm * TILE_ROWS, LANES), lambda i: (i + blk0, 0))
    full = lambda a: pl.BlockSpec(a.shape, lambda i: (0,) * a.ndim)
    ins = [x, y, bonus, g, oa, sg, *consts]
    in_specs = [row(D_MODEL), row(R_WIDTH), row(R_WIDTH), row(R_WIDTH), row(A_WIDTH), row(2 * D_MODEL)]
    in_specs += [full(a) for a in consts]
    aliases = {}
    if into is not None:
        aliases = {len(ins) + i: i for i in range(len(into))}
        in_specs += [pl.BlockSpec(memory_space=pl.ANY)] * len(into)
        ins += list(into)
    return pl.pallas_call(
        functools.partial(_post_kernel, aliased=len(aliases), n_main=n_main),
        grid=(steps,),
        in_specs=in_specs,
        out_specs=[orow(D_MODEL), otile, orow(LANES), orow(LANES)],
        out_shape=[jax.ShapeDtypeStruct((n_total, D_MODEL), F32),
                   jax.ShapeDtypeStruct((n_total * TILE_ROWS, LANES), F32),
                   jax.ShapeDtypeStruct((n_total, LANES), jnp.int32),
                   jax.ShapeDtypeStruct((n_total, LANES), F32)],
        input_output_aliases=aliases,
        compiler_params=_cparams("parallel"),
        name="post",
    )(*ins)


def _start_rows(rows_ref, hbm, buf, sem, *, rows, to_hbm):
    for r in range(rows):
        at = pl.ds(pl.multiple_of(rows_ref[0, 0, r], TILE_ROWS), TILE_ROWS)
        here = pl.ds(r * TILE_ROWS, TILE_ROWS)
        prio = r % 2
        if to_hbm:
            pltpu.make_async_copy(buf.at[here], hbm.at[at], sem).start(priority=prio)
        else:
            pltpu.make_async_copy(hbm.at[at], buf.at[here], sem).start(priority=prio)


def _wait_rows(hbm, buf, sem, *, rows, to_hbm):
    whole = hbm.at[pl.ds(0, rows * TILE_ROWS)]
    (pltpu.make_async_copy(buf, whole, sem) if to_hbm else pltpu.make_async_copy(whole, buf, sem)).wait()


def _expert_mlp(xb, wgu_bf, wd_bf, bgu_ref, bd_ref, rows):
    acc = jnp.broadcast_to(bd_ref[0], (rows, D_MODEL))
    cw = 512
    for c in range(D_MODEL // cw):
        gs = slice(c * cw, (c + 1) * cw)
        us = slice(D_MODEL + c * cw, D_MODEL + (c + 1) * cw)
        gt = jnp.dot(xb, wgu_bf[:, gs], preferred_element_type=F32) + bgu_ref[0, :, gs]
        up = jnp.dot(xb, wgu_bf[:, us], preferred_element_type=F32) + bgu_ref[0, :, us]
        gt = jnp.minimum(gt, SWIGLU_LIMIT)
        up = jnp.clip(up, -SWIGLU_LIMIT, SWIGLU_LIMIT)
        act = (up + 1.0) * (gt * jax.nn.sigmoid(gt * SWIGLU_ALPHA))
        acc = acc + jnp.dot(act.astype(BF16), wd_bf[gs, :], preferred_element_type=F32)
    return acc


def _moe_kernel(be_ref, src_cur, src_nxt, dst_prv, dst_cur, h_hbm, wgu_ref, bgu_ref, wd_ref, bd_ref, ys_hbm,
                x0, x1, y0, y1, wgu_bf, wd_bf, gsem, ssem, *, rows, plane_rows, n_tok):
    i = pl.program_id(0)
    last = pl.num_programs(0) - 1

    changed = jnp.logical_or(i == 0, be_ref[i] != be_ref[jnp.maximum(i - 1, 0)])

    @pl.when(changed)
    def _():
        step = 128

        def cast(j, c):
            rs = pl.ds(pl.multiple_of(j * step, step), step)
            wgu_bf[rs, :] = wgu_ref[0, rs, :].astype(BF16)
            wd_bf[rs, :] = wd_ref[0, rs, :].astype(BF16)
            return c

        lax.fori_loop(0, D_MODEL // step, cast, 0)

    def run(par):
        xc, xn = (x0, x1) if par == 0 else (x1, x0)
        yc, yp = (y0, y1) if par == 0 else (y1, y0)
        gather = functools.partial(_start_rows, hbm=h_hbm, rows=rows, to_hbm=False)
        scatter = functools.partial(_start_rows, hbm=ys_hbm, rows=rows, to_hbm=True)
        gathered = functools.partial(_wait_rows, h_hbm, rows=rows, to_hbm=False)
        scattered = functools.partial(_wait_rows, ys_hbm, rows=rows, to_hbm=True)

        @pl.when(i == 0)
        def _():
            yp[...] = jnp.zeros(yp.shape, yp.dtype)
            spare = plane_rows - n_tok
            fills = [pltpu.make_async_copy(yp.at[pl.ds(0, spare * TILE_ROWS)],
                                           ys_hbm.at[pl.ds((k * plane_rows + n_tok) * TILE_ROWS, spare * TILE_ROWS)],
                                           ssem.at[1 - par])
                     for k in range(TOP_K)]
            for d in fills:
                d.start()
            for d in fills:
                d.wait()
            gather(src_cur, buf=xc, sem=gsem.at[par])

        gathered(xc, gsem.at[par])
        gather(src_nxt, buf=xn, sem=gsem.at[1 - par])

        @pl.when(i >= 1)
        def _():
            scattered(yc, ssem.at[par])

        used = i < be_ref[last + 1]

        @pl.when(used)
        def _():
            scatter(dst_prv, buf=yp, sem=ssem.at[1 - par])
            xb = _load_token_tiles(xc, rows).astype(BF16)
            _store_token_tiles(yc, _expert_mlp(xb, wgu_bf, wd_bf, bgu_ref, bd_ref, rows))

        @pl.when(jnp.logical_not(used))
        def _():
            scatter(dst_prv, buf=yp, sem=ssem.at[1 - par])

        @pl.when(i == last)
        def _():
            scatter(dst_cur, buf=yc, sem=ssem.at[par])
            gathered(xn, gsem.at[1 - par])
            scattered(yp, ssem.at[1 - par])
            scattered(yc, ssem.at[par])

    for par in range(2):
        pl.when(lax.rem(i, 2) == par)(functools.partial(run, par))


def _moe_experts(block_e, ids, h2, w_gate_up, b_gate_up, w_down, b_down, *, rows, plane_rows, n_tok):
    n_blocks = ids.shape[0] - 2
    ids = lax.optimization_barrier(ids)
    tok = lax.shift_right_logical(ids, 2)
    src = tok * TILE_ROWS
    dst = ((ids & (TOP_K - 1)) * plane_rows + tok) * TILE_ROWS
    idblk = lambda off: pl.BlockSpec((1, 1, rows), lambda i, be: (i + off, 0, 0), memory_space=pltpu.SMEM)
    grid_spec = pltpu.PrefetchScalarGridSpec(
        num_scalar_prefetch=1,
        grid=(n_blocks,),
        in_specs=[
            idblk(1), idblk(2), idblk(0), idblk(1),
            pl.BlockSpec(memory_space=pl.ANY),
            pl.BlockSpec((1, D_MODEL, 2 * D_MODEL), lambda i, be: (be[i], 0, 0)),
            pl.BlockSpec((1, 1, 2 * D_MODEL), lambda i, be: (be[i], 0, 0)),
            pl.BlockSpec((1, D_MODEL, D_MODEL), lambda i, be: (be[i], 0, 0)),
            pl.BlockSpec((1, 1, D_MODEL), lambda i, be: (be[i], 0, 0)),
        ],
        out_specs=pl.BlockSpec(memory_space=pl.ANY),
        scratch_shapes=[pltpu.VMEM((rows * TILE_ROWS, LANES), F32)] * 4 + [
            pltpu.VMEM((D_MODEL, 2 * D_MODEL), BF16),
            pltpu.VMEM((D_MODEL, D_MODEL), BF16),
            pltpu.SemaphoreType.DMA((2,)),
            pltpu.SemaphoreType.DMA((2,))],
    )
    return pl.pallas_call(
        functools.partial(_moe_kernel, rows=rows, plane_rows=plane_rows, n_tok=n_tok),
        grid_spec=grid_spec,
        out_shape=jax.ShapeDtypeStruct((TOP_K * plane_rows * TILE_ROWS, LANES), F32),
        compiler_params=_cparams("arbitrary"),
        name="moe_experts",
    )(block_e, src, src, dst, dst, h2, w_gate_up, b_gate_up[:, None], w_down, b_down[:, None])


def _combine_kernel(ys_ref, x1_ref, gate_ref, lnf_ref, y_ref):
    acc = x1_ref[...]
    tm = acc.shape[0]
    for kslot in range(TOP_K):
        acc = acc + gate_ref[:, kslot:kslot + 1] * _load_token_tiles(ys_ref, tm, kslot)
    y_ref[...] = _rms(acc, lnf_ref[...])


def _combine(ys, x1, gates, ln_f, *, tm, n, row0):
    blk0 = row0 // tm
    return pl.pallas_call(
        _combine_kernel,
        grid=(n // tm,),
        in_specs=[
            pl.BlockSpec((TOP_K, tm * TILE_ROWS, LANES), lambda i: (0, i + blk0, 0)),
            pl.BlockSpec((tm, D_MODEL), lambda i: (i + blk0, 0)),
            pl.BlockSpec((tm, LANES), lambda i: (i + blk0, 0)),
            pl.BlockSpec(ln_f.shape, lambda i: (0, 0)),
        ],
        out_specs=pl.BlockSpec((tm, D_MODEL), lambda i: (i, 0)),
        out_shape=jax.ShapeDtypeStruct((n, D_MODEL), F32),
        compiler_params=_cparams("parallel"),
        name="combine",
    )(ys.reshape(TOP_K, -1, LANES), x1, gates, ln_f)


def _route(topi, n_pad, rows):
    n_tok = topi.shape[0]
    e_flat = topi.reshape(-1)
    nk = e_flat.shape[0]
    n_blocks = -(-(nk + N_EXPERTS * (rows - 1)) // rows)
    spare = n_pad - n_tok
    assert 3 * rows <= TOP_K * spare
    experts = jnp.arange(N_EXPERTS, dtype=jnp.int32)
    order = jnp.argsort(e_flat).astype(jnp.int32)
    sizes = jnp.sum((e_flat[:, None] == experts[None, :]).astype(jnp.int32), axis=0)
    padded = (sizes + rows - 1) // rows * rows
    pends = jnp.cumsum(padded)
    slot = jnp.arange(n_blocks * rows, dtype=jnp.int32)
    past = (slot[:, None] >= pends[None, :]).astype(jnp.int32)
    e_slot = jnp.sum(past, axis=1)
    off = slot - jnp.sum(past * padded[None, :], axis=1)
    src = jnp.sum(past * sizes[None, :], axis=1) + off
    size_e = jnp.sum((e_slot[:, None] == experts[None, :]) * sizes[None, :], axis=1)

    def pad_ids(d):
        return (TOP_K * (n_tok + d % spare) + d // spare).astype(jnp.int32)

    ids = jnp.where(off < size_e, order[jnp.clip(src, 0, nk - 1)], pad_ids((slot // rows) % 2 * rows + slot % rows))
    end_ids = pad_ids(2 * rows + jnp.arange(rows, dtype=jnp.int32))
    ids = jnp.concatenate([end_ids, ids, end_ids]).reshape(n_blocks + 2, 1, rows)
    block_e = jnp.minimum(e_slot[::rows], N_EXPERTS - 1).astype(jnp.int32)
    n_used = (pends[-1] // rows).astype(jnp.int32)
    return jnp.concatenate([block_e, n_used[None]]), ids


def kernel(x_prompt, x_sample, cache_k, cache_v, state_wkv, state_shift, rel_bias, ln1, w_in, rwkv_mu, w0,
           w_decay_up, a0, w_a_up, w_g_up, k_k, k_a, r_k, gn_w, gn_b, w_branch_a, w_branch_b, w_out, ln2,
           w_router, b_router, w_gate_up, b_gate_up, w_down, b_down, ln_f):
    bp, seq, _ = x_prompt.shape
    db = x_sample.shape[0]
    n_p = bp * seq
    n_tot = n_p + db
    n_pad = -(-(n_tot + -(-3 * MOE_ROWS // TOP_K)) // TOKEN_TILE) * TOKEN_TILE
    l = 0

    row = lambda a: a.reshape(1, -1)
    seg = _head_seg()
    prep_params = (row(rwkv_mu[l]), row(w0[l]), w_decay_up[l], row(a0[l]), w_a_up[l], w_g_up[l],
                   row(k_k[l]), row(k_a[l]), row(r_k[l]), seg)
    wr_pad = jnp.pad(w_router[l], ((0, 0), (0, LANES - N_EXPERTS))).astype(BF16)
    br_pad = jnp.pad(b_router[l], (0, LANES - N_EXPERTS), constant_values=NEG).reshape(1, LANES)
    post_small = (row(gn_w[l]), row(gn_b[l]), seg)
    post_tail = (row(ln2[l]), wr_pad, br_pad)

    w_in_bf = w_in[l].astype(BF16)
    consts = (post_small + (w_branch_a[l].astype(BF16), w_branch_b[l].astype(BF16), w_out[l].astype(BF16))
              + post_tail)

    xp = x_prompt.reshape(n_p, D_MODEL)
    q_p, k_p, v_p, sg_p, kt_p, vt_p, shift_p, *prep_p = _inproj_prompt(xp, row(ln1[l]), w_in_bf, prep_params,
                                                                     tm=TOKEN_TILE, seq=seq)
    as3 = lambda a: a.reshape(bp, seq, -1)
    oa_p = _attn_prompt(as3(q_p), as3(k_p), as3(v_p), _prompt_bias(rel_bias))
    r_, w_, k2_, vr_, kk_, ka_, bonus_p, g_p = prep_p
    s0_p = jnp.zeros((bp, R_HEADS, R_HEAD, R_HEAD), F32)
    y_p, wkv_p = _lane_scan(r_, w_, k2_, vr_, kk_, ka_, s0_p, tc=SCAN_CHUNK)
    flat = lambda a: a.reshape(n_p, -1)
    bufs = _post(xp, flat(y_p), flat(bonus_p), flat(g_p), flat(oa_p), sg_p, consts,
                 tm=TOKEN_TILE, n_total=n_pad)

    xs = x_sample.reshape(db, D_MODEL)
    q_s, k_s, v_s, rc_s, sg_s = _inproj(xs, row(ln1[l]), w_in_bf, tm=db)
    bias_s, b0_s = _sample_bias(rel_bias, cache_k.shape[2])
    oa_s = _attn_sample(q_s, k_s, v_s, cache_k[l], cache_v[l], bias_s, b0_s)
    prep_s = _rwkv_prep_sample(rc_s, state_shift[l], prep_params)
    sr, sw, sk2, svr, skk, ska, bonus_s, g_s = [a[:, None] for a in prep_s]
    y_s, wkv_s = _rwkv_scan(sr, sw, sk2, svr, skk, ska, state_wkv[l], bb=SCAN_BATCH, tc=1)
    x1_all, h2_all, ti_all, tg_all = _post(xs, y_s[:, 0], bonus_s[:, 0], g_s[:, 0], oa_s, sg_s, consts,
                                           tm=db, n_total=n_pad, row0=n_p, into=bufs)

    block_e, ids = _route(ti_all[:n_tot, :TOP_K], n_pad, MOE_ROWS)
    ys = _moe_experts(block_e, ids, h2_all, w_gate_up[l], b_gate_up[l], w_down[l], b_down[l],
                      rows=MOE_ROWS, plane_rows=n_pad, n_tok=n_tot)
    lnf = row(ln_f)
    y_prompt = _combine(ys, x1_all, tg_all, lnf, tm=TOKEN_TILE, n=n_p, row0=0)
    y_sample = _combine(ys, x1_all, tg_all, lnf, tm=db, n=db, row0=n_p)

    heads = lambda a, b_: a.reshape(1, b_, -1, A_HEADS, HEAD_DIM)
    return (y_prompt.reshape(bp, seq, D_MODEL), y_sample.reshape(db, 1, D_MODEL),
            kt_p.transpose(0, 3, 1, 2)[None], vt_p.transpose(0, 3, 1, 2)[None], wkv_p[None], shift_p[:, 0][None],
            heads(k_s, db), heads(v_s, db), wkv_s[None], rc_s[None])
```

```python
import functools
import math

import jax
import jax.numpy as jnp
from jax import lax
from jax.experimental import pallas as pl
from jax.experimental.pallas import tpu as pltpu

F32 = jnp.float32
BF16 = jnp.bfloat16

D_MODEL = 1024
A_HEADS = 8
HEAD_DIM = 64
A_WIDTH = A_HEADS * HEAD_DIM
PATTERNS = ((128, 1), (512, 4), (2048, 16))
BLOCK = 128
NUM_BUCKETS = 32
MAX_DISTANCE = 2048
SCALE = HEAD_DIM ** -0.5
NEG = -1e30
R_HEADS = 8
R_HEAD = 64
R_WIDTH = R_HEADS * R_HEAD
DECAY_LORA = 64
AAA_LORA = 64
GATE_LORA = 128
GN_EPS = 64e-5
N_RWKV_COLS = 3 * R_WIDTH + DECAY_LORA + AAA_LORA + GATE_LORA
N_IN_COLS = 3 * A_WIDTH + N_RWKV_COLS + 2 * D_MODEL
N_EXPERTS = 32
TOP_K = 4
SWIGLU_LIMIT = 7.0
SWIGLU_ALPHA = 1.702
RMS_EPS = 1e-6

LANES = 128
SUBLANES = 8
VMEM_LIMIT = 56 * 1024 * 1024
TOKEN_TILE = 256
MOE_ROWS = 512
SCAN_BATCH = 2
SCAN_CHUNK = 64


def _cparams(*sem):
    return pltpu.CompilerParams(dimension_semantics=sem, vmem_limit_bytes=VMEM_LIMIT)


def _rms(x, g):
    return x * lax.rsqrt(jnp.mean(x * x, axis=-1, keepdims=True) + RMS_EPS) * g


def _dot(a, b):
    return jnp.dot(a.astype(BF16), b.astype(BF16), preferred_element_type=F32)


def _bf(x):
    return x.astype(BF16).astype(F32)


def _split3(x):
    hi = x.astype(BF16)
    r1 = x - hi.astype(F32)
    mid = r1.astype(BF16)
    lo = (r1 - mid.astype(F32)).astype(BF16)
    return hi, mid, lo


def _segdot(x, seg):
    hi, mid, lo = _split3(x)
    d = lambda p: jnp.dot(p, seg, preferred_element_type=F32)
    return d(hi) + d(mid) + d(lo)


TILE_ROWS = D_MODEL // LANES
assert TILE_ROWS == SUBLANES


def _load_token_tiles(ref, n, *lead):
    return jnp.concatenate([ref[(*lead, pl.ds(p, n, stride=TILE_ROWS), slice(None))] for p in range(TILE_ROWS)], axis=1)


def _store_token_tiles(ref, x):
    n = x.shape[0]
    for p in range(TILE_ROWS):
        ref[pl.ds(p, n, stride=TILE_ROWS), :] = x[:, p * LANES:(p + 1) * LANES]


def _project(x_ref, g_ref, w_ref):
    h = _rms(x_ref[...], g_ref[...]).astype(BF16)
    return lambda c0, c1: _dot(h, w_ref[:, c0:c1])


C_RWKV = 3 * A_WIDTH
C_GATE = C_RWKV + N_RWKV_COLS


def _inproj_kernel(x_ref, g_ref, w_ref, q_ref, k_ref, v_ref, rc_ref, sg_ref):
    mm = _project(x_ref, g_ref, w_ref)
    q_ref[...] = mm(0, A_WIDTH)
    k_ref[...] = mm(A_WIDTH, 2 * A_WIDTH)
    v_ref[...] = mm(2 * A_WIDTH, C_RWKV)
    rc_ref[...] = mm(C_RWKV, C_GATE)
    sg_ref[:, :D_MODEL] = jax.nn.sigmoid(mm(C_GATE, C_GATE + D_MODEL))
    sg_ref[:, D_MODEL:] = jax.nn.sigmoid(mm(C_GATE + D_MODEL, N_IN_COLS))


def _inproj(x, ln1, w, *, tm):
    n = x.shape[0]
    row = lambda c: pl.BlockSpec((tm, c), lambda i: (i, 0))
    full = lambda a: pl.BlockSpec(a.shape, lambda i: (0,) * a.ndim)
    out_cols = (A_WIDTH, A_WIDTH, A_WIDTH, N_RWKV_COLS, 2 * D_MODEL)
    return pl.pallas_call(
        _inproj_kernel,
        grid=(n // tm,),
        in_specs=[row(D_MODEL), full(ln1), full(w)],
        out_specs=[row(c) for c in out_cols],
        out_shape=[jax.ShapeDtypeStruct((n, c), F32) for c in out_cols],
        compiler_params=_cparams("parallel"),
        name="inproj",
    )(x, ln1, w)


N_PREP_PARAMS = 10
N_PREP_OUTS = 8


def _inproj_prompt_kernel(x_ref, g_ref, w_ref, *refs, per):
    params = [r[...] for r in refs[:N_PREP_PARAMS]]
    q_ref, k_ref, v_ref, sg_ref, kt_ref, vt_ref, shift_ref = refs[N_PREP_PARAMS:N_PREP_PARAMS + 7]
    prep_refs = refs[N_PREP_PARAMS + 7:-1]
    carry = refs[-1]
    mm = _project(x_ref, g_ref, w_ref)
    q_ref[...] = mm(0, A_WIDTH)
    k = mm(A_WIDTH, 2 * A_WIDTH)
    v = mm(2 * A_WIDTH, C_RWKV)
    k_ref[...] = k
    v_ref[...] = v
    tm = k.shape[0]
    kt_ref[0] = k.T.reshape(A_HEADS, HEAD_DIM, tm)
    vt_ref[0] = v.T.reshape(A_HEADS, HEAD_DIM, tm)
    sg_ref[:, :D_MODEL] = jax.nn.sigmoid(mm(C_GATE, C_GATE + D_MODEL))
    sg_ref[:, D_MODEL:] = jax.nn.sigmoid(mm(C_GATE + D_MODEL, N_IN_COLS))

    @pl.when(lax.rem(pl.program_id(0), per) == 0)
    def _():
        carry[...] = jnp.zeros(carry.shape, carry.dtype)

    rc = mm(C_RWKV, C_GATE)
    row = lax.broadcasted_iota(jnp.int32, (tm, 1), 0)
    prev = jnp.where(row == 0, carry[...], pltpu.roll(rc, 1, axis=0))
    for o, val in zip(prep_refs, _rwkv_prep_math(rc, prev, *params)):
        o[0] = val
    last = rc[tm - 1:tm, :]
    carry[...] = last
    shift_ref[0] = last


def _inproj_prompt(x, ln1, w, prep_params, *, tm, seq):
    n = x.shape[0]
    per = seq // tm
    row = lambda c: pl.BlockSpec((tm, c), lambda i: (i, 0))
    full = lambda a: pl.BlockSpec(a.shape, lambda i: (0,) * a.ndim)
    t_spec = pl.BlockSpec((1, A_HEADS, HEAD_DIM, tm), lambda i: (i // per, 0, 0, i % per))
    t_shape = jax.ShapeDtypeStruct((n // seq, A_HEADS, HEAD_DIM, seq), F32)
    cols = (A_WIDTH, A_WIDTH, A_WIDTH, 2 * D_MODEL)
    out_specs = ([row(c) for c in cols] + [t_spec, t_spec, pl.BlockSpec((1, 1, N_RWKV_COLS), lambda i: (i // per, 0, 0))]
                 + [pl.BlockSpec((1, tm, R_WIDTH), lambda i: (i // per, i % per, 0))] * N_PREP_OUTS)
    out_shape = ([jax.ShapeDtypeStruct((n, c), F32) for c in cols]
                 + [t_shape, t_shape, jax.ShapeDtypeStruct((n // seq, 1, N_RWKV_COLS), F32)]
                 + [jax.ShapeDtypeStruct((n // seq, seq, R_WIDTH), F32)] * N_PREP_OUTS)
    return pl.pallas_call(
        functools.partial(_inproj_prompt_kernel, per=per),
        grid=(n // tm,),
        in_specs=[row(D_MODEL), full(ln1), full(w)] + [full(a) for a in prep_params],
        out_specs=out_specs,
        out_shape=out_shape,
        scratch_shapes=[pltpu.VMEM((1, N_RWKV_COLS), F32)],
        compiler_params=_cparams("arbitrary"),
        name="inproj_prompt",
    )(x, ln1, w, *prep_params)


def _t5_bucket(dist):
    max_exact = NUM_BUCKETS // 2
    d = jnp.maximum(dist, 1).astype(F32)
    large = max_exact + (jnp.log(d / max_exact) / math.log(MAX_DISTANCE / max_exact)
                         * (NUM_BUCKETS - max_exact)).astype(jnp.int32)
    large = jnp.minimum(large, NUM_BUCKETS - 1)
    return jnp.where(dist < max_exact, dist, large)


def _bias_lookup(rel_bias, dist):
    onehot = (_t5_bucket(dist)[..., None] == jnp.arange(NUM_BUCKETS)).astype(F32)
    return jnp.moveaxis(jnp.dot(onehot, rel_bias.astype(F32), precision=lax.Precision.HIGHEST), -1, 0)


def _prompt_bias(rel_bias):
    i = jnp.arange(BLOCK)[:, None]
    j = jnp.arange(2 * BLOCK)[None, :]
    delta = i + BLOCK - j
    out = []
    for window, dil in PATTERNS:
        n = window // dil
        band = (delta >= 0) & (delta <= n)
        out.append(jnp.where(band[None], _bias_lookup(rel_bias, jnp.clip(delta, 0, n) * dil), NEG))
    return jnp.stack(out, 0)


def _sample_bias(rel_bias, wb):
    dist = wb - jnp.arange(wb)
    out = []
    for window, dil in PATTERNS:
        member = (dist % dil == 0) & (dist <= window)
        out.append(jnp.where(member[None], _bias_lookup(rel_bias, dist), NEG))
    b0 = _bias_lookup(rel_bias, jnp.zeros((1,), jnp.int32))
    return jnp.stack(out, 0), b0


def _attn_prompt_kernel(q_ref, k_ref, v_ref, bias_ref, o_ref, os_ref, ls_ref, *, seq):
    nt = (((1,), (1,)), ((), ()))
    npat = len(PATTERNS)
    w2 = 2 * HEAD_DIM
    lo = lax.broadcasted_iota(jnp.int32, (BLOCK, w2), 1) < HEAD_DIM
    key = lax.broadcasted_iota(jnp.int32, (2 * BLOCK, 2 * BLOCK), 1)

    for g, (window, dil) in enumerate(PATTERNS):
        nb = seq // (BLOCK * dil)
        sh = dil.bit_length() - 1

        def body(it, carry, g=g, dil=dil, nb=nb, sh=sh):
            r = it & (dil - 1)
            c = it >> sh
            start = c * (BLOCK * dil) + r
            rows = pl.ds(start, BLOCK, stride=dil) if dil > 1 else pl.ds(start, BLOCK)
            qb = q_ref[0, rows, :]
            kb = k_ref[0, rows, :]
            vb = v_ref[0, rows, :]
            if nb > 1:
                pstart = jnp.maximum(c - 1, 0) * (BLOCK * dil) + r
                prow = pl.ds(pstart, BLOCK, stride=dil) if dil > 1 else pl.ds(pstart, BLOCK)
                kp = k_ref[0, prow, :]
                vp = v_ref[0, prow, :]
            q2 = jnp.concatenate([jnp.where(lo, qb, 0.0), jnp.where(lo, 0.0, qb)], axis=0).astype(BF16)
            if nb > 1:
                k2 = jnp.concatenate([kp, kb], axis=0)
                v2 = jnp.concatenate([vp, vb], axis=0)
                bias = bias_ref[g, 0]
                valid = (bias > 0.5 * NEG) & ((key >= BLOCK) | (c > 0))
            else:
                k2, v2 = kb, vb
                bias = bias_ref[g, 0, :, BLOCK:]
                valid = bias > 0.5 * NEG
            s = lax.dot_general(q2, k2.astype(BF16), nt, preferred_element_type=F32)
            s = jnp.where(valid, s * SCALE + bias, NEG)
            m = jnp.max(s, axis=-1, keepdims=True)
            p = jnp.exp(s - m)
            l = jnp.sum(p, axis=-1, keepdims=True)
            o2 = jnp.dot((p * (1.0 / l)).astype(BF16), v2.astype(BF16), preferred_element_type=F32)
            lse = jnp.broadcast_to(m + jnp.log(l), (2 * BLOCK, w2))
            os_ref[g, rows, :] = jnp.where(lo, o2[:BLOCK], o2[BLOCK:])
            ls_ref[g, rows, :] = jnp.where(lo, lse[:BLOCK], lse[BLOCK:])
            return carry

        lax.fori_loop(0, dil * nb, body, 0, unroll=8)

    chunk = 256

    def merge(ch, carry):
        rows = pl.ds(pl.multiple_of(ch * chunk, chunk), chunk)
        lse = [ls_ref[g, rows, :] for g in range(npat)]
        top = functools.reduce(jnp.maximum, lse)
        e = [jnp.exp(x - top) for x in lse]
        inv = 1.0 / functools.reduce(lambda a, b_: a + b_, e)
        acc = jnp.zeros((chunk, 2 * HEAD_DIM), F32)
        for g in range(npat):
            acc = acc + (e[g] * inv) * os_ref[g, rows, :]
        o_ref[0, rows, :] = acc
        return carry

    lax.fori_loop(0, seq // chunk, merge, 0)


def _attn_prompt(q, k, v, bias):
    b, s, _ = q.shape
    w2 = 2 * HEAD_DIM
    qspec = pl.BlockSpec((1, s, w2), lambda i, p: (i, 0, p))
    return pl.pallas_call(
        functools.partial(_attn_prompt_kernel, seq=s),
        grid=(b, A_HEADS // 2),
        in_specs=[qspec, qspec, qspec,
                  pl.BlockSpec((len(PATTERNS), 1, 2 * BLOCK, 2 * BLOCK), lambda i, p: (0, p, 0, 0))],
        out_specs=qspec,
        out_shape=jax.ShapeDtypeStruct((b, s, A_WIDTH), F32),
        scratch_shapes=[pltpu.VMEM((len(PATTERNS), s, w2), F32)] * 2,
        compiler_params=_cparams("parallel", "parallel"),
        name="attn_prompt",
    )(q, k, v, bias.reshape(len(PATTERNS), A_HEADS // 2, 2 * BLOCK, 2 * BLOCK))


def _attn_sample_kernel(q_ref, kn_ref, vn_ref, kt_ref, vt_ref, bias_ref, b0_ref, o_ref):
    npat = len(PATTERNS)
    nt = (((1,), (1,)), ((), ()))
    q = q_ref[0]
    v_new = _bf(vn_ref[0])
    s0 = jnp.sum(_bf(q) * _bf(kn_ref[0]), axis=-1, keepdims=True) * SCALE + b0_ref[...]
    s = jnp.concatenate([_dot(q[h:h + 1], kt_ref[0, h]) for h in range(A_HEADS)], axis=0) * SCALE
    ps, p0s, lses = [], [], []
    for g in range(npat):
        bias = bias_ref[g]
        sg = jnp.where(bias > 0.5 * NEG, s + bias, NEG)
        m = jnp.maximum(jnp.max(sg, axis=-1, keepdims=True), s0)
        l = jnp.sum(jnp.exp(sg - m), axis=-1, keepdims=True) + jnp.exp(s0 - m)
        lse = m + jnp.log(l)
        ps.append(jnp.exp(sg - lse))
        p0s.append(_bf(jnp.exp(s0 - lse)))
        lses.append(lse)
    top = functools.reduce(jnp.maximum, lses)
    e = [jnp.exp(x - top) for x in lses]
    inv = 1.0 / functools.reduce(lambda a, b_: a + b_, e)
    w = [_bf(x * inv) for x in e]
    rows = []
    for h in range(A_HEADS):
        p_h = jnp.concatenate([p[h:h + 1] for p in ps], axis=0)
        o_h = lax.dot_general(p_h.astype(BF16), vt_ref[0, h].astype(BF16), nt, preferred_element_type=F32)
        acc = jnp.zeros((1, HEAD_DIM), F32)
        for g in range(npat):
            acc = acc + w[g][h:h + 1] * _bf(o_h[g:g + 1] + p0s[g][h:h + 1] * v_new[h:h + 1])
        rows.append(acc)
    o_ref[0] = jnp.concatenate(rows, axis=0)


def _attn_sample(q, k_new, v_new, cache_k, cache_v, bias, b0):
    db, wb = cache_k.shape[:2]
    heads = lambda a: a.reshape(db, A_HEADS, HEAD_DIM)
    vec = pl.BlockSpec((1, A_HEADS, HEAD_DIM), lambda i: (i, 0, 0))
    full = lambda a: pl.BlockSpec(a.shape, lambda i: (0,) * a.ndim)
    rows_minor = lambda a: a.transpose(0, 2, 3, 1)
    cache = pl.BlockSpec((1, A_HEADS, HEAD_DIM, wb), lambda i: (i, 0, 0, 0))
    args = [heads(q), heads(k_new), heads(v_new), rows_minor(cache_k), rows_minor(cache_v), bias, b0]
    return pl.pallas_call(
        _attn_sample_kernel,
        grid=(db,),
        in_specs=[vec, vec, vec, cache, cache, full(bias), full(b0)],
        out_specs=vec,
        out_shape=jax.ShapeDtypeStruct((db, A_HEADS, HEAD_DIM), F32),
        compiler_params=_cparams("parallel"),
        name="attn_sample",
    )(*args).reshape(db, A_WIDTH)


def _rwkv_prep_math(rc, prev, mu, w0, wd, a0, wa, wg, k_k, k_a, r_k, seg):
    xr = rc + (prev - rc) * mu
    o1, o2, o3 = R_WIDTH, 2 * R_WIDTH, 3 * R_WIDTH
    r = xr[:, :o1]
    kr = xr[:, o1:o2]
    vr = xr[:, o2:o3]
    xw = xr[:, o3:o3 + DECAY_LORA]
    xa = xr[:, o3 + DECAY_LORA:o3 + DECAY_LORA + AAA_LORA]
    xg = xr[:, o3 + DECAY_LORA + AAA_LORA:]
    z = -(w0 + _dot(jnp.tanh(xw), wd))
    softplus = jnp.maximum(z, 0.0) + jnp.log(1.0 + jnp.exp(-jnp.abs(z)))
    decay = jnp.exp(-jnp.exp(-softplus - 0.5))
    a = jax.nn.sigmoid(a0 + _dot(xa, wa))
    g = _dot(jax.nn.sigmoid(xg), wg)
    kk = kr * k_k
    k2 = kr * (1.0 + (a - 1.0) * k_a)
    kk = kk / jnp.maximum(jnp.sqrt(_segdot(kk * kk, seg)), 1e-12)
    bonus = _segdot(r * k2 * r_k, seg) * vr
    return _bf(r), decay, k2, vr, _bf(kk), kk * a, bonus, g


def _rwkv_prep_sample_kernel(rc_ref, prev_ref, *refs):
    p = [x[...] for x in refs[:N_PREP_PARAMS]]
    outs = refs[N_PREP_PARAMS:]
    for o, val in zip(outs, _rwkv_prep_math(rc_ref[...], prev_ref[...], *p)):
        o[...] = val


def _head_seg():
    head = jnp.arange(R_WIDTH) // R_HEAD
    return (head[:, None] == head[None, :]).astype(BF16)


def _rwkv_prep_sample(rc, prev, params):
    n = rc.shape[0]
    full = lambda a: pl.BlockSpec(a.shape, lambda i: (0,) * a.ndim)
    return pl.pallas_call(
        _rwkv_prep_sample_kernel,
        grid=(1,),
        in_specs=[full(rc), full(prev)] + [full(a) for a in params],
        out_specs=[pl.BlockSpec((n, R_WIDTH), lambda i: (0, 0))] * 8,
        out_shape=[jax.ShapeDtypeStruct((n, R_WIDTH), F32)] * 8,
        compiler_params=_cparams("arbitrary"),
        name="rwkv_prep_sample",
    )(rc, prev, *params)


def _seg_lane_sum(x, lo_mask):
    lo = jnp.sum(jnp.where(lo_mask, x, 0.0), axis=-1, keepdims=True)
    hi = jnp.sum(jnp.where(lo_mask, 0.0, x), axis=-1, keepdims=True)
    return jnp.where(lo_mask, lo, hi)


def _rwkv_scan_kernel(r_ref, w_ref, k_ref, v_ref, kk_ref, ka_ref, s0_ref, y_ref, sT_ref, st_ref, *, bb, tc):
    pairs = R_HEADS // 2
    w2 = 2 * R_HEAD

    @pl.when(pl.program_id(1) == 0)
    def _():
        for b in range(bb):
            for p in range(pairs):
                st_ref[b, p] = jnp.concatenate([s0_ref[b, 2 * p], s0_ref[b, 2 * p + 1]], axis=1)

    lane = lax.broadcasted_iota(jnp.int32, (R_HEAD, w2), 1)
    sub = lax.broadcasted_iota(jnp.int32, (R_HEAD, w2), 0)
    lo_mask = lane < R_HEAD
    eye2 = (lane & (R_HEAD - 1)) == sub

    grp = min(8, tc)

    def group(tg, carry):
        rows = pl.ds(pl.multiple_of(tg * grp, grp), grp)
        for b in range(bb):
            for p in range(pairs):
                cols = slice(p * w2, (p + 1) * w2)
                r8, w8, k8, v8, kk8, ka8 = (ref[b, rows, cols] for ref in (r_ref, w_ref, k_ref, v_ref, kk_ref, ka_ref))
                s = st_ref[b, p]
                sb = _bf(s)
                ys = []
                for j in range(grp):
                    row = lambda a: a[j:j + 1, :]
                    sa = -_seg_lane_sum(sb * row(kk8), lo_mask)
                    vcol = _seg_lane_sum(jnp.where(eye2, jnp.broadcast_to(row(v8), (R_HEAD, w2)), 0.0), lo_mask)
                    s = s * row(w8) + sa * row(ka8) + vcol * row(k8)
                    sb = _bf(s)
                    yfull = _seg_lane_sum(sb * row(r8), lo_mask)
                    ys.append(jnp.sum(jnp.where(eye2, yfull, 0.0), axis=0, keepdims=True))
                st_ref[b, p] = s
                y_ref[b, rows, cols] = jnp.concatenate(ys, axis=0) if grp > 1 else ys[0]
        return carry

    lax.fori_loop(0, tc // grp, group, 0)

    @pl.when(pl.program_id(1) == pl.num_programs(1) - 1)
    def _():
        for b in range(bb):
            for p in range(pairs):
                s = st_ref[b, p]
                sT_ref[b, 2 * p] = s[:, :R_HEAD]
                sT_ref[b, 2 * p + 1] = s[:, R_HEAD:]


def _rwkv_scan(r, w, k, v, kk, ka, s0, *, bb, tc):
    b, t, _ = r.shape
    seq = pl.BlockSpec((bb, tc, R_WIDTH), lambda i, j: (i, j, 0))
    state = pl.BlockSpec((bb, R_HEADS, R_HEAD, R_HEAD), lambda i, j: (i, 0, 0, 0))
    return pl.pallas_call(
        functools.partial(_rwkv_scan_kernel, bb=bb, tc=tc),
        grid=(b // bb, t // tc),
        in_specs=[seq] * 6 + [state],
        out_specs=[seq, state],
        out_shape=[jax.ShapeDtypeStruct((b, t, R_WIDTH), F32),
                   jax.ShapeDtypeStruct((b, R_HEADS, R_HEAD, R_HEAD), F32)],
        scratch_shapes=[pltpu.VMEM((bb, R_HEADS // 2, R_HEAD, 2 * R_HEAD), F32)],
        compiler_params=_cparams("parallel", "arbitrary"),
        name="rwkv_scan",
    )(r, w, k, v, kk, ka, s0)


CHAINS = LANES // 2
K2 = R_HEAD // 2


def _chain_rows(x):
    b, t, _ = x.shape
    x = x.reshape(b, t, R_HEADS, 2, K2).transpose(1, 4, 3, 0, 2)
    return x.reshape(t, K2, LANES)


def _lane_scan_kernel(kk_ref, w_ref, ka_ref, kx_ref, r_ref, v_ref, s0_ref, y_ref, sT_ref, s_ref, *, tc):
    lo = lax.broadcasted_iota(jnp.int32, (K2, LANES), 1) < CHAINS

    @pl.when(pl.program_id(0) == 0)
    def _():
        for k2 in range(K2):
            s_ref[k2] = s0_ref[k2]

    def both_halves(x):
        return x + pltpu.roll(x, CHAINS, axis=1)

    nhalf = 2
    vh = R_HEAD // nhalf

    def row(ref, t, k2):
        return jnp.broadcast_to(ref[t, k2:k2 + 1, :], (vh, LANES))

    acc0 = []
    for hf in range(nhalf):
        vs = slice(hf * vh, (hf + 1) * vh)
        a = jnp.zeros((vh, LANES), F32)
        for k2 in range(K2):
            a = a + _bf(s_ref[k2, vs, :]) * row(kk_ref, 0, k2)
        acc0.append(a)

    def step(t, acc):
        tn = jnp.minimum(t + 1, tc - 1)
        nxt, ys = [], []
        v_t = v_ref[t]
        v_sw = pltpu.roll(v_t, CHAINS, axis=1)
        for hf in range(nhalf):
            vs = slice(hf * vh, (hf + 1) * vh)
            vv = jnp.where(lo, v_t, v_sw) if hf == 0 else jnp.where(lo, v_sw, v_t)
            sa = -both_halves(acc[hf])
            yacc = jnp.zeros((vh, LANES), F32)
            nacc = jnp.zeros((vh, LANES), F32)
            for k2 in range(K2):
                s = s_ref[k2, vs, :] * row(w_ref, t, k2) + sa * row(ka_ref, t, k2) + vv * row(kx_ref, t, k2)
                s_ref[k2, vs, :] = s
                sb = _bf(s)
                yacc = yacc + sb * row(r_ref, t, k2)
                nacc = nacc + sb * row(kk_ref, tn, k2)
            ys.append(both_halves(yacc))
            nxt.append(nacc)
        y_ref[t] = jnp.where(lo, ys[0], ys[1])
        return tuple(nxt)

    lax.fori_loop(0, tc, step, tuple(acc0))

    @pl.when(pl.program_id(0) == pl.num_programs(0) - 1)
    def _():
        for k2 in range(K2):
            sT_ref[k2] = s_ref[k2]


def _lane_scan(r, w, k, v, kk, ka, s0, *, tc):
    b, t, _ = r.shape
    assert b * R_HEADS == CHAINS
    ops = [_chain_rows(x) for x in (kk, w, ka, k, r, v)]
    s0c = s0.reshape(b, R_HEADS, 2, K2, 2, K2).transpose(5, 2, 3, 4, 0, 1).reshape(K2, R_HEAD, LANES)
    rows = pl.BlockSpec((tc, K2, LANES), lambda i: (i, 0, 0))
    state = pl.BlockSpec((K2, R_HEAD, LANES), lambda i: (0, 0, 0))
    y, st = pl.pallas_call(
        functools.partial(_lane_scan_kernel, tc=tc),
        grid=(t // tc,),
        in_specs=[rows] * 6 + [state],
        out_specs=[rows, state],
        out_shape=[jax.ShapeDtypeStruct((t, K2, LANES), F32), jax.ShapeDtypeStruct((K2, R_HEAD, LANES), F32)],
        scratch_shapes=[pltpu.VMEM((K2, R_HEAD, LANES), F32)],
        compiler_params=_cparams("arbitrary"),
        name="rwkv_lane_scan",
    )(*ops, s0c)
    y = y.reshape(t, K2, 2, b, R_HEADS).transpose(3, 0, 4, 2, 1).reshape(b, t, R_WIDTH)
    st = st.reshape(K2, 2, K2, 2, b, R_HEADS).transpose(4, 5, 1, 2, 3, 0).reshape(b, R_HEADS, R_HEAD, R_HEAD)
    return y, st


def _post_kernel(x_ref, y_ref, bonus_ref, g_ref, oa_ref, sg_ref, gnw_ref, gnb_ref, seg_ref, wba_ref, wbb_ref,
                 wout_ref, ln2_ref, wr_ref, br_ref, *rest, aliased, n_main):
    outs = rest[aliased:]
    x1_ref, h2_ref, ti_ref, tg_ref = outs

    @pl.when(pl.program_id(0) >= n_main)
    def _():
        for o in outs:
            o[...] = jnp.zeros(o.shape, o.dtype)

    @pl.when(pl.program_id(0) < n_main)
    def _():
        _post_body(x_ref, y_ref, bonus_ref, g_ref, oa_ref, sg_ref, gnw_ref, gnb_ref, seg_ref, wba_ref, wbb_ref,
                   wout_ref, ln2_ref, wr_ref, br_ref, x1_ref, h2_ref, ti_ref, tg_ref)


def _post_body(x_ref, y_ref, bonus_ref, g_ref, oa_ref, sg_ref, gnw_ref, gnb_ref, seg_ref, wba_ref, wbb_ref,
               wout_ref, ln2_ref, wr_ref, br_ref, x1_ref, h2_ref, ti_ref, tg_ref):
    y = y_ref[...]
    seg = seg_ref[...]
    mu = _segdot(y, seg) * (1.0 / R_HEAD)
    yc = y - mu
    var = _segdot(yc * yc, seg) * (1.0 / R_HEAD)
    yn = yc * lax.rsqrt(var + GN_EPS) * gnw_ref[...] + gnb_ref[...]
    o_b = (yn + bonus_ref[...]) * g_ref[...]
    mixed = (sg_ref[:, :D_MODEL] * _dot(oa_ref[...], wba_ref[...])
             + sg_ref[:, D_MODEL:] * _dot(o_b, wbb_ref[...]))
    x1 = x_ref[...] + _dot(mixed, wout_ref[...])
    x1_ref[...] = x1
    h2 = _rms(x1, ln2_ref[...])
    _store_token_tiles(h2_ref, h2)
    logits = _dot(h2, wr_ref[...]) + br_ref[...]
    lane = lax.broadcasted_iota(jnp.int32, logits.shape, 1).astype(F32)
    work = logits
    vals, idxs = [], []
    for _ in range(TOP_K):
        m = jnp.max(work, axis=-1, keepdims=True)
        idx = jnp.min(jnp.where(work == m, lane, float(LANES)), axis=-1, keepdims=True)
        vals.append(m)
        idxs.append(idx)
        work = jnp.where(lane == idx, -jnp.inf, work)
    es = [jnp.exp(v - vals[0]) for v in vals]
    tot = es[0] + es[1] + es[2] + es[3]
    ti = jnp.zeros(logits.shape, F32)
    tg = jnp.zeros(logits.shape, F32)
    for kslot in range(TOP_K):
        ti = jnp.where(lane == float(kslot), idxs[kslot], ti)
        tg = jnp.where(lane == float(kslot), es[kslot] / tot, tg)
    ti_ref[...] = ti.astype(jnp.int32)
    tg_ref[...] = tg


def _post(x, y, bonus, g, oa, sg, consts, *, tm, n_total, row0=0, into=None):
    n = x.shape[0]
    blk0 = row0 // tm
    n_main = n // tm
    steps = n_total // tm if into is None else n_main
    row = lambda c: pl.BlockSpec((tm, c), lambda i: (jnp.minimum(i, n_main - 1), 0))
    orow = lambda c: pl.BlockSpec((tm, c), lambda i: (i + blk0, 0))
    otile = pl.BlockSpec((tm * TILE_ROWS, LANES), lambda i: (i + blk0, 0))
    full = lambda a: pl.BlockSpec(a.shape, lambda i: (0,) * a.ndim)
    ins = [x, y, bonus, g, oa, sg, *consts]
    in_specs = [row(D_MODEL), row(R_WIDTH), row(R_WIDTH), row(R_WIDTH), row(A_WIDTH), row(2 * D_MODEL)]
    in_specs += [full(a) for a in consts]
    aliases = {}
    if into is not None:
        aliases = {len(ins) + i: i for i in range(len(into))}
        in_specs += [pl.BlockSpec(memory_space=pl.ANY)] * len(into)
        ins += list(into)
    return pl.pallas_call(
        functools.partial(_post_kernel, aliased=len(aliases), n_main=n_main),
        grid=(steps,),
        in_specs=in_specs,
        out_specs=[orow(D_MODEL), otile, orow(LANES), orow(LANES)],
        out_shape=[jax.ShapeDtypeStruct((n_total, D_MODEL), F32),
                   jax.ShapeDtypeStruct((n_total * TILE_ROWS, LANES), F32),
                   jax.ShapeDtypeStruct((n_total, LANES), jnp.int32),
                   jax.ShapeDtypeStruct((n_total, LANES), F32)],
        input_output_aliases=aliases,
        compiler_params=_cparams("parallel"),
        name="post",
    )(*ins)


def _start_rows(rows_ref, hbm, buf, sem, *, rows, to_hbm):
    for r in range(rows):
        at = pl.ds(pl.multiple_of(rows_ref[0, 0, r], TILE_ROWS), TILE_ROWS)
        here = pl.ds(r * TILE_ROWS, TILE_ROWS)
        prio = r % 2
        if to_hbm:
            pltpu.make_async_copy(buf.at[here], hbm.at[at], sem).start(priority=prio)
        else:
            pltpu.make_async_copy(hbm.at[at], buf.at[here], sem).start(priority=prio)


def _wait_rows(hbm, buf, sem, *, rows, to_hbm):
    whole = hbm.at[pl.ds(0, rows * TILE_ROWS)]
    (pltpu.make_async_copy(buf, whole, sem) if to_hbm else pltpu.make_async_copy(whole, buf, sem)).wait()


def _expert_mlp(xb, wgu_bf, wd_bf, bgu_ref, bd_ref, rows):
    acc = jnp.broadcast_to(bd_ref[0], (rows, D_MODEL))
    cw = 512
    for c in range(D_MODEL // cw):
        gs = slice(c * cw, (c + 1) * cw)
        us = slice(D_MODEL + c * cw, D_MODEL + (c + 1) * cw)
        gt = jnp.dot(xb, wgu_bf[:, gs], preferred_element_type=F32) + bgu_ref[0, :, gs]
        up = jnp.dot(xb, wgu_bf[:, us], preferred_element_type=F32) + bgu_ref[0, :, us]
        gt = jnp.minimum(gt, SWIGLU_LIMIT)
        up = jnp.clip(up, -SWIGLU_LIMIT, SWIGLU_LIMIT)
        act = (up + 1.0) * (gt * jax.nn.sigmoid(gt * SWIGLU_ALPHA))
        acc = acc + jnp.dot(act.astype(BF16), wd_bf[gs, :], preferred_element_type=F32)
    return acc


def _moe_kernel(be_ref, src_cur, src_nxt, dst_prv, dst_cur, h_hbm, wgu_ref, bgu_ref, wd_ref, bd_ref, ys_hbm,
                x0, x1, y0, y1, wgu_bf, wd_bf, gsem, ssem, *, rows, plane_rows, n_tok):
    i = pl.program_id(0)
    last = pl.num_programs(0) - 1

    changed = jnp.logical_or(i == 0, be_ref[i] != be_ref[jnp.maximum(i - 1, 0)])

    @pl.when(changed)
    def _():
        step = 128

        def cast(j, c):
            rs = pl.ds(pl.multiple_of(j * step, step), step)
            wgu_bf[rs, :] = wgu_ref[0, rs, :].astype(BF16)
            wd_bf[rs, :] = wd_ref[0, rs, :].astype(BF16)
            return c

        lax.fori_loop(0, D_MODEL // step, cast, 0)

    def run(par):
        xc, xn = (x0, x1) if par == 0 else (x1, x0)
        yc, yp = (y0, y1) if par == 0 else (y1, y0)
        gather = functools.partial(_start_rows, hbm=h_hbm, rows=rows, to_hbm=False)
        scatter = functools.partial(_start_rows, hbm=ys_hbm, rows=rows, to_hbm=True)
        gathered = functools.partial(_wait_rows, h_hbm, rows=rows, to_hbm=False)
        scattered = functools.partial(_wait_rows, ys_hbm, rows=rows, to_hbm=True)

        @pl.when(i == 0)
        def _():
            yp[...] = jnp.zeros(yp.shape, yp.dtype)
            spare = plane_rows - n_tok
            fills = [pltpu.make_async_copy(yp.at[pl.ds(0, spare * TILE_ROWS)],
                                           ys_hbm.at[pl.ds((k * plane_rows + n_tok) * TILE_ROWS, spare * TILE_ROWS)],
                                           ssem.at[1 - par])
                     for k in range(TOP_K)]
            for d in fills:
                d.start()
            for d in fills:
                d.wait()
            gather(src_cur, buf=xc, sem=gsem.at[par])

        gathered(xc, gsem.at[par])
        gather(src_nxt, buf=xn, sem=gsem.at[1 - par])

        @pl.when(i >= 1)
        def _():
            scattered(yc, ssem.at[par])

        used = i < be_ref[last + 1]

        @pl.when(used)
        def _():
            scatter(dst_prv, buf=yp, sem=ssem.at[1 - par])
            xb = _load_token_tiles(xc, rows).astype(BF16)
            _store_token_tiles(yc, _expert_mlp(xb, wgu_bf, wd_bf, bgu_ref, bd_ref, rows))

        @pl.when(jnp.logical_not(used))
        def _():
            scatter(dst_prv, buf=yp, sem=ssem.at[1 - par])

        @pl.when(i == last)
        def _():
            scatter(dst_cur, buf=yc, sem=ssem.at[par])
            gathered(xn, gsem.at[1 - par])
            scattered(yp, ssem.at[1 - par])
            scattered(yc, ssem.at[par])

    for par in range(2):
        pl.when(lax.rem(i, 2) == par)(functools.partial(run, par))


def _moe_experts(block_e, ids, h2, w_gate_up, b_gate_up, w_down, b_down, *, rows, plane_rows, n_tok):
    n_blocks = ids.shape[0] - 2
    ids = lax.optimization_barrier(ids)
    tok = lax.shift_right_logical(ids, 2)
    src = tok * TILE_ROWS
    dst = ((ids & (TOP_K - 1)) * plane_rows + tok) * TILE_ROWS
    idblk = lambda off: pl.BlockSpec((1, 1, rows), lambda i, be: (i + off, 0, 0), memory_space=pltpu.SMEM)
    grid_spec = pltpu.PrefetchScalarGridSpec(
        num_scalar_prefetch=1,
        grid=(n_blocks,),
        in_specs=[
            idblk(1), idblk(2), idblk(0), idblk(1),
            pl.BlockSpec(memory_space=pl.ANY),
            pl.BlockSpec((1, D_MODEL, 2 * D_MODEL), lambda i, be: (be[i], 0, 0)),
            pl.BlockSpec((1, 1, 2 * D_MODEL), lambda i, be: (be[i], 0, 0)),
            pl.BlockSpec((1, D_MODEL, D_MODEL), lambda i, be: (be[i], 0, 0)),
            pl.BlockSpec((1, 1, D_MODEL), lambda i, be: (be[i], 0, 0)),
        ],
        out_specs=pl.BlockSpec(memory_space=pl.ANY),
        scratch_shapes=[pltpu.VMEM((rows * TILE_ROWS, LANES), F32)] * 4 + [
            pltpu.VMEM((D_MODEL, 2 * D_MODEL), BF16),
            pltpu.VMEM((D_MODEL, D_MODEL), BF16),
            pltpu.SemaphoreType.DMA((2,)),
            pltpu.SemaphoreType.DMA((2,))],
    )
    return pl.pallas_call(
        functools.partial(_moe_kernel, rows=rows, plane_rows=plane_rows, n_tok=n_tok),
        grid_spec=grid_spec,
        out_shape=jax.ShapeDtypeStruct((TOP_K * plane_rows * TILE_ROWS, LANES), F32),
        compiler_params=_cparams("arbitrary"),
        name="moe_experts",
    )(block_e, src, src, dst, dst, h2, w_gate_up, b_gate_up[:, None], w_down, b_down[:, None])


def _combine_kernel(ys_ref, x1_ref, gate_ref, lnf_ref, y_ref):
    acc = x1_ref[...]
    tm = acc.shape[0]
    for kslot in range(TOP_K):
        acc = acc + gate_ref[:, kslot:kslot + 1] * _load_token_tiles(ys_ref, tm, kslot)
    y_ref[...] = _rms(acc, lnf_ref[...])


def _combine(ys, x1, gates, ln_f, *, tm, n, row0):
    blk0 = row0 // tm
    return pl.pallas_call(
        _combine_kernel,
        grid=(n // tm,),
        in_specs=[
            pl.BlockSpec((TOP_K, tm * TILE_ROWS, LANES), lambda i: (0, i + blk0, 0)),
            pl.BlockSpec((tm, D_MODEL), lambda i: (i + blk0, 0)),
            pl.BlockSpec((tm, LANES), lambda i: (i + blk0, 0)),
            pl.BlockSpec(ln_f.shape, lambda i: (0, 0)),
        ],
        out_specs=pl.BlockSpec((tm, D_MODEL), lambda i: (i, 0)),
        out_shape=jax.ShapeDtypeStruct((n, D_MODEL), F32),
        compiler_params=_cparams("parallel"),
        name="combine",
    )(ys.reshape(TOP_K, -1, LANES), x1, gates, ln_f)


def _route(topi, n_pad, rows):
    n_tok = topi.shape[0]
    e_flat = topi.reshape(-1)
    nk = e_flat.shape[0]
    n_blocks = -(-(nk + N_EXPERTS * (rows - 1)) // rows)
    spare = n_pad - n_tok
    assert 3 * rows <= TOP_K * spare
    experts = jnp.arange(N_EXPERTS, dtype=jnp.int32)
    id_bits = max(nk - 1, 1).bit_length()
    assert id_bits + (N_EXPERTS - 1).bit_length() < 32
    order = jnp.sort(e_flat.astype(jnp.int32) << id_bits | jnp.arange(nk, dtype=jnp.int32)) & ((1 << id_bits) - 1)
    sizes = jnp.sum((e_flat[:, None] == experts[None, :]).astype(jnp.int32), axis=0)
    padded = (sizes + rows - 1) // rows * rows
    pends = jnp.cumsum(padded)
    slot = jnp.arange(n_blocks * rows, dtype=jnp.int32)
    past = (slot[:, None] >= pends[None, :]).astype(jnp.int32)
    e_slot = jnp.sum(past, axis=1)
    off = slot - jnp.sum(past * padded[None, :], axis=1)
    src = jnp.sum(past * sizes[None, :], axis=1) + off
    size_e = jnp.sum((e_slot[:, None] == experts[None, :]) * sizes[None, :], axis=1)

    def pad_ids(d):
        return (TOP_K * (n_tok + d % spare) + d // spare).astype(jnp.int32)

    ids = jnp.where(off < size_e, order[jnp.clip(src, 0, nk - 1)], pad_ids((slot // rows) % 2 * rows + slot % rows))
    end_ids = pad_ids(2 * rows + jnp.arange(rows, dtype=jnp.int32))
    ids = jnp.concatenate([end_ids, ids, end_ids]).reshape(n_blocks + 2, 1, rows)
    block_e = jnp.minimum(e_slot[::rows], N_EXPERTS - 1).astype(jnp.int32)
    n_used = (pends[-1] // rows).astype(jnp.int32)
    return jnp.concatenate([block_e, n_used[None]]), ids


def kernel(x_prompt, x_sample, cache_k, cache_v, state_wkv, state_shift, rel_bias, ln1, w_in, rwkv_mu, w0,
           w_decay_up, a0, w_a_up, w_g_up, k_k, k_a, r_k, gn_w, gn_b, w_branch_a, w_branch_b, w_out, ln2,
           w_router, b_router, w_gate_up, b_gate_up, w_down, b_down, ln_f):
    bp, seq, _ = x_prompt.shape
    db = x_sample.shape[0]
    n_p = bp * seq
    n_tot = n_p + db
    n_pad = -(-(n_tot + -(-3 * MOE_ROWS // TOP_K)) // TOKEN_TILE) * TOKEN_TILE
    l = 0

    row = lambda a: a.reshape(1, -1)
    seg = _head_seg()
    prep_params = (row(rwkv_mu[l]), row(w0[l]), w_decay_up[l], row(a0[l]), w_a_up[l], w_g_up[l],
                   row(k_k[l]), row(k_a[l]), row(r_k[l]), seg)
    wr_pad = jnp.pad(w_router[l], ((0, 0), (0, LANES - N_EXPERTS))).astype(BF16)
    br_pad = jnp.pad(b_router[l], (0, LANES - N_EXPERTS), constant_values=NEG).reshape(1, LANES)
    post_small = (row(gn_w[l]), row(gn_b[l]), seg)
    post_tail = (row(ln2[l]), wr_pad, br_pad)

    w_in_bf = w_in[l].astype(BF16)
    consts = (post_small + (w_branch_a[l].astype(BF16), w_branch_b[l].astype(BF16), w_out[l].astype(BF16))
              + post_tail)

    xp = x_prompt.reshape(n_p, D_MODEL)
    q_p, k_p, v_p, sg_p, kt_p, vt_p, shift_p, *prep_p = _inproj_prompt(xp, row(ln1[l]), w_in_bf, prep_params,
                                                                     tm=TOKEN_TILE, seq=seq)
    as3 = lambda a: a.reshape(bp, seq, -1)
    oa_p = _attn_prompt(as3(q_p), as3(k_p), as3(v_p), _prompt_bias(rel_bias))
    r_, w_, k2_, vr_, kk_, ka_, bonus_p, g_p = prep_p
    s0_p = jnp.zeros((bp, R_HEADS, R_HEAD, R_HEAD), F32)
    y_p, wkv_p = _lane_scan(r_, w_, k2_, vr_, kk_, ka_, s0_p, tc=SCAN_CHUNK)
    flat = lambda a: a.reshape(n_p, -1)
    bufs = _post(xp, flat(y_p), flat(bonus_p), flat(g_p), flat(oa_p), sg_p, consts,
                 tm=TOKEN_TILE, n_total=n_pad)

    xs = x_sample.reshape(db, D_MODEL)
    q_s, k_s, v_s, rc_s, sg_s = _inproj(xs, row(ln1[l]), w_in_bf, tm=db)
    bias_s, b0_s = _sample_bias(rel_bias, cache_k.shape[2])
    oa_s = _attn_sample(q_s, k_s, v_s, cache_k[l], cache_v[l], bias_s, b0_s)
    prep_s = _rwkv_prep_sample(rc_s, state_shift[l], prep_params)
    sr, sw, sk2, svr, skk, ska, bonus_s, g_s = [a[:, None] for a in prep_s]
    y_s, wkv_s = _rwkv_scan(sr, sw, sk2, svr, skk, ska, state_wkv[l], bb=SCAN_BATCH, tc=1)
    x1_all, h2_all, ti_all, tg_all = _post(xs, y_s[:, 0], bonus_s[:, 0], g_s[:, 0], oa_s, sg_s, consts,
                                           tm=db, n_total=n_pad, row0=n_p, into=bufs)

    block_e, ids = _route(ti_all[:n_tot, :TOP_K], n_pad, MOE_ROWS)
    ys = _moe_experts(block_e, ids, h2_all, w_gate_up[l], b_gate_up[l], w_down[l], b_down[l],
                      rows=MOE_ROWS, plane_rows=n_pad, n_tok=n_tot)
    lnf = row(ln_f)
    y_prompt = _combine(ys, x1_all, tg_all, lnf, tm=TOKEN_TILE, n=n_p, row0=0)
    y_sample = _combine(ys, x1_all, tg_all, lnf, tm=db, n=db, row0=n_p)

    heads = lambda a, b_: a.reshape(1, b_, -1, A_HEADS, HEAD_DIM)
    return (y_prompt.reshape(bp, seq, D_MODEL), y_sample.reshape(db, 1, D_MODEL),
            kt_p.transpose(0, 3, 1, 2)[None], vt_p.transpose(0, 3, 1, 2)[None], wkv_p[None], shift_p[:, 0][None],
            heads(k_s, db), heads(v_s, db), wkv_s[None], rc_s[None])
```

```python
import functools
import math

import jax
import jax.numpy as jnp
from jax import lax
from jax.experimental import pallas as pl
from jax.experimental.pallas import tpu as pltpu

F32 = jnp.float32
BF16 = jnp.bfloat16

D_MODEL = 1024
A_HEADS = 8
HEAD_DIM = 64
A_WIDTH = A_HEADS * HEAD_DIM
PATTERNS = ((128, 1), (512, 4), (2048, 16))
BLOCK = 128
NUM_BUCKETS = 32
MAX_DISTANCE = 2048
SCALE = HEAD_DIM ** -0.5
NEG = -1e30
R_HEADS = 8
R_HEAD = 64
R_WIDTH = R_HEADS * R_HEAD
DECAY_LORA = 64
AAA_LORA = 64
GATE_LORA = 128
GN_EPS = 64e-5
N_RWKV_COLS = 3 * R_WIDTH + DECAY_LORA + AAA_LORA + GATE_LORA
N_IN_COLS = 3 * A_WIDTH + N_RWKV_COLS + 2 * D_MODEL
N_EXPERTS = 32
TOP_K = 4
SWIGLU_LIMIT = 7.0
SWIGLU_ALPHA = 1.702
RMS_EPS = 1e-6

LANES = 128
SUBLANES = 8
VMEM_LIMIT = 56 * 1024 * 1024
TOKEN_TILE = 256
MOE_ROWS = 512
SCAN_BATCH = 2
SCAN_CHUNK = 64


def _cparams(*sem):
    return pltpu.CompilerParams(dimension_semantics=sem, vmem_limit_bytes=VMEM_LIMIT)


def _rms(x, g):
    return x * lax.rsqrt(jnp.mean(x * x, axis=-1, keepdims=True) + RMS_EPS) * g


def _dot(a, b):
    return jnp.dot(a.astype(BF16), b.astype(BF16), preferred_element_type=F32)


def _bf(x):
    return x.astype(BF16).astype(F32)


def _split3(x):
    hi = x.astype(BF16)
    r1 = x - hi.astype(F32)
    mid = r1.astype(BF16)
    lo = (r1 - mid.astype(F32)).astype(BF16)
    return hi, mid, lo


def _segdot(x, seg):
    hi, mid, lo = _split3(x)
    d = lambda p: jnp.dot(p, seg, preferred_element_type=F32)
    return d(hi) + d(mid) + d(lo)


TILE_ROWS = D_MODEL // LANES
assert TILE_ROWS == SUBLANES


def _load_token_tiles(ref, n, *lead):
    return jnp.concatenate([ref[(*lead, pl.ds(p, n, stride=TILE_ROWS), slice(None))] for p in range(TILE_ROWS)], axis=1)


def _store_token_tiles(ref, x):
    n = x.shape[0]
    for p in range(TILE_ROWS):
        ref[pl.ds(p, n, stride=TILE_ROWS), :] = x[:, p * LANES:(p + 1) * LANES]


def _project(x_ref, g_ref, w_ref):
    h = _rms(x_ref[...], g_ref[...]).astype(BF16)
    return lambda c0, c1: _dot(h, w_ref[:, c0:c1])


C_RWKV = 3 * A_WIDTH
C_GATE = C_RWKV + N_RWKV_COLS


def _inproj_kernel(x_ref, g_ref, w_ref, q_ref, k_ref, v_ref, rc_ref, sg_ref):
    mm = _project(x_ref, g_ref, w_ref)
    q_ref[...] = mm(0, A_WIDTH)
    k_ref[...] = mm(A_WIDTH, 2 * A_WIDTH)
    v_ref[...] = mm(2 * A_WIDTH, C_RWKV)
    rc_ref[...] = mm(C_RWKV, C_GATE)
    sg_ref[:, :D_MODEL] = jax.nn.sigmoid(mm(C_GATE, C_GATE + D_MODEL))
    sg_ref[:, D_MODEL:] = jax.nn.sigmoid(mm(C_GATE + D_MODEL, N_IN_COLS))


def _inproj(x, ln1, w, *, tm):
    n = x.shape[0]
    row = lambda c: pl.BlockSpec((tm, c), lambda i: (i, 0))
    full = lambda a: pl.BlockSpec(a.shape, lambda i: (0,) * a.ndim)
    out_cols = (A_WIDTH, A_WIDTH, A_WIDTH, N_RWKV_COLS, 2 * D_MODEL)
    return pl.pallas_call(
        _inproj_kernel,
        grid=(n // tm,),
        in_specs=[row(D_MODEL), full(ln1), full(w)],
        out_specs=[row(c) for c in out_cols],
        out_shape=[jax.ShapeDtypeStruct((n, c), F32) for c in out_cols],
        compiler_params=_cparams("parallel"),
        name="inproj",
    )(x, ln1, w)


N_PREP_PARAMS = 10
N_PREP_OUTS = 8
PREP_DTYPES = (BF16, F32, F32, F32, BF16, F32, F32, F32)


def _inproj_prompt_kernel(x_ref, g_ref, w_ref, *refs, per):
    params = [r[...] for r in refs[:N_PREP_PARAMS]]
    q_ref, k_ref, v_ref, sg_ref, kt_ref, vt_ref, shift_ref = refs[N_PREP_PARAMS:N_PREP_PARAMS + 7]
    prep_refs = refs[N_PREP_PARAMS + 7:-1]
    carry = refs[-1]
    mm = _project(x_ref, g_ref, w_ref)
    q_ref[...] = mm(0, A_WIDTH)
    k = mm(A_WIDTH, 2 * A_WIDTH)
    v = mm(2 * A_WIDTH, C_RWKV)
    k_ref[...] = k
    v_ref[...] = v
    tm = k.shape[0]
    kt_ref[0] = k.T.reshape(A_HEADS, HEAD_DIM, tm)
    vt_ref[0] = v.T.reshape(A_HEADS, HEAD_DIM, tm)
    sg_ref[:, :D_MODEL] = jax.nn.sigmoid(mm(C_GATE, C_GATE + D_MODEL))
    sg_ref[:, D_MODEL:] = jax.nn.sigmoid(mm(C_GATE + D_MODEL, N_IN_COLS))

    @pl.when(lax.rem(pl.program_id(0), per) == 0)
    def _():
        carry[...] = jnp.zeros(carry.shape, carry.dtype)

    rc = mm(C_RWKV, C_GATE)
    row = lax.broadcasted_iota(jnp.int32, (tm, 1), 0)
    prev = jnp.where(row == 0, carry[...], pltpu.roll(rc, 1, axis=0))
    for o, val in zip(prep_refs, _rwkv_prep_math(rc, prev, *params)):
        o[0] = val.astype(o.dtype)
    last = rc[tm - 1:tm, :]
    carry[...] = last
    shift_ref[0] = last


def _inproj_prompt(x, ln1, w, prep_params, *, tm, seq):
    n = x.shape[0]
    per = seq // tm
    row = lambda c: pl.BlockSpec((tm, c), lambda i: (i, 0))
    full = lambda a: pl.BlockSpec(a.shape, lambda i: (0,) * a.ndim)
    t_spec = pl.BlockSpec((1, A_HEADS, HEAD_DIM, tm), lambda i: (i // per, 0, 0, i % per))
    t_shape = jax.ShapeDtypeStruct((n // seq, A_HEADS, HEAD_DIM, seq), F32)
    cols = (A_WIDTH, A_WIDTH, A_WIDTH, 2 * D_MODEL)
    out_specs = ([row(c) for c in cols] + [t_spec, t_spec, pl.BlockSpec((1, 1, N_RWKV_COLS), lambda i: (i // per, 0, 0))]
                 + [pl.BlockSpec((1, tm, R_WIDTH), lambda i: (i // per, i % per, 0))] * N_PREP_OUTS)
    out_shape = ([jax.ShapeDtypeStruct((n, c), F32) for c in cols]
                 + [t_shape, t_shape, jax.ShapeDtypeStruct((n // seq, 1, N_RWKV_COLS), F32)]
                 + [jax.ShapeDtypeStruct((n // seq, seq, R_WIDTH), dt) for dt in PREP_DTYPES])
    return pl.pallas_call(
        functools.partial(_inproj_prompt_kernel, per=per),
        grid=(n // tm,),
        in_specs=[row(D_MODEL), full(ln1), full(w)] + [full(a) for a in prep_params],
        out_specs=out_specs,
        out_shape=out_shape,
        scratch_shapes=[pltpu.VMEM((1, N_RWKV_COLS), F32)],
        compiler_params=_cparams("arbitrary"),
        name="inproj_prompt",
    )(x, ln1, w, *prep_params)


def _t5_bucket(dist):
    max_exact = NUM_BUCKETS // 2
    d = jnp.maximum(dist, 1).astype(F32)
    large = max_exact + (jnp.log(d / max_exact) / math.log(MAX_DISTANCE / max_exact)
                         * (NUM_BUCKETS - max_exact)).astype(jnp.int32)
    large = jnp.minimum(large, NUM_BUCKETS - 1)
    return jnp.where(dist < max_exact, dist, large)


def _bias_lookup(rel_bias, dist):
    onehot = (_t5_bucket(dist)[..., None] == jnp.arange(NUM_BUCKETS)).astype(F32)
    return jnp.moveaxis(jnp.dot(onehot, rel_bias.astype(F32), precision=lax.Precision.HIGHEST), -1, 0)


def _prompt_bias(rel_bias):
    i = jnp.arange(BLOCK)[:, None]
    j = jnp.arange(2 * BLOCK)[None, :]
    delta = i + BLOCK - j
    out = []
    for window, dil in PATTERNS:
        n = window // dil
        band = (delta >= 0) & (delta <= n)
        out.append(jnp.where(band[None], _bias_lookup(rel_bias, jnp.clip(delta, 0, n) * dil), NEG))
    return jnp.stack(out, 0)


def _sample_bias(rel_bias, wb):
    dist = wb - jnp.arange(wb)
    out = []
    for window, dil in PATTERNS:
        member = (dist % dil == 0) & (dist <= window)
        out.append(jnp.where(member[None], _bias_lookup(rel_bias, dist), NEG))
    b0 = _bias_lookup(rel_bias, jnp.zeros((1,), jnp.int32))
    return jnp.stack(out, 0), b0


def _attn_prompt_kernel(q_ref, k_ref, v_ref, bias_ref, o_ref, os_ref, ls_ref, *, seq):
    nt = (((1,), (1,)), ((), ()))
    npat = len(PATTERNS)
    w2 = 2 * HEAD_DIM
    lo = lax.broadcasted_iota(jnp.int32, (BLOCK, w2), 1) < HEAD_DIM
    key = lax.broadcasted_iota(jnp.int32, (2 * BLOCK, 2 * BLOCK), 1)

    for g, (window, dil) in enumerate(PATTERNS):
        nb = seq // (BLOCK * dil)
        sh = dil.bit_length() - 1

        def body(it, carry, g=g, dil=dil, nb=nb, sh=sh):
            r = it & (dil - 1)
            c = it >> sh
            start = c * (BLOCK * dil) + r
            rows = pl.ds(start, BLOCK, stride=dil) if dil > 1 else pl.ds(start, BLOCK)
            qb = q_ref[0, rows, :]
            kb = k_ref[0, rows, :]
            vb = v_ref[0, rows, :]
            if nb > 1:
                pstart = jnp.maximum(c - 1, 0) * (BLOCK * dil) + r
                prow = pl.ds(pstart, BLOCK, stride=dil) if dil > 1 else pl.ds(pstart, BLOCK)
                kp = k_ref[0, prow, :]
                vp = v_ref[0, prow, :]
            q2 = jnp.concatenate([jnp.where(lo, qb, 0.0), jnp.where(lo, 0.0, qb)], axis=0).astype(BF16)
            if nb > 1:
                k2 = jnp.concatenate([kp, kb], axis=0)
                v2 = jnp.concatenate([vp, vb], axis=0)
                bias = bias_ref[g, 0]
                valid = (bias > 0.5 * NEG) & ((key >= BLOCK) | (c > 0))
            else:
                k2, v2 = kb, vb
                bias = bias_ref[g, 0, :, BLOCK:]
                valid = bias > 0.5 * NEG
            s = lax.dot_general(q2, k2.astype(BF16), nt, preferred_element_type=F32)
            s = jnp.where(valid, s * SCALE + bias, NEG)
            m = jnp.max(s, axis=-1, keepdims=True)
            p = jnp.exp(s - m)
            l = jnp.sum(p, axis=-1, keepdims=True)
            o2 = jnp.dot((p * (1.0 / l)).astype(BF16), v2.astype(BF16), preferred_element_type=F32)
            lse = jnp.broadcast_to(m + jnp.log(l), (2 * BLOCK, w2))
            os_ref[g, rows, :] = jnp.where(lo, o2[:BLOCK], o2[BLOCK:])
            ls_ref[g, rows, :] = jnp.where(lo, lse[:BLOCK], lse[BLOCK:])
            return carry

        lax.fori_loop(0, dil * nb, body, 0, unroll=8)

    chunk = 256

    def merge(ch, carry):
        rows = pl.ds(pl.multiple_of(ch * chunk, chunk), chunk)
        lse = [ls_ref[g, rows, :] for g in range(npat)]
        top = functools.reduce(jnp.maximum, lse)
        e = [jnp.exp(x - top) for x in lse]
        inv = 1.0 / functools.reduce(lambda a, b_: a + b_, e)
        acc = jnp.zeros((chunk, 2 * HEAD_DIM), F32)
        for g in range(npat):
            acc = acc + (e[g] * inv) * os_ref[g, rows, :]
        o_ref[0, rows, :] = acc
        return carry

    lax.fori_loop(0, seq // chunk, merge, 0)


def _attn_prompt(q, k, v, bias):
    b, s, _ = q.shape
    w2 = 2 * HEAD_DIM
    qspec = pl.BlockSpec((1, s, w2), lambda i, p: (i, 0, p))
    return pl.pallas_call(
        functools.partial(_attn_prompt_kernel, seq=s),
        grid=(b, A_HEADS // 2),
        in_specs=[qspec, qspec, qspec,
                  pl.BlockSpec((len(PATTERNS), 1, 2 * BLOCK, 2 * BLOCK), lambda i, p: (0, p, 0, 0))],
        out_specs=qspec,
        out_shape=jax.ShapeDtypeStruct((b, s, A_WIDTH), F32),
        scratch_shapes=[pltpu.VMEM((len(PATTERNS), s, w2), F32)] * 2,
        compiler_params=_cparams("parallel", "parallel"),
        name="attn_prompt",
    )(q, k, v, bias.reshape(len(PATTERNS), A_HEADS // 2, 2 * BLOCK, 2 * BLOCK))


def _attn_sample_kernel(q_ref, kn_ref, vn_ref, kt_ref, vt_ref, bias_ref, b0_ref, o_ref):
    npat = len(PATTERNS)
    nt = (((1,), (1,)), ((), ()))
    q = q_ref[0]
    v_new = _bf(vn_ref[0])
    s0 = jnp.sum(_bf(q) * _bf(kn_ref[0]), axis=-1, keepdims=True) * SCALE + b0_ref[...]
    s = jnp.concatenate([_dot(q[h:h + 1], kt_ref[0, h]) for h in range(A_HEADS)], axis=0) * SCALE
    ps, p0s, lses = [], [], []
    for g in range(npat):
        bias = bias_ref[g]
        sg = jnp.where(bias > 0.5 * NEG, s + bias, NEG)
        m = jnp.maximum(jnp.max(sg, axis=-1, keepdims=True), s0)
        l = jnp.sum(jnp.exp(sg - m), axis=-1, keepdims=True) + jnp.exp(s0 - m)
        lse = m + jnp.log(l)
        ps.append(jnp.exp(sg - lse))
        p0s.append(_bf(jnp.exp(s0 - lse)))
        lses.append(lse)
    top = functools.reduce(jnp.maximum, lses)
    e = [jnp.exp(x - top) for x in lses]
    inv = 1.0 / functools.reduce(lambda a, b_: a + b_, e)
    w = [_bf(x * inv) for x in e]
    rows = []
    for h in range(A_HEADS):
        p_h = jnp.concatenate([p[h:h + 1] for p in ps], axis=0)
        o_h = lax.dot_general(p_h.astype(BF16), vt_ref[0, h].astype(BF16), nt, preferred_element_type=F32)
        acc = jnp.zeros((1, HEAD_DIM), F32)
        for g in range(npat):
            acc = acc + w[g][h:h + 1] * _bf(o_h[g:g + 1] + p0s[g][h:h + 1] * v_new[h:h + 1])
        rows.append(acc)
    o_ref[0] = jnp.concatenate(rows, axis=0)


def _attn_sample(q, k_new, v_new, cache_k, cache_v, bias, b0):
    db, wb = cache_k.shape[:2]
    heads = lambda a: a.reshape(db, A_HEADS, HEAD_DIM)
    vec = pl.BlockSpec((1, A_HEADS, HEAD_DIM), lambda i: (i, 0, 0))
    full = lambda a: pl.BlockSpec(a.shape, lambda i: (0,) * a.ndim)
    rows_minor = lambda a: a.transpose(0, 2, 3, 1)
    cache = pl.BlockSpec((1, A_HEADS, HEAD_DIM, wb), lambda i: (i, 0, 0, 0))
    args = [heads(q), heads(k_new), heads(v_new), rows_minor(cache_k), rows_minor(cache_v), bias, b0]
    return pl.pallas_call(
        _attn_sample_kernel,
        grid=(db,),
        in_specs=[vec, vec, vec, cache, cache, full(bias), full(b0)],
        out_specs=vec,
        out_shape=jax.ShapeDtypeStruct((db, A_HEADS, HEAD_DIM), F32),
        compiler_params=_cparams("parallel"),
        name="attn_sample",
    )(*args).reshape(db, A_WIDTH)


def _rwkv_prep_math(rc, prev, mu, w0, wd, a0, wa, wg, k_k, k_a, r_k, seg):
    xr = rc + (prev - rc) * mu
    o1, o2, o3 = R_WIDTH, 2 * R_WIDTH, 3 * R_WIDTH
    r = xr[:, :o1]
    kr = xr[:, o1:o2]
    vr = xr[:, o2:o3]
    xw = xr[:, o3:o3 + DECAY_LORA]
    xa = xr[:, o3 + DECAY_LORA:o3 + DECAY_LORA + AAA_LORA]
    xg = xr[:, o3 + DECAY_LORA + AAA_LORA:]
    z = -(w0 + _dot(jnp.tanh(xw), wd))
    softplus = jnp.maximum(z, 0.0) + jnp.log(1.0 + jnp.exp(-jnp.abs(z)))
    decay = jnp.exp(-jnp.exp(-softplus - 0.5))
    a = jax.nn.sigmoid(a0 + _dot(xa, wa))
    g = _dot(jax.nn.sigmoid(xg), wg)
    kk = kr * k_k
    k2 = kr * (1.0 + (a - 1.0) * k_a)
    kk = kk / jnp.maximum(jnp.sqrt(_segdot(kk * kk, seg)), 1e-12)
    bonus = _segdot(r * k2 * r_k, seg) * vr
    return _bf(r), decay, k2, vr, _bf(kk), kk * a, bonus, g


def _rwkv_prep_sample_kernel(rc_ref, prev_ref, *refs):
    p = [x[...] for x in refs[:N_PREP_PARAMS]]
    outs = refs[N_PREP_PARAMS:]
    for o, val in zip(outs, _rwkv_prep_math(rc_ref[...], prev_ref[...], *p)):
        o[...] = val


def _head_seg():
    head = jnp.arange(R_WIDTH) // R_HEAD
    return (head[:, None] == head[None, :]).astype(BF16)


def _rwkv_prep_sample(rc, prev, params):
    n = rc.shape[0]
    full = lambda a: pl.BlockSpec(a.shape, lambda i: (0,) * a.ndim)
    return pl.pallas_call(
        _rwkv_prep_sample_kernel,
        grid=(1,),
        in_specs=[full(rc), full(prev)] + [full(a) for a in params],
        out_specs=[pl.BlockSpec((n, R_WIDTH), lambda i: (0, 0))] * 8,
        out_shape=[jax.ShapeDtypeStruct((n, R_WIDTH), F32)] * 8,
        compiler_params=_cparams("arbitrary"),
        name="rwkv_prep_sample",
    )(rc, prev, *params)


def _seg_lane_sum(x, lo_mask):
    lo = jnp.sum(jnp.where(lo_mask, x, 0.0), axis=-1, keepdims=True)
    hi = jnp.sum(jnp.where(lo_mask, 0.0, x), axis=-1, keepdims=True)
    return jnp.where(lo_mask, lo, hi)


def _rwkv_scan_kernel(r_ref, w_ref, k_ref, v_ref, kk_ref, ka_ref, s0_ref, y_ref, sT_ref, st_ref, *, bb, tc):
    pairs = R_HEADS // 2
    w2 = 2 * R_HEAD

    @pl.when(pl.program_id(1) == 0)
    def _():
        for b in range(bb):
            for p in range(pairs):
                st_ref[b, p] = jnp.concatenate([s0_ref[b, 2 * p], s0_ref[b, 2 * p + 1]], axis=1)

    lane = lax.broadcasted_iota(jnp.int32, (R_HEAD, w2), 1)
    sub = lax.broadcasted_iota(jnp.int32, (R_HEAD, w2), 0)
    lo_mask = lane < R_HEAD
    eye2 = (lane & (R_HEAD - 1)) == sub

    grp = min(8, tc)

    def group(tg, carry):
        rows = pl.ds(pl.multiple_of(tg * grp, grp), grp)
        for b in range(bb):
            for p in range(pairs):
                cols = slice(p * w2, (p + 1) * w2)
                r8, w8, k8, v8, kk8, ka8 = (ref[b, rows, cols] for ref in (r_ref, w_ref, k_ref, v_ref, kk_ref, ka_ref))
                s = st_ref[b, p]
                sb = _bf(s)
                ys = []
                for j in range(grp):
                    row = lambda a: a[j:j + 1, :]
                    sa = -_seg_lane_sum(sb * row(kk8), lo_mask)
                    vcol = _seg_lane_sum(jnp.where(eye2, jnp.broadcast_to(row(v8), (R_HEAD, w2)), 0.0), lo_mask)
                    s = s * row(w8) + sa * row(ka8) + vcol * row(k8)
                    sb = _bf(s)
                    yfull = _seg_lane_sum(sb * row(r8), lo_mask)
                    ys.append(jnp.sum(jnp.where(eye2, yfull, 0.0), axis=0, keepdims=True))
                st_ref[b, p] = s
                y_ref[b, rows, cols] = jnp.concatenate(ys, axis=0) if grp > 1 else ys[0]
        return carry

    lax.fori_loop(0, tc // grp, group, 0)

    @pl.when(pl.program_id(1) == pl.num_programs(1) - 1)
    def _():
        for b in range(bb):
            for p in range(pairs):
                s = st_ref[b, p]
                sT_ref[b, 2 * p] = s[:, :R_HEAD]
                sT_ref[b, 2 * p + 1] = s[:, R_HEAD:]


def _rwkv_scan(r, w, k, v, kk, ka, s0, *, bb, tc):
    b, t, _ = r.shape
    seq = pl.BlockSpec((bb, tc, R_WIDTH), lambda i, j: (i, j, 0))
    state = pl.BlockSpec((bb, R_HEADS, R_HEAD, R_HEAD), lambda i, j: (i, 0, 0, 0))
    return pl.pallas_call(
        functools.partial(_rwkv_scan_kernel, bb=bb, tc=tc),
        grid=(b // bb, t // tc),
        in_specs=[seq] * 6 + [state],
        out_specs=[seq, state],
        out_shape=[jax.ShapeDtypeStruct((b, t, R_WIDTH), F32),
                   jax.ShapeDtypeStruct((b, R_HEADS, R_HEAD, R_HEAD), F32)],
        scratch_shapes=[pltpu.VMEM((bb, R_HEADS // 2, R_HEAD, 2 * R_HEAD), F32)],
        compiler_params=_cparams("parallel", "arbitrary"),
        name="rwkv_scan",
    )(r, w, k, v, kk, ka, s0)


CHAINS = LANES // 2
K2 = R_HEAD // 2


def _chain_rows(x):
    b, t, _ = x.shape
    x = x.reshape(b, t, R_HEADS, 2, K2).transpose(1, 4, 3, 0, 2)
    return x.reshape(t, K2, LANES)


def _lane_scan_kernel(kk_ref, w_ref, ka_ref, kx_ref, r_ref, v_ref, s0_ref, y_ref, sT_ref, s_ref, *, tc):
    lo = lax.broadcasted_iota(jnp.int32, (K2, LANES), 1) < CHAINS

    @pl.when(pl.program_id(0) == 0)
    def _():
        for k2 in range(K2):
            s_ref[k2] = s0_ref[k2]

    def both_halves(x):
        return x + pltpu.roll(x, CHAINS, axis=1)

    nhalf = 2
    vh = R_HEAD // nhalf

    def row(ref, t, k2):
        return jnp.broadcast_to(ref[t, k2:k2 + 1, :].astype(F32), (vh, LANES))

    acc0 = []
    for hf in range(nhalf):
        vs = slice(hf * vh, (hf + 1) * vh)
        a = jnp.zeros((vh, LANES), F32)
        for k2 in range(K2):
            a = a + _bf(s_ref[k2, vs, :]) * row(kk_ref, 0, k2)
        acc0.append(a)

    def step(t, acc):
        tn = jnp.minimum(t + 1, tc - 1)
        nxt, ys = [], []
        v_t = v_ref[t]
        v_sw = pltpu.roll(v_t, CHAINS, axis=1)
        for hf in range(nhalf):
            vs = slice(hf * vh, (hf + 1) * vh)
            vv = jnp.where(lo, v_t, v_sw) if hf == 0 else jnp.where(lo, v_sw, v_t)
            sa = -both_halves(acc[hf])
            yacc = jnp.zeros((vh, LANES), F32)
            nacc = jnp.zeros((vh, LANES), F32)
            for k2 in range(K2):
                s = s_ref[k2, vs, :] * row(w_ref, t, k2) + sa * row(ka_ref, t, k2) + vv * row(kx_ref, t, k2)
                s_ref[k2, vs, :] = s
                sb = _bf(s)
                yacc = yacc + sb * row(r_ref, t, k2)
                nacc = nacc + sb * row(kk_ref, tn, k2)
            ys.append(both_halves(yacc))
            nxt.append(nacc)
        y_ref[t] = jnp.where(lo, ys[0], ys[1])
        return tuple(nxt)

    lax.fori_loop(0, tc, step, tuple(acc0))

    @pl.when(pl.program_id(0) == pl.num_programs(0) - 1)
    def _():
        for k2 in range(K2):
            sT_ref[k2] = s_ref[k2]


def _lane_scan(r, w, k, v, kk, ka, s0, *, tc):
    b, t, _ = r.shape
    assert b * R_HEADS == CHAINS
    ops = [_chain_rows(x) for x in (kk, w, ka, k, r, v)]
    s0c = s0.reshape(b, R_HEADS, 2, K2, 2, K2).transpose(5, 2, 3, 4, 0, 1).reshape(K2, R_HEAD, LANES)
    rows = pl.BlockSpec((tc, K2, LANES), lambda i: (i, 0, 0))
    state = pl.BlockSpec((K2, R_HEAD, LANES), lambda i: (0, 0, 0))
    y, st = pl.pallas_call(
        functools.partial(_lane_scan_kernel, tc=tc),
        grid=(t // tc,),
        in_specs=[rows] * 6 + [state],
        out_specs=[rows, state],
        out_shape=[jax.ShapeDtypeStruct((t, K2, LANES), F32), jax.ShapeDtypeStruct((K2, R_HEAD, LANES), F32)],
        scratch_shapes=[pltpu.VMEM((K2, R_HEAD, LANES), F32)],
        compiler_params=_cparams("arbitrary"),
        name="rwkv_lane_scan",
    )(*ops, s0c)
    y = y.reshape(t, K2, 2, b, R_HEADS).transpose(3, 0, 4, 2, 1).reshape(b, t, R_WIDTH)
    st = st.reshape(K2, 2, K2, 2, b, R_HEADS).transpose(4, 5, 1, 2, 3, 0).reshape(b, R_HEADS, R_HEAD, R_HEAD)
    return y, st


def _post_kernel(x_ref, y_ref, bonus_ref, g_ref, oa_ref, sg_ref, gnw_ref, gnb_ref, seg_ref, wba_ref, wbb_ref,
                 wout_ref, ln2_ref, wr_ref, br_ref, *rest, aliased, n_main):
    outs = rest[aliased:]
    x1_ref, h2_ref, ti_ref, tg_ref = outs

    @pl.when(pl.program_id(0) >= n_main)
    def _():
        for o in outs:
            o[...] = jnp.zeros(o.shape, o.dtype)

    @pl.when(pl.program_id(0) < n_main)
    def _():
        _post_body(x_ref, y_ref, bonus_ref, g_ref, oa_ref, sg_ref, gnw_ref, gnb_ref, seg_ref, wba_ref, wbb_ref,
                   wout_ref, ln2_ref, wr_ref, br_ref, x1_ref, h2_ref, ti_ref, tg_ref)


def _post_body(x_ref, y_ref, bonus_ref, g_ref, oa_ref, sg_ref, gnw_ref, gnb_ref, seg_ref, wba_ref, wbb_ref,
               wout_ref, ln2_ref, wr_ref, br_ref, x1_ref, h2_ref, ti_ref, tg_ref):
    y = y_ref[...]
    seg = seg_ref[...]
    mu = _segdot(y, seg) * (1.0 / R_HEAD)
    yc = y - mu
    var = _segdot(yc * yc, seg) * (1.0 / R_HEAD)
    yn = yc * lax.rsqrt(var + GN_EPS) * gnw_ref[...] + gnb_ref[...]
    o_b = (yn + bonus_ref[...]) * g_ref[...]
    mixed = (sg_ref[:, :D_MODEL] * _dot(oa_ref[...], wba_ref[...])
             + sg_ref[:, D_MODEL:] * _dot(o_b, wbb_ref[...]))
    x1 = x_ref[...] + _dot(mixed, wout_ref[...])
    x1_ref[...] = x1
    h2 = _rms(x1, ln2_ref[...])
    _store_token_tiles(h2_ref, h2)
    logits = _dot(h2, wr_ref[...]) + br_ref[...]
    lane = lax.broadcasted_iota(jnp.int32, logits.shape, 1).astype(F32)
    work = logits
    vals, idxs = [], []
    for _ in range(TOP_K):
        m = jnp.max(work, axis=-1, keepdims=True)
        idx = jnp.min(jnp.where(work == m, lane, float(LANES)), axis=-1, keepdims=True)
        vals.append(m)
        idxs.append(idx)
        work = jnp.where(lane == idx, -jnp.inf, work)
    es = [jnp.exp(v - vals[0]) for v in vals]
    tot = es[0] + es[1] + es[2] + es[3]
    ti = jnp.zeros(logits.shape, F32)
    tg = jnp.zeros(logits.shape, F32)
    for kslot in range(TOP_K):
        ti = jnp.where(lane == float(kslot), idxs[kslot], ti)
        tg = jnp.where(lane == float(kslot), es[kslot] / tot, tg)
    ti_ref[...] = ti.astype(jnp.int32)
    tg_ref[...] = tg


def _post(x, y, bonus, g, oa, sg, consts, *, tm, n_total, row0=0, into=None):
    n = x.shape[0]
    blk0 = row0 // tm
    n_main = n // tm
    steps = n_total // tm if into is None else n_main
    row = lambda c: pl.BlockSpec((tm, c), lambda i: (jnp.minimum(i, n_main - 1), 0))
    orow = lambda c: pl.BlockSpec((tm, c), lambda i: (i + blk0, 0))
    otile = pl.BlockSpec((tm * TILE_ROWS, LANES), lambda i: (i + blk0, 0))
    full = lambda a: pl.BlockSpec(a.shape, lambda i: (0,) * a.ndim)
    ins = [x, y, bonus, g, oa, sg, *consts]
    in_specs = [row(D_MODEL), row(R_WIDTH), row(R_WIDTH), row(R_WIDTH), row(A_WIDTH), row(2 * D_MODEL)]
    in_specs += [full(a) for a in consts]
    aliases = {}
    if into is not None:
        aliases = {len(ins) + i: i for i in range(len(into))}
        in_specs += [pl.BlockSpec(memory_space=pl.ANY)] * len(into)
        ins += list(into)
    return pl.pallas_call(
        functools.partial(_post_kernel, aliased=len(aliases), n_main=n_main),
        grid=(steps,),
        in_specs=in_specs,
        out_specs=[orow(D_MODEL), otile, orow(LANES), orow(LANES)],
        out_shape=[jax.ShapeDtypeStruct((n_total, D_MODEL), F32),
                   jax.ShapeDtypeStruct((n_total * TILE_ROWS, LANES), F32),
                   jax.ShapeDtypeStruct((n_total, LANES), jnp.int32),
                   jax.ShapeDtypeStruct((n_total, LANES), F32)],
        input_output_aliases=aliases,
        compiler_params=_cparams("parallel"),
        name="post",
    )(*ins)


def _start_rows(rows_ref, hbm, buf, sem, *, rows, to_hbm):
    for r in range(rows):
        at = pl.ds(pl.multiple_of(rows_ref[0, 0, r], TILE_ROWS), TILE_ROWS)
        here = pl.ds(r * TILE_ROWS, TILE_ROWS)
        prio = r % 2
        if to_hbm:
            pltpu.make_async_copy(buf.at[here], hbm.at[at], sem).start(priority=prio)
        else:
            pltpu.make_async_copy(hbm.at[at], buf.at[here], sem).start(priority=prio)


def _wait_rows(hbm, buf, sem, *, rows, to_hbm):
    whole = hbm.at[pl.ds(0, rows * TILE_ROWS)]
    (pltpu.make_async_copy(buf, whole, sem) if to_hbm else pltpu.make_async_copy(whole, buf, sem)).wait()


def _expert_mlp(xb, wgu_bf, wd_bf, bgu_ref, bd_ref, rows):
    acc = jnp.broadcast_to(bd_ref[0], (rows, D_MODEL))
    cw = 512
    for c in range(D_MODEL // cw):
        gs = slice(c * cw, (c + 1) * cw)
        us = slice(D_MODEL + c * cw, D_MODEL + (c + 1) * cw)
        gt = jnp.dot(xb, wgu_bf[:, gs], preferred_element_type=F32) + bgu_ref[0, :, gs]
        up = jnp.dot(xb, wgu_bf[:, us], preferred_element_type=F32) + bgu_ref[0, :, us]
        gt = jnp.minimum(gt, SWIGLU_LIMIT)
        up = jnp.clip(up, -SWIGLU_LIMIT, SWIGLU_LIMIT)
        act = (up + 1.0) * (gt * jax.nn.sigmoid(gt * SWIGLU_ALPHA))
        acc = acc + jnp.dot(act.astype(BF16), wd_bf[gs, :], preferred_element_type=F32)
    return acc


def _moe_kernel(be_ref, src_cur, src_nxt, dst_prv, dst_cur, h_hbm, wgu_ref, bgu_ref, wd_ref, bd_ref, ys_hbm,
                x0, x1, y0, y1, wgu_bf, wd_bf, gsem, ssem, *, rows, plane_rows, n_tok):
    i = pl.program_id(0)
    last = pl.num_programs(0) - 1

    changed = jnp.logical_or(i == 0, be_ref[i] != be_ref[jnp.maximum(i - 1, 0)])

    @pl.when(changed)
    def _():
        step = 128

        def cast(j, c):
            rs = pl.ds(pl.multiple_of(j * step, step), step)
            wgu_bf[rs, :] = wgu_ref[0, rs, :].astype(BF16)
            wd_bf[rs, :] = wd_ref[0, rs, :].astype(BF16)
            return c

        lax.fori_loop(0, D_MODEL // step, cast, 0)

    def run(par):
        xc, xn = (x0, x1) if par == 0 else (x1, x0)
        yc, yp = (y0, y1) if par == 0 else (y1, y0)
        gather = functools.partial(_start_rows, hbm=h_hbm, rows=rows, to_hbm=False)
        scatter = functools.partial(_start_rows, hbm=ys_hbm, rows=rows, to_hbm=True)
        gathered = functools.partial(_wait_rows, h_hbm, rows=rows, to_hbm=False)
        scattered = functools.partial(_wait_rows, ys_hbm, rows=rows, to_hbm=True)

        @pl.when(i == 0)
        def _():
            yp[...] = jnp.zeros(yp.shape, yp.dtype)
            spare = plane_rows - n_tok
            fills = [pltpu.make_async_copy(yp.at[pl.ds(0, spare * TILE_ROWS)],
                                           ys_hbm.at[pl.ds((k * plane_rows + n_tok) * TILE_ROWS, spare * TILE_ROWS)],
                                           ssem.at[1 - par])
                     for k in range(TOP_K)]
            for d in fills:
                d.start()
            for d in fills:
                d.wait()
            gather(src_cur, buf=xc, sem=gsem.at[par])

        gathered(xc, gsem.at[par])
        gather(src_nxt, buf=xn, sem=gsem.at[1 - par])

        @pl.when(i >= 1)
        def _():
            scattered(yc, ssem.at[par])

        used = i < be_ref[last + 1]

        @pl.when(used)
        def _():
            scatter(dst_prv, buf=yp, sem=ssem.at[1 - par])
            xb = _load_token_tiles(xc, rows).astype(BF16)
            _store_token_tiles(yc, _expert_mlp(xb, wgu_bf, wd_bf, bgu_ref, bd_ref, rows))

        @pl.when(jnp.logical_not(used))
        def _():
            scatter(dst_prv, buf=yp, sem=ssem.at[1 - par])

        @pl.when(i == last)
        def _():
            scatter(dst_cur, buf=yc, sem=ssem.at[par])
            gathered(xn, gsem.at[1 - par])
            scattered(yp, ssem.at[1 - par])
            scattered(yc, ssem.at[par])

    for par in range(2):
        pl.when(lax.rem(i, 2) == par)(functools.partial(run, par))


def _moe_experts(block_e, ids, h2, w_gate_up, b_gate_up, w_down, b_down, *, rows, plane_rows, n_tok):
    n_blocks = ids.shape[0] - 2
    ids = lax.optimization_barrier(ids)
    tok = lax.shift_right_logical(ids, 2)
    src = tok * TILE_ROWS
    dst = ((ids & (TOP_K - 1)) * plane_rows + tok) * TILE_ROWS
    idblk = lambda off: pl.BlockSpec((1, 1, rows), lambda i, be: (i + off, 0, 0), memory_space=pltpu.SMEM)
    grid_spec = pltpu.PrefetchScalarGridSpec(
        num_scalar_prefetch=1,
        grid=(n_blocks,),
        in_specs=[
            idblk(1), idblk(2), idblk(0), idblk(1),
            pl.BlockSpec(memory_space=pl.ANY),
            pl.BlockSpec((1, D_MODEL, 2 * D_MODEL), lambda i, be: (be[i], 0, 0)),
            pl.BlockSpec((1, 1, 2 * D_MODEL), lambda i, be: (be[i], 0, 0)),
            pl.BlockSpec((1, D_MODEL, D_MODEL), lambda i, be: (be[i], 0, 0)),
            pl.BlockSpec((1, 1, D_MODEL), lambda i, be: (be[i], 0, 0)),
        ],
        out_specs=pl.BlockSpec(memory_space=pl.ANY),
        scratch_shapes=[pltpu.VMEM((rows * TILE_ROWS, LANES), F32)] * 4 + [
            pltpu.VMEM((D_MODEL, 2 * D_MODEL), BF16),
            pltpu.VMEM((D_MODEL, D_MODEL), BF16),
            pltpu.SemaphoreType.DMA((2,)),
            pltpu.SemaphoreType.DMA((2,))],
    )
    return pl.pallas_call(
        functools.partial(_moe_kernel, rows=rows, plane_rows=plane_rows, n_tok=n_tok),
        grid_spec=grid_spec,
        out_shape=jax.ShapeDtypeStruct((TOP_K * plane_rows * TILE_ROWS, LANES), F32),
        compiler_params=_cparams("arbitrary"),
        name="moe_experts",
    )(block_e, src, src, dst, dst, h2, w_gate_up, b_gate_up[:, None], w_down, b_down[:, None])


def _combine_kernel(ys_ref, x1_ref, gate_ref, lnf_ref, y_ref):
    acc = x1_ref[...]
    tm = acc.shape[0]
    for kslot in range(TOP_K):
        acc = acc + gate_ref[:, kslot:kslot + 1] * _load_token_tiles(ys_ref, tm, kslot)
    y_ref[...] = _rms(acc, lnf_ref[...])


def _combine(ys, x1, gates, ln_f, *, tm, n, row0):
    blk0 = row0 // tm
    return pl.pallas_call(
        _combine_kernel,
        grid=(n // tm,),
        in_specs=[
            pl.BlockSpec((TOP_K, tm * TILE_ROWS, LANES), lambda i: (0, i + blk0, 0)),
            pl.BlockSpec((tm, D_MODEL), lambda i: (i + blk0, 0)),
            pl.BlockSpec((tm, LANES), lambda i: (i + blk0, 0)),
            pl.BlockSpec(ln_f.shape, lambda i: (0, 0)),
        ],
        out_specs=pl.BlockSpec((tm, D_MODEL), lambda i: (i, 0)),
        out_shape=jax.ShapeDtypeStruct((n, D_MODEL), F32),
        compiler_params=_cparams("parallel"),
        name="combine",
    )(ys.reshape(TOP_K, -1, LANES), x1, gates, ln_f)


def _route(topi, n_pad, rows):
    n_tok = topi.shape[0]
    e_flat = topi.reshape(-1)
    nk = e_flat.shape[0]
    n_blocks = -(-(nk + N_EXPERTS * (rows - 1)) // rows)
    spare = n_pad - n_tok
    assert 3 * rows <= TOP_K * spare
    experts = jnp.arange(N_EXPERTS, dtype=jnp.int32)
    order = jnp.argsort(e_flat).astype(jnp.int32)
    sizes = jnp.sum((e_flat[:, None] == experts[None, :]).astype(jnp.int32), axis=0)
    padded = (sizes + rows - 1) // rows * rows
    pends = jnp.cumsum(padded)
    slot = jnp.arange(n_blocks * rows, dtype=jnp.int32)
    past = (slot[:, None] >= pends[None, :]).astype(jnp.int32)
    e_slot = jnp.sum(past, axis=1)
    off = slot - jnp.sum(past * padded[None, :], axis=1)
    src = jnp.sum(past * sizes[None, :], axis=1) + off
    size_e = jnp.sum((e_slot[:, None] == experts[None, :]) * sizes[None, :], axis=1)

    def pad_ids(d):
        return (TOP_K * (n_tok + d % spare) + d // spare).astype(jnp.int32)

    ids = jnp.where(off < size_e, order[jnp.clip(src, 0, nk - 1)], pad_ids((slot // rows) % 2 * rows + slot % rows))
    end_ids = pad_ids(2 * rows + jnp.arange(rows, dtype=jnp.int32))
    ids = jnp.concatenate([end_ids, ids, end_ids]).reshape(n_blocks + 2, 1, rows)
    block_e = jnp.minimum(e_slot[::rows], N_EXPERTS - 1).astype(jnp.int32)
    n_used = (pends[-1] // rows).astype(jnp.int32)
    return jnp.concatenate([block_e, n_used[None]]), ids


def kernel(x_prompt, x_sample, cache_k, cache_v, state_wkv, state_shift, rel_bias, ln1, w_in, rwkv_mu, w0,
           w_decay_up, a0, w_a_up, w_g_up, k_k, k_a, r_k, gn_w, gn_b, w_branch_a, w_branch_b, w_out, ln2,
           w_router, b_router, w_gate_up, b_gate_up, w_down, b_down, ln_f):
    bp, seq, _ = x_prompt.shape
    db = x_sample.shape[0]
    n_p = bp * seq
    n_tot = n_p + db
    n_pad = -(-(n_tot + -(-3 * MOE_ROWS // TOP_K)) // TOKEN_TILE) * TOKEN_TILE
    l = 0

    row = lambda a: a.reshape(1, -1)
    seg = _head_seg()
    prep_params = (row(rwkv_mu[l]), row(w0[l]), w_decay_up[l], row(a0[l]), w_a_up[l], w_g_up[l],
                   row(k_k[l]), row(k_a[l]), row(r_k[l]), seg)
    wr_pad = jnp.pad(w_router[l], ((0, 0), (0, LANES - N_EXPERTS))).astype(BF16)
    br_pad = jnp.pad(b_router[l], (0, LANES - N_EXPERTS), constant_values=NEG).reshape(1, LANES)
    post_small = (row(gn_w[l]), row(gn_b[l]), seg)
    post_tail = (row(ln2[l]), wr_pad, br_pad)

    w_in_bf = w_in[l].astype(BF16)
    consts = (post_small + (w_branch_a[l].astype(BF16), w_branch_b[l].astype(BF16), w_out[l].astype(BF16))
              + post_tail)

    xp = x_prompt.reshape(n_p, D_MODEL)
    q_p, k_p, v_p, sg_p, kt_p, vt_p, shift_p, *prep_p = _inproj_prompt(xp, row(ln1[l]), w_in_bf, prep_params,
                                                                     tm=TOKEN_TILE, seq=seq)
    as3 = lambda a: a.reshape(bp, seq, -1)
    oa_p = _attn_prompt(as3(q_p), as3(k_p), as3(v_p), _prompt_bias(rel_bias))
    r_, w_, k2_, vr_, kk_, ka_, bonus_p, g_p = prep_p
    s0_p = jnp.zeros((bp, R_HEADS, R_HEAD, R_HEAD), F32)
    y_p, wkv_p = _lane_scan(r_, w_, k2_, vr_, kk_, ka_, s0_p, tc=SCAN_CHUNK)
    flat = lambda a: a.reshape(n_p, -1)
    bufs = _post(xp, flat(y_p), flat(bonus_p), flat(g_p), flat(oa_p), sg_p, consts,
                 tm=TOKEN_TILE, n_total=n_pad)

    xs = x_sample.reshape(db, D_MODEL)
    q_s, k_s, v_s, rc_s, sg_s = _inproj(xs, row(ln1[l]), w_in_bf, tm=db)
    bias_s, b0_s = _sample_bias(rel_bias, cache_k.shape[2])
    oa_s = _attn_sample(q_s, k_s, v_s, cache_k[l], cache_v[l], bias_s, b0_s)
    prep_s = _rwkv_prep_sample(rc_s, state_shift[l], prep_params)
    sr, sw, sk2, svr, skk, ska, bonus_s, g_s = [a[:, None] for a in prep_s]
    y_s, wkv_s = _rwkv_scan(sr, sw, sk2, svr, skk, ska, state_wkv[l], bb=SCAN_BATCH, tc=1)
    x1_all, h2_all, ti_all, tg_all = _post(xs, y_s[:, 0], bonus_s[:, 0], g_s[:, 0], oa_s, sg_s, consts,
                                           tm=db, n_total=n_pad, row0=n_p, into=bufs)

    block_e, ids = _route(ti_all[:n_tot, :TOP_K], n_pad, MOE_ROWS)
    ys = _moe_experts(block_e, ids, h2_all, w_gate_up[l], b_gate_up[l], w_down[l], b_down[l],
                      rows=MOE_ROWS, plane_rows=n_pad, n_tok=n_tot)
    lnf = row(ln_f)
    y_prompt = _combine(ys, x1_all, tg_all, lnf, tm=TOKEN_TILE, n=n_p, row0=0)
    y_sample = _combine(ys, x1_all, tg_all, lnf, tm=db, n=db, row0=n_p)

    heads = lambda a, b_: a.reshape(1, b_, -1, A_HEADS, HEAD_DIM)
    return (y_prompt.reshape(bp, seq, D_MODEL), y_sample.reshape(db, 1, D_MODEL),
            kt_p.transpose(0, 3, 1, 2)[None], vt_p.transpose(0, 3, 1, 2)[None], wkv_p[None], shift_p[:, 0][None],
            heads(k_s, db), heads(v_s, db), wkv_s[None], rc_s[None])
```

```python
import functools
import math

import jax
import jax.numpy as jnp
from jax import lax
from jax.experimental import pallas as pl
from jax.experimental.pallas import tpu as pltpu

F32 = jnp.float32
BF16 = jnp.bfloat16

D_MODEL = 1024
A_HEADS = 8
HEAD_DIM = 64
A_WIDTH = A_HEADS * HEAD_DIM
PATTERNS = ((128, 1), (512, 4), (2048, 16))
BLOCK = 128
NUM_BUCKETS = 32
MAX_DISTANCE = 2048
SCALE = HEAD_DIM ** -0.5
NEG = -1e30
R_HEADS = 8
R_HEAD = 64
R_WIDTH = R_HEADS * R_HEAD
DECAY_LORA = 64
AAA_LORA = 64
GATE_LORA = 128
GN_EPS = 64e-5
N_RWKV_COLS = 3 * R_WIDTH + DECAY_LORA + AAA_LORA + GATE_LORA
N_IN_COLS = 3 * A_WIDTH + N_RWKV_COLS + 2 * D_MODEL
N_EXPERTS = 32
TOP_K = 4
SWIGLU_LIMIT = 7.0
SWIGLU_ALPHA = 1.702
RMS_EPS = 1e-6

LANES = 128
SUBLANES = 8
VMEM_LIMIT = 56 * 1024 * 1024
TOKEN_TILE = 256
MOE_ROWS = 512
SCAN_BATCH = 2
SCAN_CHUNK = 64


def _cparams(*sem):
    return pltpu.CompilerParams(dimension_semantics=sem, vmem_limit_bytes=VMEM_LIMIT)


def _rms(x, g):
    return x * lax.rsqrt(jnp.mean(x * x, axis=-1, keepdims=True) + RMS_EPS) * g


def _dot(a, b):
    return jnp.dot(a.astype(BF16), b.astype(BF16), preferred_element_type=F32)


def _bf(x):
    return x.astype(BF16).astype(F32)


def _split3(x):
    hi = x.astype(BF16)
    r1 = x - hi.astype(F32)
    mid = r1.astype(BF16)
    lo = (r1 - mid.astype(F32)).astype(BF16)
    return hi, mid, lo


def _segdot(x, seg):
    hi, mid, lo = _split3(x)
    d = lambda p: jnp.dot(p, seg, preferred_element_type=F32)
    return d(hi) + d(mid) + d(lo)


def _head_sums(x, seg):
    return jnp.concatenate([_segdot(x[:, j:j + LANES], seg) for j in range(0, x.shape[1], LANES)], axis=1)


TILE_ROWS = D_MODEL // LANES
assert TILE_ROWS == SUBLANES


def _load_token_tiles(ref, n, *lead):
    return jnp.concatenate([ref[(*lead, pl.ds(p, n, stride=TILE_ROWS), slice(None))] for p in range(TILE_ROWS)], axis=1)


def _store_token_tiles(ref, x):
    n = x.shape[0]
    for p in range(TILE_ROWS):
        ref[pl.ds(p, n, stride=TILE_ROWS), :] = x[:, p * LANES:(p + 1) * LANES]


def _project(x_ref, g_ref, w_ref):
    h = _rms(x_ref[...], g_ref[...]).astype(BF16)
    return lambda c0, c1: _dot(h, w_ref[:, c0:c1])


C_RWKV = 3 * A_WIDTH
C_GATE = C_RWKV + N_RWKV_COLS


def _inproj_kernel(x_ref, g_ref, w_ref, q_ref, k_ref, v_ref, rc_ref, sg_ref):
    mm = _project(x_ref, g_ref, w_ref)
    q_ref[...] = mm(0, A_WIDTH)
    k_ref[...] = mm(A_WIDTH, 2 * A_WIDTH)
    v_ref[...] = mm(2 * A_WIDTH, C_RWKV)
    rc_ref[...] = mm(C_RWKV, C_GATE)
    sg_ref[:, :D_MODEL] = jax.nn.sigmoid(mm(C_GATE, C_GATE + D_MODEL))
    sg_ref[:, D_MODEL:] = jax.nn.sigmoid(mm(C_GATE + D_MODEL, N_IN_COLS))


def _inproj(x, ln1, w, *, tm):
    n = x.shape[0]
    row = lambda c: pl.BlockSpec((tm, c), lambda i: (i, 0))
    full = lambda a: pl.BlockSpec(a.shape, lambda i: (0,) * a.ndim)
    out_cols = (A_WIDTH, A_WIDTH, A_WIDTH, N_RWKV_COLS, 2 * D_MODEL)
    return pl.pallas_call(
        _inproj_kernel,
        grid=(n // tm,),
        in_specs=[row(D_MODEL), full(ln1), full(w)],
        out_specs=[row(c) for c in out_cols],
        out_shape=[jax.ShapeDtypeStruct((n, c), F32) for c in out_cols],
        compiler_params=_cparams("parallel"),
        name="inproj",
    )(x, ln1, w)


N_PREP_PARAMS = 10
N_PREP_OUTS = 8


def _inproj_prompt_kernel(x_ref, g_ref, w_ref, *refs, per):
    params = [r[...] for r in refs[:N_PREP_PARAMS]]
    q_ref, k_ref, v_ref, sg_ref, kt_ref, vt_ref, shift_ref = refs[N_PREP_PARAMS:N_PREP_PARAMS + 7]
    prep_refs = refs[N_PREP_PARAMS + 7:-1]
    carry = refs[-1]
    mm = _project(x_ref, g_ref, w_ref)
    q_ref[...] = mm(0, A_WIDTH)
    k = mm(A_WIDTH, 2 * A_WIDTH)
    v = mm(2 * A_WIDTH, C_RWKV)
    k_ref[...] = k
    v_ref[...] = v
    tm = k.shape[0]
    kt_ref[0] = k.T.reshape(A_HEADS, HEAD_DIM, tm)
    vt_ref[0] = v.T.reshape(A_HEADS, HEAD_DIM, tm)
    sg_ref[:, :D_MODEL] = jax.nn.sigmoid(mm(C_GATE, C_GATE + D_MODEL))
    sg_ref[:, D_MODEL:] = jax.nn.sigmoid(mm(C_GATE + D_MODEL, N_IN_COLS))

    @pl.when(lax.rem(pl.program_id(0), per) == 0)
    def _():
        carry[...] = jnp.zeros(carry.shape, carry.dtype)

    rc = mm(C_RWKV, C_GATE)
    row = lax.broadcasted_iota(jnp.int32, (tm, 1), 0)
    prev = jnp.where(row == 0, carry[...], pltpu.roll(rc, 1, axis=0))
    for o, val in zip(prep_refs, _rwkv_prep_math(rc, prev, *params)):
        o[0] = val
    last = rc[tm - 1:tm, :]
    carry[...] = last
    shift_ref[0] = last


def _inproj_prompt(x, ln1, w, prep_params, *, tm, seq):
    n = x.shape[0]
    per = seq // tm
    row = lambda c: pl.BlockSpec((tm, c), lambda i: (i, 0))
    full = lambda a: pl.BlockSpec(a.shape, lambda i: (0,) * a.ndim)
    t_spec = pl.BlockSpec((1, A_HEADS, HEAD_DIM, tm), lambda i: (i // per, 0, 0, i % per))
    t_shape = jax.ShapeDtypeStruct((n // seq, A_HEADS, HEAD_DIM, seq), F32)
    cols = (A_WIDTH, A_WIDTH, A_WIDTH, 2 * D_MODEL)
    out_specs = ([row(c) for c in cols] + [t_spec, t_spec, pl.BlockSpec((1, 1, N_RWKV_COLS), lambda i: (i // per, 0, 0))]
                 + [pl.BlockSpec((1, tm, R_WIDTH), lambda i: (i // per, i % per, 0))] * N_PREP_OUTS)
    out_shape = ([jax.ShapeDtypeStruct((n, c), F32) for c in cols]
                 + [t_shape, t_shape, jax.ShapeDtypeStruct((n // seq, 1, N_RWKV_COLS), F32)]
                 + [jax.ShapeDtypeStruct((n // seq, seq, R_WIDTH), F32)] * N_PREP_OUTS)
    return pl.pallas_call(
        functools.partial(_inproj_prompt_kernel, per=per),
        grid=(n // tm,),
        in_specs=[row(D_MODEL), full(ln1), full(w)] + [full(a) for a in prep_params],
        out_specs=out_specs,
        out_shape=out_shape,
        scratch_shapes=[pltpu.VMEM((1, N_RWKV_COLS), F32)],
        compiler_params=_cparams("arbitrary"),
        name="inproj_prompt",
    )(x, ln1, w, *prep_params)


def _t5_bucket(dist):
    max_exact = NUM_BUCKETS // 2
    d = jnp.maximum(dist, 1).astype(F32)
    large = max_exact + (jnp.log(d / max_exact) / math.log(MAX_DISTANCE / max_exact)
                         * (NUM_BUCKETS - max_exact)).astype(jnp.int32)
    large = jnp.minimum(large, NUM_BUCKETS - 1)
    return jnp.where(dist < max_exact, dist, large)


def _bias_lookup(rel_bias, dist):
    onehot = (_t5_bucket(dist)[..., None] == jnp.arange(NUM_BUCKETS)).astype(F32)
    return jnp.moveaxis(jnp.dot(onehot, rel_bias.astype(F32), precision=lax.Precision.HIGHEST), -1, 0)


def _prompt_bias(rel_bias):
    i = jnp.arange(BLOCK)[:, None]
    j = jnp.arange(2 * BLOCK)[None, :]
    delta = i + BLOCK - j
    out = []
    for window, dil in PATTERNS:
        n = window // dil
        band = (delta >= 0) & (delta <= n)
        out.append(jnp.where(band[None], _bias_lookup(rel_bias, jnp.clip(delta, 0, n) * dil), NEG))
    return jnp.stack(out, 0)


def _sample_bias(rel_bias, wb):
    dist = wb - jnp.arange(wb)
    out = []
    for window, dil in PATTERNS:
        member = (dist % dil == 0) & (dist <= window)
        out.append(jnp.where(member[None], _bias_lookup(rel_bias, dist), NEG))
    b0 = _bias_lookup(rel_bias, jnp.zeros((1,), jnp.int32))
    return jnp.stack(out, 0), b0


def _attn_prompt_kernel(q_ref, k_ref, v_ref, bias_ref, o_ref, os_ref, ls_ref, *, seq):
    nt = (((1,), (1,)), ((), ()))
    npat = len(PATTERNS)
    w2 = 2 * HEAD_DIM
    lo = lax.broadcasted_iota(jnp.int32, (BLOCK, w2), 1) < HEAD_DIM
    key = lax.broadcasted_iota(jnp.int32, (2 * BLOCK, 2 * BLOCK), 1)

    for g, (window, dil) in enumerate(PATTERNS):
        nb = seq // (BLOCK * dil)
        sh = dil.bit_length() - 1

        def body(it, carry, g=g, dil=dil, nb=nb, sh=sh):
            r = it & (dil - 1)
            c = it >> sh
            start = c * (BLOCK * dil) + r
            rows = pl.ds(start, BLOCK, stride=dil) if dil > 1 else pl.ds(start, BLOCK)
            qb = q_ref[0, rows, :]
            kb = k_ref[0, rows, :]
            vb = v_ref[0, rows, :]
            if nb > 1:
                pstart = jnp.maximum(c - 1, 0) * (BLOCK * dil) + r
                prow = pl.ds(pstart, BLOCK, stride=dil) if dil > 1 else pl.ds(pstart, BLOCK)
                kp = k_ref[0, prow, :]
                vp = v_ref[0, prow, :]
            q2 = jnp.concatenate([jnp.where(lo, qb, 0.0), jnp.where(lo, 0.0, qb)], axis=0).astype(BF16)
            if nb > 1:
                k2 = jnp.concatenate([kp, kb], axis=0)
                v2 = jnp.concatenate([vp, vb], axis=0)
                bias = bias_ref[g, 0]
                valid = (bias > 0.5 * NEG) & ((key >= BLOCK) | (c > 0))
            else:
                k2, v2 = kb, vb
                bias = bias_ref[g, 0, :, BLOCK:]
                valid = bias > 0.5 * NEG
            s = lax.dot_general(q2, k2.astype(BF16), nt, preferred_element_type=F32)
            s = jnp.where(valid, s * SCALE + bias, NEG)
            m = jnp.max(s, axis=-1, keepdims=True)
            p = jnp.exp(s - m)
            l = jnp.sum(p, axis=-1, keepdims=True)
            o2 = jnp.dot((p * (1.0 / l)).astype(BF16), v2.astype(BF16), preferred_element_type=F32)
            lse = jnp.broadcast_to(m + jnp.log(l), (2 * BLOCK, w2))
            os_ref[g, rows, :] = jnp.where(lo, o2[:BLOCK], o2[BLOCK:])
            ls_ref[g, rows, :] = jnp.where(lo, lse[:BLOCK], lse[BLOCK:])
            return carry

        lax.fori_loop(0, dil * nb, body, 0, unroll=8)

    chunk = 256

    def merge(ch, carry):
        rows = pl.ds(pl.multiple_of(ch * chunk, chunk), chunk)
        lse = [ls_ref[g, rows, :] for g in range(npat)]
        top = functools.reduce(jnp.maximum, lse)
        e = [jnp.exp(x - top) for x in lse]
        inv = 1.0 / functools.reduce(lambda a, b_: a + b_, e)
        acc = jnp.zeros((chunk, 2 * HEAD_DIM), F32)
        for g in range(npat):
            acc = acc + (e[g] * inv) * os_ref[g, rows, :]
        o_ref[0, rows, :] = acc
        return carry

    lax.fori_loop(0, seq // chunk, merge, 0)


def _attn_prompt(q, k, v, bias):
    b, s, _ = q.shape
    w2 = 2 * HEAD_DIM
    qspec = pl.BlockSpec((1, s, w2), lambda i, p: (i, 0, p))
    return pl.pallas_call(
        functools.partial(_attn_prompt_kernel, seq=s),
        grid=(b, A_HEADS // 2),
        in_specs=[qspec, qspec, qspec,
                  pl.BlockSpec((len(PATTERNS), 1, 2 * BLOCK, 2 * BLOCK), lambda i, p: (0, p, 0, 0))],
        out_specs=qspec,
        out_shape=jax.ShapeDtypeStruct((b, s, A_WIDTH), F32),
        scratch_shapes=[pltpu.VMEM((len(PATTERNS), s, w2), F32)] * 2,
        compiler_params=_cparams("parallel", "parallel"),
        name="attn_prompt",
    )(q, k, v, bias.reshape(len(PATTERNS), A_HEADS // 2, 2 * BLOCK, 2 * BLOCK))


def _attn_sample_kernel(q_ref, kn_ref, vn_ref, kt_ref, vt_ref, bias_ref, b0_ref, o_ref):
    npat = len(PATTERNS)
    nt = (((1,), (1,)), ((), ()))
    q = q_ref[0]
    v_new = _bf(vn_ref[0])
    s0 = jnp.sum(_bf(q) * _bf(kn_ref[0]), axis=-1, keepdims=True) * SCALE + b0_ref[...]
    s = jnp.concatenate([_dot(q[h:h + 1], kt_ref[0, h]) for h in range(A_HEADS)], axis=0) * SCALE
    ps, p0s, lses = [], [], []
    for g in range(npat):
        bias = bias_ref[g]
        sg = jnp.where(bias > 0.5 * NEG, s + bias, NEG)
        m = jnp.maximum(jnp.max(sg, axis=-1, keepdims=True), s0)
        l = jnp.sum(jnp.exp(sg - m), axis=-1, keepdims=True) + jnp.exp(s0 - m)
        lse = m + jnp.log(l)
        ps.append(jnp.exp(sg - lse))
        p0s.append(_bf(jnp.exp(s0 - lse)))
        lses.append(lse)
    top = functools.reduce(jnp.maximum, lses)
    e = [jnp.exp(x - top) for x in lses]
    inv = 1.0 / functools.reduce(lambda a, b_: a + b_, e)
    w = [_bf(x * inv) for x in e]
    rows = []
    for h in range(A_HEADS):
        p_h = jnp.concatenate([p[h:h + 1] for p in ps], axis=0)
        o_h = lax.dot_general(p_h.astype(BF16), vt_ref[0, h].astype(BF16), nt, preferred_element_type=F32)
        acc = jnp.zeros((1, HEAD_DIM), F32)
        for g in range(npat):
            acc = acc + w[g][h:h + 1] * _bf(o_h[g:g + 1] + p0s[g][h:h + 1] * v_new[h:h + 1])
        rows.append(acc)
    o_ref[0] = jnp.concatenate(rows, axis=0)


def _attn_sample(q, k_new, v_new, cache_k, cache_v, bias, b0):
    db, wb = cache_k.shape[:2]
    heads = lambda a: a.reshape(db, A_HEADS, HEAD_DIM)
    vec = pl.BlockSpec((1, A_HEADS, HEAD_DIM), lambda i: (i, 0, 0))
    full = lambda a: pl.BlockSpec(a.shape, lambda i: (0,) * a.ndim)
    rows_minor = lambda a: a.transpose(0, 2, 3, 1)
    cache = pl.BlockSpec((1, A_HEADS, HEAD_DIM, wb), lambda i: (i, 0, 0, 0))
    args = [heads(q), heads(k_new), heads(v_new), rows_minor(cache_k), rows_minor(cache_v), bias, b0]
    return pl.pallas_call(
        _attn_sample_kernel,
        grid=(db,),
        in_specs=[vec, vec, vec, cache, cache, full(bias), full(b0)],
        out_specs=vec,
        out_shape=jax.ShapeDtypeStruct((db, A_HEADS, HEAD_DIM), F32),
        compiler_params=_cparams("parallel"),
        name="attn_sample",
    )(*args).reshape(db, A_WIDTH)


def _rwkv_prep_math(rc, prev, mu, w0, wd, a0, wa, wg, k_k, k_a, r_k, seg):
    xr = rc + (prev - rc) * mu
    o1, o2, o3 = R_WIDTH, 2 * R_WIDTH, 3 * R_WIDTH
    r = xr[:, :o1]
    kr = xr[:, o1:o2]
    vr = xr[:, o2:o3]
    xw = xr[:, o3:o3 + DECAY_LORA]
    xa = xr[:, o3 + DECAY_LORA:o3 + DECAY_LORA + AAA_LORA]
    xg = xr[:, o3 + DECAY_LORA + AAA_LORA:]
    z = -(w0 + _dot(jnp.tanh(xw), wd))
    softplus = jnp.maximum(z, 0.0) + jnp.log(1.0 + jnp.exp(-jnp.abs(z)))
    decay = jnp.exp(-jnp.exp(-softplus - 0.5))
    a = jax.nn.sigmoid(a0 + _dot(xa, wa))
    g = _dot(jax.nn.sigmoid(xg), wg)
    kk = kr * k_k
    k2 = kr * (1.0 + (a - 1.0) * k_a)
    kk = kk / jnp.maximum(jnp.sqrt(_head_sums(kk * kk, seg)), 1e-12)
    bonus = _head_sums(r * k2 * r_k, seg) * vr
    return _bf(r), decay, k2, vr, _bf(kk), kk * a, bonus, g


def _rwkv_prep_sample_kernel(rc_ref, prev_ref, *refs):
    p = [x[...] for x in refs[:N_PREP_PARAMS]]
    outs = refs[N_PREP_PARAMS:]
    for o, val in zip(outs, _rwkv_prep_math(rc_ref[...], prev_ref[...], *p)):
        o[...] = val


def _head_seg():
    head = jnp.arange(LANES) // R_HEAD
    return (head[:, None] == head[None, :]).astype(BF16)


def _rwkv_prep_sample(rc, prev, params):
    n = rc.shape[0]
    full = lambda a: pl.BlockSpec(a.shape, lambda i: (0,) * a.ndim)
    return pl.pallas_call(
        _rwkv_prep_sample_kernel,
        grid=(1,),
        in_specs=[full(rc), full(prev)] + [full(a) for a in params],
        out_specs=[pl.BlockSpec((n, R_WIDTH), lambda i: (0, 0))] * 8,
        out_shape=[jax.ShapeDtypeStruct((n, R_WIDTH), F32)] * 8,
        compiler_params=_cparams("arbitrary"),
        name="rwkv_prep_sample",
    )(rc, prev, *params)


def _seg_lane_sum(x, lo_mask):
    lo = jnp.sum(jnp.where(lo_mask, x, 0.0), axis=-1, keepdims=True)
    hi = jnp.sum(jnp.where(lo_mask, 0.0, x), axis=-1, keepdims=True)
    return jnp.where(lo_mask, lo, hi)


def _rwkv_scan_kernel(r_ref, w_ref, k_ref, v_ref, kk_ref, ka_ref, s0_ref, y_ref, sT_ref, st_ref, *, bb, tc):
    pairs = R_HEADS // 2
    w2 = 2 * R_HEAD

    @pl.when(pl.program_id(1) == 0)
    def _():
        for b in range(bb):
            for p in range(pairs):
                st_ref[b, p] = jnp.concatenate([s0_ref[b, 2 * p], s0_ref[b, 2 * p + 1]], axis=1)

    lane = lax.broadcasted_iota(jnp.int32, (R_HEAD, w2), 1)
    sub = lax.broadcasted_iota(jnp.int32, (R_HEAD, w2), 0)
    lo_mask = lane < R_HEAD
    eye2 = (lane & (R_HEAD - 1)) == sub

    grp = min(8, tc)

    def group(tg, carry):
        rows = pl.ds(pl.multiple_of(tg * grp, grp), grp)
        for b in range(bb):
            for p in range(pairs):
                cols = slice(p * w2, (p + 1) * w2)
                r8, w8, k8, v8, kk8, ka8 = (ref[b, rows, cols] for ref in (r_ref, w_ref, k_ref, v_ref, kk_ref, ka_ref))
                s = st_ref[b, p]
                sb = _bf(s)
                ys = []
                for j in range(grp):
                    row = lambda a: a[j:j + 1, :]
                    sa = -_seg_lane_sum(sb * row(kk8), lo_mask)
                    vcol = _seg_lane_sum(jnp.where(eye2, jnp.broadcast_to(row(v8), (R_HEAD, w2)), 0.0), lo_mask)
                    s = s * row(w8) + sa * row(ka8) + vcol * row(k8)
                    sb = _bf(s)
                    yfull = _seg_lane_sum(sb * row(r8), lo_mask)
                    ys.append(jnp.sum(jnp.where(eye2, yfull, 0.0), axis=0, keepdims=True))
                st_ref[b, p] = s
                y_ref[b, rows, cols] = jnp.concatenate(ys, axis=0) if grp > 1 else ys[0]
        return carry

    lax.fori_loop(0, tc // grp, group, 0)

    @pl.when(pl.program_id(1) == pl.num_programs(1) - 1)
    def _():
        for b in range(bb):
            for p in range(pairs):
                s = st_ref[b, p]
                sT_ref[b, 2 * p] = s[:, :R_HEAD]
                sT_ref[b, 2 * p + 1] = s[:, R_HEAD:]


def _rwkv_scan(r, w, k, v, kk, ka, s0, *, bb, tc):
    b, t, _ = r.shape
    seq = pl.BlockSpec((bb, tc, R_WIDTH), lambda i, j: (i, j, 0))
    state = pl.BlockSpec((bb, R_HEADS, R_HEAD, R_HEAD), lambda i, j: (i, 0, 0, 0))
    return pl.pallas_call(
        functools.partial(_rwkv_scan_kernel, bb=bb, tc=tc),
        grid=(b // bb, t // tc),
        in_specs=[seq] * 6 + [state],
        out_specs=[seq, state],
        out_shape=[jax.ShapeDtypeStruct((b, t, R_WIDTH), F32),
                   jax.ShapeDtypeStruct((b, R_HEADS, R_HEAD, R_HEAD), F32)],
        scratch_shapes=[pltpu.VMEM((bb, R_HEADS // 2, R_HEAD, 2 * R_HEAD), F32)],
        compiler_params=_cparams("parallel", "arbitrary"),
        name="rwkv_scan",
    )(r, w, k, v, kk, ka, s0)


CHAINS = LANES // 2
K2 = R_HEAD // 2


def _chain_rows(x):
    b, t, _ = x.shape
    x = x.reshape(b, t, R_HEADS, 2, K2).transpose(1, 4, 3, 0, 2)
    return x.reshape(t, K2, LANES)


def _lane_scan_kernel(kk_ref, w_ref, ka_ref, kx_ref, r_ref, v_ref, s0_ref, y_ref, sT_ref, s_ref, *, tc):
    lo = lax.broadcasted_iota(jnp.int32, (K2, LANES), 1) < CHAINS

    @pl.when(pl.program_id(0) == 0)
    def _():
        for k2 in range(K2):
            s_ref[k2] = s0_ref[k2]

    def both_halves(x):
        return x + pltpu.roll(x, CHAINS, axis=1)

    nhalf = 2
    vh = R_HEAD // nhalf

    def row(ref, t, k2):
        return jnp.broadcast_to(ref[t, k2:k2 + 1, :], (vh, LANES))

    acc0 = []
    for hf in range(nhalf):
        vs = slice(hf * vh, (hf + 1) * vh)
        a = jnp.zeros((vh, LANES), F32)
        for k2 in range(K2):
            a = a + _bf(s_ref[k2, vs, :]) * row(kk_ref, 0, k2)
        acc0.append(a)

    def step(t, acc):
        tn = jnp.minimum(t + 1, tc - 1)
        nxt, ys = [], []
        v_t = v_ref[t]
        v_sw = pltpu.roll(v_t, CHAINS, axis=1)
        for hf in range(nhalf):
            vs = slice(hf * vh, (hf + 1) * vh)
            vv = jnp.where(lo, v_t, v_sw) if hf == 0 else jnp.where(lo, v_sw, v_t)
            sa = -both_halves(acc[hf])
            yacc = jnp.zeros((vh, LANES), F32)
            nacc = jnp.zeros((vh, LANES), F32)
            for k2 in range(K2):
                s = s_ref[k2, vs, :] * row(w_ref, t, k2) + sa * row(ka_ref, t, k2) + vv * row(kx_ref, t, k2)
                s_ref[k2, vs, :] = s
                sb = _bf(s)
                yacc = yacc + sb * row(r_ref, t, k2)
                nacc = nacc + sb * row(kk_ref, tn, k2)
            ys.append(both_halves(yacc))
            nxt.append(nacc)
        y_ref[t] = jnp.where(lo, ys[0], ys[1])
        return tuple(nxt)

    lax.fori_loop(0, tc, step, tuple(acc0))

    @pl.when(pl.program_id(0) == pl.num_programs(0) - 1)
    def _():
        for k2 in range(K2):
            sT_ref[k2] = s_ref[k2]


def _lane_scan(r, w, k, v, kk, ka, s0, *, tc):
    b, t, _ = r.shape
    assert b * R_HEADS == CHAINS
    ops = [_chain_rows(x) for x in (kk, w, ka, k, r, v)]
    s0c = s0.reshape(b, R_HEADS, 2, K2, 2, K2).transpose(5, 2, 3, 4, 0, 1).reshape(K2, R_HEAD, LANES)
    rows = pl.BlockSpec((tc, K2, LANES), lambda i: (i, 0, 0))
    state = pl.BlockSpec((K2, R_HEAD, LANES), lambda i: (0, 0, 0))
    y, st = pl.pallas_call(
        functools.partial(_lane_scan_kernel, tc=tc),
        grid=(t // tc,),
        in_specs=[rows] * 6 + [state],
        out_specs=[rows, state],
        out_shape=[jax.ShapeDtypeStruct((t, K2, LANES), F32), jax.ShapeDtypeStruct((K2, R_HEAD, LANES), F32)],
        scratch_shapes=[pltpu.VMEM((K2, R_HEAD, LANES), F32)],
        compiler_params=_cparams("arbitrary"),
        name="rwkv_lane_scan",
    )(*ops, s0c)
    y = y.reshape(t, K2, 2, b, R_HEADS).transpose(3, 0, 4, 2, 1).reshape(b, t, R_WIDTH)
    st = st.reshape(K2, 2, K2, 2, b, R_HEADS).transpose(4, 5, 1, 2, 3, 0).reshape(b, R_HEADS, R_HEAD, R_HEAD)
    return y, st


def _post_kernel(x_ref, y_ref, bonus_ref, g_ref, oa_ref, sg_ref, gnw_ref, gnb_ref, seg_ref, wba_ref, wbb_ref,
                 wout_ref, ln2_ref, wr_ref, br_ref, *rest, aliased, n_main):
    outs = rest[aliased:]
    x1_ref, h2_ref, ti_ref, tg_ref = outs

    @pl.when(pl.program_id(0) >= n_main)
    def _():
        for o in outs:
            o[...] = jnp.zeros(o.shape, o.dtype)

    @pl.when(pl.program_id(0) < n_main)
    def _():
        _post_body(x_ref, y_ref, bonus_ref, g_ref, oa_ref, sg_ref, gnw_ref, gnb_ref, seg_ref, wba_ref, wbb_ref,
                   wout_ref, ln2_ref, wr_ref, br_ref, x1_ref, h2_ref, ti_ref, tg_ref)


def _post_body(x_ref, y_ref, bonus_ref, g_ref, oa_ref, sg_ref, gnw_ref, gnb_ref, seg_ref, wba_ref, wbb_ref,
               wout_ref, ln2_ref, wr_ref, br_ref, x1_ref, h2_ref, ti_ref, tg_ref):
    y = y_ref[...]
    seg = seg_ref[...]
    mu = _head_sums(y, seg) * (1.0 / R_HEAD)
    yc = y - mu
    var = _head_sums(yc * yc, seg) * (1.0 / R_HEAD)
    yn = yc * lax.rsqrt(var + GN_EPS) * gnw_ref[...] + gnb_ref[...]
    o_b = (yn + bonus_ref[...]) * g_ref[...]
    mixed = (sg_ref[:, :D_MODEL] * _dot(oa_ref[...], wba_ref[...])
             + sg_ref[:, D_MODEL:] * _dot(o_b, wbb_ref[...]))
    x1 = x_ref[...] + _dot(mixed, wout_ref[...])
    x1_ref[...] = x1
    h2 = _rms(x1, ln2_ref[...])
    _store_token_tiles(h2_ref, h2)
    logits = _dot(h2, wr_ref[...]) + br_ref[...]
    lane = lax.broadcasted_iota(jnp.int32, logits.shape, 1).astype(F32)
    work = logits
    vals, idxs = [], []
    for _ in range(TOP_K):
        m = jnp.max(work, axis=-1, keepdims=True)
        idx = jnp.min(jnp.where(work == m, lane, float(LANES)), axis=-1, keepdims=True)
        vals.append(m)
        idxs.append(idx)
        work = jnp.where(lane == idx, -jnp.inf, work)
    es = [jnp.exp(v - vals[0]) for v in vals]
    tot = es[0] + es[1] + es[2] + es[3]
    ti = jnp.zeros(logits.shape, F32)
    tg = jnp.zeros(logits.shape, F32)
    for kslot in range(TOP_K):
        ti = jnp.where(lane == float(kslot), idxs[kslot], ti)
        tg = jnp.where(lane == float(kslot), es[kslot] / tot, tg)
    ti_ref[...] = ti.astype(jnp.int32)
    tg_ref[...] = tg


def _post(x, y, bonus, g, oa, sg, consts, *, tm, n_total, row0=0, into=None):
    n = x.shape[0]
    blk0 = row0 // tm
    n_main = n // tm
    steps = n_total // tm if into is None else n_main
    row = lambda c: pl.BlockSpec((tm, c), lambda i: (jnp.minimum(i, n_main - 1), 0))
    orow = lambda c: pl.BlockSpec((tm, c), lambda i: (i + blk0, 0))
    otile = pl.BlockSpec((tm * TILE_ROWS, LANES), lambda i: (i + blk0, 0))
    full = lambda a: pl.BlockSpec(a.shape, lambda i: (0,) * a.ndim)
    ins = [x, y, bonus, g, oa, sg, *consts]
    in_specs = [row(D_MODEL), row(R_WIDTH), row(R_WIDTH), row(R_WIDTH), row(A_WIDTH), row(2 * D_MODEL)]
    in_specs += [full(a) for a in consts]
    aliases = {}
    if into is not None:
        aliases = {len(ins) + i: i for i in range(len(into))}
        in_specs += [pl.BlockSpec(memory_space=pl.ANY)] * len(into)
        ins += list(into)
    return pl.pallas_call(
        functools.partial(_post_kernel, aliased=len(aliases), n_main=n_main),
        grid=(steps,),
        in_specs=in_specs,
        out_specs=[orow(D_MODEL), otile, orow(LANES), orow(LANES)],
        out_shape=[jax.ShapeDtypeStruct((n_total, D_MODEL), F32),
                   jax.ShapeDtypeStruct((n_total * TILE_ROWS, LANES), F32),
                   jax.ShapeDtypeStruct((n_total, LANES), jnp.int32),
                   jax.ShapeDtypeStruct((n_total, LANES), F32)],
        input_output_aliases=aliases,
        compiler_params=_cparams("parallel"),
        name="post",
    )(*ins)


def _start_rows(rows_ref, hbm, buf, sem, *, rows, to_hbm):
    for r in range(rows):
        at = pl.ds(pl.multiple_of(rows_ref[0, 0, r], TILE_ROWS), TILE_ROWS)
        here = pl.ds(r * TILE_ROWS, TILE_ROWS)
        prio = r % 2
        if to_hbm:
            pltpu.make_async_copy(buf.at[here], hbm.at[at], sem).start(priority=prio)
        else:
            pltpu.make_async_copy(hbm.at[at], buf.at[here], sem).start(priority=prio)


def _wait_rows(hbm, buf, sem, *, rows, to_hbm):
    whole = hbm.at[pl.ds(0, rows * TILE_ROWS)]
    (pltpu.make_async_copy(buf, whole, sem) if to_hbm else pltpu.make_async_copy(whole, buf, sem)).wait()


def _expert_mlp(xb, wgu_bf, wd_bf, bgu_ref, bd_ref, rows):
    acc = jnp.broadcast_to(bd_ref[0], (rows, D_MODEL))
    cw = 512
    for c in range(D_MODEL // cw):
        gs = slice(c * cw, (c + 1) * cw)
        us = slice(D_MODEL + c * cw, D_MODEL + (c + 1) * cw)
        gt = jnp.dot(xb, wgu_bf[:, gs], preferred_element_type=F32) + bgu_ref[0, :, gs]
        up = jnp.dot(xb, wgu_bf[:, us], preferred_element_type=F32) + bgu_ref[0, :, us]
        gt = jnp.minimum(gt, SWIGLU_LIMIT)
        up = jnp.clip(up, -SWIGLU_LIMIT, SWIGLU_LIMIT)
        act = (up + 1.0) * (gt * jax.nn.sigmoid(gt * SWIGLU_ALPHA))
        acc = acc + jnp.dot(act.astype(BF16), wd_bf[gs, :], preferred_element_type=F32)
    return acc


def _moe_kernel(be_ref, src_cur, src_nxt, dst_prv, dst_cur, h_hbm, wgu_ref, bgu_ref, wd_ref, bd_ref, ys_hbm,
                x0, x1, y0, y1, wgu_bf, wd_bf, gsem, ssem, *, rows, plane_rows, n_tok):
    i = pl.program_id(0)
    last = pl.num_programs(0) - 1

    changed = jnp.logical_or(i == 0, be_ref[i] != be_ref[jnp.maximum(i - 1, 0)])

    @pl.when(changed)
    def _():
        step = 128

        def cast(j, c):
            rs = pl.ds(pl.multiple_of(j * step, step), step)
            wgu_bf[rs, :] = wgu_ref[0, rs, :].astype(BF16)
            wd_bf[rs, :] = wd_ref[0, rs, :].astype(BF16)
            return c

        lax.fori_loop(0, D_MODEL // step, cast, 0)

    def run(par):
        xc, xn = (x0, x1) if par == 0 else (x1, x0)
        yc, yp = (y0, y1) if par == 0 else (y1, y0)
        gather = functools.partial(_start_rows, hbm=h_hbm, rows=rows, to_hbm=False)
        scatter = functools.partial(_start_rows, hbm=ys_hbm, rows=rows, to_hbm=True)
        gathered = functools.partial(_wait_rows, h_hbm, rows=rows, to_hbm=False)
        scattered = functools.partial(_wait_rows, ys_hbm, rows=rows, to_hbm=True)

        @pl.when(i == 0)
        def _():
            yp[...] = jnp.zeros(yp.shape, yp.dtype)
            spare = plane_rows - n_tok
            fills = [pltpu.make_async_copy(yp.at[pl.ds(0, spare * TILE_ROWS)],
                                           ys_hbm.at[pl.ds((k * plane_rows + n_tok) * TILE_ROWS, spare * TILE_ROWS)],
                                           ssem.at[1 - par])
                     for k in range(TOP_K)]
            for d in fills:
                d.start()
            for d in fills:
                d.wait()
            gather(src_cur, buf=xc, sem=gsem.at[par])

        gathered(xc, gsem.at[par])
        gather(src_nxt, buf=xn, sem=gsem.at[1 - par])

        @pl.when(i >= 1)
        def _():
            scattered(yc, ssem.at[par])

        used = i < be_ref[last + 1]

        @pl.when(used)
        def _():
            scatter(dst_prv, buf=yp, sem=ssem.at[1 - par])
            xb = _load_token_tiles(xc, rows).astype(BF16)
            _store_token_tiles(yc, _expert_mlp(xb, wgu_bf, wd_bf, bgu_ref, bd_ref, rows))

        @pl.when(jnp.logical_not(used))
        def _():
            scatter(dst_prv, buf=yp, sem=ssem.at[1 - par])

        @pl.when(i == last)
        def _():
            scatter(dst_cur, buf=yc, sem=ssem.at[par])
            gathered(xn, gsem.at[1 - par])
            scattered(yp, ssem.at[1 - par])
            scattered(yc, ssem.at[par])

    for par in range(2):
        pl.when(lax.rem(i, 2) == par)(functools.partial(run, par))


def _moe_experts(block_e, ids, h2, w_gate_up, b_gate_up, w_down, b_down, *, rows, plane_rows, n_tok):
    n_blocks = ids.shape[0] - 2
    ids = lax.optimization_barrier(ids)
    tok = lax.shift_right_logical(ids, 2)
    src = tok * TILE_ROWS
    dst = ((ids & (TOP_K - 1)) * plane_rows + tok) * TILE_ROWS
    idblk = lambda off: pl.BlockSpec((1, 1, rows), lambda i, be: (i + off, 0, 0), memory_space=pltpu.SMEM)
    grid_spec = pltpu.PrefetchScalarGridSpec(
        num_scalar_prefetch=1,
        grid=(n_blocks,),
        in_specs=[
            idblk(1), idblk(2), idblk(0), idblk(1),
            pl.BlockSpec(memory_space=pl.ANY),
            pl.BlockSpec((1, D_MODEL, 2 * D_MODEL), lambda i, be: (be[i], 0, 0)),
            pl.BlockSpec((1, 1, 2 * D_MODEL), lambda i, be: (be[i], 0, 0)),
            pl.BlockSpec((1, D_MODEL, D_MODEL), lambda i, be: (be[i], 0, 0)),
            pl.BlockSpec((1, 1, D_MODEL), lambda i, be: (be[i], 0, 0)),
        ],
        out_specs=pl.BlockSpec(memory_space=pl.ANY),
        scratch_shapes=[pltpu.VMEM((rows * TILE_ROWS, LANES), F32)] * 4 + [
            pltpu.VMEM((D_MODEL, 2 * D_MODEL), BF16),
            pltpu.VMEM((D_MODEL, D_MODEL), BF16),
            pltpu.SemaphoreType.DMA((2,)),
            pltpu.SemaphoreType.DMA((2,))],
    )
    return pl.pallas_call(
        functools.partial(_moe_kernel, rows=rows, plane_rows=plane_rows, n_tok=n_tok),
        grid_spec=grid_spec,
        out_shape=jax.ShapeDtypeStruct((TOP_K * plane_rows * TILE_ROWS, LANES), F32),
        compiler_params=_cparams("arbitrary"),
        name="moe_experts",
    )(block_e, src, src, dst, dst, h2, w_gate_up, b_gate_up[:, None], w_down, b_down[:, None])


def _combine_kernel(ys_ref, x1_ref, gate_ref, lnf_ref, y_ref):
    acc = x1_ref[...]
    tm = acc.shape[0]
    for kslot in range(TOP_K):
        acc = acc + gate_ref[:, kslot:kslot + 1] * _load_token_tiles(ys_ref, tm, kslot)
    y_ref[...] = _rms(acc, lnf_ref[...])


def _combine(ys, x1, gates, ln_f, *, tm, n, row0):
    blk0 = row0 // tm
    return pl.pallas_call(
        _combine_kernel,
        grid=(n // tm,),
        in_specs=[
            pl.BlockSpec((TOP_K, tm * TILE_ROWS, LANES), lambda i: (0, i + blk0, 0)),
            pl.BlockSpec((tm, D_MODEL), lambda i: (i + blk0, 0)),
            pl.BlockSpec((tm, LANES), lambda i: (i + blk0, 0)),
            pl.BlockSpec(ln_f.shape, lambda i: (0, 0)),
        ],
        out_specs=pl.BlockSpec((tm, D_MODEL), lambda i: (i, 0)),
        out_shape=jax.ShapeDtypeStruct((n, D_MODEL), F32),
        compiler_params=_cparams("parallel"),
        name="combine",
    )(ys.reshape(TOP_K, -1, LANES), x1, gates, ln_f)


def _route(topi, n_pad, rows):
    n_tok = topi.shape[0]
    e_flat = topi.reshape(-1)
    nk = e_flat.shape[0]
    n_blocks = -(-(nk + N_EXPERTS * (rows - 1)) // rows)
    spare = n_pad - n_tok
    assert 3 * rows <= TOP_K * spare
    experts = jnp.arange(N_EXPERTS, dtype=jnp.int32)
    order = jnp.argsort(e_flat).astype(jnp.int32)
    sizes = jnp.sum((e_flat[:, None] == experts[None, :]).astype(jnp.int32), axis=0)
    padded = (sizes + rows - 1) // rows * rows
    pends = jnp.cumsum(padded)
    slot = jnp.arange(n_blocks * rows, dtype=jnp.int32)
    past = (slot[:, None] >= pends[None, :]).astype(jnp.int32)
    e_slot = jnp.sum(past, axis=1)
    off = slot - jnp.sum(past * padded[None, :], axis=1)
    src = jnp.sum(past * sizes[None, :], axis=1) + off
    size_e = jnp.sum((e_slot[:, None] == experts[None, :]) * sizes[None, :], axis=1)

    def pad_ids(d):
        return (TOP_K * (n_tok + d % spare) + d // spare).astype(jnp.int32)

    ids = jnp.where(off < size_e, order[jnp.clip(src, 0, nk - 1)], pad_ids((slot // rows) % 2 * rows + slot % rows))
    end_ids = pad_ids(2 * rows + jnp.arange(rows, dtype=jnp.int32))
    ids = jnp.concatenate([end_ids, ids, end_ids]).reshape(n_blocks + 2, 1, rows)
    block_e = jnp.minimum(e_slot[::rows], N_EXPERTS - 1).astype(jnp.int32)
    n_used = (pends[-1] // rows).astype(jnp.int32)
    return jnp.concatenate([block_e, n_used[None]]), ids


def kernel(x_prompt, x_sample, cache_k, cache_v, state_wkv, state_shift, rel_bias, ln1, w_in, rwkv_mu, w0,
           w_decay_up, a0, w_a_up, w_g_up, k_k, k_a, r_k, gn_w, gn_b, w_branch_a, w_branch_b, w_out, ln2,
           w_router, b_router, w_gate_up, b_gate_up, w_down, b_down, ln_f):
    bp, seq, _ = x_prompt.shape
    db = x_sample.shape[0]
    n_p = bp * seq
    n_tot = n_p + db
    n_pad = -(-(n_tot + -(-3 * MOE_ROWS // TOP_K)) // TOKEN_TILE) * TOKEN_TILE
    l = 0

    row = lambda a: a.reshape(1, -1)
    seg = _head_seg()
    prep_params = (row(rwkv_mu[l]), row(w0[l]), w_decay_up[l], row(a0[l]), w_a_up[l], w_g_up[l],
                   row(k_k[l]), row(k_a[l]), row(r_k[l]), seg)
    wr_pad = jnp.pad(w_router[l], ((0, 0), (0, LANES - N_EXPERTS))).astype(BF16)
    br_pad = jnp.pad(b_router[l], (0, LANES - N_EXPERTS), constant_values=NEG).reshape(1, LANES)
    post_small = (row(gn_w[l]), row(gn_b[l]), seg)
    post_tail = (row(ln2[l]), wr_pad, br_pad)

    w_in_bf = w_in[l].astype(BF16)
    consts = (post_small + (w_branch_a[l].astype(BF16), w_branch_b[l].astype(BF16), w_out[l].astype(BF16))
              + post_tail)

    xp = x_prompt.reshape(n_p, D_MODEL)
    q_p, k_p, v_p, sg_p, kt_p, vt_p, shift_p, *prep_p = _inproj_prompt(xp, row(ln1[l]), w_in_bf, prep_params,
                                                                     tm=TOKEN_TILE, seq=seq)
    as3 = lambda a: a.reshape(bp, seq, -1)
    oa_p = _attn_prompt(as3(q_p), as3(k_p), as3(v_p), _prompt_bias(rel_bias))
    r_, w_, k2_, vr_, kk_, ka_, bonus_p, g_p = prep_p
    s0_p = jnp.zeros((bp, R_HEADS, R_HEAD, R_HEAD), F32)
    y_p, wkv_p = _lane_scan(r_, w_, k2_, vr_, kk_, ka_, s0_p, tc=SCAN_CHUNK)
    flat = lambda a: a.reshape(n_p, -1)
    bufs = _post(xp, flat(y_p), flat(bonus_p), flat(g_p), flat(oa_p), sg_p, consts,
                 tm=TOKEN_TILE, n_total=n_pad)

    xs = x_sample.reshape(db, D_MODEL)
    q_s, k_s, v_s, rc_s, sg_s = _inproj(xs, row(ln1[l]), w_in_bf, tm=db)
    bias_s, b0_s = _sample_bias(rel_bias, cache_k.shape[2])
    oa_s = _attn_sample(q_s, k_s, v_s, cache_k[l], cache_v[l], bias_s, b0_s)
    prep_s = _rwkv_prep_sample(rc_s, state_shift[l], prep_params)
    sr, sw, sk2, svr, skk, ska, bonus_s, g_s = [a[:, None] for a in prep_s]
    y_s, wkv_s = _rwkv_scan(sr, sw, sk2, svr, skk, ska, state_wkv[l], bb=SCAN_BATCH, tc=1)
    x1_all, h2_all, ti_all, tg_all = _post(xs, y_s[:, 0], bonus_s[:, 0], g_s[:, 0], oa_s, sg_s, consts,
                                           tm=db, n_total=n_pad, row0=n_p, into=bufs)

    block_e, ids = _route(ti_all[:n_tot, :TOP_K], n_pad, MOE_ROWS)
    ys = _moe_experts(block_e, ids, h2_all, w_gate_up[l], b_gate_up[l], w_down[l], b_down[l],
                      rows=MOE_ROWS, plane_rows=n_pad, n_tok=n_tot)
    lnf = row(ln_f)
    y_prompt = _combine(ys, x1_all, tg_all, lnf, tm=TOKEN_TILE, n=n_p, row0=0)
    y_sample = _combine(ys, x1_all, tg_all, lnf, tm=db, n=db, row0=n_p)

    heads = lambda a, b_: a.reshape(1, b_, -1, A_HEADS, HEAD_DIM)
    return (y_prompt.reshape(bp, seq, D_MODEL), y_sample.reshape(db, 1, D_MODEL),
            kt_p.transpose(0, 3, 1, 2)[None], vt_p.transpose(0, 3, 1, 2)[None], wkv_p[None], shift_p[:, 0][None],
            heads(k_s, db), heads(v_s, db), wkv_s[None], rc_s[None])
```

```python
import functools
import math

import jax
import jax.numpy as jnp
from jax import lax
from jax.experimental import pallas as pl
from jax.experimental.pallas import tpu as pltpu

F32 = jnp.float32
BF16 = jnp.bfloat16

D_MODEL = 1024
A_HEADS = 8
HEAD_DIM = 64
A_WIDTH = A_HEADS * HEAD_DIM
PATTERNS = ((128, 1), (512, 4), (2048, 16))
BLOCK = 128
NUM_BUCKETS = 32
MAX_DISTANCE = 2048
SCALE = HEAD_DIM ** -0.5
NEG = -1e30
R_HEADS = 8
R_HEAD = 64
R_WIDTH = R_HEADS * R_HEAD
DECAY_LORA = 64
AAA_LORA = 64
GATE_LORA = 128
GN_EPS = 64e-5
N_RWKV_COLS = 3 * R_WIDTH + DECAY_LORA + AAA_LORA + GATE_LORA
N_IN_COLS = 3 * A_WIDTH + N_RWKV_COLS + 2 * D_MODEL
N_EXPERTS = 32
TOP_K = 4
SWIGLU_LIMIT = 7.0
SWIGLU_ALPHA = 1.702
RMS_EPS = 1e-6

LANES = 128
SUBLANES = 8
VMEM_LIMIT = 56 * 1024 * 1024
TOKEN_TILE = 256
MOE_ROWS = 512
SCAN_BATCH = 2
SCAN_CHUNK = 64


def _cparams(*sem):
    return pltpu.CompilerParams(dimension_semantics=sem, vmem_limit_bytes=VMEM_LIMIT)


def _rms(x, g):
    return x * lax.rsqrt(jnp.mean(x * x, axis=-1, keepdims=True) + RMS_EPS) * g


def _dot(a, b):
    return jnp.dot(a.astype(BF16), b.astype(BF16), preferred_element_type=F32)


def _bf(x):
    return x.astype(BF16).astype(F32)


def _split3(x):
    hi = x.astype(BF16)
    r1 = x - hi.astype(F32)
    mid = r1.astype(BF16)
    lo = (r1 - mid.astype(F32)).astype(BF16)
    return hi, mid, lo


def _segdot(x, seg):
    hi, mid, lo = _split3(x)
    d = lambda p: jnp.dot(p, seg, preferred_element_type=F32)
    return d(hi) + d(mid) + d(lo)


def _head_sums(x, seg):
    return jnp.concatenate([_segdot(x[:, j:j + LANES], seg) for j in range(0, x.shape[1], LANES)], axis=1)


TILE_ROWS = D_MODEL // LANES
assert TILE_ROWS == SUBLANES


def _load_token_tiles(ref, n, *lead):
    return jnp.concatenate([ref[(*lead, pl.ds(p, n, stride=TILE_ROWS), slice(None))] for p in range(TILE_ROWS)], axis=1)


def _store_token_tiles(ref, x):
    n = x.shape[0]
    for p in range(TILE_ROWS):
        ref[pl.ds(p, n, stride=TILE_ROWS), :] = x[:, p * LANES:(p + 1) * LANES]


def _project(x_ref, g_ref, w_ref):
    h = _rms(x_ref[...], g_ref[...]).astype(BF16)
    return lambda c0, c1: _dot(h, w_ref[:, c0:c1])


C_RWKV = 3 * A_WIDTH
C_GATE = C_RWKV + N_RWKV_COLS


def _inproj_kernel(x_ref, g_ref, w_ref, q_ref, k_ref, v_ref, rc_ref, sg_ref):
    mm = _project(x_ref, g_ref, w_ref)
    q_ref[...] = mm(0, A_WIDTH)
    k_ref[...] = mm(A_WIDTH, 2 * A_WIDTH)
    v_ref[...] = mm(2 * A_WIDTH, C_RWKV)
    rc_ref[...] = mm(C_RWKV, C_GATE)
    sg_ref[:, :D_MODEL] = jax.nn.sigmoid(mm(C_GATE, C_GATE + D_MODEL))
    sg_ref[:, D_MODEL:] = jax.nn.sigmoid(mm(C_GATE + D_MODEL, N_IN_COLS))


def _inproj(x, ln1, w, *, tm):
    n = x.shape[0]
    row = lambda c: pl.BlockSpec((tm, c), lambda i: (i, 0))
    full = lambda a: pl.BlockSpec(a.shape, lambda i: (0,) * a.ndim)
    out_cols = (A_WIDTH, A_WIDTH, A_WIDTH, N_RWKV_COLS, 2 * D_MODEL)
    return pl.pallas_call(
        _inproj_kernel,
        grid=(n // tm,),
        in_specs=[row(D_MODEL), full(ln1), full(w)],
        out_specs=[row(c) for c in out_cols],
        out_shape=[jax.ShapeDtypeStruct((n, c), F32) for c in out_cols],
        compiler_params=_cparams("parallel"),
        name="inproj",
    )(x, ln1, w)


N_PREP_PARAMS = 10
N_PREP_OUTS = 8


def _inproj_prompt_kernel(x_ref, g_ref, w_ref, *refs, per):
    params = [r[...] for r in refs[:N_PREP_PARAMS]]
    q_ref, k_ref, v_ref, sg_ref, kt_ref, vt_ref, shift_ref = refs[N_PREP_PARAMS:N_PREP_PARAMS + 7]
    prep_refs = refs[N_PREP_PARAMS + 7:-1]
    carry = refs[-1]
    mm = _project(x_ref, g_ref, w_ref)
    q_ref[...] = mm(0, A_WIDTH)
    k = mm(A_WIDTH, 2 * A_WIDTH)
    v = mm(2 * A_WIDTH, C_RWKV)
    k_ref[...] = k
    v_ref[...] = v
    tm = k.shape[0]
    kt_ref[0] = k.T.reshape(A_HEADS, HEAD_DIM, tm)
    vt_ref[0] = v.T.reshape(A_HEADS, HEAD_DIM, tm)
    sg_ref[:, :D_MODEL] = jax.nn.sigmoid(mm(C_GATE, C_GATE + D_MODEL))
    sg_ref[:, D_MODEL:] = jax.nn.sigmoid(mm(C_GATE + D_MODEL, N_IN_COLS))

    @pl.when(lax.rem(pl.program_id(0), per) == 0)
    def _():
        carry[...] = jnp.zeros(carry.shape, carry.dtype)

    rc = mm(C_RWKV, C_GATE)
    row = lax.broadcasted_iota(jnp.int32, (tm, 1), 0)
    prev = jnp.where(row == 0, carry[...], pltpu.roll(rc, 1, axis=0))
    for o, val in zip(prep_refs, _rwkv_prep_math(rc, prev, *params)):
        o[0] = val
    last = rc[tm - 1:tm, :]
    carry[...] = last
    shift_ref[0] = last


def _inproj_prompt(x, ln1, w, prep_params, *, tm, seq):
    n = x.shape[0]
    per = seq // tm
    row = lambda c: pl.BlockSpec((tm, c), lambda i: (i, 0))
    full = lambda a: pl.BlockSpec(a.shape, lambda i: (0,) * a.ndim)
    t_spec = pl.BlockSpec((1, A_HEADS, HEAD_DIM, tm), lambda i: (i // per, 0, 0, i % per))
    t_shape = jax.ShapeDtypeStruct((n // seq, A_HEADS, HEAD_DIM, seq), F32)
    cols = (A_WIDTH, A_WIDTH, A_WIDTH, 2 * D_MODEL)
    out_specs = ([row(c) for c in cols] + [t_spec, t_spec, pl.BlockSpec((1, 1, N_RWKV_COLS), lambda i: (i // per, 0, 0))]
                 + [pl.BlockSpec((1, tm, R_WIDTH), lambda i: (i // per, i % per, 0))] * N_PREP_OUTS)
    out_shape = ([jax.ShapeDtypeStruct((n, c), F32) for c in cols]
                 + [t_shape, t_shape, jax.ShapeDtypeStruct((n // seq, 1, N_RWKV_COLS), F32)]
                 + [jax.ShapeDtypeStruct((n // seq, seq, R_WIDTH), F32)] * N_PREP_OUTS)
    return pl.pallas_call(
        functools.partial(_inproj_prompt_kernel, per=per),
        grid=(n // tm,),
        in_specs=[row(D_MODEL), full(ln1), full(w)] + [full(a) for a in prep_params],
        out_specs=out_specs,
        out_shape=out_shape,
        scratch_shapes=[pltpu.VMEM((1, N_RWKV_COLS), F32)],
        compiler_params=_cparams("arbitrary"),
        name="inproj_prompt",
    )(x, ln1, w, *prep_params)


def _t5_bucket(dist):
    max_exact = NUM_BUCKETS // 2
    d = jnp.maximum(dist, 1).astype(F32)
    large = max_exact + (jnp.log(d / max_exact) / math.log(MAX_DISTANCE / max_exact)
                         * (NUM_BUCKETS - max_exact)).astype(jnp.int32)
    large = jnp.minimum(large, NUM_BUCKETS - 1)
    return jnp.where(dist < max_exact, dist, large)


def _bias_lookup(rel_bias, dist):
    onehot = (_t5_bucket(dist)[..., None] == jnp.arange(NUM_BUCKETS)).astype(F32)
    return jnp.moveaxis(jnp.dot(onehot, rel_bias.astype(F32), precision=lax.Precision.HIGHEST), -1, 0)


def _prompt_bias(rel_bias):
    i = jnp.arange(BLOCK)[:, None]
    j = jnp.arange(2 * BLOCK)[None, :]
    delta = i + BLOCK - j
    out = []
    for window, dil in PATTERNS:
        n = window // dil
        band = (delta >= 0) & (delta <= n)
        out.append(jnp.where(band[None], _bias_lookup(rel_bias, jnp.clip(delta, 0, n) * dil), NEG))
    return jnp.stack(out, 0)


def _sample_bias(rel_bias, wb):
    dist = wb - jnp.arange(wb)
    out = []
    for window, dil in PATTERNS:
        member = (dist % dil == 0) & (dist <= window)
        out.append(jnp.where(member[None], _bias_lookup(rel_bias, dist), NEG))
    b0 = _bias_lookup(rel_bias, jnp.zeros((1,), jnp.int32))
    return jnp.stack(out, 0), b0


def _attn_prompt_kernel(q_ref, k_ref, v_ref, bias_ref, o_ref, os_ref, ls_ref, *, seq):
    nt = (((1,), (1,)), ((), ()))
    npat = len(PATTERNS)
    w2 = 2 * HEAD_DIM
    lo = lax.broadcasted_iota(jnp.int32, (BLOCK, w2), 1) < HEAD_DIM
    key = lax.broadcasted_iota(jnp.int32, (2 * BLOCK, 2 * BLOCK), 1)

    for g, (window, dil) in enumerate(PATTERNS):
        nb = seq // (BLOCK * dil)
        sh = dil.bit_length() - 1

        def body(it, carry, g=g, dil=dil, nb=nb, sh=sh):
            r = it & (dil - 1)
            c = it >> sh
            start = c * (BLOCK * dil) + r
            rows = pl.ds(start, BLOCK, stride=dil) if dil > 1 else pl.ds(start, BLOCK)
            qb = q_ref[0, rows, :]
            kb = k_ref[0, rows, :]
            vb = v_ref[0, rows, :]
            if nb > 1:
                pstart = jnp.maximum(c - 1, 0) * (BLOCK * dil) + r
                prow = pl.ds(pstart, BLOCK, stride=dil) if dil > 1 else pl.ds(pstart, BLOCK)
                kp = k_ref[0, prow, :]
                vp = v_ref[0, prow, :]
            q2 = jnp.concatenate([jnp.where(lo, qb, 0.0), jnp.where(lo, 0.0, qb)], axis=0).astype(BF16)
            if nb > 1:
                k2 = jnp.concatenate([kp, kb], axis=0)
                v2 = jnp.concatenate([vp, vb], axis=0)
                bias = bias_ref[g, 0]
                valid = (bias > 0.5 * NEG) & ((key >= BLOCK) | (c > 0))
            else:
                k2, v2 = kb, vb
                bias = bias_ref[g, 0, :, BLOCK:]
                valid = bias > 0.5 * NEG
            s = lax.dot_general(q2, k2.astype(BF16), nt, preferred_element_type=F32)
            s = jnp.where(valid, s * SCALE + bias, NEG)
            m = jnp.max(s, axis=-1, keepdims=True)
            p = jnp.exp(s - m)
            l = jnp.sum(p, axis=-1, keepdims=True)
            o2 = jnp.dot((p * (1.0 / l)).astype(BF16), v2.astype(BF16), preferred_element_type=F32)
            lse = jnp.broadcast_to(m + jnp.log(l), (2 * BLOCK, w2))
            os_ref[g, rows, :] = jnp.where(lo, o2[:BLOCK], o2[BLOCK:])
            ls_ref[g, rows, :] = jnp.where(lo, lse[:BLOCK], lse[BLOCK:])
            return carry

        lax.fori_loop(0, dil * nb, body, 0, unroll=16)

    chunk = 256

    def merge(ch, carry):
        rows = pl.ds(pl.multiple_of(ch * chunk, chunk), chunk)
        lse = [ls_ref[g, rows, :] for g in range(npat)]
        top = functools.reduce(jnp.maximum, lse)
        e = [jnp.exp(x - top) for x in lse]
        inv = 1.0 / functools.reduce(lambda a, b_: a + b_, e)
        acc = jnp.zeros((chunk, 2 * HEAD_DIM), F32)
        for g in range(npat):
            acc = acc + (e[g] * inv) * os_ref[g, rows, :]
        o_ref[0, rows, :] = acc
        return carry

    lax.fori_loop(0, seq // chunk, merge, 0)


def _attn_prompt(q, k, v, bias):
    b, s, _ = q.shape
    w2 = 2 * HEAD_DIM
    qspec = pl.BlockSpec((1, s, w2), lambda i, p: (i, 0, p))
    return pl.pallas_call(
        functools.partial(_attn_prompt_kernel, seq=s),
        grid=(b, A_HEADS // 2),
        in_specs=[qspec, qspec, qspec,
                  pl.BlockSpec((len(PATTERNS), 1, 2 * BLOCK, 2 * BLOCK), lambda i, p: (0, p, 0, 0))],
        out_specs=qspec,
        out_shape=jax.ShapeDtypeStruct((b, s, A_WIDTH), F32),
        scratch_shapes=[pltpu.VMEM((len(PATTERNS), s, w2), F32)] * 2,
        compiler_params=_cparams("parallel", "parallel"),
        name="attn_prompt",
    )(q, k, v, bias.reshape(len(PATTERNS), A_HEADS // 2, 2 * BLOCK, 2 * BLOCK))


def _attn_sample_kernel(q_ref, kn_ref, vn_ref, kt_ref, vt_ref, bias_ref, b0_ref, o_ref):
    npat = len(PATTERNS)
    nt = (((1,), (1,)), ((), ()))
    q = q_ref[0]
    v_new = _bf(vn_ref[0])
    s0 = jnp.sum(_bf(q) * _bf(kn_ref[0]), axis=-1, keepdims=True) * SCALE + b0_ref[...]
    s = jnp.concatenate([_dot(q[h:h + 1], kt_ref[0, h]) for h in range(A_HEADS)], axis=0) * SCALE
    ps, p0s, lses = [], [], []
    for g in range(npat):
        bias = bias_ref[g]
        sg = jnp.where(bias > 0.5 * NEG, s + bias, NEG)
        m = jnp.maximum(jnp.max(sg, axis=-1, keepdims=True), s0)
        l = jnp.sum(jnp.exp(sg - m), axis=-1, keepdims=True) + jnp.exp(s0 - m)
        lse = m + jnp.log(l)
        ps.append(jnp.exp(sg - lse))
        p0s.append(_bf(jnp.exp(s0 - lse)))
        lses.append(lse)
    top = functools.reduce(jnp.maximum, lses)
    e = [jnp.exp(x - top) for x in lses]
    inv = 1.0 / functools.reduce(lambda a, b_: a + b_, e)
    w = [_bf(x * inv) for x in e]
    rows = []
    for h in range(A_HEADS):
        p_h = jnp.concatenate([p[h:h + 1] for p in ps], axis=0)
        o_h = lax.dot_general(p_h.astype(BF16), vt_ref[0, h].astype(BF16), nt, preferred_element_type=F32)
        acc = jnp.zeros((1, HEAD_DIM), F32)
        for g in range(npat):
            acc = acc + w[g][h:h + 1] * _bf(o_h[g:g + 1] + p0s[g][h:h + 1] * v_new[h:h + 1])
        rows.append(acc)
    o_ref[0] = jnp.concatenate(rows, axis=0)


def _attn_sample(q, k_new, v_new, cache_k, cache_v, bias, b0):
    db, wb = cache_k.shape[:2]
    heads = lambda a: a.reshape(db, A_HEADS, HEAD_DIM)
    vec = pl.BlockSpec((1, A_HEADS, HEAD_DIM), lambda i: (i, 0, 0))
    full = lambda a: pl.BlockSpec(a.shape, lambda i: (0,) * a.ndim)
    rows_minor = lambda a: a.transpose(0, 2, 3, 1)
    cache = pl.BlockSpec((1, A_HEADS, HEAD_DIM, wb), lambda i: (i, 0, 0, 0))
    args = [heads(q), heads(k_new), heads(v_new), rows_minor(cache_k), rows_minor(cache_v), bias, b0]
    return pl.pallas_call(
        _attn_sample_kernel,
        grid=(db,),
        in_specs=[vec, vec, vec, cache, cache, full(bias), full(b0)],
        out_specs=vec,
        out_shape=jax.ShapeDtypeStruct((db, A_HEADS, HEAD_DIM), F32),
        compiler_params=_cparams("parallel"),
        name="attn_sample",
    )(*args).reshape(db, A_WIDTH)


def _rwkv_prep_math(rc, prev, mu, w0, wd, a0, wa, wg, k_k, k_a, r_k, seg):
    xr = rc + (prev - rc) * mu
    o1, o2, o3 = R_WIDTH, 2 * R_WIDTH, 3 * R_WIDTH
    r = xr[:, :o1]
    kr = xr[:, o1:o2]
    vr = xr[:, o2:o3]
    xw = xr[:, o3:o3 + DECAY_LORA]
    xa = xr[:, o3 + DECAY_LORA:o3 + DECAY_LORA + AAA_LORA]
    xg = xr[:, o3 + DECAY_LORA + AAA_LORA:]
    z = -(w0 + _dot(jnp.tanh(xw), wd))
    softplus = jnp.maximum(z, 0.0) + jnp.log(1.0 + jnp.exp(-jnp.abs(z)))
    decay = jnp.exp(-jnp.exp(-softplus - 0.5))
    a = jax.nn.sigmoid(a0 + _dot(xa, wa))
    g = _dot(jax.nn.sigmoid(xg), wg)
    kk = kr * k_k
    k2 = kr * (1.0 + (a - 1.0) * k_a)
    kk = kk / jnp.maximum(jnp.sqrt(_head_sums(kk * kk, seg)), 1e-12)
    bonus = _head_sums(r * k2 * r_k, seg) * vr
    return _bf(r), decay, k2, vr, _bf(kk), kk * a, bonus, g


def _rwkv_prep_sample_kernel(rc_ref, prev_ref, *refs):
    p = [x[...] for x in refs[:N_PREP_PARAMS]]
    outs = refs[N_PREP_PARAMS:]
    for o, val in zip(outs, _rwkv_prep_math(rc_ref[...], prev_ref[...], *p)):
        o[...] = val


def _head_seg():
    head = jnp.arange(LANES) // R_HEAD
    return (head[:, None] == head[None, :]).astype(BF16)


def _rwkv_prep_sample(rc, prev, params):
    n = rc.shape[0]
    full = lambda a: pl.BlockSpec(a.shape, lambda i: (0,) * a.ndim)
    return pl.pallas_call(
        _rwkv_prep_sample_kernel,
        grid=(1,),
        in_specs=[full(rc), full(prev)] + [full(a) for a in params],
        out_specs=[pl.BlockSpec((n, R_WIDTH), lambda i: (0, 0))] * 8,
        out_shape=[jax.ShapeDtypeStruct((n, R_WIDTH), F32)] * 8,
        compiler_params=_cparams("arbitrary"),
        name="rwkv_prep_sample",
    )(rc, prev, *params)


def _seg_lane_sum(x, lo_mask):
    lo = jnp.sum(jnp.where(lo_mask, x, 0.0), axis=-1, keepdims=True)
    hi = jnp.sum(jnp.where(lo_mask, 0.0, x), axis=-1, keepdims=True)
    return jnp.where(lo_mask, lo, hi)


def _rwkv_scan_kernel(r_ref, w_ref, k_ref, v_ref, kk_ref, ka_ref, s0_ref, y_ref, sT_ref, st_ref, *, bb, tc):
    pairs = R_HEADS // 2
    w2 = 2 * R_HEAD

    @pl.when(pl.program_id(1) == 0)
    def _():
        for b in range(bb):
            for p in range(pairs):
                st_ref[b, p] = jnp.concatenate([s0_ref[b, 2 * p], s0_ref[b, 2 * p + 1]], axis=1)

    lane = lax.broadcasted_iota(jnp.int32, (R_HEAD, w2), 1)
    sub = lax.broadcasted_iota(jnp.int32, (R_HEAD, w2), 0)
    lo_mask = lane < R_HEAD
    eye2 = (lane & (R_HEAD - 1)) == sub

    grp = min(8, tc)

    def group(tg, carry):
        rows = pl.ds(pl.multiple_of(tg * grp, grp), grp)
        for b in range(bb):
            for p in range(pairs):
                cols = slice(p * w2, (p + 1) * w2)
                r8, w8, k8, v8, kk8, ka8 = (ref[b, rows, cols] for ref in (r_ref, w_ref, k_ref, v_ref, kk_ref, ka_ref))
                s = st_ref[b, p]
                sb = _bf(s)
                ys = []
                for j in range(grp):
                    row = lambda a: a[j:j + 1, :]
                    sa = -_seg_lane_sum(sb * row(kk8), lo_mask)
                    vcol = _seg_lane_sum(jnp.where(eye2, jnp.broadcast_to(row(v8), (R_HEAD, w2)), 0.0), lo_mask)
                    s = s * row(w8) + sa * row(ka8) + vcol * row(k8)
                    sb = _bf(s)
                    yfull = _seg_lane_sum(sb * row(r8), lo_mask)
                    ys.append(jnp.sum(jnp.where(eye2, yfull, 0.0), axis=0, keepdims=True))
                st_ref[b, p] = s
                y_ref[b, rows, cols] = jnp.concatenate(ys, axis=0) if grp > 1 else ys[0]
        return carry

    lax.fori_loop(0, tc // grp, group, 0)

    @pl.when(pl.program_id(1) == pl.num_programs(1) - 1)
    def _():
        for b in range(bb):
            for p in range(pairs):
                s = st_ref[b, p]
                sT_ref[b, 2 * p] = s[:, :R_HEAD]
                sT_ref[b, 2 * p + 1] = s[:, R_HEAD:]


def _rwkv_scan(r, w, k, v, kk, ka, s0, *, bb, tc):
    b, t, _ = r.shape
    seq = pl.BlockSpec((bb, tc, R_WIDTH), lambda i, j: (i, j, 0))
    state = pl.BlockSpec((bb, R_HEADS, R_HEAD, R_HEAD), lambda i, j: (i, 0, 0, 0))
    return pl.pallas_call(
        functools.partial(_rwkv_scan_kernel, bb=bb, tc=tc),
        grid=(b // bb, t // tc),
        in_specs=[seq] * 6 + [state],
        out_specs=[seq, state],
        out_shape=[jax.ShapeDtypeStruct((b, t, R_WIDTH), F32),
                   jax.ShapeDtypeStruct((b, R_HEADS, R_HEAD, R_HEAD), F32)],
        scratch_shapes=[pltpu.VMEM((bb, R_HEADS // 2, R_HEAD, 2 * R_HEAD), F32)],
        compiler_params=_cparams("parallel", "arbitrary"),
        name="rwkv_scan",
    )(r, w, k, v, kk, ka, s0)


CHAINS = LANES // 2
K2 = R_HEAD // 2


def _chain_rows(x):
    b, t, _ = x.shape
    x = x.reshape(b, t, R_HEADS, 2, K2).transpose(1, 4, 3, 0, 2)
    return x.reshape(t, K2, LANES)


def _lane_scan_kernel(kk_ref, w_ref, ka_ref, kx_ref, r_ref, v_ref, s0_ref, y_ref, sT_ref, s_ref, *, tc):
    lo = lax.broadcasted_iota(jnp.int32, (K2, LANES), 1) < CHAINS

    @pl.when(pl.program_id(0) == 0)
    def _():
        for k2 in range(K2):
            s_ref[k2] = s0_ref[k2]

    def both_halves(x):
        return x + pltpu.roll(x, CHAINS, axis=1)

    nhalf = 2
    vh = R_HEAD // nhalf

    def row(ref, t, k2):
        return jnp.broadcast_to(ref[t, k2:k2 + 1, :], (vh, LANES))

    acc0 = []
    for hf in range(nhalf):
        vs = slice(hf * vh, (hf + 1) * vh)
        a = jnp.zeros((vh, LANES), F32)
        for k2 in range(K2):
            a = a + _bf(s_ref[k2, vs, :]) * row(kk_ref, 0, k2)
        acc0.append(a)

    def step(t, acc):
        tn = jnp.minimum(t + 1, tc - 1)
        nxt, ys = [], []
        v_t = v_ref[t]
        v_sw = pltpu.roll(v_t, CHAINS, axis=1)
        for hf in range(nhalf):
            vs = slice(hf * vh, (hf + 1) * vh)
            vv = jnp.where(lo, v_t, v_sw) if hf == 0 else jnp.where(lo, v_sw, v_t)
            sa = -both_halves(acc[hf])
            yacc = jnp.zeros((vh, LANES), F32)
            nacc = jnp.zeros((vh, LANES), F32)
            for k2 in range(K2):
                s = s_ref[k2, vs, :] * row(w_ref, t, k2) + sa * row(ka_ref, t, k2) + vv * row(kx_ref, t, k2)
                s_ref[k2, vs, :] = s
                sb = _bf(s)
                yacc = yacc + sb * row(r_ref, t, k2)
                nacc = nacc + sb * row(kk_ref, tn, k2)
            ys.append(both_halves(yacc))
            nxt.append(nacc)
        y_ref[t] = jnp.where(lo, ys[0], ys[1])
        return tuple(nxt)

    lax.fori_loop(0, tc, step, tuple(acc0))

    @pl.when(pl.program_id(0) == pl.num_programs(0) - 1)
    def _():
        for k2 in range(K2):
            sT_ref[k2] = s_ref[k2]


def _lane_scan(r, w, k, v, kk, ka, s0, *, tc):
    b, t, _ = r.shape
    assert b * R_HEADS == CHAINS
    ops = [_chain_rows(x) for x in (kk, w, ka, k, r, v)]
    s0c = s0.reshape(b, R_HEADS, 2, K2, 2, K2).transpose(5, 2, 3, 4, 0, 1).reshape(K2, R_HEAD, LANES)
    rows = pl.BlockSpec((tc, K2, LANES), lambda i: (i, 0, 0))
    state = pl.BlockSpec((K2, R_HEAD, LANES), lambda i: (0, 0, 0))
    y, st = pl.pallas_call(
        functools.partial(_lane_scan_kernel, tc=tc),
        grid=(t // tc,),
        in_specs=[rows] * 6 + [state],
        out_specs=[rows, state],
        out_shape=[jax.ShapeDtypeStruct((t, K2, LANES), F32), jax.ShapeDtypeStruct((K2, R_HEAD, LANES), F32)],
        scratch_shapes=[pltpu.VMEM((K2, R_HEAD, LANES), F32)],
        compiler_params=_cparams("arbitrary"),
        name="rwkv_lane_scan",
    )(*ops, s0c)
    y = y.reshape(t, K2, 2, b, R_HEADS).transpose(3, 0, 4, 2, 1).reshape(b, t, R_WIDTH)
    st = st.reshape(K2, 2, K2, 2, b, R_HEADS).transpose(4, 5, 1, 2, 3, 0).reshape(b, R_HEADS, R_HEAD, R_HEAD)
    return y, st


def _post_kernel(x_ref, y_ref, bonus_ref, g_ref, oa_ref, sg_ref, gnw_ref, gnb_ref, seg_ref, wba_ref, wbb_ref,
                 wout_ref, ln2_ref, wr_ref, br_ref, *rest, aliased, n_main):
    outs = rest[aliased:]
    x1_ref, h2_ref, ti_ref, tg_ref = outs

    @pl.when(pl.program_id(0) >= n_main)
    def _():
        for o in outs:
            o[...] = jnp.zeros(o.shape, o.dtype)

    @pl.when(pl.program_id(0) < n_main)
    def _():
        _post_body(x_ref, y_ref, bonus_ref, g_ref, oa_ref, sg_ref, gnw_ref, gnb_ref, seg_ref, wba_ref, wbb_ref,
                   wout_ref, ln2_ref, wr_ref, br_ref, x1_ref, h2_ref, ti_ref, tg_ref)


def _post_body(x_ref, y_ref, bonus_ref, g_ref, oa_ref, sg_ref, gnw_ref, gnb_ref, seg_ref, wba_ref, wbb_ref,
               wout_ref, ln2_ref, wr_ref, br_ref, x1_ref, h2_ref, ti_ref, tg_ref):
    y = y_ref[...]
    seg = seg_ref[...]
    mu = _head_sums(y, seg) * (1.0 / R_HEAD)
    yc = y - mu
    var = _head_sums(yc * yc, seg) * (1.0 / R_HEAD)
    yn = yc * lax.rsqrt(var + GN_EPS) * gnw_ref[...] + gnb_ref[...]
    o_b = (yn + bonus_ref[...]) * g_ref[...]
    mixed = (sg_ref[:, :D_MODEL] * _dot(oa_ref[...], wba_ref[...])
             + sg_ref[:, D_MODEL:] * _dot(o_b, wbb_ref[...]))
    x1 = x_ref[...] + _dot(mixed, wout_ref[...])
    x1_ref[...] = x1
    h2 = _rms(x1, ln2_ref[...])
    _store_token_tiles(h2_ref, h2)
    logits = _dot(h2, wr_ref[...]) + br_ref[...]
    lane = lax.broadcasted_iota(jnp.int32, logits.shape, 1).astype(F32)
    work = logits
    vals, idxs = [], []
    for _ in range(TOP_K):
        m = jnp.max(work, axis=-1, keepdims=True)
        idx = jnp.min(jnp.where(work == m, lane, float(LANES)), axis=-1, keepdims=True)
        vals.append(m)
        idxs.append(idx)
        work = jnp.where(lane == idx, -jnp.inf, work)
    es = [jnp.exp(v - vals[0]) for v in vals]
    tot = es[0] + es[1] + es[2] + es[3]
    ti = jnp.zeros(logits.shape, F32)
    tg = jnp.zeros(logits.shape, F32)
    for kslot in range(TOP_K):
        ti = jnp.where(lane == float(kslot), idxs[kslot], ti)
        tg = jnp.where(lane == float(kslot), es[kslot] / tot, tg)
    ti_ref[...] = ti.astype(jnp.int32)
    tg_ref[...] = tg


def _post(x, y, bonus, g, oa, sg, consts, *, tm, n_total, row0=0, into=None):
    n = x.shape[0]
    blk0 = row0 // tm
    n_main = n // tm
    steps = n_total // tm if into is None else n_main
    row = lambda c: pl.BlockSpec((tm, c), lambda i: (jnp.minimum(i, n_main - 1), 0))
    orow = lambda c: pl.BlockSpec((tm, c), lambda i: (i + blk0, 0))
    otile = pl.BlockSpec((tm * TILE_ROWS, LANES), lambda i: (i + blk0, 0))
    full = lambda a: pl.BlockSpec(a.shape, lambda i: (0,) * a.ndim)
    ins = [x, y, bonus, g, oa, sg, *consts]
    in_specs = [row(D_MODEL), row(R_WIDTH), row(R_WIDTH), row(R_WIDTH), row(A_WIDTH), row(2 * D_MODEL)]
    in_specs += [full(a) for a in consts]
    aliases = {}
    if into is not None:
        aliases = {len(ins) + i: i for i in range(len(into))}
        in_specs += [pl.BlockSpec(memory_space=pl.ANY)] * len(into)
        ins += list(into)
    return pl.pallas_call(
        functools.partial(_post_kernel, aliased=len(aliases), n_main=n_main),
        grid=(steps,),
        in_specs=in_specs,
        out_specs=[orow(D_MODEL), otile, orow(LANES), orow(LANES)],
        out_shape=[jax.ShapeDtypeStruct((n_total, D_MODEL), F32),
                   jax.ShapeDtypeStruct((n_total * TILE_ROWS, LANES), F32),
                   jax.ShapeDtypeStruct((n_total, LANES), jnp.int32),
                   jax.ShapeDtypeStruct((n_total, LANES), F32)],
        input_output_aliases=aliases,
        compiler_params=_cparams("parallel"),
        name="post",
    )(*ins)


def _start_rows(rows_ref, hbm, buf, sem, *, rows, to_hbm):
    for r in range(rows):
        at = pl.ds(pl.multiple_of(rows_ref[0, 0, r], TILE_ROWS), TILE_ROWS)
        here = pl.ds(r * TILE_ROWS, TILE_ROWS)
        prio = r % 2
        if to_hbm:
            pltpu.make_async_copy(buf.at[here], hbm.at[at], sem).start(priority=prio)
        else:
            pltpu.make_async_copy(hbm.at[at], buf.at[here], sem).start(priority=prio)


def _wait_rows(hbm, buf, sem, *, rows, to_hbm):
    whole = hbm.at[pl.ds(0, rows * TILE_ROWS)]
    (pltpu.make_async_copy(buf, whole, sem) if to_hbm else pltpu.make_async_copy(whole, buf, sem)).wait()


def _expert_mlp(xb, wgu_bf, wd_bf, bgu_ref, bd_ref, rows):
    acc = jnp.broadcast_to(bd_ref[0], (rows, D_MODEL))
    cw = 512
    for c in range(D_MODEL // cw):
        gs = slice(c * cw, (c + 1) * cw)
        us = slice(D_MODEL + c * cw, D_MODEL + (c + 1) * cw)
        gt = jnp.dot(xb, wgu_bf[:, gs], preferred_element_type=F32) + bgu_ref[0, :, gs]
        up = jnp.dot(xb, wgu_bf[:, us], preferred_element_type=F32) + bgu_ref[0, :, us]
        gt = jnp.minimum(gt, SWIGLU_LIMIT)
        up = jnp.clip(up, -SWIGLU_LIMIT, SWIGLU_LIMIT)
        act = (up + 1.0) * (gt * jax.nn.sigmoid(gt * SWIGLU_ALPHA))
        acc = acc + jnp.dot(act.astype(BF16), wd_bf[gs, :], preferred_element_type=F32)
    return acc


def _moe_kernel(be_ref, src_cur, src_nxt, dst_prv, dst_cur, h_hbm, wgu_ref, bgu_ref, wd_ref, bd_ref, ys_hbm,
                x0, x1, y0, y1, wgu_bf, wd_bf, gsem, ssem, *, rows, plane_rows, n_tok):
    i = pl.program_id(0)
    last = pl.num_programs(0) - 1

    changed = jnp.logical_or(i == 0, be_ref[i] != be_ref[jnp.maximum(i - 1, 0)])

    @pl.when(changed)
    def _():
        step = 128

        def cast(j, c):
            rs = pl.ds(pl.multiple_of(j * step, step), step)
            wgu_bf[rs, :] = wgu_ref[0, rs, :].astype(BF16)
            wd_bf[rs, :] = wd_ref[0, rs, :].astype(BF16)
            return c

        lax.fori_loop(0, D_MODEL // step, cast, 0)

    def run(par):
        xc, xn = (x0, x1) if par == 0 else (x1, x0)
        yc, yp = (y0, y1) if par == 0 else (y1, y0)
        gather = functools.partial(_start_rows, hbm=h_hbm, rows=rows, to_hbm=False)
        scatter = functools.partial(_start_rows, hbm=ys_hbm, rows=rows, to_hbm=True)
        gathered = functools.partial(_wait_rows, h_hbm, rows=rows, to_hbm=False)
        scattered = functools.partial(_wait_rows, ys_hbm, rows=rows, to_hbm=True)

        @pl.when(i == 0)
        def _():
            yp[...] = jnp.zeros(yp.shape, yp.dtype)
            spare = plane_rows - n_tok
            fills = [pltpu.make_async_copy(yp.at[pl.ds(0, spare * TILE_ROWS)],
                                           ys_hbm.at[pl.ds((k * plane_rows + n_tok) * TILE_ROWS, spare * TILE_ROWS)],
                                           ssem.at[1 - par])
                     for k in range(TOP_K)]
            for d in fills:
                d.start()
            for d in fills:
                d.wait()
            gather(src_cur, buf=xc, sem=gsem.at[par])

        gathered(xc, gsem.at[par])
        gather(src_nxt, buf=xn, sem=gsem.at[1 - par])

        @pl.when(i >= 1)
        def _():
            scattered(yc, ssem.at[par])

        used = i < be_ref[last + 1]

        @pl.when(used)
        def _():
            scatter(dst_prv, buf=yp, sem=ssem.at[1 - par])
            xb = _load_token_tiles(xc, rows).astype(BF16)
            _store_token_tiles(yc, _expert_mlp(xb, wgu_bf, wd_bf, bgu_ref, bd_ref, rows))

        @pl.when(jnp.logical_not(used))
        def _():
            scatter(dst_prv, buf=yp, sem=ssem.at[1 - par])

        @pl.when(i == last)
        def _():
            scatter(dst_cur, buf=yc, sem=ssem.at[par])
            gathered(xn, gsem.at[1 - par])
            scattered(yp, ssem.at[1 - par])
            scattered(yc, ssem.at[par])

    for par in range(2):
        pl.when(lax.rem(i, 2) == par)(functools.partial(run, par))


def _moe_experts(block_e, ids, h2, w_gate_up, b_gate_up, w_down, b_down, *, rows, plane_rows, n_tok):
    n_blocks = ids.shape[0] - 2
    ids = lax.optimization_barrier(ids)
    tok = lax.shift_right_logical(ids, 2)
    src = tok * TILE_ROWS
    dst = ((ids & (TOP_K - 1)) * plane_rows + tok) * TILE_ROWS
    idblk = lambda off: pl.BlockSpec((1, 1, rows), lambda i, be: (i + off, 0, 0), memory_space=pltpu.SMEM)
    grid_spec = pltpu.PrefetchScalarGridSpec(
        num_scalar_prefetch=1,
        grid=(n_blocks,),
        in_specs=[
            idblk(1), idblk(2), idblk(0), idblk(1),
            pl.BlockSpec(memory_space=pl.ANY),
            pl.BlockSpec((1, D_MODEL, 2 * D_MODEL), lambda i, be: (be[i], 0, 0)),
            pl.BlockSpec((1, 1, 2 * D_MODEL), lambda i, be: (be[i], 0, 0)),
            pl.BlockSpec((1, D_MODEL, D_MODEL), lambda i, be: (be[i], 0, 0)),
            pl.BlockSpec((1, 1, D_MODEL), lambda i, be: (be[i], 0, 0)),
        ],
        out_specs=pl.BlockSpec(memory_space=pl.ANY),
        scratch_shapes=[pltpu.VMEM((rows * TILE_ROWS, LANES), F32)] * 4 + [
            pltpu.VMEM((D_MODEL, 2 * D_MODEL), BF16),
            pltpu.VMEM((D_MODEL, D_MODEL), BF16),
            pltpu.SemaphoreType.DMA((2,)),
            pltpu.SemaphoreType.DMA((2,))],
    )
    return pl.pallas_call(
        functools.partial(_moe_kernel, rows=rows, plane_rows=plane_rows, n_tok=n_tok),
        grid_spec=grid_spec,
        out_shape=jax.ShapeDtypeStruct((TOP_K * plane_rows * TILE_ROWS, LANES), F32),
        compiler_params=_cparams("arbitrary"),
        name="moe_experts",
    )(block_e, src, src, dst, dst, h2, w_gate_up, b_gate_up[:, None], w_down, b_down[:, None])


def _combine_kernel(ys_ref, x1_ref, gate_ref, lnf_ref, y_ref):
    acc = x1_ref[...]
    tm = acc.shape[0]
    for kslot in range(TOP_K):
        acc = acc + gate_ref[:, kslot:kslot + 1] * _load_token_tiles(ys_ref, tm, kslot)
    y_ref[...] = _rms(acc, lnf_ref[...])


def _combine(ys, x1, gates, ln_f, *, tm, n, row0):
    blk0 = row0 // tm
    return pl.pallas_call(
        _combine_kernel,
        grid=(n // tm,),
        in_specs=[
            pl.BlockSpec((TOP_K, tm * TILE_ROWS, LANES), lambda i: (0, i + blk0, 0)),
            pl.BlockSpec((tm, D_MODEL), lambda i: (i + blk0, 0)),
            pl.BlockSpec((tm, LANES), lambda i: (i + blk0, 0)),
            pl.BlockSpec(ln_f.shape, lambda i: (0, 0)),
        ],
        out_specs=pl.BlockSpec((tm, D_MODEL), lambda i: (i, 0)),
        out_shape=jax.ShapeDtypeStruct((n, D_MODEL), F32),
        compiler_params=_cparams("parallel"),
        name="combine",
    )(ys.reshape(TOP_K, -1, LANES), x1, gates, ln_f)


def _route(topi, n_pad, rows):
    n_tok = topi.shape[0]
    e_flat = topi.reshape(-1)
    nk = e_flat.shape[0]
    n_blocks = -(-(nk + N_EXPERTS * (rows - 1)) // rows)
    spare = n_pad - n_tok
    assert 3 * rows <= TOP_K * spare
    experts = jnp.arange(N_EXPERTS, dtype=jnp.int32)
    order = jnp.argsort(e_flat).astype(jnp.int32)
    sizes = jnp.sum((e_flat[:, None] == experts[None, :]).astype(jnp.int32), axis=0)
    padded = (sizes + rows - 1) // rows * rows
    pends = jnp.cumsum(padded)
    slot = jnp.arange(n_blocks * rows, dtype=jnp.int32)
    past = (slot[:, None] >= pends[None, :]).astype(jnp.int32)
    e_slot = jnp.sum(past, axis=1)
    off = slot - jnp.sum(past * padded[None, :], axis=1)
    src = jnp.sum(past * sizes[None, :], axis=1) + off
    size_e = jnp.sum((e_slot[:, None] == experts[None, :]) * sizes[None, :], axis=1)

    def pad_ids(d):
        return (TOP_K * (n_tok + d % spare) + d // spare).astype(jnp.int32)

    ids = jnp.where(off < size_e, order[jnp.clip(src, 0, nk - 1)], pad_ids((slot // rows) % 2 * rows + slot % rows))
    end_ids = pad_ids(2 * rows + jnp.arange(rows, dtype=jnp.int32))
    ids = jnp.concatenate([end_ids, ids, end_ids]).reshape(n_blocks + 2, 1, rows)
    block_e = jnp.minimum(e_slot[::rows], N_EXPERTS - 1).astype(jnp.int32)
    n_used = (pends[-1] // rows).astype(jnp.int32)
    return jnp.concatenate([block_e, n_used[None]]), ids


def kernel(x_prompt, x_sample, cache_k, cache_v, state_wkv, state_shift, rel_bias, ln1, w_in, rwkv_mu, w0,
           w_decay_up, a0, w_a_up, w_g_up, k_k, k_a, r_k, gn_w, gn_b, w_branch_a, w_branch_b, w_out, ln2,
           w_router, b_router, w_gate_up, b_gate_up, w_down, b_down, ln_f):
    bp, seq, _ = x_prompt.shape
    db = x_sample.shape[0]
    n_p = bp * seq
    n_tot = n_p + db
    n_pad = -(-(n_tot + -(-3 * MOE_ROWS // TOP_K)) // TOKEN_TILE) * TOKEN_TILE
    l = 0

    row = lambda a: a.reshape(1, -1)
    seg = _head_seg()
    prep_params = (row(rwkv_mu[l]), row(w0[l]), w_decay_up[l], row(a0[l]), w_a_up[l], w_g_up[l],
                   row(k_k[l]), row(k_a[l]), row(r_k[l]), seg)
    wr_pad = jnp.pad(w_router[l], ((0, 0), (0, LANES - N_EXPERTS))).astype(BF16)
    br_pad = jnp.pad(b_router[l], (0, LANES - N_EXPERTS), constant_values=NEG).reshape(1, LANES)
    post_small = (row(gn_w[l]), row(gn_b[l]), seg)
    post_tail = (row(ln2[l]), wr_pad, br_pad)

    w_in_bf = w_in[l].astype(BF16)
    consts = (post_small + (w_branch_a[l].astype(BF16), w_branch_b[l].astype(BF16), w_out[l].astype(BF16))
              + post_tail)

    xp = x_prompt.reshape(n_p, D_MODEL)
    q_p, k_p, v_p, sg_p, kt_p, vt_p, shift_p, *prep_p = _inproj_prompt(xp, row(ln1[l]), w_in_bf, prep_params,
                                                                     tm=TOKEN_TILE, seq=seq)
    as3 = lambda a: a.reshape(bp, seq, -1)
    oa_p = _attn_prompt(as3(q_p), as3(k_p), as3(v_p), _prompt_bias(rel_bias))
    r_, w_, k2_, vr_, kk_, ka_, bonus_p, g_p = prep_p
    s0_p = jnp.zeros((bp, R_HEADS, R_HEAD, R_HEAD), F32)
    y_p, wkv_p = _lane_scan(r_, w_, k2_, vr_, kk_, ka_, s0_p, tc=SCAN_CHUNK)
    flat = lambda a: a.reshape(n_p, -1)
    bufs = _post(xp, flat(y_p), flat(bonus_p), flat(g_p), flat(oa_p), sg_p, consts,
                 tm=TOKEN_TILE, n_total=n_pad)

    xs = x_sample.reshape(db, D_MODEL)
    q_s, k_s, v_s, rc_s, sg_s = _inproj(xs, row(ln1[l]), w_in_bf, tm=db)
    bias_s, b0_s = _sample_bias(rel_bias, cache_k.shape[2])
    oa_s = _attn_sample(q_s, k_s, v_s, cache_k[l], cache_v[l], bias_s, b0_s)
    prep_s = _rwkv_prep_sample(rc_s, state_shift[l], prep_params)
    sr, sw, sk2, svr, skk, ska, bonus_s, g_s = [a[:, None] for a in prep_s]
    y_s, wkv_s = _rwkv_scan(sr, sw, sk2, svr, skk, ska, state_wkv[l], bb=SCAN_BATCH, tc=1)
    x1_all, h2_all, ti_all, tg_all = _post(xs, y_s[:, 0], bonus_s[:, 0], g_s[:, 0], oa_s, sg_s, consts,
                                           tm=db, n_total=n_pad, row0=n_p, into=bufs)

    block_e, ids = _route(ti_all[:n_tot, :TOP_K], n_pad, MOE_ROWS)
    ys = _moe_experts(block_e, ids, h2_all, w_gate_up[l], b_gate_up[l], w_down[l], b_down[l],
                      rows=MOE_ROWS, plane_rows=n_pad, n_tok=n_tot)
    lnf = row(ln_f)
    y_prompt = _combine(ys, x1_all, tg_all, lnf, tm=TOKEN_TILE, n=n_p, row0=0)
    y_sample = _combine(ys, x1_all, tg_all, lnf, tm=db, n=db, row0=n_p)

    heads = lambda a, b_: a.reshape(1, b_, -1, A_HEADS, HEAD_DIM)
    return (y_prompt.reshape(bp, seq, D_MODEL), y_sample.reshape(db, 1, D_MODEL),
            kt_p.transpose(0, 3, 1, 2)[None], vt_p.transpose(0, 3, 1, 2)[None], wkv_p[None], shift_p[:, 0][None],
            heads(k_s, db), heads(v_s, db), wkv_s[None], rc_s[None])
```

```python
import functools
import math

import jax
import jax.numpy as jnp
from jax import lax
from jax.experimental import pallas as pl
from jax.experimental.pallas import tpu as pltpu

F32 = jnp.float32
BF16 = jnp.bfloat16

D_MODEL = 1024
A_HEADS = 8
HEAD_DIM = 64
A_WIDTH = A_HEADS * HEAD_DIM
PATTERNS = ((128, 1), (512, 4), (2048, 16))
BLOCK = 128
NUM_BUCKETS = 32
MAX_DISTANCE = 2048
SCALE = HEAD_DIM ** -0.5
NEG = -1e30
R_HEADS = 8
R_HEAD = 64
R_WIDTH = R_HEADS * R_HEAD
DECAY_LORA = 64
AAA_LORA = 64
GATE_LORA = 128
GN_EPS = 64e-5
N_RWKV_COLS = 3 * R_WIDTH + DECAY_LORA + AAA_LORA + GATE_LORA
N_IN_COLS = 3 * A_WIDTH + N_RWKV_COLS + 2 * D_MODEL
N_EXPERTS = 32
TOP_K = 4
SWIGLU_LIMIT = 7.0
SWIGLU_ALPHA = 1.702
RMS_EPS = 1e-6

LANES = 128
SUBLANES = 8
VMEM_LIMIT = 56 * 1024 * 1024
TOKEN_TILE = 256
POST_TILE = 512
MOE_ROWS = 512
SCAN_BATCH = 2
SCAN_CHUNK = 64


def _cparams(*sem):
    return pltpu.CompilerParams(dimension_semantics=sem, vmem_limit_bytes=VMEM_LIMIT)


def _rms(x, g):
    return x * lax.rsqrt(jnp.mean(x * x, axis=-1, keepdims=True) + RMS_EPS) * g


def _dot(a, b):
    return jnp.dot(a.astype(BF16), b.astype(BF16), preferred_element_type=F32)


def _bf(x):
    return x.astype(BF16).astype(F32)


def _split3(x):
    hi = x.astype(BF16)
    r1 = x - hi.astype(F32)
    mid = r1.astype(BF16)
    lo = (r1 - mid.astype(F32)).astype(BF16)
    return hi, mid, lo


def _segdot(x, seg):
    hi, mid, lo = _split3(x)
    d = lambda p: jnp.dot(p, seg, preferred_element_type=F32)
    return d(hi) + d(mid) + d(lo)


def _head_sums(x, seg):
    return jnp.concatenate([_segdot(x[:, j:j + LANES], seg) for j in range(0, x.shape[1], LANES)], axis=1)


TILE_ROWS = D_MODEL // LANES
assert TILE_ROWS == SUBLANES


def _load_token_tiles(ref, n, *lead):
    return jnp.concatenate([ref[(*lead, pl.ds(p, n, stride=TILE_ROWS), slice(None))] for p in range(TILE_ROWS)], axis=1)


def _store_token_tiles(ref, x):
    n = x.shape[0]
    for p in range(TILE_ROWS):
        ref[pl.ds(p, n, stride=TILE_ROWS), :] = x[:, p * LANES:(p + 1) * LANES]


def _project(x_ref, g_ref, w_ref):
    h = _rms(x_ref[...], g_ref[...]).astype(BF16)
    return lambda c0, c1: _dot(h, w_ref[:, c0:c1])


C_RWKV = 3 * A_WIDTH
C_GATE = C_RWKV + N_RWKV_COLS


def _inproj_kernel(x_ref, g_ref, w_ref, q_ref, k_ref, v_ref, rc_ref, sg_ref):
    mm = _project(x_ref, g_ref, w_ref)
    q_ref[...] = mm(0, A_WIDTH)
    k_ref[...] = mm(A_WIDTH, 2 * A_WIDTH)
    v_ref[...] = mm(2 * A_WIDTH, C_RWKV)
    rc_ref[...] = mm(C_RWKV, C_GATE)
    sg_ref[:, :D_MODEL] = jax.nn.sigmoid(mm(C_GATE, C_GATE + D_MODEL))
    sg_ref[:, D_MODEL:] = jax.nn.sigmoid(mm(C_GATE + D_MODEL, N_IN_COLS))


def _inproj(x, ln1, w, *, tm):
    n = x.shape[0]
    row = lambda c: pl.BlockSpec((tm, c), lambda i: (i, 0))
    full = lambda a: pl.BlockSpec(a.shape, lambda i: (0,) * a.ndim)
    out_cols = (A_WIDTH, A_WIDTH, A_WIDTH, N_RWKV_COLS, 2 * D_MODEL)
    return pl.pallas_call(
        _inproj_kernel,
        grid=(n // tm,),
        in_specs=[row(D_MODEL), full(ln1), full(w)],
        out_specs=[row(c) for c in out_cols],
        out_shape=[jax.ShapeDtypeStruct((n, c), F32) for c in out_cols],
        compiler_params=_cparams("parallel"),
        name="inproj",
    )(x, ln1, w)


N_PREP_PARAMS = 10
N_PREP_OUTS = 8


def _inproj_prompt_kernel(x_ref, g_ref, w_ref, *refs, per):
    params = [r[...] for r in refs[:N_PREP_PARAMS]]
    q_ref, k_ref, v_ref, sg_ref, kt_ref, vt_ref, shift_ref = refs[N_PREP_PARAMS:N_PREP_PARAMS + 7]
    prep_refs = refs[N_PREP_PARAMS + 7:-1]
    carry = refs[-1]
    mm = _project(x_ref, g_ref, w_ref)
    q_ref[...] = mm(0, A_WIDTH)
    k = mm(A_WIDTH, 2 * A_WIDTH)
    v = mm(2 * A_WIDTH, C_RWKV)
    k_ref[...] = k
    v_ref[...] = v
    tm = k.shape[0]
    kt_ref[0] = k.T.reshape(A_HEADS, HEAD_DIM, tm)
    vt_ref[0] = v.T.reshape(A_HEADS, HEAD_DIM, tm)
    sg_ref[:, :D_MODEL] = jax.nn.sigmoid(mm(C_GATE, C_GATE + D_MODEL))
    sg_ref[:, D_MODEL:] = jax.nn.sigmoid(mm(C_GATE + D_MODEL, N_IN_COLS))

    @pl.when(lax.rem(pl.program_id(0), per) == 0)
    def _():
        carry[...] = jnp.zeros(carry.shape, carry.dtype)

    rc = mm(C_RWKV, C_GATE)
    row = lax.broadcasted_iota(jnp.int32, (tm, 1), 0)
    prev = jnp.where(row == 0, carry[...], pltpu.roll(rc, 1, axis=0))
    for o, val in zip(prep_refs, _rwkv_prep_math(rc, prev, *params)):
        o[0] = val
    last = rc[tm - 1:tm, :]
    carry[...] = last
    shift_ref[0] = last


def _inproj_prompt(x, ln1, w, prep_params, *, tm, seq):
    n = x.shape[0]
    per = seq // tm
    row = lambda c: pl.BlockSpec((tm, c), lambda i: (i, 0))
    full = lambda a: pl.BlockSpec(a.shape, lambda i: (0,) * a.ndim)
    t_spec = pl.BlockSpec((1, A_HEADS, HEAD_DIM, tm), lambda i: (i // per, 0, 0, i % per))
    t_shape = jax.ShapeDtypeStruct((n // seq, A_HEADS, HEAD_DIM, seq), F32)
    cols = (A_WIDTH, A_WIDTH, A_WIDTH, 2 * D_MODEL)
    out_specs = ([row(c) for c in cols] + [t_spec, t_spec, pl.BlockSpec((1, 1, N_RWKV_COLS), lambda i: (i // per, 0, 0))]
                 + [pl.BlockSpec((1, tm, R_WIDTH), lambda i: (i // per, i % per, 0))] * N_PREP_OUTS)
    out_shape = ([jax.ShapeDtypeStruct((n, c), F32) for c in cols]
                 + [t_shape, t_shape, jax.ShapeDtypeStruct((n // seq, 1, N_RWKV_COLS), F32)]
                 + [jax.ShapeDtypeStruct((n // seq, seq, R_WIDTH), F32)] * N_PREP_OUTS)
    return pl.pallas_call(
        functools.partial(_inproj_prompt_kernel, per=per),
        grid=(n // tm,),
        in_specs=[row(D_MODEL), full(ln1), full(w)] + [full(a) for a in prep_params],
        out_specs=out_specs,
        out_shape=out_shape,
        scratch_shapes=[pltpu.VMEM((1, N_RWKV_COLS), F32)],
        compiler_params=_cparams("arbitrary"),
        name="inproj_prompt",
    )(x, ln1, w, *prep_params)


def _t5_bucket(dist):
    max_exact = NUM_BUCKETS // 2
    d = jnp.maximum(dist, 1).astype(F32)
    large = max_exact + (jnp.log(d / max_exact) / math.log(MAX_DISTANCE / max_exact)
                         * (NUM_BUCKETS - max_exact)).astype(jnp.int32)
    large = jnp.minimum(large, NUM_BUCKETS - 1)
    return jnp.where(dist < max_exact, dist, large)


def _bias_lookup(rel_bias, dist):
    onehot = (_t5_bucket(dist)[..., None] == jnp.arange(NUM_BUCKETS)).astype(F32)
    return jnp.moveaxis(jnp.dot(onehot, rel_bias.astype(F32), precision=lax.Precision.HIGHEST), -1, 0)


def _prompt_bias(rel_bias):
    i = jnp.arange(BLOCK)[:, None]
    j = jnp.arange(2 * BLOCK)[None, :]
    delta = i + BLOCK - j
    out = []
    for window, dil in PATTERNS:
        n = window // dil
        band = (delta >= 0) & (delta <= n)
        out.append(jnp.where(band[None], _bias_lookup(rel_bias, jnp.clip(delta, 0, n) * dil), NEG))
    return jnp.stack(out, 0)


def _sample_bias(rel_bias, wb):
    dist = wb - jnp.arange(wb)
    out = []
    for window, dil in PATTERNS:
        member = (dist % dil == 0) & (dist <= window)
        out.append(jnp.where(member[None], _bias_lookup(rel_bias, dist), NEG))
    b0 = _bias_lookup(rel_bias, jnp.zeros((1,), jnp.int32))
    return jnp.stack(out, 0), b0


def _attn_prompt_kernel(q_ref, k_ref, v_ref, bias_ref, o_ref, os_ref, ls_ref, *, seq):
    nt = (((1,), (1,)), ((), ()))
    npat = len(PATTERNS)
    w2 = 2 * HEAD_DIM
    lo = lax.broadcasted_iota(jnp.int32, (BLOCK, w2), 1) < HEAD_DIM
    key = lax.broadcasted_iota(jnp.int32, (2 * BLOCK, 2 * BLOCK), 1)

    for g, (window, dil) in enumerate(PATTERNS):
        nb = seq // (BLOCK * dil)
        sh = dil.bit_length() - 1

        def body(it, carry, g=g, dil=dil, nb=nb, sh=sh):
            r = it & (dil - 1)
            c = it >> sh
            start = c * (BLOCK * dil) + r
            rows = pl.ds(start, BLOCK, stride=dil) if dil > 1 else pl.ds(start, BLOCK)
            qb = q_ref[0, rows, :]
            kb = k_ref[0, rows, :]
            vb = v_ref[0, rows, :]
            if nb > 1:
                pstart = jnp.maximum(c - 1, 0) * (BLOCK * dil) + r
                prow = pl.ds(pstart, BLOCK, stride=dil) if dil > 1 else pl.ds(pstart, BLOCK)
                kp = k_ref[0, prow, :]
                vp = v_ref[0, prow, :]
            q2 = jnp.concatenate([jnp.where(lo, qb, 0.0), jnp.where(lo, 0.0, qb)], axis=0).astype(BF16)
            if nb > 1:
                k2 = jnp.concatenate([kp, kb], axis=0)
                v2 = jnp.concatenate([vp, vb], axis=0)
                bias = bias_ref[g, 0]
                valid = (bias > 0.5 * NEG) & ((key >= BLOCK) | (c > 0))
            else:
                k2, v2 = kb, vb
                bias = bias_ref[g, 0, :, BLOCK:]
                valid = bias > 0.5 * NEG
            s = lax.dot_general(q2, k2.astype(BF16), nt, preferred_element_type=F32)
            s = jnp.where(valid, s * SCALE + bias, NEG)
            m = jnp.max(s, axis=-1, keepdims=True)
            p = jnp.exp(s - m)
            l = jnp.sum(p, axis=-1, keepdims=True)
            o2 = jnp.dot((p * (1.0 / l)).astype(BF16), v2.astype(BF16), preferred_element_type=F32)
            lse = jnp.broadcast_to(m + jnp.log(l), (2 * BLOCK, w2))
            os_ref[g, rows, :] = jnp.where(lo, o2[:BLOCK], o2[BLOCK:])
            ls_ref[g, rows, :] = jnp.where(lo, lse[:BLOCK], lse[BLOCK:])
            return carry

        lax.fori_loop(0, dil * nb, body, 0, unroll=16)

    chunk = 256

    def merge(ch, carry):
        rows = pl.ds(pl.multiple_of(ch * chunk, chunk), chunk)
        lse = [ls_ref[g, rows, :] for g in range(npat)]
        top = functools.reduce(jnp.maximum, lse)
        e = [jnp.exp(x - top) for x in lse]
        inv = 1.0 / functools.reduce(lambda a, b_: a + b_, e)
        acc = jnp.zeros((chunk, 2 * HEAD_DIM), F32)
        for g in range(npat):
            acc = acc + (e[g] * inv) * os_ref[g, rows, :]
        o_ref[0, rows, :] = acc
        return carry

    lax.fori_loop(0, seq // chunk, merge, 0)


def _attn_prompt(q, k, v, bias):
    b, s, _ = q.shape
    w2 = 2 * HEAD_DIM
    qspec = pl.BlockSpec((1, s, w2), lambda i, p: (i, 0, p))
    return pl.pallas_call(
        functools.partial(_attn_prompt_kernel, seq=s),
        grid=(b, A_HEADS // 2),
        in_specs=[qspec, qspec, qspec,
                  pl.BlockSpec((len(PATTERNS), 1, 2 * BLOCK, 2 * BLOCK), lambda i, p: (0, p, 0, 0))],
        out_specs=qspec,
        out_shape=jax.ShapeDtypeStruct((b, s, A_WIDTH), F32),
        scratch_shapes=[pltpu.VMEM((len(PATTERNS), s, w2), F32)] * 2,
        compiler_params=_cparams("parallel", "parallel"),
        name="attn_prompt",
    )(q, k, v, bias.reshape(len(PATTERNS), A_HEADS // 2, 2 * BLOCK, 2 * BLOCK))


def _attn_sample_kernel(q_ref, kn_ref, vn_ref, kt_ref, vt_ref, bias_ref, b0_ref, o_ref):
    npat = len(PATTERNS)
    nt = (((1,), (1,)), ((), ()))
    q = q_ref[0]
    v_new = _bf(vn_ref[0])
    s0 = jnp.sum(_bf(q) * _bf(kn_ref[0]), axis=-1, keepdims=True) * SCALE + b0_ref[...]
    s = jnp.concatenate([_dot(q[h:h + 1], kt_ref[0, h]) for h in range(A_HEADS)], axis=0) * SCALE
    ps, p0s, lses = [], [], []
    for g in range(npat):
        bias = bias_ref[g]
        sg = jnp.where(bias > 0.5 * NEG, s + bias, NEG)
        m = jnp.maximum(jnp.max(sg, axis=-1, keepdims=True), s0)
        l = jnp.sum(jnp.exp(sg - m), axis=-1, keepdims=True) + jnp.exp(s0 - m)
        lse = m + jnp.log(l)
        ps.append(jnp.exp(sg - lse))
        p0s.append(_bf(jnp.exp(s0 - lse)))
        lses.append(lse)
    top = functools.reduce(jnp.maximum, lses)
    e = [jnp.exp(x - top) for x in lses]
    inv = 1.0 / functools.reduce(lambda a, b_: a + b_, e)
    w = [_bf(x * inv) for x in e]
    rows = []
    for h in range(A_HEADS):
        p_h = jnp.concatenate([p[h:h + 1] for p in ps], axis=0)
        o_h = lax.dot_general(p_h.astype(BF16), vt_ref[0, h].astype(BF16), nt, preferred_element_type=F32)
        acc = jnp.zeros((1, HEAD_DIM), F32)
        for g in range(npat):
            acc = acc + w[g][h:h + 1] * _bf(o_h[g:g + 1] + p0s[g][h:h + 1] * v_new[h:h + 1])
        rows.append(acc)
    o_ref[0] = jnp.concatenate(rows, axis=0)


def _attn_sample(q, k_new, v_new, cache_k, cache_v, bias, b0):
    db, wb = cache_k.shape[:2]
    heads = lambda a: a.reshape(db, A_HEADS, HEAD_DIM)
    vec = pl.BlockSpec((1, A_HEADS, HEAD_DIM), lambda i: (i, 0, 0))
    full = lambda a: pl.BlockSpec(a.shape, lambda i: (0,) * a.ndim)
    rows_minor = lambda a: a.transpose(0, 2, 3, 1)
    cache = pl.BlockSpec((1, A_HEADS, HEAD_DIM, wb), lambda i: (i, 0, 0, 0))
    args = [heads(q), heads(k_new), heads(v_new), rows_minor(cache_k), rows_minor(cache_v), bias, b0]
    return pl.pallas_call(
        _attn_sample_kernel,
        grid=(db,),
        in_specs=[vec, vec, vec, cache, cache, full(bias), full(b0)],
        out_specs=vec,
        out_shape=jax.ShapeDtypeStruct((db, A_HEADS, HEAD_DIM), F32),
        compiler_params=_cparams("parallel"),
        name="attn_sample",
    )(*args).reshape(db, A_WIDTH)


def _rwkv_prep_math(rc, prev, mu, w0, wd, a0, wa, wg, k_k, k_a, r_k, seg):
    xr = rc + (prev - rc) * mu
    o1, o2, o3 = R_WIDTH, 2 * R_WIDTH, 3 * R_WIDTH
    r = xr[:, :o1]
    kr = xr[:, o1:o2]
    vr = xr[:, o2:o3]
    xw = xr[:, o3:o3 + DECAY_LORA]
    xa = xr[:, o3 + DECAY_LORA:o3 + DECAY_LORA + AAA_LORA]
    xg = xr[:, o3 + DECAY_LORA + AAA_LORA:]
    z = -(w0 + _dot(jnp.tanh(xw), wd))
    softplus = jnp.maximum(z, 0.0) + jnp.log(1.0 + jnp.exp(-jnp.abs(z)))
    decay = jnp.exp(-jnp.exp(-softplus - 0.5))
    a = jax.nn.sigmoid(a0 + _dot(xa, wa))
    g = _dot(jax.nn.sigmoid(xg), wg)
    kk = kr * k_k
    k2 = kr * (1.0 + (a - 1.0) * k_a)
    kk = kk / jnp.maximum(jnp.sqrt(_head_sums(kk * kk, seg)), 1e-12)
    bonus = _head_sums(r * k2 * r_k, seg) * vr
    return _bf(r), decay, k2, vr, _bf(kk), kk * a, bonus, g


def _rwkv_prep_sample_kernel(rc_ref, prev_ref, *refs):
    p = [x[...] for x in refs[:N_PREP_PARAMS]]
    outs = refs[N_PREP_PARAMS:]
    for o, val in zip(outs, _rwkv_prep_math(rc_ref[...], prev_ref[...], *p)):
        o[...] = val


def _head_seg():
    head = jnp.arange(LANES) // R_HEAD
    return (head[:, None] == head[None, :]).astype(BF16)


def _rwkv_prep_sample(rc, prev, params):
    n = rc.shape[0]
    full = lambda a: pl.BlockSpec(a.shape, lambda i: (0,) * a.ndim)
    return pl.pallas_call(
        _rwkv_prep_sample_kernel,
        grid=(1,),
        in_specs=[full(rc), full(prev)] + [full(a) for a in params],
        out_specs=[pl.BlockSpec((n, R_WIDTH), lambda i: (0, 0))] * 8,
        out_shape=[jax.ShapeDtypeStruct((n, R_WIDTH), F32)] * 8,
        compiler_params=_cparams("arbitrary"),
        name="rwkv_prep_sample",
    )(rc, prev, *params)


def _seg_lane_sum(x, lo_mask):
    lo = jnp.sum(jnp.where(lo_mask, x, 0.0), axis=-1, keepdims=True)
    hi = jnp.sum(jnp.where(lo_mask, 0.0, x), axis=-1, keepdims=True)
    return jnp.where(lo_mask, lo, hi)


def _rwkv_scan_kernel(r_ref, w_ref, k_ref, v_ref, kk_ref, ka_ref, s0_ref, y_ref, sT_ref, st_ref, *, bb, tc):
    pairs = R_HEADS // 2
    w2 = 2 * R_HEAD

    @pl.when(pl.program_id(1) == 0)
    def _():
        for b in range(bb):
            for p in range(pairs):
                st_ref[b, p] = jnp.concatenate([s0_ref[b, 2 * p], s0_ref[b, 2 * p + 1]], axis=1)

    lane = lax.broadcasted_iota(jnp.int32, (R_HEAD, w2), 1)
    sub = lax.broadcasted_iota(jnp.int32, (R_HEAD, w2), 0)
    lo_mask = lane < R_HEAD
    eye2 = (lane & (R_HEAD - 1)) == sub

    grp = min(8, tc)

    def group(tg, carry):
        rows = pl.ds(pl.multiple_of(tg * grp, grp), grp)
        for b in range(bb):
            for p in range(pairs):
                cols = slice(p * w2, (p + 1) * w2)
                r8, w8, k8, v8, kk8, ka8 = (ref[b, rows, cols] for ref in (r_ref, w_ref, k_ref, v_ref, kk_ref, ka_ref))
                s = st_ref[b, p]
                sb = _bf(s)
                ys = []
                for j in range(grp):
                    row = lambda a: a[j:j + 1, :]
                    sa = -_seg_lane_sum(sb * row(kk8), lo_mask)
                    vcol = _seg_lane_sum(jnp.where(eye2, jnp.broadcast_to(row(v8), (R_HEAD, w2)), 0.0), lo_mask)
                    s = s * row(w8) + sa * row(ka8) + vcol * row(k8)
                    sb = _bf(s)
                    yfull = _seg_lane_sum(sb * row(r8), lo_mask)
                    ys.append(jnp.sum(jnp.where(eye2, yfull, 0.0), axis=0, keepdims=True))
                st_ref[b, p] = s
                y_ref[b, rows, cols] = jnp.concatenate(ys, axis=0) if grp > 1 else ys[0]
        return carry

    lax.fori_loop(0, tc // grp, group, 0)

    @pl.when(pl.program_id(1) == pl.num_programs(1) - 1)
    def _():
        for b in range(bb):
            for p in range(pairs):
                s = st_ref[b, p]
                sT_ref[b, 2 * p] = s[:, :R_HEAD]
                sT_ref[b, 2 * p + 1] = s[:, R_HEAD:]


def _rwkv_scan(r, w, k, v, kk, ka, s0, *, bb, tc):
    b, t, _ = r.shape
    seq = pl.BlockSpec((bb, tc, R_WIDTH), lambda i, j: (i, j, 0))
    state = pl.BlockSpec((bb, R_HEADS, R_HEAD, R_HEAD), lambda i, j: (i, 0, 0, 0))
    return pl.pallas_call(
        functools.partial(_rwkv_scan_kernel, bb=bb, tc=tc),
        grid=(b // bb, t // tc),
        in_specs=[seq] * 6 + [state],
        out_specs=[seq, state],
        out_shape=[jax.ShapeDtypeStruct((b, t, R_WIDTH), F32),
                   jax.ShapeDtypeStruct((b, R_HEADS, R_HEAD, R_HEAD), F32)],
        scratch_shapes=[pltpu.VMEM((bb, R_HEADS // 2, R_HEAD, 2 * R_HEAD), F32)],
        compiler_params=_cparams("parallel", "arbitrary"),
        name="rwkv_scan",
    )(r, w, k, v, kk, ka, s0)


CHAINS = LANES // 2
K2 = R_HEAD // 2


def _chain_rows(x):
    b, t, _ = x.shape
    x = x.reshape(b, t, R_HEADS, 2, K2).transpose(1, 4, 3, 0, 2)
    return x.reshape(t, K2, LANES)


def _lane_scan_kernel(kk_ref, w_ref, ka_ref, kx_ref, r_ref, v_ref, s0_ref, y_ref, sT_ref, s_ref, *, tc):
    lo = lax.broadcasted_iota(jnp.int32, (K2, LANES), 1) < CHAINS

    @pl.when(pl.program_id(0) == 0)
    def _():
        for k2 in range(K2):
            s_ref[k2] = s0_ref[k2]

    def both_halves(x):
        return x + pltpu.roll(x, CHAINS, axis=1)

    nhalf = 2
    vh = R_HEAD // nhalf

    def row(ref, t, k2):
        return jnp.broadcast_to(ref[t, k2:k2 + 1, :], (vh, LANES))

    acc0 = []
    for hf in range(nhalf):
        vs = slice(hf * vh, (hf + 1) * vh)
        a = jnp.zeros((vh, LANES), F32)
        for k2 in range(K2):
            a = a + _bf(s_ref[k2, vs, :]) * row(kk_ref, 0, k2)
        acc0.append(a)

    def step(t, acc):
        tn = jnp.minimum(t + 1, tc - 1)
        nxt, ys = [], []
        v_t = v_ref[t]
        v_sw = pltpu.roll(v_t, CHAINS, axis=1)
        for hf in range(nhalf):
            vs = slice(hf * vh, (hf + 1) * vh)
            vv = jnp.where(lo, v_t, v_sw) if hf == 0 else jnp.where(lo, v_sw, v_t)
            sa = -both_halves(acc[hf])
            yacc = jnp.zeros((vh, LANES), F32)
            nacc = jnp.zeros((vh, LANES), F32)
            for k2 in range(K2):
                s = s_ref[k2, vs, :] * row(w_ref, t, k2) + sa * row(ka_ref, t, k2) + vv * row(kx_ref, t, k2)
                s_ref[k2, vs, :] = s
                sb = _bf(s)
                yacc = yacc + sb * row(r_ref, t, k2)
                nacc = nacc + sb * row(kk_ref, tn, k2)
            ys.append(both_halves(yacc))
            nxt.append(nacc)
        y_ref[t] = jnp.where(lo, ys[0], ys[1])
        return tuple(nxt)

    lax.fori_loop(0, tc, step, tuple(acc0))

    @pl.when(pl.program_id(0) == pl.num_programs(0) - 1)
    def _():
        for k2 in range(K2):
            sT_ref[k2] = s_ref[k2]


def _lane_scan(r, w, k, v, kk, ka, s0, *, tc):
    b, t, _ = r.shape
    assert b * R_HEADS == CHAINS
    ops = [_chain_rows(x) for x in (kk, w, ka, k, r, v)]
    s0c = s0.reshape(b, R_HEADS, 2, K2, 2, K2).transpose(5, 2, 3, 4, 0, 1).reshape(K2, R_HEAD, LANES)
    rows = pl.BlockSpec((tc, K2, LANES), lambda i: (i, 0, 0))
    state = pl.BlockSpec((K2, R_HEAD, LANES), lambda i: (0, 0, 0))
    y, st = pl.pallas_call(
        functools.partial(_lane_scan_kernel, tc=tc),
        grid=(t // tc,),
        in_specs=[rows] * 6 + [state],
        out_specs=[rows, state],
        out_shape=[jax.ShapeDtypeStruct((t, K2, LANES), F32), jax.ShapeDtypeStruct((K2, R_HEAD, LANES), F32)],
        scratch_shapes=[pltpu.VMEM((K2, R_HEAD, LANES), F32)],
        compiler_params=_cparams("arbitrary"),
        name="rwkv_lane_scan",
    )(*ops, s0c)
    y = y.reshape(t, K2, 2, b, R_HEADS).transpose(3, 0, 4, 2, 1).reshape(b, t, R_WIDTH)
    st = st.reshape(K2, 2, K2, 2, b, R_HEADS).transpose(4, 5, 1, 2, 3, 0).reshape(b, R_HEADS, R_HEAD, R_HEAD)
    return y, st


def _post_kernel(x_ref, y_ref, bonus_ref, g_ref, oa_ref, sg_ref, gnw_ref, gnb_ref, seg_ref, wba_ref, wbb_ref,
                 wout_ref, ln2_ref, wr_ref, br_ref, *rest, aliased, n_main):
    outs = rest[aliased:]
    x1_ref, h2_ref, ti_ref, tg_ref = outs

    @pl.when(pl.program_id(0) >= n_main)
    def _():
        for o in outs:
            o[...] = jnp.zeros(o.shape, o.dtype)

    @pl.when(pl.program_id(0) < n_main)
    def _():
        _post_body(x_ref, y_ref, bonus_ref, g_ref, oa_ref, sg_ref, gnw_ref, gnb_ref, seg_ref, wba_ref, wbb_ref,
                   wout_ref, ln2_ref, wr_ref, br_ref, x1_ref, h2_ref, ti_ref, tg_ref)


def _post_body(x_ref, y_ref, bonus_ref, g_ref, oa_ref, sg_ref, gnw_ref, gnb_ref, seg_ref, wba_ref, wbb_ref,
               wout_ref, ln2_ref, wr_ref, br_ref, x1_ref, h2_ref, ti_ref, tg_ref):
    y = y_ref[...]
    seg = seg_ref[...]
    mu = _head_sums(y, seg) * (1.0 / R_HEAD)
    yc = y - mu
    var = _head_sums(yc * yc, seg) * (1.0 / R_HEAD)
    yn = yc * lax.rsqrt(var + GN_EPS) * gnw_ref[...] + gnb_ref[...]
    o_b = (yn + bonus_ref[...]) * g_ref[...]
    mixed = (sg_ref[:, :D_MODEL] * _dot(oa_ref[...], wba_ref[...])
             + sg_ref[:, D_MODEL:] * _dot(o_b, wbb_ref[...]))
    x1 = x_ref[...] + _dot(mixed, wout_ref[...])
    x1_ref[...] = x1
    h2 = _rms(x1, ln2_ref[...])
    _store_token_tiles(h2_ref, h2)
    logits = _dot(h2, wr_ref[...]) + br_ref[...]
    lane = lax.broadcasted_iota(jnp.int32, logits.shape, 1).astype(F32)
    work = logits
    vals, idxs = [], []
    for _ in range(TOP_K):
        m = jnp.max(work, axis=-1, keepdims=True)
        idx = jnp.min(jnp.where(work == m, lane, float(LANES)), axis=-1, keepdims=True)
        vals.append(m)
        idxs.append(idx)
        work = jnp.where(lane == idx, -jnp.inf, work)
    es = [jnp.exp(v - vals[0]) for v in vals]
    tot = es[0] + es[1] + es[2] + es[3]
    ti = jnp.zeros(logits.shape, F32)
    tg = jnp.zeros(logits.shape, F32)
    for kslot in range(TOP_K):
        ti = jnp.where(lane == float(kslot), idxs[kslot], ti)
        tg = jnp.where(lane == float(kslot), es[kslot] / tot, tg)
    ti_ref[...] = ti.astype(jnp.int32)
    tg_ref[...] = tg


def _post(x, y, bonus, g, oa, sg, consts, *, tm, n_total, row0=0, into=None):
    n = x.shape[0]
    blk0 = row0 // tm
    n_main = n // tm
    assert n % tm == 0 and n_total % tm == 0 and row0 % tm == 0
    steps = n_total // tm if into is None else n_main
    row = lambda c: pl.BlockSpec((tm, c), lambda i: (jnp.minimum(i, n_main - 1), 0))
    orow = lambda c: pl.BlockSpec((tm, c), lambda i: (i + blk0, 0))
    otile = pl.BlockSpec((tm * TILE_ROWS, LANES), lambda i: (i + blk0, 0))
    full = lambda a: pl.BlockSpec(a.shape, lambda i: (0,) * a.ndim)
    ins = [x, y, bonus, g, oa, sg, *consts]
    in_specs = [row(D_MODEL), row(R_WIDTH), row(R_WIDTH), row(R_WIDTH), row(A_WIDTH), row(2 * D_MODEL)]
    in_specs += [full(a) for a in consts]
    aliases = {}
    if into is not None:
        aliases = {len(ins) + i: i for i in range(len(into))}
        in_specs += [pl.BlockSpec(memory_space=pl.ANY)] * len(into)
        ins += list(into)
    return pl.pallas_call(
        functools.partial(_post_kernel, aliased=len(aliases), n_main=n_main),
        grid=(steps,),
        in_specs=in_specs,
        out_specs=[orow(D_MODEL), otile, orow(LANES), orow(LANES)],
        out_shape=[jax.ShapeDtypeStruct((n_total, D_MODEL), F32),
                   jax.ShapeDtypeStruct((n_total * TILE_ROWS, LANES), F32),
                   jax.ShapeDtypeStruct((n_total, LANES), jnp.int32),
                   jax.ShapeDtypeStruct((n_total, LANES), F32)],
        input_output_aliases=aliases,
        compiler_params=_cparams("parallel"),
        name="post",
    )(*ins)


def _start_rows(rows_ref, hbm, buf, sem, *, rows, to_hbm):
    for r in range(rows):
        at = pl.ds(pl.multiple_of(rows_ref[0, 0, r], TILE_ROWS), TILE_ROWS)
        here = pl.ds(r * TILE_ROWS, TILE_ROWS)
        prio = r % 2
        if to_hbm:
            pltpu.make_async_copy(buf.at[here], hbm.at[at], sem).start(priority=prio)
        else:
            pltpu.make_async_copy(hbm.at[at], buf.at[here], sem).start(priority=prio)


def _wait_rows(hbm, buf, sem, *, rows, to_hbm):
    whole = hbm.at[pl.ds(0, rows * TILE_ROWS)]
    (pltpu.make_async_copy(buf, whole, sem) if to_hbm else pltpu.make_async_copy(whole, buf, sem)).wait()


def _expert_mlp(xb, wgu_bf, wd_bf, bgu_ref, bd_ref, rows):
    acc = jnp.broadcast_to(bd_ref[0], (rows, D_MODEL))
    cw = 512
    for c in range(D_MODEL // cw):
        gs = slice(c * cw, (c + 1) * cw)
        us = slice(D_MODEL + c * cw, D_MODEL + (c + 1) * cw)
        gt = jnp.dot(xb, wgu_bf[:, gs], preferred_element_type=F32) + bgu_ref[0, :, gs]
        up = jnp.dot(xb, wgu_bf[:, us], preferred_element_type=F32) + bgu_ref[0, :, us]
        gt = jnp.minimum(gt, SWIGLU_LIMIT)
        up = jnp.clip(up, -SWIGLU_LIMIT, SWIGLU_LIMIT)
        act = (up + 1.0) * (gt * jax.nn.sigmoid(gt * SWIGLU_ALPHA))
        acc = acc + jnp.dot(act.astype(BF16), wd_bf[gs, :], preferred_element_type=F32)
    return acc


def _moe_kernel(be_ref, src_cur, src_nxt, dst_prv, dst_cur, h_hbm, wgu_ref, bgu_ref, wd_ref, bd_ref, ys_hbm,
                x0, x1, y0, y1, wgu_bf, wd_bf, gsem, ssem, *, rows, plane_rows, n_tok):
    i = pl.program_id(0)
    last = pl.num_programs(0) - 1

    changed = jnp.logical_or(i == 0, be_ref[i] != be_ref[jnp.maximum(i - 1, 0)])

    @pl.when(changed)
    def _():
        step = 128

        def cast(j, c):
            rs = pl.ds(pl.multiple_of(j * step, step), step)
            wgu_bf[rs, :] = wgu_ref[0, rs, :].astype(BF16)
            wd_bf[rs, :] = wd_ref[0, rs, :].astype(BF16)
            return c

        lax.fori_loop(0, D_MODEL // step, cast, 0)

    def run(par):
        xc, xn = (x0, x1) if par == 0 else (x1, x0)
        yc, yp = (y0, y1) if par == 0 else (y1, y0)
        gather = functools.partial(_start_rows, hbm=h_hbm, rows=rows, to_hbm=False)
        scatter = functools.partial(_start_rows, hbm=ys_hbm, rows=rows, to_hbm=True)
        gathered = functools.partial(_wait_rows, h_hbm, rows=rows, to_hbm=False)
        scattered = functools.partial(_wait_rows, ys_hbm, rows=rows, to_hbm=True)

        @pl.when(i == 0)
        def _():
            yp[...] = jnp.zeros(yp.shape, yp.dtype)
            spare = plane_rows - n_tok
            fills = [pltpu.make_async_copy(yp.at[pl.ds(0, spare * TILE_ROWS)],
                                           ys_hbm.at[pl.ds((k * plane_rows + n_tok) * TILE_ROWS, spare * TILE_ROWS)],
                                           ssem.at[1 - par])
                     for k in range(TOP_K)]
            for d in fills:
                d.start()
            for d in fills:
                d.wait()
            gather(src_cur, buf=xc, sem=gsem.at[par])

        gathered(xc, gsem.at[par])
        gather(src_nxt, buf=xn, sem=gsem.at[1 - par])

        @pl.when(i >= 1)
        def _():
            scattered(yc, ssem.at[par])

        used = i < be_ref[last + 1]

        @pl.when(used)
        def _():
            scatter(dst_prv, buf=yp, sem=ssem.at[1 - par])
            xb = _load_token_tiles(xc, rows).astype(BF16)
            _store_token_tiles(yc, _expert_mlp(xb, wgu_bf, wd_bf, bgu_ref, bd_ref, rows))

        @pl.when(jnp.logical_not(used))
        def _():
            scatter(dst_prv, buf=yp, sem=ssem.at[1 - par])

        @pl.when(i == last)
        def _():
            scatter(dst_cur, buf=yc, sem=ssem.at[par])
            gathered(xn, gsem.at[1 - par])
            scattered(yp, ssem.at[1 - par])
            scattered(yc, ssem.at[par])

    for par in range(2):
        pl.when(lax.rem(i, 2) == par)(functools.partial(run, par))


def _moe_experts(block_e, ids, h2, w_gate_up, b_gate_up, w_down, b_down, *, rows, plane_rows, n_tok):
    n_blocks = ids.shape[0] - 2
    ids = lax.optimization_barrier(ids)
    tok = lax.shift_right_logical(ids, 2)
    src = tok * TILE_ROWS
    dst = ((ids & (TOP_K - 1)) * plane_rows + tok) * TILE_ROWS
    idblk = lambda off: pl.BlockSpec((1, 1, rows), lambda i, be: (i + off, 0, 0), memory_space=pltpu.SMEM)
    grid_spec = pltpu.PrefetchScalarGridSpec(
        num_scalar_prefetch=1,
        grid=(n_blocks,),
        in_specs=[
            idblk(1), idblk(2), idblk(0), idblk(1),
            pl.BlockSpec(memory_space=pl.ANY),
            pl.BlockSpec((1, D_MODEL, 2 * D_MODEL), lambda i, be: (be[i], 0, 0)),
            pl.BlockSpec((1, 1, 2 * D_MODEL), lambda i, be: (be[i], 0, 0)),
            pl.BlockSpec((1, D_MODEL, D_MODEL), lambda i, be: (be[i], 0, 0)),
            pl.BlockSpec((1, 1, D_MODEL), lambda i, be: (be[i], 0, 0)),
        ],
        out_specs=pl.BlockSpec(memory_space=pl.ANY),
        scratch_shapes=[pltpu.VMEM((rows * TILE_ROWS, LANES), F32)] * 4 + [
            pltpu.VMEM((D_MODEL, 2 * D_MODEL), BF16),
            pltpu.VMEM((D_MODEL, D_MODEL), BF16),
            pltpu.SemaphoreType.DMA((2,)),
            pltpu.SemaphoreType.DMA((2,))],
    )
    return pl.pallas_call(
        functools.partial(_moe_kernel, rows=rows, plane_rows=plane_rows, n_tok=n_tok),
        grid_spec=grid_spec,
        out_shape=jax.ShapeDtypeStruct((TOP_K * plane_rows * TILE_ROWS, LANES), F32),
        compiler_params=_cparams("arbitrary"),
        name="moe_experts",
    )(block_e, src, src, dst, dst, h2, w_gate_up, b_gate_up[:, None], w_down, b_down[:, None])


def _combine_kernel(ys_ref, x1_ref, gate_ref, lnf_ref, y_ref):
    acc = x1_ref[...]
    tm = acc.shape[0]
    for kslot in range(TOP_K):
        acc = acc + gate_ref[:, kslot:kslot + 1] * _load_token_tiles(ys_ref, tm, kslot)
    y_ref[...] = _rms(acc, lnf_ref[...])


def _combine(ys, x1, gates, ln_f, *, tm, n, row0):
    blk0 = row0 // tm
    return pl.pallas_call(
        _combine_kernel,
        grid=(n // tm,),
        in_specs=[
            pl.BlockSpec((TOP_K, tm * TILE_ROWS, LANES), lambda i: (0, i + blk0, 0)),
            pl.BlockSpec((tm, D_MODEL), lambda i: (i + blk0, 0)),
            pl.BlockSpec((tm, LANES), lambda i: (i + blk0, 0)),
            pl.BlockSpec(ln_f.shape, lambda i: (0, 0)),
        ],
        out_specs=pl.BlockSpec((tm, D_MODEL), lambda i: (i, 0)),
        out_shape=jax.ShapeDtypeStruct((n, D_MODEL), F32),
        compiler_params=_cparams("parallel"),
        name="combine",
    )(ys.reshape(TOP_K, -1, LANES), x1, gates, ln_f)


def _route(topi, n_pad, rows):
    n_tok = topi.shape[0]
    e_flat = topi.reshape(-1)
    nk = e_flat.shape[0]
    n_blocks = -(-(nk + N_EXPERTS * (rows - 1)) // rows)
    spare = n_pad - n_tok
    assert 3 * rows <= TOP_K * spare
    experts = jnp.arange(N_EXPERTS, dtype=jnp.int32)
    order = jnp.argsort(e_flat).astype(jnp.int32)
    sizes = jnp.sum((e_flat[:, None] == experts[None, :]).astype(jnp.int32), axis=0)
    padded = (sizes + rows - 1) // rows * rows
    pends = jnp.cumsum(padded)
    slot = jnp.arange(n_blocks * rows, dtype=jnp.int32)
    past = (slot[:, None] >= pends[None, :]).astype(jnp.int32)
    e_slot = jnp.sum(past, axis=1)
    off = slot - jnp.sum(past * padded[None, :], axis=1)
    src = jnp.sum(past * sizes[None, :], axis=1) + off
    size_e = jnp.sum((e_slot[:, None] == experts[None, :]) * sizes[None, :], axis=1)

    def pad_ids(d):
        return (TOP_K * (n_tok + d % spare) + d // spare).astype(jnp.int32)

    ids = jnp.where(off < size_e, order[jnp.clip(src, 0, nk - 1)], pad_ids((slot // rows) % 2 * rows + slot % rows))
    end_ids = pad_ids(2 * rows + jnp.arange(rows, dtype=jnp.int32))
    ids = jnp.concatenate([end_ids, ids, end_ids]).reshape(n_blocks + 2, 1, rows)
    block_e = jnp.minimum(e_slot[::rows], N_EXPERTS - 1).astype(jnp.int32)
    n_used = (pends[-1] // rows).astype(jnp.int32)
    return jnp.concatenate([block_e, n_used[None]]), ids


def kernel(x_prompt, x_sample, cache_k, cache_v, state_wkv, state_shift, rel_bias, ln1, w_in, rwkv_mu, w0,
           w_decay_up, a0, w_a_up, w_g_up, k_k, k_a, r_k, gn_w, gn_b, w_branch_a, w_branch_b, w_out, ln2,
           w_router, b_router, w_gate_up, b_gate_up, w_down, b_down, ln_f):
    bp, seq, _ = x_prompt.shape
    db = x_sample.shape[0]
    n_p = bp * seq
    n_tot = n_p + db
    n_pad = -(-(n_tot + -(-3 * MOE_ROWS // TOP_K)) // TOKEN_TILE) * TOKEN_TILE
    l = 0

    row = lambda a: a.reshape(1, -1)
    seg = _head_seg()
    prep_params = (row(rwkv_mu[l]), row(w0[l]), w_decay_up[l], row(a0[l]), w_a_up[l], w_g_up[l],
                   row(k_k[l]), row(k_a[l]), row(r_k[l]), seg)
    wr_pad = jnp.pad(w_router[l], ((0, 0), (0, LANES - N_EXPERTS))).astype(BF16)
    br_pad = jnp.pad(b_router[l], (0, LANES - N_EXPERTS), constant_values=NEG).reshape(1, LANES)
    post_small = (row(gn_w[l]), row(gn_b[l]), seg)
    post_tail = (row(ln2[l]), wr_pad, br_pad)

    w_in_bf = w_in[l].astype(BF16)
    consts = (post_small + (w_branch_a[l].astype(BF16), w_branch_b[l].astype(BF16), w_out[l].astype(BF16))
              + post_tail)

    xp = x_prompt.reshape(n_p, D_MODEL)
    q_p, k_p, v_p, sg_p, kt_p, vt_p, shift_p, *prep_p = _inproj_prompt(xp, row(ln1[l]), w_in_bf, prep_params,
                                                                     tm=TOKEN_TILE, seq=seq)
    as3 = lambda a: a.reshape(bp, seq, -1)
    oa_p = _attn_prompt(as3(q_p), as3(k_p), as3(v_p), _prompt_bias(rel_bias))
    r_, w_, k2_, vr_, kk_, ka_, bonus_p, g_p = prep_p
    s0_p = jnp.zeros((bp, R_HEADS, R_HEAD, R_HEAD), F32)
    y_p, wkv_p = _lane_scan(r_, w_, k2_, vr_, kk_, ka_, s0_p, tc=SCAN_CHUNK)
    flat = lambda a: a.reshape(n_p, -1)
    bufs = _post(xp, flat(y_p), flat(bonus_p), flat(g_p), flat(oa_p), sg_p, consts,
                 tm=POST_TILE, n_total=n_pad)

    xs = x_sample.reshape(db, D_MODEL)
    q_s, k_s, v_s, rc_s, sg_s = _inproj(xs, row(ln1[l]), w_in_bf, tm=db)
    bias_s, b0_s = _sample_bias(rel_bias, cache_k.shape[2])
    oa_s = _attn_sample(q_s, k_s, v_s, cache_k[l], cache_v[l], bias_s, b0_s)
    prep_s = _rwkv_prep_sample(rc_s, state_shift[l], prep_params)
    sr, sw, sk2, svr, skk, ska, bonus_s, g_s = [a[:, None] for a in prep_s]
    y_s, wkv_s = _rwkv_scan(sr, sw, sk2, svr, skk, ska, state_wkv[l], bb=SCAN_BATCH, tc=1)
    x1_all, h2_all, ti_all, tg_all = _post(xs, y_s[:, 0], bonus_s[:, 0], g_s[:, 0], oa_s, sg_s, consts,
                                           tm=db, n_total=n_pad, row0=n_p, into=bufs)

    block_e, ids = _route(ti_all[:n_tot, :TOP_K], n_pad, MOE_ROWS)
    ys = _moe_experts(block_e, ids, h2_all, w_gate_up[l], b_gate_up[l], w_down[l], b_down[l],
                      rows=MOE_ROWS, plane_rows=n_pad, n_tok=n_tot)
    lnf = row(ln_f)
    y_prompt = _combine(ys, x1_all, tg_all, lnf, tm=TOKEN_TILE, n=n_p, row0=0)
    y_sample = _combine(ys, x1_all, tg_all, lnf, tm=db, n=db, row0=n_p)

    heads = lambda a, b_: a.reshape(1, b_, -1, A_HEADS, HEAD_DIM)
    return (y_prompt.reshape(bp, seq, D_MODEL), y_sample.reshape(db, 1, D_MODEL),
            kt_p.transpose(0, 3, 1, 2)[None], vt_p.transpose(0, 3, 1, 2)[None], wkv_p[None], shift_p[:, 0][None],
            heads(k_s, db), heads(v_s, db), wkv_s[None], rc_s[None])
```
